```python
import jax
import jax.numpy as jnp
from jax import lax
import numpy as np

D_MODEL = 1024
BATCH = 16
SEQ = 256
DEPTH = 2
DEC_BATCH = 4
DEC_SEQ = 4096
PAST_LEN = 256

GRID_W = 64
N_EVEN = (DEPTH + 1) // 2
N_ODD = DEPTH // 2
N_MOD = 6
EPS = 1e-6
ROPE_BASE = 10000.0
NEG_INF = -1e30
Q_BLOCK = 128

MLA_HEADS = 8
MLA_NOPE = 64
MLA_ROPE = 32
MLA_V = 64
MLA_Q_RANK = 384
MLA_KV_RANK = 256
GLA_HEADS = 4
GLA_DK = 64
GLA_DV = 128
GLA_GATE_RANK = 16
GLA_GATE_NORM = 16.0
GLA_CHUNK = 64
IN_AB = (MLA_Q_RANK + MLA_KV_RANK + MLA_ROPE + 2 * GLA_HEADS * GLA_DK
         + 2 * GLA_HEADS * GLA_DV + 2 * GLA_GATE_RANK)
MIX_AB = MLA_HEADS * MLA_V + GLA_HEADS * GLA_DV
GQA_HEADS = 16
GQA_KV_HEADS = 4
GQA_GROUP = GQA_HEADS // GQA_KV_HEADS
GQA_HEAD_DIM = 64
WINDOW = 128
BAND_BLOCK = 128
GQA_Q_DIM = GQA_HEADS * GQA_HEAD_DIM
GQA_KV_DIM = GQA_KV_HEADS * GQA_HEAD_DIM
D_FF = 2816
N_EXPERTS = 8
TOP_K = 2

kernel_name = 'hybrid_dit_mla_gla_swa_step'


def rms_norm(x, g):
    xf = x.astype(jnp.float32)
    y = xf * lax.rsqrt(jnp.mean(xf * xf, axis=-1, keepdims=True) + EPS)
    return (y * g.astype(jnp.float32)).astype(x.dtype)


def modulation(cvec, w, b):
    m = (jax.nn.silu(cvec) @ w + b)[:, None, :]
    return jnp.split(m, N_MOD, axis=-1)


def modulate(x, g, shift, scale):
    return rms_norm(x, g) * (1 + scale) + shift


def _rope_axis(x, pos):
    r = x.shape[-1]
    inv = ROPE_BASE ** (-jnp.arange(0, r, 2, dtype=jnp.float32) / r)
    ang = pos.astype(jnp.float32)[:, None] * inv[None, :]
    cos, sin = jnp.cos(ang)[:, None, :], jnp.sin(ang)[:, None, :]
    x1, x2 = jnp.split(x.astype(jnp.float32), 2, axis=-1)
    return jnp.concatenate([x1 * cos - x2 * sin, x2 * cos + x1 * sin], axis=-1).astype(x.dtype)


def rope_2d(x):
    s = x.shape[1]
    rows = s // GRID_W
    row = jnp.repeat(jnp.arange(rows), GRID_W)
    col = jnp.tile(jnp.arange(GRID_W), rows)
    xr, xc = jnp.split(x, 2, axis=-1)
    return jnp.concatenate([_rope_axis(xr, row), _rope_axis(xc, col)], axis=-1)


def blockwise_queries(fn, *qs):
    b, q = qs[0].shape[:2]
    nb = q // Q_BLOCK
    xs = tuple(jnp.moveaxis(a.reshape((b, nb, Q_BLOCK) + a.shape[2:]), 1, 0) for a in qs)
    out = lax.map(lambda args: fn(*args), xs)
    return jnp.moveaxis(out, 0, 1).reshape((b, q) + out.shape[3:])


def softmax_with_sink(s, sink):
    col = jnp.broadcast_to(sink, s.shape[:-1] + (1,))
    return jax.nn.softmax(jnp.concatenate([s, col], axis=-1), axis=-1)[..., :-1]


def mla_attend(q_nope, q_pe, k_nope, k_pe, v):
    scale = (MLA_NOPE + MLA_ROPE) ** -0.5

    def blk(qn, qp):
        s = (jnp.einsum('bqhd,bkhd->bhqk', qn, k_nope)
             + jnp.einsum('bqhr,bkr->bhqk', qp, k_pe)).astype(jnp.float32) * scale
        p = jax.nn.softmax(s, axis=-1).astype(v.dtype)
        return jnp.einsum('bhqk,bkhd->bqhd', p, v)

    return blockwise_queries(blk, q_nope, q_pe)


def gla_log_gate(gz, w, b):
    bsz, s, _ = gz.shape
    la = jax.nn.log_sigmoid((gz @ w + b).astype(jnp.float32)) / GLA_GATE_NORM
    return la.reshape(bsz, s, GLA_HEADS, GLA_DK).transpose(0, 2, 1, 3)


def gla_chunked(q, k, v, log_a, s0):
    bsz, h, t, dk = q.shape
    dv = v.shape[-1]
    c = GLA_CHUNK
    n = t // c
    f32 = jnp.float32
    q = q.astype(f32).reshape(bsz, h, n, c, dk)
    k = k.astype(f32).reshape(bsz, h, n, c, dk)
    v = v.astype(f32).reshape(bsz, h, n, c, dv)
    b = jnp.cumsum(log_a.astype(f32).reshape(bsz, h, n, c, dk), axis=3)
    b_mid = b[:, :, :, c // 2 - 1:c // 2, :]
    b_last = b[:, :, :, -1:, :]
    lower = jnp.tril(jnp.ones((c, c), dtype=bool))
    a_intra = jnp.einsum('bhncd,bhned->bhnce', q * jnp.exp(b - b_mid), k * jnp.exp(b_mid - b))
    a_intra = jnp.where(lower, a_intra, 0.0)
    o_intra = jnp.einsum('bhnce,bhnev->bhncv', a_intra, v)
    kv_chunk = jnp.einsum('bhncd,bhncv->bhndv', k * jnp.exp(b_last - b), v)
    decay = jnp.exp(b_last[:, :, :, 0, :])

    def step(s, xs):
        dec, kv = xs
        return dec[..., None] * s + kv, s

    s_fin, s_prev = lax.scan(step, s0.astype(f32),
                             (jnp.moveaxis(decay, 2, 0), jnp.moveaxis(kv_chunk, 2, 0)))
    s_prev = jnp.moveaxis(s_prev, 0, 2)
    o_inter = jnp.einsum('bhncd,bhndv->bhncv', q * jnp.exp(b), s_prev)
    return (o_intra + o_inter).reshape(bsz, h, t, dv), s_fin


def gla_bidir(q, k, v, la_f, la_b, s0_f, s0_b):
    o_f, s_f = gla_chunked(q, k, v, la_f, s0_f)
    flip = lambda a: jnp.flip(a, axis=2)
    o_b, s_b = gla_chunked(flip(q), flip(k), flip(v), flip(la_b), s0_b)
    return o_f + flip(o_b), s_f, s_b


def split_ab(z):
    sizes = [MLA_Q_RANK, MLA_KV_RANK, MLA_ROPE, GLA_HEADS * GLA_DK, GLA_HEADS * GLA_DK,
             GLA_HEADS * GLA_DV, GLA_HEADS * GLA_DV, GLA_GATE_RANK, GLA_GATE_RANK]
    return jnp.split(z, [int(s) for s in np.cumsum(sizes)[:-1]], axis=-1)


def even_mixer(h, w_in, q_norm, w_q_up, kv_norm, w_kv_up, wgf, bgf, wgb, bgb, o_norm, w_out, ctx=None):
    bsz, s, _ = h.shape
    cq, ckv, kpe, gq, gk, gv, gr, gaf, gab = split_ab(h @ w_in)
    q = (rms_norm(cq, q_norm) @ w_q_up).reshape(bsz, s, MLA_HEADS, MLA_NOPE + MLA_ROPE)
    q_nope, q_pe = q[..., :MLA_NOPE], q[..., MLA_NOPE:]
    ckv = rms_norm(ckv, kv_norm)
    if ctx is None:
        ckv_all, kpe_all = ckv, kpe
        s0f = jnp.zeros((bsz, GLA_HEADS, GLA_DK, GLA_DV), jnp.float32)
        s0b = s0f
    else:
        q_pe = rope_2d(q_pe)
        kpe = rope_2d(kpe[:, :, None, :])[:, :, 0, :]
        ckv_all = jnp.concatenate([ckv, ctx[0].astype(ckv.dtype)], axis=1)
        kpe_all = jnp.concatenate([kpe, ctx[1].astype(kpe.dtype)], axis=1)
        s0f, s0b = ctx[2], ctx[3]
    kv = (ckv_all @ w_kv_up).reshape(bsz, -1, MLA_HEADS, MLA_NOPE + MLA_V)
    o_a = mla_attend(q_nope, q_pe, kv[..., :MLA_NOPE], kpe_all, kv[..., MLA_NOPE:])
    qg = (gq * GLA_DK ** -0.5).reshape(bsz, s, GLA_HEADS, GLA_DK).transpose(0, 2, 1, 3)
    kg = gk.reshape(bsz, s, GLA_HEADS, GLA_DK).transpose(0, 2, 1, 3)
    vg = gv.reshape(bsz, s, GLA_HEADS, GLA_DV).transpose(0, 2, 1, 3)
    o_b, s_f, s_b = gla_bidir(qg, kg, vg, gla_log_gate(gaf, wgf, bgf), gla_log_gate(gab, wgb, bgb), s0f, s0b)
    o_b = o_b.transpose(0, 2, 1, 3).astype(h.dtype)
    o_b = rms_norm(o_b, o_norm) * jax.nn.silu(gr.reshape(bsz, s, GLA_HEADS, GLA_DV))
    merged = jnp.concatenate([o_a.reshape(bsz, s, MLA_HEADS * MLA_V),
                              o_b.reshape(bsz, s, GLA_HEADS * GLA_DV)], axis=-1)
    return merged @ w_out, (ckv, kpe, s_f, s_b)


def gqa_ctx_attend(q, k, v, sink):
    scale = GQA_HEAD_DIM ** -0.5
    sink_b = sink.astype(jnp.float32).reshape(GQA_KV_HEADS, GQA_GROUP)[None, :, :, None, None]

    def blk(qb):
        s = jnp.einsum('bqngd,bknd->bngqk', qb, k).astype(jnp.float32) * scale
        p = softmax_with_sink(s, sink_b).astype(v.dtype)
        return jnp.einsum('bngqk,bknd->bqngd', p, v)

    return blockwise_queries(blk, q)


def gqa_latent_attend(q, k, v, k_ctx, v_ctx, sink):
    bsz, t = q.shape[:2]
    w = BAND_BLOCK
    nb = t // w
    scale = GQA_HEAD_DIM ** -0.5
    sink_b = sink.astype(jnp.float32).reshape(GQA_KV_HEADS, GQA_GROUP)[None, :, :, None, None]
    pad = jnp.zeros((bsz, w) + k.shape[2:], k.dtype)

    def band(a):
        ap = jnp.concatenate([pad, a, pad], axis=1).reshape((bsz, nb + 2, w) + a.shape[2:])
        return jnp.concatenate([ap[:, :-2], ap[:, 1:-1], ap[:, 2:]], axis=2)

    kb, vb = band(k), band(v)
    qb = q.reshape(bsz, nb, w, GQA_KV_HEADS, GQA_GROUP, GQA_HEAD_DIM)
    qpos = jnp.arange(nb)[:, None, None] * w + jnp.arange(w)[None, :, None]
    kpos = (jnp.arange(nb)[:, None, None] - 1) * w + jnp.arange(3 * w)[None, None, :]
    valid = (jnp.abs(qpos - kpos) <= WINDOW) & (kpos >= 0) & (kpos < t)
    k_ctx = k_ctx.astype(q.dtype)
    v_ctx = v_ctx.astype(v.dtype)

    def blk(args):
        qc, kc, vc, mc = args
        s_loc = jnp.einsum('bqngd,bknd->bngqk', qc, kc).astype(jnp.float32) * scale
        s_loc = jnp.where(mc, s_loc, NEG_INF)
        s_ctx = jnp.einsum('bqngd,bknd->bngqk', qc, k_ctx).astype(jnp.float32) * scale
        p = softmax_with_sink(jnp.concatenate([s_loc, s_ctx], axis=-1), sink_b).astype(v.dtype)
        return (jnp.einsum('bngqk,bknd->bqngd', p[..., :3 * w], vc)
                + jnp.einsum('bngqk,bknd->bqngd', p[..., 3 * w:], v_ctx))

    out = lax.map(blk, (jnp.moveaxis(qb, 1, 0), jnp.moveaxis(kb, 1, 0), jnp.moveaxis(vb, 1, 0), valid))
    return jnp.moveaxis(out, 0, 1).reshape(bsz, t, GQA_KV_HEADS, GQA_GROUP, GQA_HEAD_DIM)


def odd_mixer(h, w_in, sink, w_out, ctx=None):
    bsz, s, _ = h.shape
    q, k, v = jnp.split(h @ w_in, [GQA_Q_DIM, GQA_Q_DIM + GQA_KV_DIM], axis=-1)
    q = q.reshape(bsz, s, GQA_HEADS, GQA_HEAD_DIM)
    k = k.reshape(bsz, s, GQA_KV_HEADS, GQA_HEAD_DIM)
    v = v.reshape(bsz, s, GQA_KV_HEADS, GQA_HEAD_DIM)
    if ctx is None:
        o = gqa_ctx_attend(q.reshape(bsz, s, GQA_KV_HEADS, GQA_GROUP, GQA_HEAD_DIM), k, v, sink)
    else:
        q, k = rope_2d(q), rope_2d(k)
        o = gqa_latent_attend(q.reshape(bsz, s, GQA_KV_HEADS, GQA_GROUP, GQA_HEAD_DIM), k, v,
                              ctx[0], ctx[1], sink)
    return o.reshape(bsz, s, GQA_Q_DIM) @ w_out, (k, v)


def swiglu(h, w_gate, w_up, w_down):
    return (jax.nn.silu(h @ w_gate) * (h @ w_up)) @ w_down


def moe_swiglu(h, w_router, b_router, w_gate, w_up, w_down):
    logits = (h @ w_router + b_router).astype(jnp.float32)
    top_val, top_idx = lax.top_k(logits, TOP_K)
    top_w = jax.nn.softmax(top_val, axis=-1)
    gates = jnp.sum(jax.nn.one_hot(top_idx, N_EXPERTS, dtype=jnp.float32) * top_w[..., None], axis=-2)
    gates = gates.astype(h.dtype)
    out = jnp.zeros_like(h)
    for e in range(N_EXPERTS):
        out = out + gates[..., e:e + 1] * swiglu(h, w_gate[e], w_up[e], w_down[e])
    return out


def setup_inputs(seed: int = 0) -> dict:
    key = jax.random.key(seed)
    ks = iter(jax.random.split(key, 40))
    nrm = lambda shape, scale: jax.random.normal(next(ks), shape, jnp.float32) * scale
    gain = lambda shape: 1.0 + nrm(shape, 0.05)
    d = D_MODEL
    return {
        'x_prompt': nrm((BATCH, SEQ, d), 1.0),
        'x_sample': nrm((DEC_BATCH, DEC_SEQ, d), 1.0),
        'cache_mla_ckv': nrm((DEC_BATCH, N_EVEN, PAST_LEN, MLA_KV_RANK), 1.0),
        'cache_mla_kpe': nrm((DEC_BATCH, N_EVEN, PAST_LEN, MLA_ROPE), 1.0),
        'state_gla_fwd': nrm((DEC_BATCH, N_EVEN, GLA_HEADS, GLA_DK, GLA_DV), 1.0),
        'state_gla_bwd': nrm((DEC_BATCH, N_EVEN, GLA_HEADS, GLA_DK, GLA_DV), 1.0),
        'cache_gqa_k': nrm((DEC_BATCH, N_ODD, PAST_LEN, GQA_KV_HEADS, GQA_HEAD_DIM), 1.0),
        'cache_gqa_v': nrm((DEC_BATCH, N_ODD, PAST_LEN, GQA_KV_HEADS, GQA_HEAD_DIM), 1.0),
        'c': nrm((DEC_BATCH, d), 1.0),
        'c_ctx': nrm((d,), 1.0),
        'w_mod': nrm((DEPTH, d, N_MOD * d), d ** -0.5),
        'b_mod': nrm((DEPTH, N_MOD * d), 0.02),
        'norm_g': gain((DEPTH, 4, d)),
        'w_in_ab': nrm((N_EVEN, d, IN_AB), d ** -0.5),
        'mla_q_norm': gain((N_EVEN, MLA_Q_RANK)),
        'mla_w_q_up': nrm((N_EVEN, MLA_Q_RANK, MLA_HEADS * (MLA_NOPE + MLA_ROPE)), MLA_Q_RANK ** -0.5),
        'mla_kv_norm': gain((N_EVEN, MLA_KV_RANK)),
        'mla_w_kv_up': nrm((N_EVEN, MLA_KV_RANK, MLA_HEADS * (MLA_NOPE + MLA_V)), MLA_KV_RANK ** -0.5),
        'gla_w_gate_f': nrm((N_EVEN, GLA_GATE_RANK, GLA_HEADS * GLA_DK), GLA_GATE_RANK ** -0.5),
        'gla_b_gate_f': nrm((N_EVEN, GLA_HEADS * GLA_DK), 0.1),
        'gla_w_gate_b': nrm((N_EVEN, GLA_GATE_RANK, GLA_HEADS * GLA_DK), GLA_GATE_RANK ** -0.5),
        'gla_b_gate_b': nrm((N_EVEN, GLA_HEADS * GLA_DK), 0.1),
        'gla_norm': gain((N_EVEN, GLA_DV)),
        'w_out_ab': nrm((N_EVEN, MIX_AB, d), MIX_AB ** -0.5),
        'ffn_w_gate': nrm((N_EVEN, d, D_FF), d ** -0.5),
        'ffn_w_up': nrm((N_EVEN, d, D_FF), d ** -0.5),
        'ffn_w_down': nrm((N_EVEN, D_FF, d), D_FF ** -0.5),
        'w_in_c': nrm((N_ODD, d, GQA_Q_DIM + 2 * GQA_KV_DIM), d ** -0.5),
        'gqa_sink': nrm((N_ODD, GQA_HEADS), 1.0),
        'w_out_c': nrm((N_ODD, GQA_Q_DIM, d), GQA_Q_DIM ** -0.5),
        'moe_w_router': nrm((N_ODD, d, N_EXPERTS), d ** -0.5),
        'moe_b_router': nrm((N_ODD, N_EXPERTS), 0.01),
        'moe_w_gate': nrm((N_ODD, N_EXPERTS, d, D_FF), d ** -0.5),
        'moe_w_up': nrm((N_ODD, N_EXPERTS, d, D_FF), d ** -0.5),
        'moe_w_down': nrm((N_ODD, N_EXPERTS, D_FF, d), D_FF ** -0.5),
    }


def reference(x_prompt, x_sample, cache_mla_ckv, cache_mla_kpe, state_gla_fwd, state_gla_bwd,
              cache_gqa_k, cache_gqa_v, c, c_ctx, w_mod, b_mod, norm_g, w_in_ab, mla_q_norm,
              mla_w_q_up, mla_kv_norm, mla_w_kv_up, gla_w_gate_f, gla_b_gate_f, gla_w_gate_b,
              gla_b_gate_b, gla_norm, w_out_ab, ffn_w_gate, ffn_w_up, ffn_w_down, w_in_c, gqa_sink,
              w_out_c, moe_w_router, moe_b_router, moe_w_gate, moe_w_up, moe_w_down):
    xp, xs = x_prompt, x_sample
    ckv_l, kpe_l, sf_l, sb_l, k_l, v_l = [], [], [], [], [], []
    for l in range(DEPTH):
        i = l // 2
        mp = modulation(c_ctx[None, :], w_mod[l], b_mod[l])
        ms = modulation(c, w_mod[l], b_mod[l])
        g = norm_g[l]
        hp = modulate(xp, g[0], mp[0], mp[1])
        hs = modulate(xs, g[0], ms[0], ms[1])
        if l % 2 == 0:
            prm = (w_in_ab[i], mla_q_norm[i], mla_w_q_up[i], mla_kv_norm[i], mla_w_kv_up[i],
                   gla_w_gate_f[i], gla_b_gate_f[i], gla_w_gate_b[i], gla_b_gate_b[i],
                   gla_norm[i], w_out_ab[i])
            op, (ckv, kpe, sf, sb) = even_mixer(hp, *prm)
            os_, _ = even_mixer(hs, *prm, ctx=(cache_mla_ckv[:, i], cache_mla_kpe[:, i],
                                              state_gla_fwd[:, i], state_gla_bwd[:, i]))
            ckv_l.append(ckv)
            kpe_l.append(kpe)
            sf_l.append(sf)
            sb_l.append(sb)
        else:
            prm = (w_in_c[i], gqa_sink[i], w_out_c[i])
            op, (kc, vc) = odd_mixer(hp, *prm)
            os_, _ = odd_mixer(hs, *prm, ctx=(cache_gqa_k[:, i], cache_gqa_v[:, i]))
            k_l.append(kc)
            v_l.append(vc)
        xp = xp + mp[2] * rms_norm(op, g[1])
        xs = xs + ms[2] * rms_norm(os_, g[1])
        hp = modulate(xp, g[2], mp[3], mp[4])
        hs = modulate(xs, g[2], ms[3], ms[4])
        if l % 2 == 0:
            fp = swiglu(hp, ffn_w_gate[i], ffn_w_up[i], ffn_w_down[i])
            fs = swiglu(hs, ffn_w_gate[i], ffn_w_up[i], ffn_w_down[i])
        else:
            fp = moe_swiglu(hp, moe_w_router[i], moe_b_router[i], moe_w_gate[i], moe_w_up[i], moe_w_down[i])
            fs = moe_swiglu(hs, moe_w_router[i], moe_b_router[i], moe_w_gate[i], moe_w_up[i], moe_w_down[i])
        xp = xp + mp[5] * rms_norm(fp, g[3])
        xs = xs + ms[5] * rms_norm(fs, g[3])
    return (xp, xs, jnp.stack(ckv_l, axis=1), jnp.stack(kpe_l, axis=1), jnp.stack(sf_l, axis=1),
            jnp.stack(sb_l, axis=1), jnp.stack(k_l, axis=1), jnp.stack(v_l, axis=1))
```

```python
import functools

import jax
import jax.numpy as jnp
import numpy as np
from jax import lax
from jax.experimental import pallas as pl
from jax.experimental.pallas import tpu as pltpu

F32 = jnp.float32
BF16 = jnp.bfloat16

D_MODEL = 1024
BATCH = 16
SEQ = 256
DEPTH = 2
DEC_BATCH = 4
DEC_SEQ = 4096
PAST_LEN = 256
GRID_W = 64
N_MOD = 6
EPS = 1e-6
ROPE_BASE = 10000.0
NEG_INF = -1e30

MLA_HEADS = 8
MLA_NOPE = 64
MLA_ROPE = 32
MLA_V = 64
MLA_Q_RANK = 384
MLA_KV_RANK = 256
GLA_HEADS = 4
GLA_DK = 64
GLA_DV = 128
GLA_GATE_RANK = 16
GLA_GATE_NORM = 16.0
GLA_CHUNK = 64
GQA_HEADS = 16
GQA_KV_HEADS = 4
GQA_GROUP = GQA_HEADS // GQA_KV_HEADS
GQA_HEAD_DIM = 64
WINDOW = 128
D_FF = 2816
N_EXPERTS = 8
TOP_K = 2

NP_TOK = BATCH * SEQ
NS_TOK = DEC_BATCH * DEC_SEQ
N_TOK = NP_TOK + NS_TOK
N_GROUPS = 1 + DEC_BATCH
MOD_ROWS = 8

LANES = 128
VMEM_LIMIT_BYTES = 56 * 1024 * 1024

TOK_TILE = 512
CUMSUM_BLOCK = 256
MLA_Q_TILE = 512
GQA_Q_TILE = 256
MOE_TOK_TILE = 1024
FF_CHUNK = 1408

_C_CQ = 0
_C_CKV = _C_CQ + MLA_Q_RANK
_C_GQ = _C_CKV + MLA_KV_RANK
_C_GK = _C_GQ + GLA_HEADS * GLA_DK
_C_GV = _C_GK + GLA_HEADS * GLA_DK
_C_GR = _C_GV + GLA_HEADS * GLA_DV
_C_SMALL = _C_GR + GLA_HEADS * GLA_DV
IN_AB_EXT = _C_SMALL + LANES
_S_KPE, _S_KPER, _S_GAF, _S_GAB = 0, MLA_ROPE, 2 * MLA_ROPE, 2 * MLA_ROPE + GLA_GATE_RANK
MLA_HEAD_PAD = LANES
MLA_QK_W = MLA_HEADS * MLA_HEAD_PAD
MLA_VEXT_W = (MLA_HEADS // 2) * 2 * LANES


def _cparams(semantics):
    return pltpu.CompilerParams(dimension_semantics=semantics, vmem_limit_bytes=VMEM_LIMIT_BYTES)


def _const_spec(shape):
    nd = len(shape)
    return pl.BlockSpec(shape, lambda *_: (0,) * nd, pipeline_mode=pl.Buffered(1))


def _log_sigmoid(x):
    return jnp.minimum(x, 0.0) - jnp.log1p(jnp.exp(-jnp.abs(x)))


def _rms(x, g):
    return (x * lax.rsqrt(jnp.mean(x * x, axis=-1, keepdims=True) + EPS)) * g


def _modulate(x, g, shift, scale):
    return _rms(x, g) * (1.0 + scale) + shift


def _dot(a, b):
    return jnp.dot(a, b, preferred_element_type=F32)


def _dot_nt(a, b):
    return lax.dot_general(a, b, (((1,), (1,)), ((), ())), preferred_element_type=F32)


def _dot_tn(a, b):
    return lax.dot_general(a, b, (((0,), (0,)), ((), ())), preferred_element_type=F32)


def _split3(x):
    hi = x.astype(BF16)
    r1 = x - hi.astype(F32)
    mid = r1.astype(BF16)
    lo = (r1 - mid.astype(F32)).astype(BF16)
    return hi, mid, lo


def _lane_tile(x, reps):
    return jnp.concatenate([x] * reps, axis=1)


def _mod_kernel(c_ref, w_ref, b_ref, o_ref):
    c = c_ref[...]
    s = c * jax.nn.sigmoid(c)
    o_ref[...] = _dot(s.astype(BF16), w_ref[...].astype(BF16)) + b_ref[...]


def _modulation(cvec, w_mod, b_mod):
    ncol = N_MOD * D_MODEL
    blk = 1536
    return pl.pallas_call(
        _mod_kernel,
        out_shape=jax.ShapeDtypeStruct((DEPTH, MOD_ROWS, ncol), F32),
        grid=(DEPTH, ncol // blk),
        in_specs=[
            pl.BlockSpec((MOD_ROWS, D_MODEL), lambda l, j: (0, 0)),
            pl.BlockSpec((None, D_MODEL, blk), lambda l, j: (l, 0, j)),
            pl.BlockSpec((None, 1, blk), lambda l, j: (l, 0, j)),
        ],
        out_specs=pl.BlockSpec((None, MOD_ROWS, blk), lambda l, j: (l, 0, j)),
        compiler_params=_cparams(("arbitrary", "arbitrary")),
        name="modulation",
    )(cvec, w_mod, b_mod.reshape(DEPTH, 1, ncol))


def _mod_spec(layer, j, tile):
    tpg = NP_TOK // tile
    return pl.BlockSpec((None, None, None, 1, D_MODEL), lambda i: (layer, i // tpg, j, 0, 0))


def _tok_spec(tile, width):
    return pl.BlockSpec((tile, width), lambda i: (i, 0))


def _rope_row_spec(tile, width):
    npt = NP_TOK // tile
    spt = DEC_SEQ // tile
    return pl.BlockSpec((tile, width), lambda i: (jnp.where(i < npt, 0, 1 + (i - npt) % spt), 0))


def _even_in_kernel(x_ref, g_ref, shift_ref, scale_ref, win_ref, qn_ref, wq_ref, kvn_ref, wkk_ref,
                    wkv_ref, vbias_ref, epl_ref, wgf_ref, bgf_ref, wgb_ref, bgb_ref, lmat_ref,
                    umat_ref, cq_ref, sq_ref, ck_ref, sk_ref,
                    q_out, k_out, v_out, ckv_out, kpe_out, gq_out, gk_out, gv_out, gr_out,
                    bf_out, bb_out):
    h = _modulate(x_ref[...], g_ref[...], shift_ref[...], scale_ref[...])
    z = _dot(h.astype(BF16), win_ref[...])

    cqn = _rms(z[:, _C_CQ:_C_CQ + MLA_Q_RANK], qn_ref[...]).astype(BF16)
    qf = _dot(cqn, wq_ref[...])
    cq_t = _lane_tile(cq_ref[...], MLA_HEADS)
    sq_t = _lane_tile(sq_ref[...], MLA_HEADS)
    q_out[...] = (qf[:, :MLA_QK_W] * cq_t + qf[:, MLA_QK_W:] * sq_t).astype(BF16)

    ckvn = _rms(z[:, _C_CKV:_C_CKV + MLA_KV_RANK], kvn_ref[...])
    ckv_out[...] = ckvn
    small = z[:, _C_SMALL:_C_SMALL + LANES]
    kpe = (small[:, _S_KPE:_S_KPE + MLA_ROPE] * ck_ref[...]
           + small[:, _S_KPER:_S_KPER + MLA_ROPE] * sk_ref[...])
    kpe_out[...] = kpe
    ckvn_b = ckvn.astype(BF16)
    k_out[...] = (_dot(ckvn_b, wkk_ref[...]) + _dot(kpe.astype(BF16), epl_ref[...])).astype(BF16)
    v_out[...] = (_dot(ckvn_b, wkv_ref[...]) + vbias_ref[...]).astype(BF16)

    gq_out[...] = z[:, _C_GQ:_C_GQ + GLA_HEADS * GLA_DK] * (GLA_DK ** -0.5)
    gk_out[...] = z[:, _C_GK:_C_GK + GLA_HEADS * GLA_DK]
    gv_out[...] = z[:, _C_GV:_C_GV + GLA_HEADS * GLA_DV].astype(BF16)
    gr_out[...] = z[:, _C_GR:_C_GR + GLA_HEADS * GLA_DV]

    small_b = small.astype(BF16)
    la_f = _log_sigmoid(_dot(small_b, wgf_ref[...]) + bgf_ref[...]) * (1.0 / GLA_GATE_NORM)
    la_b = _log_sigmoid(_dot(small_b, wgb_ref[...]) + bgb_ref[...]) * (1.0 / GLA_GATE_NORM)
    lmat = lmat_ref[...]
    umat = umat_ref[...]
    for r in range(TOK_TILE // CUMSUM_BLOCK):
        rows = slice(r * CUMSUM_BLOCK, (r + 1) * CUMSUM_BLOCK)
        f_hi, f_mid, f_lo = _split3(la_f[rows])
        bf_out[rows, :] = _dot(lmat, f_hi) + _dot(lmat, f_mid) + _dot(lmat, f_lo)
        b_hi, b_mid, b_lo = _split3(la_b[rows])
        bb_out[rows, :] = _dot(umat, b_hi) + _dot(umat, b_mid) + _dot(umat, b_lo)


def _even_in_proj(x, mod5, layer, g, wts, tabs):
    t = TOK_TILE
    out_widths = [(MLA_QK_W, BF16), (MLA_QK_W, BF16), (MLA_VEXT_W, BF16), (MLA_KV_RANK, F32),
                  (MLA_ROPE, F32), (GLA_HEADS * GLA_DK, F32), (GLA_HEADS * GLA_DK, F32),
                  (GLA_HEADS * GLA_DV, BF16), (GLA_HEADS * GLA_DV, F32),
                  (GLA_HEADS * GLA_DK, F32), (GLA_HEADS * GLA_DK, F32)]
    const_names = ["win", "qn", "wq", "kvn", "wkk", "wkv", "vbias", "epl", "wgf", "bgf", "wgb",
                   "bgb", "lmat", "umat"]
    consts = [wts[n] for n in const_names]
    in_specs = ([_tok_spec(t, D_MODEL), _const_spec((1, D_MODEL)), _mod_spec(layer, 0, t),
                 _mod_spec(layer, 1, t)]
                + [_const_spec(c.shape) for c in consts]
                + [_rope_row_spec(t, LANES), _rope_row_spec(t, LANES),
                   _rope_row_spec(t, MLA_ROPE), _rope_row_spec(t, MLA_ROPE)])
    return pl.pallas_call(
        _even_in_kernel,
        out_shape=[jax.ShapeDtypeStruct((N_TOK, w), dt) for w, dt in out_widths],
        grid=(N_TOK // t,),
        in_specs=in_specs,
        out_specs=[_tok_spec(t, w) for w, _ in out_widths],
        compiler_params=_cparams(("parallel",)),
        name="even_in_proj",
    )(x, g, mod5, mod5, *consts, tabs["mla_cq"], tabs["mla_sq"], tabs["mla_ck"], tabs["mla_sk"])


def _cache_kv_kernel(ckv_ref, kpe_ref, wkk_ref, wkv_ref, vbias_ref, epl_ref, k_out, v_out):
    ckv_b = ckv_ref[...].astype(BF16)
    k_out[...] = (_dot(ckv_b, wkk_ref[...]) + _dot(kpe_ref[...].astype(BF16), epl_ref[...])).astype(BF16)
    v_out[...] = (_dot(ckv_b, wkv_ref[...]) + vbias_ref[...]).astype(BF16)


def _cache_kv(ckv, kpe, wts):
    n = ckv.shape[0]
    consts = [wts[k] for k in ("wkk", "wkv", "vbias", "epl")]
    return pl.pallas_call(
        _cache_kv_kernel,
        out_shape=[jax.ShapeDtypeStruct((n, MLA_QK_W), BF16), jax.ShapeDtypeStruct((n, MLA_VEXT_W), BF16)],
        grid=(1,),
        in_specs=[_const_spec(ckv.shape), _const_spec(kpe.shape)] + [_const_spec(c.shape) for c in consts],
        out_specs=[_const_spec((n, MLA_QK_W)), _const_spec((n, MLA_VEXT_W))],
        compiler_params=_cparams(("arbitrary",)),
        name="mla_cache_kv",
    )(ckv, kpe, *consts)


def _mla_attn_kernel(*refs, n_seg):
    q_ref = refs[0]
    k_refs = refs[1:1 + n_seg]
    v_refs = refs[1 + n_seg:1 + 2 * n_seg]
    o_ref = refs[1 + 2 * n_seg]
    scale = (MLA_NOPE + MLA_ROPE) ** -0.5
    c = scale * float(np.log2(np.e))
    lane = lax.broadcasted_iota(jnp.int32, (q_ref.shape[0], LANES), 1)
    res = []
    for j in range(2):
        qj = q_ref[:, j * LANES:(j + 1) * LANES]
        s = [_dot_nt(qj, k[:, j * LANES:(j + 1) * LANES]) for k in k_refs]
        m = s[0].max(axis=-1, keepdims=True)
        for sj in s[1:]:
            m = jnp.maximum(m, sj.max(axis=-1, keepdims=True))
        r = None
        for sj, v in zip(s, v_refs):
            p = jnp.exp2((sj - m) * c).astype(BF16)
            rj = _dot(p, v[...])
            r = rj if r is None else r + rj
        res.append(r[:, :LANES] / r[:, LANES:])
    o_ref[...] = jnp.where(lane < MLA_V, res[0], res[1]).astype(BF16)


def _mla_attention(q, ks, vs, *, n_batch, seq_q, q_tile, tok_off, k_batch_rows):
    n_seg = len(ks)
    nq = seq_q // q_tile
    qoff = tok_off // q_tile
    grid = (n_batch, MLA_HEADS // 2, nq)
    in_specs = [pl.BlockSpec((q_tile, 2 * LANES), lambda b, hp, i: (qoff + b * nq + i, hp))]
    for s in range(n_seg):
        rows = k_batch_rows[s]
        off = (tok_off // rows) if s == 0 else 0
        in_specs.append(pl.BlockSpec((rows, 2 * LANES), functools.partial(lambda b, hp, i, off: (off + b, hp), off=off)))
    for s in range(n_seg):
        rows = k_batch_rows[s]
        off = (tok_off // rows) if s == 0 else 0
        in_specs.append(pl.BlockSpec((rows, 2 * LANES), functools.partial(lambda b, hp, i, off: (off + b, hp), off=off)))
    return pl.pallas_call(
        functools.partial(_mla_attn_kernel, n_seg=n_seg),
        out_shape=jax.ShapeDtypeStruct((n_batch * seq_q, MLA_HEADS * MLA_V), BF16),
        grid=grid,
        in_specs=in_specs,
        out_specs=pl.BlockSpec((q_tile, LANES), lambda b, hp, i: (b * nq + i, hp)),
        compiler_params=_cparams(("parallel", "parallel", "arbitrary")),
        name=f"mla_attention_{n_seg}seg",
    )(q, *ks, *vs)


def _gla_kernel(q_ref, k_ref, v_ref, gr_ref, bf_ref, bb_ref, s0f_ref, s0b_ref, gn_ref,
                o_ref, sf_ref, sb_ref, acc_ref, *, n_chunks):
    c = GLA_CHUNK
    lane = lax.broadcasted_iota(jnp.int32, (c, LANES), 1)
    lo = lane < GLA_DK
    row = lax.broadcasted_iota(jnp.int32, (c, c), 0)
    col = lax.broadcasted_iota(jnp.int32, (c, c), 1)
    tril = row >= col
    triu = row <= col
    zero_blk = jnp.zeros((GLA_DK, GLA_DV), F32)

    def pair_state_t(s_ref):
        blockdiag = jnp.concatenate(
            [jnp.concatenate([s_ref[0], zero_blk], axis=1),
             jnp.concatenate([zero_blk, s_ref[1]], axis=1)], axis=0)
        return blockdiag.T

    def chunk(ci, b_ref, st, mid_row, last_row, causal):
        rows = pl.ds(pl.multiple_of(ci * c, c), c)
        q = q_ref[rows, :]
        k = k_ref[rows, :]
        v = v_ref[rows, :]
        b = b_ref[rows, :]
        b_mid = b[mid_row:mid_row + 1, :]
        b_last = b[last_row:last_row + 1, :]
        qe = q * jnp.exp(b - b_mid)
        ke = (k * jnp.exp(b_mid - b)).astype(BF16)
        kd = (k * jnp.exp(b_last - b)).astype(BF16)
        qd = q * jnp.exp(b)
        st_b = st.astype(BF16)
        for j in range(2):
            sel = lo if j == 0 else jnp.logical_not(lo)
            a = _dot_nt(jnp.where(sel, qe, 0.0).astype(BF16), ke)
            a = jnp.where(causal, a, 0.0).astype(BF16)
            o = (_dot(a, v[:, j * GLA_DV:(j + 1) * GLA_DV])
                 + _dot_nt(jnp.where(sel, qd, 0.0).astype(BF16), st_b[j * GLA_DV:(j + 1) * GLA_DV, :]))
            acc_ref[rows, j * GLA_DV:(j + 1) * GLA_DV] += o
        return st * jnp.exp(b_last) + _dot_tn(v, kd)

    acc_ref[...] = jnp.zeros_like(acc_ref)

    def body(i, carry):
        st_f, st_b = carry
        st_f = chunk(i, bf_ref, st_f, c // 2 - 1, c - 1, tril)
        st_b = chunk(n_chunks - 1 - i, bb_ref, st_b, c // 2, 0, triu)
        return st_f, st_b

    st_f, st_b = lax.fori_loop(0, n_chunks, body, (pair_state_t(s0f_ref), pair_state_t(s0b_ref)))
    s_f = st_f.T
    s_b = st_b.T
    sf_ref[0] = s_f[:GLA_DK, :GLA_DV]
    sf_ref[1] = s_f[GLA_DK:, GLA_DV:]
    sb_ref[0] = s_b[:GLA_DK, :GLA_DV]
    sb_ref[1] = s_b[GLA_DK:, GLA_DV:]

    gn = gn_ref[...]
    for j in range(2):
        cols = slice(j * GLA_DV, (j + 1) * GLA_DV)
        gr = gr_ref[:, cols]
        o_ref[:, cols] = (_rms(acc_ref[:, cols], gn) * (gr * jax.nn.sigmoid(gr))).astype(BF16)


def _gla(gq, gk, gv, gr, bf, bb, s0f, s0b, gnorm, *, n_batch, seq, tok_off):
    n_chunks = seq // GLA_CHUNK
    boff = tok_off // seq
    hp = GLA_HEADS // 2
    tok = lambda w: pl.BlockSpec((seq, w), lambda b, p: (boff + b, p))
    st = pl.BlockSpec((None, 2, GLA_DK, GLA_DV), lambda b, p: (b, p, 0, 0))
    return pl.pallas_call(
        functools.partial(_gla_kernel, n_chunks=n_chunks),
        out_shape=[jax.ShapeDtypeStruct((n_batch * seq, GLA_HEADS * GLA_DV), BF16),
                   jax.ShapeDtypeStruct((n_batch, GLA_HEADS, GLA_DK, GLA_DV), F32),
                   jax.ShapeDtypeStruct((n_batch, GLA_HEADS, GLA_DK, GLA_DV), F32)],
        grid=(n_batch, hp),
        in_specs=[tok(2 * GLA_DK), tok(2 * GLA_DK), tok(2 * GLA_DV), tok(2 * GLA_DV),
                  tok(2 * GLA_DK), tok(2 * GLA_DK), st, st, _const_spec((1, GLA_DV))],
        out_specs=[pl.BlockSpec((seq, 2 * GLA_DV), lambda b, p: (b, p)), st, st],
        scratch_shapes=[pltpu.VMEM((seq, 2 * GLA_DV), F32)],
        compiler_params=_cparams(("parallel", "parallel")),
        name=f"gla_seq{seq}",
    )(gq, gk, gv, gr, bf, bb, s0f, s0b, gnorm)


def _out_proj_kernel(*refs, n_in):
    o_refs = refs[:n_in]
    (w_ref, x_ref, g1_ref, gate_ref, g2_ref, shift_ref, scale_ref, x_out, h_out) = refs[n_in:]
    o = o_refs[0][...] if n_in == 1 else jnp.concatenate([r[...] for r in o_refs], axis=1)
    y = _dot(o, w_ref[...])
    x = x_ref[...] + gate_ref[...] * _rms(y, g1_ref[...])
    x_out[...] = x
    h_out[...] = _modulate(x, g2_ref[...], shift_ref[...], scale_ref[...]).astype(BF16)


def _out_proj(os_, w, x, mod5, layer, g1, g2):
    t = TOK_TILE
    n_in = len(os_)
    in_specs = ([_tok_spec(t, o.shape[1]) for o in os_]
                + [_const_spec(w.shape), _tok_spec(t, D_MODEL), _const_spec((1, D_MODEL)),
                   _mod_spec(layer, 2, t), _const_spec((1, D_MODEL)), _mod_spec(layer, 3, t),
                   _mod_spec(layer, 4, t)])
    return pl.pallas_call(
        functools.partial(_out_proj_kernel, n_in=n_in),
        out_shape=[jax.ShapeDtypeStruct((N_TOK, D_MODEL), F32), jax.ShapeDtypeStruct((N_TOK, D_MODEL), BF16)],
        grid=(N_TOK // t,),
        in_specs=in_specs,
        out_specs=[_tok_spec(t, D_MODEL), _tok_spec(t, D_MODEL)],
        compiler_params=_cparams(("parallel",)),
        name=f"out_proj_{n_in}",
    )(*os_, w, x, g1, mod5, g2, mod5, mod5)


def _ffn_kernel(h_ref, x_ref, wg_ref, wu_ref, wd_ref, g_ref, gate_ref, x_out):
    h = h_ref[...]
    f = None
    for cidx in range(D_FF // FF_CHUNK):
        cols = slice(cidx * FF_CHUNK, (cidx + 1) * FF_CHUNK)
        a = _dot(h, wg_ref[:, cols])
        u = _dot(h, wu_ref[:, cols])
        fc = _dot(((a * jax.nn.sigmoid(a)) * u).astype(BF16), wd_ref[cols, :])
        f = fc if f is None else f + fc
    x_out[...] = x_ref[...] + gate_ref[...] * _rms(f, g_ref[...])


def _ffn(h, x, wg, wu, wd, mod5, layer, g3):
    t = TOK_TILE
    return pl.pallas_call(
        _ffn_kernel,
        out_shape=jax.ShapeDtypeStruct((N_TOK, D_MODEL), F32),
        grid=(N_TOK // t,),
        in_specs=[_tok_spec(t, D_MODEL), _tok_spec(t, D_MODEL), _const_spec(wg.shape),
                  _const_spec(wu.shape), _const_spec(wd.shape), _const_spec((1, D_MODEL)),
                  _mod_spec(layer, 5, t)],
        out_specs=_tok_spec(t, D_MODEL),
        compiler_params=_cparams(("parallel",)),
        name="ffn_swiglu",
    )(h, x, wg, wu, wd, g3, mod5)


GQA_Q_W = GQA_HEADS * GQA_HEAD_DIM
GQA_KV_W = GQA_KV_HEADS * GQA_HEAD_DIM
GQA_VEXT_W = (GQA_KV_HEADS // 2) * 2 * LANES
_O_Q, _O_QR = 0, GQA_Q_W
_O_K, _O_KR = 2 * GQA_Q_W, 2 * GQA_Q_W + GQA_KV_W
_O_V = 2 * GQA_Q_W + 2 * GQA_KV_W
IN_C_EXT = _O_V + GQA_VEXT_W


def _odd_in_kernel(x_ref, g_ref, shift_ref, scale_ref, win_ref, vbias_ref, c_ref, s_ref,
                   q_out, k_out, kb_out, v_out, vb_out):
    h = _modulate(x_ref[...], g_ref[...], shift_ref[...], scale_ref[...])
    z = _dot(h.astype(BF16), win_ref[...])
    c_t = _lane_tile(c_ref[...], GQA_Q_W // LANES)
    s_t = _lane_tile(s_ref[...], GQA_Q_W // LANES)
    q = z[:, _O_Q:_O_Q + GQA_Q_W] * c_t + z[:, _O_QR:_O_QR + GQA_Q_W] * s_t
    q_out[...] = (q * (GQA_HEAD_DIM ** -0.5)).astype(BF16)
    k = (z[:, _O_K:_O_K + GQA_KV_W] * c_t[:, :GQA_KV_W]
         + z[:, _O_KR:_O_KR + GQA_KV_W] * s_t[:, :GQA_KV_W])
    k_out[...] = k
    kb_out[...] = k.astype(BF16)
    vext = z[:, _O_V:_O_V + GQA_VEXT_W] + vbias_ref[...]
    vb_out[...] = vext.astype(BF16)
    for p in range(GQA_KV_HEADS // 2):
        v_out[:, p * LANES:(p + 1) * LANES] = vext[:, 2 * p * LANES:(2 * p + 1) * LANES]


def _odd_in_proj(x, mod5, layer, g, win, vbias, tab_c, tab_s):
    t = TOK_TILE
    out_widths = [(GQA_Q_W, BF16), (GQA_KV_W, F32), (GQA_KV_W, BF16), (GQA_KV_W, F32), (GQA_VEXT_W, BF16)]
    return pl.pallas_call(
        _odd_in_kernel,
        out_shape=[jax.ShapeDtypeStruct((N_TOK, w), dt) for w, dt in out_widths],
        grid=(N_TOK // t,),
        in_specs=[_tok_spec(t, D_MODEL), _const_spec((1, D_MODEL)), _mod_spec(layer, 0, t),
                  _mod_spec(layer, 1, t), _const_spec(win.shape), _const_spec(vbias.shape),
                  _rope_row_spec(t, LANES), _rope_row_spec(t, LANES)],
        out_specs=[_tok_spec(t, w) for w, _ in out_widths],
        compiler_params=_cparams(("parallel",)),
        name="odd_in_proj",
    )(x, g, mod5, mod5, win, vbias, tab_c, tab_s)


def _gqa_kernel(sink_ref, q_ref, *refs, local_len):
    if local_len:
        kl_ref, vl_ref, kc_ref, vc_ref, o_ref = refs
    else:
        kc_ref, vc_ref, o_ref = refs
    tq = q_ref.shape[0]
    lane = lax.broadcasted_iota(jnp.int32, (tq, LANES), 1)
    lo = lane < GQA_HEAD_DIM
    if local_len:
        i = pl.program_id(1)
        q0 = i * tq
        seq = kl_ref.shape[0]
        kstart = pl.multiple_of(jnp.clip(q0 - WINDOW, 0, seq - local_len), LANES)
        qpos = q0 + lax.broadcasted_iota(jnp.int32, (tq, local_len), 0)
        kpos = kstart + lax.broadcasted_iota(jnp.int32, (tq, local_len), 1)
        band = jnp.abs(qpos - kpos) <= WINDOW
    for p in range(GQA_KV_HEADS // 2):
        kc = kc_ref[:, p * LANES:(p + 1) * LANES]
        vc = vc_ref[:, 2 * p * LANES:(2 * p + 2) * LANES]
        if local_len:
            kl = kl_ref[pl.ds(kstart, local_len), p * LANES:(p + 1) * LANES]
            vl = vl_ref[pl.ds(kstart, local_len), 2 * p * LANES:(2 * p + 2) * LANES]
        for blk in range(GQA_GROUP):
            cols = slice((p * GQA_GROUP + blk) * LANES, (p * GQA_GROUP + blk + 1) * LANES)
            qb = q_ref[:, cols]
            res = []
            for half in range(2):
                head = (2 * p + half) * GQA_GROUP + blk
                sink = sink_ref[head]
                qh = jnp.where(lo if half == 0 else jnp.logical_not(lo), qb, jnp.zeros_like(qb))
                s_c = _dot_nt(qh, kc)
                m = jnp.maximum(s_c.max(axis=-1, keepdims=True), sink)
                if local_len:
                    s_l = jnp.where(band, _dot_nt(qh, kl), NEG_INF)
                    m = jnp.maximum(m, s_l.max(axis=-1, keepdims=True))
                r = _dot(jnp.exp(s_c - m).astype(BF16), vc)
                if local_len:
                    r = r + _dot(jnp.exp(s_l - m).astype(BF16), vl)
                res.append(r[:, :LANES] / (r[:, LANES:] + jnp.exp(sink - m)))
            o_ref[:, cols] = jnp.where(lo, res[0], res[1]).astype(BF16)


def _gqa_attention(sink, q, k_loc, v_loc, k_ctx, v_ctx, *, n_batch, seq_q, q_tile, tok_off, n_ctx, local):
    nq = seq_q // q_tile
    qoff = tok_off // q_tile
    local_len = q_tile + 2 * WINDOW if local else 0
    in_specs = [pl.BlockSpec(memory_space=pltpu.SMEM),
                pl.BlockSpec((q_tile, GQA_Q_W), lambda b, i: (qoff + b * nq + i, 0))]
    args = [sink, q]
    if local:
        boff = tok_off // seq_q
        in_specs += [pl.BlockSpec((seq_q, GQA_KV_W), lambda b, i: (boff + b, 0)),
                     pl.BlockSpec((seq_q, GQA_VEXT_W), lambda b, i: (boff + b, 0))]
        args += [k_loc, v_loc]
    in_specs += [pl.BlockSpec((n_ctx, GQA_KV_W), lambda b, i: (b, 0)),
                 pl.BlockSpec((n_ctx, GQA_VEXT_W), lambda b, i: (b, 0))]
    args += [k_ctx, v_ctx]
    return pl.pallas_call(
        functools.partial(_gqa_kernel, local_len=local_len),
        out_shape=jax.ShapeDtypeStruct((n_batch * seq_q, GQA_Q_W), BF16),
        grid=(n_batch, nq),
        in_specs=in_specs,
        out_specs=pl.BlockSpec((q_tile, GQA_Q_W), lambda b, i: (b * nq + i, 0)),
        compiler_params=_cparams(("parallel", "arbitrary")),
        name="gqa_local" if local else "gqa_ctx",
    )(*args)


def _router_kernel(h_ref, w_ref, b_ref, g_out):
    logits = _dot(h_ref[...], w_ref[...]) + b_ref[...]
    lane = lax.broadcasted_iota(jnp.int32, logits.shape, 1)
    valid = lane < N_EXPERTS
    neg = float(np.finfo(np.float32).min)
    lg = jnp.where(valid, logits, neg)
    v1 = lg.max(axis=-1, keepdims=True)
    i1 = jnp.min(jnp.where(lg == v1, lane, LANES), axis=-1, keepdims=True)
    lg2 = jnp.where(lane == i1, neg, lg)
    v2 = lg2.max(axis=-1, keepdims=True)
    i2 = jnp.min(jnp.where(lg2 == v2, lane, LANES), axis=-1, keepdims=True)
    e2 = jnp.exp(v2 - v1)
    den = 1.0 + e2
    g_out[...] = jnp.where(lane == i1, 1.0 / den, jnp.where(lane == i2, e2 / den, 0.0))


def _router(h, w, b):
    t = TOK_TILE
    return pl.pallas_call(
        _router_kernel,
        out_shape=jax.ShapeDtypeStruct((N_TOK, LANES), F32),
        grid=(N_TOK // t,),
        in_specs=[_tok_spec(t, D_MODEL), _const_spec(w.shape), _const_spec(b.shape)],
        out_specs=_tok_spec(t, LANES),
        compiler_params=_cparams(("parallel",)),
        name="moe_router",
    )(h, w, b)


def _moe_kernel(h_ref, gates_ref, x_ref, wg_ref, wu_ref, wd_ref, g_ref, gate_ref, x_out, acc_ref):
    e = pl.program_id(1)
    fc = pl.program_id(2)

    @pl.when((e == 0) & (fc == 0))
    def _():
        acc_ref[...] = jnp.zeros_like(acc_ref)

    h = h_ref[...]
    a = _dot(h, wg_ref[...])
    u = _dot(h, wu_ref[...])
    f = _dot(((a * jax.nn.sigmoid(a)) * u).astype(BF16), wd_ref[...])
    gates = gates_ref[...]
    lane = lax.broadcasted_iota(jnp.int32, gates.shape, 1)
    ge = jnp.sum(jnp.where(lane == e, gates, 0.0), axis=-1, keepdims=True)
    acc_ref[...] += ge * f

    @pl.when((e == N_EXPERTS - 1) & (fc == D_FF // FF_CHUNK - 1))
    def _():
        x_out[...] = x_ref[...] + gate_ref[...] * _rms(acc_ref[...], g_ref[...])


def _moe(h, gates, x, wg, wu, wd, mod5, layer, g3):
    t = MOE_TOK_TILE
    nfc = D_FF // FF_CHUNK
    tok = lambda w: pl.BlockSpec((t, w), lambda i, e, f: (i, 0))
    tpg = NP_TOK // t
    return pl.pallas_call(
        _moe_kernel,
        out_shape=jax.ShapeDtypeStruct((N_TOK, D_MODEL), F32),
        grid=(N_TOK // t, N_EXPERTS, nfc),
        in_specs=[tok(D_MODEL), tok(LANES), tok(D_MODEL),
                  pl.BlockSpec((None, D_MODEL, FF_CHUNK), lambda i, e, f: (e, 0, f)),
                  pl.BlockSpec((None, D_MODEL, FF_CHUNK), lambda i, e, f: (e, 0, f)),
                  pl.BlockSpec((None, FF_CHUNK, D_MODEL), lambda i, e, f: (e, f, 0)),
                  pl.BlockSpec((1, D_MODEL), lambda i, e, f: (0, 0)),
                  pl.BlockSpec((None, None, None, 1, D_MODEL), lambda i, e, f: (layer, i // tpg, 5, 0, 0))],
        out_specs=tok(D_MODEL),
        scratch_shapes=[pltpu.VMEM((t, D_MODEL), F32)],
        compiler_params=_cparams(("parallel", "arbitrary", "arbitrary")),
        name="moe_experts",
    )(h, gates, x, wg, wu, wd, g3, mod5)


def _rot_cols(w, half):
    k, n = w.shape
    wb = w.reshape(k, n // (2 * half), 2, half)
    return jnp.stack([-wb[:, :, 1], wb[:, :, 0]], axis=2).reshape(k, n)


def _axis_tables(r, pos):
    inv = ROPE_BASE ** (-jnp.arange(0, r, 2, dtype=F32) / r)
    ang = pos.astype(F32)[:, None] * inv[None, :]
    cos, sin = jnp.cos(ang), jnp.sin(ang)
    return jnp.concatenate([cos, cos], axis=1), jnp.concatenate([sin, sin], axis=1)


def _rope_tables(r):
    s = jnp.arange(DEC_SEQ)
    cr, sr = _axis_tables(r // 2, s // GRID_W)
    cc, sc = _axis_tables(r // 2, s % GRID_W)
    return jnp.concatenate([cr, cc], axis=1), jnp.concatenate([sr, sc], axis=1)


def _with_identity(tab, ident):
    return jnp.concatenate([jnp.full((TOK_TILE, tab.shape[1]), ident, F32), tab], axis=0)


def _prep_tables():
    c32, s32 = _rope_tables(MLA_ROPE)
    ones = jnp.ones((DEC_SEQ, MLA_NOPE), F32)
    pad1 = jnp.ones((DEC_SEQ, MLA_HEAD_PAD - MLA_NOPE - MLA_ROPE), F32)
    cq = jnp.concatenate([ones, c32, pad1], axis=1)
    sq = jnp.concatenate([0 * ones, s32, 0 * pad1], axis=1)
    c64, s64 = _rope_tables(GQA_HEAD_DIM)
    return {
        "mla_cq": _with_identity(cq, 1.0), "mla_sq": _with_identity(sq, 0.0),
        "mla_ck": _with_identity(c32, 1.0), "mla_sk": _with_identity(s32, 0.0),
        "gqa_c": _with_identity(jnp.concatenate([c64, c64], axis=1), 1.0),
        "gqa_s": _with_identity(jnp.concatenate([s64, s64], axis=1), 0.0),
    }


def _prep_even(w_in, q_norm, w_q_up, kv_norm, w_kv_up, wgf, bgf, wgb, bgb):
    sizes = [MLA_Q_RANK, MLA_KV_RANK, MLA_ROPE, GLA_HEADS * GLA_DK, GLA_HEADS * GLA_DK,
             GLA_HEADS * GLA_DV, GLA_HEADS * GLA_DV, GLA_GATE_RANK, GLA_GATE_RANK]
    cq, ckv, kpe, gq, gk, gv, gr, gaf, gab = jnp.split(w_in, [int(s) for s in np.cumsum(sizes)[:-1]], axis=1)
    pad = jnp.zeros((D_MODEL, LANES - 2 * MLA_ROPE - 2 * GLA_GATE_RANK), F32)
    win = jnp.concatenate([cq, ckv, gq, gk, gv, gr, kpe, _rot_cols(kpe, MLA_ROPE // 4), gaf, gab, pad], axis=1)

    wq = w_q_up.reshape(MLA_Q_RANK, MLA_HEADS, MLA_NOPE + MLA_ROPE)
    nope, pe = wq[..., :MLA_NOPE], wq[..., MLA_NOPE:]
    pe_rot = _rot_cols(pe.reshape(MLA_Q_RANK, MLA_HEADS * MLA_ROPE), MLA_ROPE // 4).reshape(pe.shape)
    zpad = jnp.zeros((MLA_Q_RANK, MLA_HEADS, MLA_HEAD_PAD - MLA_NOPE - MLA_ROPE), F32)
    wq_main = jnp.concatenate([nope, pe, zpad], axis=-1).reshape(MLA_Q_RANK, MLA_QK_W)
    wq_rot = jnp.concatenate([0 * nope, pe_rot, zpad], axis=-1).reshape(MLA_Q_RANK, MLA_QK_W)

    wkv = w_kv_up.reshape(MLA_KV_RANK, MLA_HEADS, MLA_NOPE + MLA_V)
    knope, vv = wkv[..., :MLA_NOPE], wkv[..., MLA_NOPE:]
    wkk = jnp.concatenate([knope, jnp.zeros((MLA_KV_RANK, MLA_HEADS, MLA_HEAD_PAD - MLA_NOPE), F32)],
                          axis=-1).reshape(MLA_KV_RANK, MLA_QK_W)
    vpair = vv.reshape(MLA_KV_RANK, MLA_HEADS // 2, 2 * MLA_V)
    wkv_ext = jnp.concatenate([vpair, jnp.zeros((MLA_KV_RANK, MLA_HEADS // 2, LANES), F32)],
                              axis=-1).reshape(MLA_KV_RANK, MLA_VEXT_W)
    vbias = jnp.tile(jnp.concatenate([jnp.zeros((LANES,), F32), jnp.ones((LANES,), F32)]),
                     MLA_HEADS // 2).reshape(1, MLA_VEXT_W)
    epl = jnp.tile(jnp.concatenate([jnp.zeros((MLA_ROPE, MLA_NOPE), F32), jnp.eye(MLA_ROPE, dtype=F32),
                                    jnp.zeros((MLA_ROPE, MLA_HEAD_PAD - MLA_NOPE - MLA_ROPE), F32)], axis=1),
                   (1, MLA_HEADS))

    def gate_w(w, off):
        return jnp.zeros((LANES, GLA_HEADS * GLA_DK), F32).at[off:off + GLA_GATE_RANK].set(w)

    r = np.arange(CUMSUM_BLOCK)
    same = (r[:, None] // GLA_CHUNK) == (r[None, :] // GLA_CHUNK)
    lmat = jnp.asarray(same & (r[:, None] >= r[None, :]), BF16)
    umat = jnp.asarray(same & (r[:, None] <= r[None, :]), BF16)
    return {
        "win": win.astype(BF16), "qn": q_norm.reshape(1, -1), "wq": jnp.concatenate([wq_main, wq_rot], axis=1).astype(BF16),
        "kvn": kv_norm.reshape(1, -1), "wkk": wkk.astype(BF16), "wkv": wkv_ext.astype(BF16), "vbias": vbias,
        "epl": epl.astype(BF16), "wgf": gate_w(wgf, _S_GAF).astype(BF16), "bgf": bgf.reshape(1, -1),
        "wgb": gate_w(wgb, _S_GAB).astype(BF16), "bgb": bgb.reshape(1, -1), "lmat": lmat, "umat": umat,
    }


def _gqa_head_perm():
    heads = []
    for p in range(GQA_KV_HEADS // 2):
        for i in range(GQA_GROUP):
            heads += [(2 * p) * GQA_GROUP + i, (2 * p + 1) * GQA_GROUP + i]
    return np.asarray(heads)


def _prep_odd(w_in, w_out):
    perm = _gqa_head_perm()
    wq = w_in[:, :GQA_Q_W].reshape(D_MODEL, GQA_HEADS, GQA_HEAD_DIM)[:, perm].reshape(D_MODEL, GQA_Q_W)
    wk = w_in[:, GQA_Q_W:GQA_Q_W + GQA_KV_W]
    wv = w_in[:, GQA_Q_W + GQA_KV_W:].reshape(D_MODEL, GQA_KV_HEADS // 2, 2 * GQA_HEAD_DIM)
    wv_ext = jnp.concatenate([wv, jnp.zeros((D_MODEL, GQA_KV_HEADS // 2, LANES), F32)], axis=-1).reshape(D_MODEL, GQA_VEXT_W)
    win = jnp.concatenate([wq, _rot_cols(wq, GQA_HEAD_DIM // 4), wk, _rot_cols(wk, GQA_HEAD_DIM // 4), wv_ext], axis=1)
    vbias = jnp.tile(jnp.concatenate([jnp.zeros((LANES,), F32), jnp.ones((LANES,), F32)]),
                     GQA_KV_HEADS // 2).reshape(1, GQA_VEXT_W)
    wo = w_out.reshape(GQA_HEADS, GQA_HEAD_DIM, D_MODEL)[perm].reshape(GQA_Q_W, D_MODEL)
    return win.astype(BF16), vbias, wo.astype(BF16)


def _ext_v(v):
    rows = v.shape[0]
    vp = v.reshape(rows, GQA_KV_HEADS // 2, 2 * GQA_HEAD_DIM)
    return jnp.concatenate([vp, jnp.ones((rows, GQA_KV_HEADS // 2, LANES), v.dtype)], axis=-1).reshape(rows, GQA_VEXT_W)


def kernel(x_prompt, x_sample, cache_mla_ckv, cache_mla_kpe, state_gla_fwd, state_gla_bwd, cache_gqa_k, cache_gqa_v, c, c_ctx, w_mod, b_mod, norm_g, w_in_ab, mla_q_norm, mla_w_q_up, mla_kv_norm, mla_w_kv_up, gla_w_gate_f, gla_b_gate_f, gla_w_gate_b, gla_b_gate_b, gla_norm, w_out_ab, ffn_w_gate, ffn_w_up, ffn_w_down, w_in_c, gqa_sink, w_out_c, moe_w_router, moe_b_router, moe_w_gate, moe_w_up, moe_w_down):
    x = jnp.concatenate([x_prompt.reshape(NP_TOK, D_MODEL), x_sample.reshape(NS_TOK, D_MODEL)], axis=0)
    cvec = jnp.concatenate([c_ctx[None, :], c, jnp.zeros((MOD_ROWS - N_GROUPS, D_MODEL), F32)], axis=0)
    mod5 = _modulation(cvec, w_mod, b_mod).reshape(DEPTH, MOD_ROWS, N_MOD, 1, D_MODEL)
    tabs = _prep_tables()
    gvec = lambda l, j: norm_g[l, j].reshape(1, D_MODEL)

    wts = _prep_even(w_in_ab[0], mla_q_norm[0], mla_w_q_up[0], mla_kv_norm[0], mla_w_kv_up[0],
                     gla_w_gate_f[0], gla_b_gate_f[0], gla_w_gate_b[0], gla_b_gate_b[0])
    (q, k, v, ckv, kpe, gq, gk, gv, gr, bf, bb) = _even_in_proj(x, mod5, 0, gvec(0, 0), wts, tabs)
    kc, vc = _cache_kv(cache_mla_ckv[:, 0].reshape(DEC_BATCH * PAST_LEN, MLA_KV_RANK),
                       cache_mla_kpe[:, 0].reshape(DEC_BATCH * PAST_LEN, MLA_ROPE), wts)
    oa_p = _mla_attention(q, [k], [v], n_batch=BATCH, seq_q=SEQ, q_tile=SEQ, tok_off=0, k_batch_rows=[SEQ])
    oa_s = _mla_attention(q, [k, kc], [v, vc], n_batch=DEC_BATCH, seq_q=DEC_SEQ, q_tile=MLA_Q_TILE,
                          tok_off=NP_TOK, k_batch_rows=[DEC_SEQ, PAST_LEN])
    gn = gla_norm[0].reshape(1, GLA_DV)
    zero_state = jnp.zeros((BATCH, GLA_HEADS, GLA_DK, GLA_DV), F32)
    ob_p, sf, sb = _gla(gq, gk, gv, gr, bf, bb, zero_state, zero_state, gn, n_batch=BATCH, seq=SEQ, tok_off=0)
    ob_s, _, _ = _gla(gq, gk, gv, gr, bf, bb, state_gla_fwd[:, 0], state_gla_bwd[:, 0], gn,
                      n_batch=DEC_BATCH, seq=DEC_SEQ, tok_off=NP_TOK)
    oa = jnp.concatenate([oa_p, oa_s], axis=0)
    ob = jnp.concatenate([ob_p, ob_s], axis=0)
    x, h = _out_proj([oa, ob], w_out_ab[0].astype(BF16), x, mod5, 0, gvec(0, 1), gvec(0, 2))
    x = _ffn(h, x, ffn_w_gate[0].astype(BF16), ffn_w_up[0].astype(BF16), ffn_w_down[0].astype(BF16),
             mod5, 0, gvec(0, 3))

    win_c, vbias_c, wo_c = _prep_odd(w_in_c[0], w_out_c[0])
    qg, kg, kgb, vg, vgb = _odd_in_proj(x, mod5, 1, gvec(1, 0), win_c, vbias_c, tabs["gqa_c"], tabs["gqa_s"])
    sink = gqa_sink[0]
    og_p = _gqa_attention(sink, qg, None, None, kgb, vgb, n_batch=BATCH, seq_q=SEQ, q_tile=SEQ, tok_off=0, n_ctx=SEQ, local=False)
    kc_g = cache_gqa_k[:, 0].reshape(DEC_BATCH * PAST_LEN, GQA_KV_W).astype(BF16)
    vc_g = _ext_v(cache_gqa_v[:, 0].reshape(DEC_BATCH * PAST_LEN, GQA_KV_W)).astype(BF16)
    og_s = _gqa_attention(sink, qg, kgb, vgb, kc_g, vc_g, n_batch=DEC_BATCH, seq_q=DEC_SEQ, q_tile=GQA_Q_TILE,
                          tok_off=NP_TOK, n_ctx=PAST_LEN, local=True)
    og = jnp.concatenate([og_p, og_s], axis=0)
    x, h = _out_proj([og], wo_c, x, mod5, 1, gvec(1, 1), gvec(1, 2))
    w_r = jnp.zeros((D_MODEL, LANES), F32).at[:, :N_EXPERTS].set(moe_w_router[0]).astype(BF16)
    b_r = jnp.zeros((1, LANES), F32).at[0, :N_EXPERTS].set(moe_b_router[0])
    gates = _router(h, w_r, b_r)
    x = _moe(h, gates, x, moe_w_gate[0].astype(BF16), moe_w_up[0].astype(BF16), moe_w_down[0].astype(BF16),
             mod5, 1, gvec(1, 3))

    y_prompt = x[:NP_TOK].reshape(BATCH, SEQ, D_MODEL)
    y_sample = x[NP_TOK:].reshape(DEC_BATCH, DEC_SEQ, D_MODEL)
    new_ckv = ckv[:NP_TOK].reshape(BATCH, 1, SEQ, MLA_KV_RANK)
    new_kpe = kpe[:NP_TOK].reshape(BATCH, 1, SEQ, MLA_ROPE)
    new_k = kg[:NP_TOK].reshape(BATCH, 1, SEQ, GQA_KV_HEADS, GQA_HEAD_DIM)
    new_v = vg[:NP_TOK].reshape(BATCH, 1, SEQ, GQA_KV_HEADS, GQA_HEAD_DIM)
    return (y_prompt, y_sample, new_ckv, new_kpe, sf[:, None], sb[:, None], new_k, new_v)
```

```python
import functools

import jax
import jax.numpy as jnp
import numpy as np
from jax import lax
from jax.experimental import pallas as pl
from jax.experimental.pallas import tpu as pltpu
from jax.experimental.pallas import tpu_sc as plsc

F32 = jnp.float32
BF16 = jnp.bfloat16

D_MODEL = 1024
BATCH = 16
SEQ = 256
DEPTH = 2
DEC_BATCH = 4
DEC_SEQ = 4096
PAST_LEN = 256
GRID_W = 64
N_MOD = 6
EPS = 1e-6
ROPE_BASE = 10000.0
NEG_INF = -1e30

MLA_HEADS = 8
MLA_NOPE = 64
MLA_ROPE = 32
MLA_V = 64
MLA_Q_RANK = 384
MLA_KV_RANK = 256
GLA_HEADS = 4
GLA_DK = 64
GLA_DV = 128
GLA_GATE_RANK = 16
GLA_GATE_NORM = 16.0
GLA_CHUNK = 64
GQA_HEADS = 16
GQA_KV_HEADS = 4
GQA_GROUP = GQA_HEADS // GQA_KV_HEADS
GQA_HEAD_DIM = 64
WINDOW = 128
D_FF = 2816
N_EXPERTS = 8
TOP_K = 2

NP_TOK = BATCH * SEQ
NS_TOK = DEC_BATCH * DEC_SEQ
N_TOK = NP_TOK + NS_TOK
N_GROUPS = 1 + DEC_BATCH
MOD_ROWS = 8

LANES = 128
VMEM_LIMIT_BYTES = 56 * 1024 * 1024

TOK_TILE = 512
CUMSUM_BLOCK = 256
MLA_Q_TILE = 512
GQA_Q_TILE = 256
MOE_ROW_TILE = 512
MOE_ROWS = TOP_K * N_TOK + N_EXPERTS * MOE_ROW_TILE
SC_CORES = 2
SC_SUBCORES = 16
SC_WORKERS = SC_CORES * SC_SUBCORES
SC_INDEX_BLOCK = 128
SC_GATHER_WINDOW = 32
FF_CHUNK = 1408

_C_CQ = 0
_C_CKV = _C_CQ + MLA_Q_RANK
_C_GQ = _C_CKV + MLA_KV_RANK
_C_GK = _C_GQ + GLA_HEADS * GLA_DK
_C_GV = _C_GK + GLA_HEADS * GLA_DK
_C_GR = _C_GV + GLA_HEADS * GLA_DV
_C_SMALL = _C_GR + GLA_HEADS * GLA_DV
IN_AB_EXT = _C_SMALL + LANES
_S_KPE, _S_KPER, _S_GAF, _S_GAB = 0, MLA_ROPE, 2 * MLA_ROPE, 2 * MLA_ROPE + GLA_GATE_RANK
MLA_HEAD_PAD = LANES
MLA_QK_W = MLA_HEADS * MLA_HEAD_PAD
MLA_VEXT_W = (MLA_HEADS // 2) * 2 * LANES


def _cparams(semantics):
    return pltpu.CompilerParams(dimension_semantics=semantics, vmem_limit_bytes=VMEM_LIMIT_BYTES)


def _const_spec(shape):
    nd = len(shape)
    return pl.BlockSpec(shape, lambda *_: (0,) * nd, pipeline_mode=pl.Buffered(1))


def _log_sigmoid(x):
    return jnp.minimum(x, 0.0) - jnp.log1p(jnp.exp(-jnp.abs(x)))


def _rms(x, g):
    return (x * lax.rsqrt(jnp.mean(x * x, axis=-1, keepdims=True) + EPS)) * g


def _modulate(x, g, shift, scale):
    return _rms(x, g) * (1.0 + scale) + shift


def _dot(a, b):
    return jnp.dot(a, b, preferred_element_type=F32)


def _dot_nt(a, b):
    return lax.dot_general(a, b, (((1,), (1,)), ((), ())), preferred_element_type=F32)


def _dot_tn(a, b):
    return lax.dot_general(a, b, (((0,), (0,)), ((), ())), preferred_element_type=F32)


def _split3(x):
    hi = x.astype(BF16)
    r1 = x - hi.astype(F32)
    mid = r1.astype(BF16)
    lo = (r1 - mid.astype(F32)).astype(BF16)
    return hi, mid, lo


def _lane_tile(x, reps):
    return jnp.concatenate([x] * reps, axis=1)


def _mod_kernel(c_ref, w_ref, b_ref, o_ref):
    c = c_ref[...]
    s = c * jax.nn.sigmoid(c)
    o_ref[...] = _dot(s.astype(BF16), w_ref[...].astype(BF16)) + b_ref[...]


def _modulation(cvec, w_mod, b_mod):
    ncol = N_MOD * D_MODEL
    blk = 1536
    return pl.pallas_call(
        _mod_kernel,
        out_shape=jax.ShapeDtypeStruct((DEPTH, MOD_ROWS, ncol), F32),
        grid=(DEPTH, ncol // blk),
        in_specs=[
            pl.BlockSpec((MOD_ROWS, D_MODEL), lambda l, j: (0, 0)),
            pl.BlockSpec((None, D_MODEL, blk), lambda l, j: (l, 0, j)),
            pl.BlockSpec((None, 1, blk), lambda l, j: (l, 0, j)),
        ],
        out_specs=pl.BlockSpec((None, MOD_ROWS, blk), lambda l, j: (l, 0, j)),
        compiler_params=_cparams(("arbitrary", "arbitrary")),
        name="modulation",
    )(cvec, w_mod, b_mod.reshape(DEPTH, 1, ncol))


def _mod_spec(layer, j, tile):
    tpg = NP_TOK // tile
    return pl.BlockSpec((None, None, None, 1, D_MODEL), lambda i: (layer, i // tpg, j, 0, 0))


def _tok_spec(tile, width):
    return pl.BlockSpec((tile, width), lambda i: (i, 0))


def _rope_row_spec(tile, width):
    npt = NP_TOK // tile
    spt = DEC_SEQ // tile
    return pl.BlockSpec((tile, width), lambda i: (jnp.where(i < npt, 0, 1 + (i - npt) % spt), 0))


def _even_in_kernel(x_ref, g_ref, shift_ref, scale_ref, win_ref, qn_ref, wq_ref, kvn_ref, wkk_ref,
                    wkv_ref, vbias_ref, epl_ref, wgf_ref, bgf_ref, wgb_ref, bgb_ref, lmat_ref,
                    umat_ref, cq_ref, sq_ref, ck_ref, sk_ref,
                    q_out, k_out, v_out, ckv_out, kpe_out, gq_out, gk_out, gv_out, gr_out,
                    bf_out, bb_out):
    h = _modulate(x_ref[...], g_ref[...], shift_ref[...], scale_ref[...])
    z = _dot(h.astype(BF16), win_ref[...])

    cqn = _rms(z[:, _C_CQ:_C_CQ + MLA_Q_RANK], qn_ref[...]).astype(BF16)
    qf = _dot(cqn, wq_ref[...])
    cq_t = _lane_tile(cq_ref[...], MLA_HEADS)
    sq_t = _lane_tile(sq_ref[...], MLA_HEADS)
    q_out[...] = (qf[:, :MLA_QK_W] * cq_t + qf[:, MLA_QK_W:] * sq_t).astype(BF16)

    ckvn = _rms(z[:, _C_CKV:_C_CKV + MLA_KV_RANK], kvn_ref[...])
    ckv_out[...] = ckvn
    small = z[:, _C_SMALL:_C_SMALL + LANES]
    kpe = (small[:, _S_KPE:_S_KPE + MLA_ROPE] * ck_ref[...]
           + small[:, _S_KPER:_S_KPER + MLA_ROPE] * sk_ref[...])
    kpe_out[...] = kpe
    ckvn_b = ckvn.astype(BF16)
    k_out[...] = (_dot(ckvn_b, wkk_ref[...]) + _dot(kpe.astype(BF16), epl_ref[...])).astype(BF16)
    v_out[...] = (_dot(ckvn_b, wkv_ref[...]) + vbias_ref[...]).astype(BF16)

    gq_out[...] = z[:, _C_GQ:_C_GQ + GLA_HEADS * GLA_DK] * (GLA_DK ** -0.5)
    gk_out[...] = z[:, _C_GK:_C_GK + GLA_HEADS * GLA_DK]
    gv_out[...] = z[:, _C_GV:_C_GV + GLA_HEADS * GLA_DV].astype(BF16)
    gr_out[...] = z[:, _C_GR:_C_GR + GLA_HEADS * GLA_DV]

    small_b = small.astype(BF16)
    la_f = _log_sigmoid(_dot(small_b, wgf_ref[...]) + bgf_ref[...]) * (1.0 / GLA_GATE_NORM)
    la_b = _log_sigmoid(_dot(small_b, wgb_ref[...]) + bgb_ref[...]) * (1.0 / GLA_GATE_NORM)
    lmat = lmat_ref[...]
    umat = umat_ref[...]
    for r in range(TOK_TILE // CUMSUM_BLOCK):
        rows = slice(r * CUMSUM_BLOCK, (r + 1) * CUMSUM_BLOCK)
        f_hi, f_mid, f_lo = _split3(la_f[rows])
        bf_out[rows, :] = _dot(lmat, f_hi) + _dot(lmat, f_mid) + _dot(lmat, f_lo)
        b_hi, b_mid, b_lo = _split3(la_b[rows])
        bb_out[rows, :] = _dot(umat, b_hi) + _dot(umat, b_mid) + _dot(umat, b_lo)


def _even_in_proj(x, mod5, layer, g, wts, tabs):
    t = TOK_TILE
    out_widths = [(MLA_QK_W, BF16), (MLA_QK_W, BF16), (MLA_VEXT_W, BF16), (MLA_KV_RANK, F32),
                  (MLA_ROPE, F32), (GLA_HEADS * GLA_DK, F32), (GLA_HEADS * GLA_DK, F32),
                  (GLA_HEADS * GLA_DV, BF16), (GLA_HEADS * GLA_DV, F32),
                  (GLA_HEADS * GLA_DK, F32), (GLA_HEADS * GLA_DK, F32)]
    const_names = ["win", "qn", "wq", "kvn", "wkk", "wkv", "vbias", "epl", "wgf", "bgf", "wgb",
                   "bgb", "lmat", "umat"]
    consts = [wts[n] for n in const_names]
    in_specs = ([_tok_spec(t, D_MODEL), _const_spec((1, D_MODEL)), _mod_spec(layer, 0, t),
                 _mod_spec(layer, 1, t)]
                + [_const_spec(c.shape) for c in consts]
                + [_rope_row_spec(t, LANES), _rope_row_spec(t, LANES),
                   _rope_row_spec(t, MLA_ROPE), _rope_row_spec(t, MLA_ROPE)])
    return pl.pallas_call(
        _even_in_kernel,
        out_shape=[jax.ShapeDtypeStruct((N_TOK, w), dt) for w, dt in out_widths],
        grid=(N_TOK // t,),
        in_specs=in_specs,
        out_specs=[_tok_spec(t, w) for w, _ in out_widths],
        compiler_params=_cparams(("parallel",)),
        name="even_in_proj",
    )(x, g, mod5, mod5, *consts, tabs["mla_cq"], tabs["mla_sq"], tabs["mla_ck"], tabs["mla_sk"])


def _cache_kv_kernel(ckv_ref, kpe_ref, wkk_ref, wkv_ref, vbias_ref, epl_ref, k_out, v_out):
    ckv_b = ckv_ref[...].astype(BF16)
    k_out[...] = (_dot(ckv_b, wkk_ref[...]) + _dot(kpe_ref[...].astype(BF16), epl_ref[...])).astype(BF16)
    v_out[...] = (_dot(ckv_b, wkv_ref[...]) + vbias_ref[...]).astype(BF16)


def _cache_kv(ckv, kpe, wts):
    n = ckv.shape[0]
    consts = [wts[k] for k in ("wkk", "wkv", "vbias", "epl")]
    return pl.pallas_call(
        _cache_kv_kernel,
        out_shape=[jax.ShapeDtypeStruct((n, MLA_QK_W), BF16), jax.ShapeDtypeStruct((n, MLA_VEXT_W), BF16)],
        grid=(1,),
        in_specs=[_const_spec(ckv.shape), _const_spec(kpe.shape)] + [_const_spec(c.shape) for c in consts],
        out_specs=[_const_spec((n, MLA_QK_W)), _const_spec((n, MLA_VEXT_W))],
        compiler_params=_cparams(("arbitrary",)),
        name="mla_cache_kv",
    )(ckv, kpe, *consts)


def _mla_attn_kernel(*refs, n_seg):
    q_ref = refs[0]
    k_refs = refs[1:1 + n_seg]
    v_refs = refs[1 + n_seg:1 + 2 * n_seg]
    o_ref = refs[1 + 2 * n_seg]
    scale = (MLA_NOPE + MLA_ROPE) ** -0.5
    c = scale * float(np.log2(np.e))
    lane = lax.broadcasted_iota(jnp.int32, (q_ref.shape[0], LANES), 1)
    res = []
    for j in range(2):
        qj = q_ref[:, j * LANES:(j + 1) * LANES]
        s = [_dot_nt(qj, k[:, j * LANES:(j + 1) * LANES]) for k in k_refs]
        m = s[0].max(axis=-1, keepdims=True)
        for sj in s[1:]:
            m = jnp.maximum(m, sj.max(axis=-1, keepdims=True))
        r = None
        for sj, v in zip(s, v_refs):
            p = jnp.exp2((sj - m) * c).astype(BF16)
            rj = _dot(p, v[...])
            r = rj if r is None else r + rj
        res.append(r[:, :LANES] / r[:, LANES:])
    o_ref[...] = jnp.where(lane < MLA_V, res[0], res[1]).astype(BF16)


def _mla_attention(q, ks, vs, *, n_batch, seq_q, q_tile, tok_off, k_batch_rows):
    n_seg = len(ks)
    nq = seq_q // q_tile
    qoff = tok_off // q_tile
    grid = (n_batch, MLA_HEADS // 2, nq)
    in_specs = [pl.BlockSpec((q_tile, 2 * LANES), lambda b, hp, i: (qoff + b * nq + i, hp))]
    for s in range(n_seg):
        rows = k_batch_rows[s]
        off = (tok_off // rows) if s == 0 else 0
        in_specs.append(pl.BlockSpec((rows, 2 * LANES), functools.partial(lambda b, hp, i, off: (off + b, hp), off=off)))
    for s in range(n_seg):
        rows = k_batch_rows[s]
        off = (tok_off // rows) if s == 0 else 0
        in_specs.append(pl.BlockSpec((rows, 2 * LANES), functools.partial(lambda b, hp, i, off: (off + b, hp), off=off)))
    return pl.pallas_call(
        functools.partial(_mla_attn_kernel, n_seg=n_seg),
        out_shape=jax.ShapeDtypeStruct((n_batch * seq_q, MLA_HEADS * MLA_V), BF16),
        grid=grid,
        in_specs=in_specs,
        out_specs=pl.BlockSpec((q_tile, LANES), lambda b, hp, i: (b * nq + i, hp)),
        compiler_params=_cparams(("parallel", "parallel", "arbitrary")),
        name=f"mla_attention_{n_seg}seg",
    )(q, *ks, *vs)


def _gla_kernel(q_ref, k_ref, v_ref, gr_ref, bf_ref, bb_ref, s0f_ref, s0b_ref, gn_ref,
                o_ref, sf_ref, sb_ref, acc_ref, *, n_chunks):
    c = GLA_CHUNK
    lane = lax.broadcasted_iota(jnp.int32, (c, LANES), 1)
    lo = lane < GLA_DK
    row = lax.broadcasted_iota(jnp.int32, (c, c), 0)
    col = lax.broadcasted_iota(jnp.int32, (c, c), 1)
    tril = row >= col
    triu = row <= col
    zero_blk = jnp.zeros((GLA_DK, GLA_DV), F32)

    def pair_state_t(s_ref):
        blockdiag = jnp.concatenate(
            [jnp.concatenate([s_ref[0], zero_blk], axis=1),
             jnp.concatenate([zero_blk, s_ref[1]], axis=1)], axis=0)
        return blockdiag.T

    def chunk(ci, b_ref, st, mid_row, last_row, causal):
        rows = pl.ds(pl.multiple_of(ci * c, c), c)
        q = q_ref[rows, :]
        k = k_ref[rows, :]
        v = v_ref[rows, :]
        b = b_ref[rows, :]
        b_mid = b[mid_row:mid_row + 1, :]
        b_last = b[last_row:last_row + 1, :]
        qe = q * jnp.exp(b - b_mid)
        ke = (k * jnp.exp(b_mid - b)).astype(BF16)
        kd = (k * jnp.exp(b_last - b)).astype(BF16)
        qd = q * jnp.exp(b)
        st_b = st.astype(BF16)
        for j in range(2):
            sel = lo if j == 0 else jnp.logical_not(lo)
            a = _dot_nt(jnp.where(sel, qe, 0.0).astype(BF16), ke)
            a = jnp.where(causal, a, 0.0).astype(BF16)
            o = (_dot(a, v[:, j * GLA_DV:(j + 1) * GLA_DV])
                 + _dot_nt(jnp.where(sel, qd, 0.0).astype(BF16), st_b[j * GLA_DV:(j + 1) * GLA_DV, :]))
            acc_ref[rows, j * GLA_DV:(j + 1) * GLA_DV] += o
        return st * jnp.exp(b_last) + _dot_tn(v, kd)

    acc_ref[...] = jnp.zeros_like(acc_ref)

    def body(i, carry):
        st_f, st_b = carry
        st_f = chunk(i, bf_ref, st_f, c // 2 - 1, c - 1, tril)
        st_b = chunk(n_chunks - 1 - i, bb_ref, st_b, c // 2, 0, triu)
        return st_f, st_b

    st_f, st_b = lax.fori_loop(0, n_chunks, body, (pair_state_t(s0f_ref), pair_state_t(s0b_ref)))
    s_f = st_f.T
    s_b = st_b.T
    sf_ref[0] = s_f[:GLA_DK, :GLA_DV]
    sf_ref[1] = s_f[GLA_DK:, GLA_DV:]
    sb_ref[0] = s_b[:GLA_DK, :GLA_DV]
    sb_ref[1] = s_b[GLA_DK:, GLA_DV:]

    gn = gn_ref[...]
    for j in range(2):
        cols = slice(j * GLA_DV, (j + 1) * GLA_DV)
        gr = gr_ref[:, cols]
        o_ref[:, cols] = (_rms(acc_ref[:, cols], gn) * (gr * jax.nn.sigmoid(gr))).astype(BF16)


def _gla(gq, gk, gv, gr, bf, bb, s0f, s0b, gnorm, *, n_batch, seq, tok_off):
    n_chunks = seq // GLA_CHUNK
    boff = tok_off // seq
    hp = GLA_HEADS // 2
    tok = lambda w: pl.BlockSpec((seq, w), lambda b, p: (boff + b, p))
    st = pl.BlockSpec((None, 2, GLA_DK, GLA_DV), lambda b, p: (b, p, 0, 0))
    return pl.pallas_call(
        functools.partial(_gla_kernel, n_chunks=n_chunks),
        out_shape=[jax.ShapeDtypeStruct((n_batch * seq, GLA_HEADS * GLA_DV), BF16),
                   jax.ShapeDtypeStruct((n_batch, GLA_HEADS, GLA_DK, GLA_DV), F32),
                   jax.ShapeDtypeStruct((n_batch, GLA_HEADS, GLA_DK, GLA_DV), F32)],
        grid=(n_batch, hp),
        in_specs=[tok(2 * GLA_DK), tok(2 * GLA_DK), tok(2 * GLA_DV), tok(2 * GLA_DV),
                  tok(2 * GLA_DK), tok(2 * GLA_DK), st, st, _const_spec((1, GLA_DV))],
        out_specs=[pl.BlockSpec((seq, 2 * GLA_DV), lambda b, p: (b, p)), st, st],
        scratch_shapes=[pltpu.VMEM((seq, 2 * GLA_DV), F32)],
        compiler_params=_cparams(("parallel", "parallel")),
        name=f"gla_seq{seq}",
    )(gq, gk, gv, gr, bf, bb, s0f, s0b, gnorm)


def _out_proj_kernel(*refs, n_in):
    o_refs = refs[:n_in]
    (w_ref, x_ref, g1_ref, gate_ref, g2_ref, shift_ref, scale_ref, x_out, h_out) = refs[n_in:]
    o = o_refs[0][...] if n_in == 1 else jnp.concatenate([r[...] for r in o_refs], axis=1)
    y = _dot(o, w_ref[...])
    x = x_ref[...] + gate_ref[...] * _rms(y, g1_ref[...])
    x_out[...] = x
    h_out[...] = _modulate(x, g2_ref[...], shift_ref[...], scale_ref[...]).astype(h_out.dtype)


def _out_proj(os_, w, x, mod5, layer, g1, g2, h_dtype):
    t = TOK_TILE
    n_in = len(os_)
    in_specs = ([_tok_spec(t, o.shape[1]) for o in os_]
                + [_const_spec(w.shape), _tok_spec(t, D_MODEL), _const_spec((1, D_MODEL)),
                   _mod_spec(layer, 2, t), _const_spec((1, D_MODEL)), _mod_spec(layer, 3, t),
                   _mod_spec(layer, 4, t)])
    return pl.pallas_call(
        functools.partial(_out_proj_kernel, n_in=n_in),
        out_shape=[jax.ShapeDtypeStruct((N_TOK, D_MODEL), F32), jax.ShapeDtypeStruct((N_TOK, D_MODEL), h_dtype)],
        grid=(N_TOK // t,),
        in_specs=in_specs,
        out_specs=[_tok_spec(t, D_MODEL), _tok_spec(t, D_MODEL)],
        compiler_params=_cparams(("parallel",)),
        name=f"out_proj_{n_in}",
    )(*os_, w, x, g1, mod5, g2, mod5, mod5)


def _ffn_kernel(h_ref, x_ref, wg_ref, wu_ref, wd_ref, g_ref, gate_ref, x_out):
    h = h_ref[...]
    f = None
    for cidx in range(D_FF // FF_CHUNK):
        cols = slice(cidx * FF_CHUNK, (cidx + 1) * FF_CHUNK)
        a = _dot(h, wg_ref[:, cols])
        u = _dot(h, wu_ref[:, cols])
        fc = _dot(((a * jax.nn.sigmoid(a)) * u).astype(BF16), wd_ref[cols, :])
        f = fc if f is None else f + fc
    x_out[...] = x_ref[...] + gate_ref[...] * _rms(f, g_ref[...])


def _ffn(h, x, wg, wu, wd, mod5, layer, g3):
    t = TOK_TILE
    return pl.pallas_call(
        _ffn_kernel,
        out_shape=jax.ShapeDtypeStruct((N_TOK, D_MODEL), F32),
        grid=(N_TOK // t,),
        in_specs=[_tok_spec(t, D_MODEL), _tok_spec(t, D_MODEL), _const_spec(wg.shape),
                  _const_spec(wu.shape), _const_spec(wd.shape), _const_spec((1, D_MODEL)),
                  _mod_spec(layer, 5, t)],
        out_specs=_tok_spec(t, D_MODEL),
        compiler_params=_cparams(("parallel",)),
        name="ffn_swiglu",
    )(h, x, wg, wu, wd, g3, mod5)


GQA_Q_W = GQA_HEADS * GQA_HEAD_DIM
GQA_KV_W = GQA_KV_HEADS * GQA_HEAD_DIM
GQA_VEXT_W = (GQA_KV_HEADS // 2) * 2 * LANES
_O_Q, _O_QR = 0, GQA_Q_W
_O_K, _O_KR = 2 * GQA_Q_W, 2 * GQA_Q_W + GQA_KV_W
_O_V = 2 * GQA_Q_W + 2 * GQA_KV_W
IN_C_EXT = _O_V + GQA_VEXT_W


def _odd_in_kernel(x_ref, g_ref, shift_ref, scale_ref, win_ref, vbias_ref, c_ref, s_ref,
                   q_out, k_out, kb_out, v_out, vb_out):
    h = _modulate(x_ref[...], g_ref[...], shift_ref[...], scale_ref[...])
    z = _dot(h.astype(BF16), win_ref[...])
    c_t = _lane_tile(c_ref[...], GQA_Q_W // LANES)
    s_t = _lane_tile(s_ref[...], GQA_Q_W // LANES)
    q = z[:, _O_Q:_O_Q + GQA_Q_W] * c_t + z[:, _O_QR:_O_QR + GQA_Q_W] * s_t
    q_out[...] = (q * (GQA_HEAD_DIM ** -0.5)).astype(BF16)
    k = (z[:, _O_K:_O_K + GQA_KV_W] * c_t[:, :GQA_KV_W]
         + z[:, _O_KR:_O_KR + GQA_KV_W] * s_t[:, :GQA_KV_W])
    k_out[...] = k
    kb_out[...] = k.astype(BF16)
    vext = z[:, _O_V:_O_V + GQA_VEXT_W] + vbias_ref[...]
    vb_out[...] = vext.astype(BF16)
    for p in range(GQA_KV_HEADS // 2):
        v_out[:, p * LANES:(p + 1) * LANES] = vext[:, 2 * p * LANES:(2 * p + 1) * LANES]


def _odd_in_proj(x, mod5, layer, g, win, vbias, tab_c, tab_s):
    t = TOK_TILE
    out_widths = [(GQA_Q_W, BF16), (GQA_KV_W, F32), (GQA_KV_W, BF16), (GQA_KV_W, F32), (GQA_VEXT_W, BF16)]
    return pl.pallas_call(
        _odd_in_kernel,
        out_shape=[jax.ShapeDtypeStruct((N_TOK, w), dt) for w, dt in out_widths],
        grid=(N_TOK // t,),
        in_specs=[_tok_spec(t, D_MODEL), _const_spec((1, D_MODEL)), _mod_spec(layer, 0, t),
                  _mod_spec(layer, 1, t), _const_spec(win.shape), _const_spec(vbias.shape),
                  _rope_row_spec(t, LANES), _rope_row_spec(t, LANES)],
        out_specs=[_tok_spec(t, w) for w, _ in out_widths],
        compiler_params=_cparams(("parallel",)),
        name="odd_in_proj",
    )(x, g, mod5, mod5, win, vbias, tab_c, tab_s)


def _gqa_kernel(sink_ref, q_ref, *refs, local_len):
    if local_len:
        kl_ref, vl_ref, kc_ref, vc_ref, o_ref = refs
    else:
        kc_ref, vc_ref, o_ref = refs
    tq = q_ref.shape[0]
    lane = lax.broadcasted_iota(jnp.int32, (tq, LANES), 1)
    lo = lane < GQA_HEAD_DIM
    if local_len:
        i = pl.program_id(1)
        q0 = i * tq
        seq = kl_ref.shape[0]
        kstart = pl.multiple_of(jnp.clip(q0 - WINDOW, 0, seq - local_len), LANES)
        qpos = q0 + lax.broadcasted_iota(jnp.int32, (tq, local_len), 0)
        kpos = kstart + lax.broadcasted_iota(jnp.int32, (tq, local_len), 1)
        band = jnp.abs(qpos - kpos) <= WINDOW
    for p in range(GQA_KV_HEADS // 2):
        kc = kc_ref[:, p * LANES:(p + 1) * LANES]
        vc = vc_ref[:, 2 * p * LANES:(2 * p + 2) * LANES]
        if local_len:
            kl = kl_ref[pl.ds(kstart, local_len), p * LANES:(p + 1) * LANES]
            vl = vl_ref[pl.ds(kstart, local_len), 2 * p * LANES:(2 * p + 2) * LANES]
        for blk in range(GQA_GROUP):
            cols = slice((p * GQA_GROUP + blk) * LANES, (p * GQA_GROUP + blk + 1) * LANES)
            qb = q_ref[:, cols]
            res = []
            for half in range(2):
                head = (2 * p + half) * GQA_GROUP + blk
                sink = sink_ref[head]
                qh = jnp.where(lo if half == 0 else jnp.logical_not(lo), qb, jnp.zeros_like(qb))
                s_c = _dot_nt(qh, kc)
                m = jnp.maximum(s_c.max(axis=-1, keepdims=True), sink)
                if local_len:
                    s_l = jnp.where(band, _dot_nt(qh, kl), NEG_INF)
                    m = jnp.maximum(m, s_l.max(axis=-1, keepdims=True))
                r = _dot(jnp.exp(s_c - m).astype(BF16), vc)
                if local_len:
                    r = r + _dot(jnp.exp(s_l - m).astype(BF16), vl)
                res.append(r[:, :LANES] / (r[:, LANES:] + jnp.exp(sink - m)))
            o_ref[:, cols] = jnp.where(lo, res[0], res[1]).astype(BF16)


def _gqa_attention(sink, q, k_loc, v_loc, k_ctx, v_ctx, *, n_batch, seq_q, q_tile, tok_off, n_ctx, local):
    nq = seq_q // q_tile
    qoff = tok_off // q_tile
    local_len = q_tile + 2 * WINDOW if local else 0
    in_specs = [pl.BlockSpec(memory_space=pltpu.SMEM),
                pl.BlockSpec((q_tile, GQA_Q_W), lambda b, i: (qoff + b * nq + i, 0))]
    args = [sink, q]
    if local:
        boff = tok_off // seq_q
        in_specs += [pl.BlockSpec((seq_q, GQA_KV_W), lambda b, i: (boff + b, 0)),
                     pl.BlockSpec((seq_q, GQA_VEXT_W), lambda b, i: (boff + b, 0))]
        args += [k_loc, v_loc]
    in_specs += [pl.BlockSpec((n_ctx, GQA_KV_W), lambda b, i: (b, 0)),
                 pl.BlockSpec((n_ctx, GQA_VEXT_W), lambda b, i: (b, 0))]
    args += [k_ctx, v_ctx]
    return pl.pallas_call(
        functools.partial(_gqa_kernel, local_len=local_len),
        out_shape=jax.ShapeDtypeStruct((n_batch * seq_q, GQA_Q_W), BF16),
        grid=(n_batch, nq),
        in_specs=in_specs,
        out_specs=pl.BlockSpec((q_tile, GQA_Q_W), lambda b, i: (b * nq + i, 0)),
        compiler_params=_cparams(("parallel", "arbitrary")),
        name="gqa_local" if local else "gqa_ctx",
    )(*args)


def _router_kernel(h_ref, w_ref, b_ref, ltri_ref, wsel_out, isel_out, rank_out, cnt_out, carry_ref):
    @pl.when(pl.program_id(0) == 0)
    def _():
        carry_ref[...] = jnp.zeros_like(carry_ref)

    logits = _dot(h_ref[...].astype(BF16), w_ref[...]) + b_ref[...]
    lane = lax.broadcasted_iota(jnp.int32, logits.shape, 1)
    neg = float(np.finfo(np.float32).min)
    lg = jnp.where(lane < N_EXPERTS, logits, neg)
    v1 = lg.max(axis=-1, keepdims=True)
    i1 = jnp.min(jnp.where(lg == v1, lane, LANES), axis=-1, keepdims=True)
    lg2 = jnp.where(lane == i1, neg, lg)
    v2 = lg2.max(axis=-1, keepdims=True)
    i2 = jnp.min(jnp.where(lg2 == v2, lane, LANES), axis=-1, keepdims=True)
    e2 = jnp.exp(v2 - v1)
    den = 1.0 + e2
    wsel_out[...] = jnp.where(lane == 0, 1.0 / den, jnp.where(lane == 1, e2 / den, 0.0))
    isel_out[...] = jnp.where(lane == 0, i1, jnp.where(lane == 1, i2, 0))
    hit = jnp.where(lane == i1, 1.0, jnp.where(lane == i2, 1.0, 0.0))
    carry = carry_ref[...]
    rank_out[...] = (_dot(ltri_ref[...], hit.astype(BF16)) + carry[0:1, :]).astype(jnp.int32)
    carry = carry + jnp.sum(hit, axis=0, keepdims=True)
    carry_ref[...] = carry
    cnt_out[...] = carry.astype(jnp.int32)


def _router(h, w, b):
    t = TOK_TILE
    r = np.arange(t)
    ltri = jnp.asarray(r[:, None] > r[None, :], BF16)
    return pl.pallas_call(
        _router_kernel,
        out_shape=[jax.ShapeDtypeStruct((N_TOK, LANES), F32), jax.ShapeDtypeStruct((N_TOK, LANES), jnp.int32),
                   jax.ShapeDtypeStruct((N_TOK, LANES), jnp.int32), jax.ShapeDtypeStruct((8, LANES), jnp.int32)],
        grid=(N_TOK // t,),
        in_specs=[_tok_spec(t, D_MODEL), _const_spec(w.shape), _const_spec(b.shape), _const_spec((t, t))],
        out_specs=[_tok_spec(t, LANES), _tok_spec(t, LANES), _tok_spec(t, LANES),
                   pl.BlockSpec((8, LANES), lambda i: (0, 0))],
        scratch_shapes=[pltpu.VMEM((8, LANES), F32)],
        compiler_params=_cparams(("arbitrary",)),
        name="moe_router",
    )(h, w, b, ltri)


def _route_tables(isel, rank, cnt):
    tm = MOE_ROW_TILE
    counts = cnt[0, :N_EXPERTS]
    padded = ((counts + tm - 1) // tm) * tm
    ends = jnp.cumsum(padded)
    base = ends - padded
    e_ids = jnp.arange(N_EXPERTS, dtype=jnp.int32)
    row = rank[:, :N_EXPERTS] + base[None, :]
    pos1 = jnp.sum(jnp.where(e_ids[None, :] == isel[:, 0:1], row, 0), axis=1)
    pos2 = jnp.sum(jnp.where(e_ids[None, :] == isel[:, 1:2], row, 0), axis=1)
    pos = jnp.concatenate([pos1, pos2])
    tok = jnp.arange(N_TOK, dtype=jnp.int32)
    src = jnp.zeros((MOE_ROWS,), jnp.int32).at[pos].set(jnp.concatenate([tok, tok]))
    tile_start = jnp.arange(MOE_ROWS // tm, dtype=jnp.int32) * tm
    tile_expert = jnp.minimum(jnp.sum(tile_start[:, None] >= ends[None, :], axis=1), N_EXPERTS - 1).astype(jnp.int32)
    n_used = (ends[-1] // tm).astype(jnp.int32).reshape(1)
    return pos, src, tile_expert, n_used


def _gather_rows(table, idx):
    n_idx = idx.shape[0]
    d = table.shape[1]
    per_w = n_idx // SC_WORKERS
    assert per_w * SC_WORKERS == n_idx and per_w % SC_INDEX_BLOCK == 0
    mesh = plsc.VectorSubcoreMesh(core_axis_name="core", subcore_axis_name="subcore")

    @functools.partial(
        pl.kernel, out_type=jax.ShapeDtypeStruct((n_idx, d), table.dtype), mesh=mesh,
        scratch_types=[pltpu.VMEM((SC_INDEX_BLOCK,), jnp.int32), pltpu.VMEM((SC_GATHER_WINDOW, d), table.dtype)],
        name="sc_gather_rows")
    def gather(x_hbm, i_hbm, o_hbm, idx_v, rows_v):
        wid = lax.axis_index("subcore") * SC_CORES + lax.axis_index("core")
        base = wid * per_w

        @pl.loop(0, per_w // SC_INDEX_BLOCK)
        def _(g):
            off = base + g * SC_INDEX_BLOCK
            pltpu.sync_copy(i_hbm.at[pl.ds(off, SC_INDEX_BLOCK)], idx_v)
            for s in range(SC_INDEX_BLOCK // SC_GATHER_WINDOW):
                sub = pl.ds(s * SC_GATHER_WINDOW, SC_GATHER_WINDOW)
                pltpu.sync_copy(x_hbm.at[idx_v.at[sub]], rows_v)
                pltpu.sync_copy(rows_v, o_hbm.at[pl.ds(off + s * SC_GATHER_WINDOW, SC_GATHER_WINDOW)])

    return gather(table, idx)


def _expert_ffn_kernel(te_ref, nu_ref, x_ref, wg_ref, wu_ref, wd_ref, y_out):
    @pl.when(pl.program_id(0) < nu_ref[0])
    def _():
        h = x_ref[...].astype(BF16)
        f = None
        for cidx in range(D_FF // FF_CHUNK):
            cols = slice(cidx * FF_CHUNK, (cidx + 1) * FF_CHUNK)
            a = _dot(h, wg_ref[:, cols])
            u = _dot(h, wu_ref[:, cols])
            fc = _dot(((a * jax.nn.sigmoid(a)) * u).astype(BF16), wd_ref[cols, :])
            f = fc if f is None else f + fc
        y_out[...] = f

    @pl.when(pl.program_id(0) >= nu_ref[0])
    def _():
        y_out[...] = jnp.zeros_like(y_out)


def _expert_ffn(xs, tile_expert, n_used, wg, wu, wd):
    tm = MOE_ROW_TILE
    wspec = lambda shape: pl.BlockSpec((None,) + shape, lambda j, te, nu: (te[j], 0, 0),
                                       pipeline_mode=pl.Buffered(1))
    return pl.pallas_call(
        _expert_ffn_kernel,
        out_shape=jax.ShapeDtypeStruct((MOE_ROWS, D_MODEL), F32),
        grid_spec=pltpu.PrefetchScalarGridSpec(
            num_scalar_prefetch=2,
            grid=(MOE_ROWS // tm,),
            in_specs=[pl.BlockSpec((tm, D_MODEL), lambda j, te, nu: (j, 0)),
                      wspec((D_MODEL, D_FF)), wspec((D_MODEL, D_FF)), wspec((D_FF, D_MODEL))],
            out_specs=pl.BlockSpec((tm, D_MODEL), lambda j, te, nu: (j, 0)),
        ),
        compiler_params=_cparams(("arbitrary",)),
        name="moe_expert_ffn",
    )(tile_expert, n_used, xs, wg, wu, wd)


def _moe_combine_kernel(y1_ref, y2_ref, wsel_ref, x_ref, g_ref, gate_ref, x_out):
    w = wsel_ref[...]
    f = w[:, 0:1] * y1_ref[...] + w[:, 1:2] * y2_ref[...]
    x_out[...] = x_ref[...] + gate_ref[...] * _rms(f, g_ref[...])


def _moe_combine(yg, wsel, x, mod5, layer, g3):
    t = TOK_TILE
    nt = N_TOK // t
    return pl.pallas_call(
        _moe_combine_kernel,
        out_shape=jax.ShapeDtypeStruct((N_TOK, D_MODEL), F32),
        grid=(nt,),
        in_specs=[_tok_spec(t, D_MODEL), pl.BlockSpec((t, D_MODEL), lambda i: (nt + i, 0)),
                  _tok_spec(t, LANES), _tok_spec(t, D_MODEL), _const_spec((1, D_MODEL)), _mod_spec(layer, 5, t)],
        out_specs=_tok_spec(t, D_MODEL),
        compiler_params=_cparams(("parallel",)),
        name="moe_combine",
    )(yg, yg, wsel, x, g3, mod5)


def _moe(h, x, w_router, b_router, wg, wu, wd, mod5, layer, g3):
    wsel, isel, rank, cnt = _router(h, w_router, b_router)
    pos, src, tile_expert, n_used = _route_tables(isel, rank, cnt)
    xs = _gather_rows(h, src)
    ys = _expert_ffn(xs, tile_expert, n_used, wg, wu, wd)
    yg = _gather_rows(ys, pos)
    return _moe_combine(yg, wsel, x, mod5, layer, g3)


def _rot_cols(w, half):
    k, n = w.shape
    wb = w.reshape(k, n // (2 * half), 2, half)
    return jnp.stack([-wb[:, :, 1], wb[:, :, 0]], axis=2).reshape(k, n)


def _axis_tables(r, pos):
    inv = ROPE_BASE ** (-jnp.arange(0, r, 2, dtype=F32) / r)
    ang = pos.astype(F32)[:, None] * inv[None, :]
    cos, sin = jnp.cos(ang), jnp.sin(ang)
    return jnp.concatenate([cos, cos], axis=1), jnp.concatenate([sin, sin], axis=1)


def _rope_tables(r):
    s = jnp.arange(DEC_SEQ)
    cr, sr = _axis_tables(r // 2, s // GRID_W)
    cc, sc = _axis_tables(r // 2, s % GRID_W)
    return jnp.concatenate([cr, cc], axis=1), jnp.concatenate([sr, sc], axis=1)


def _with_identity(tab, ident):
    return jnp.concatenate([jnp.full((TOK_TILE, tab.shape[1]), ident, F32), tab], axis=0)


def _prep_tables():
    c32, s32 = _rope_tables(MLA_ROPE)
    ones = jnp.ones((DEC_SEQ, MLA_NOPE), F32)
    pad1 = jnp.ones((DEC_SEQ, MLA_HEAD_PAD - MLA_NOPE - MLA_ROPE), F32)
    cq = jnp.concatenate([ones, c32, pad1], axis=1)
    sq = jnp.concatenate([0 * ones, s32, 0 * pad1], axis=1)
    c64, s64 = _rope_tables(GQA_HEAD_DIM)
    return {
        "mla_cq": _with_identity(cq, 1.0), "mla_sq": _with_identity(sq, 0.0),
        "mla_ck": _with_identity(c32, 1.0), "mla_sk": _with_identity(s32, 0.0),
        "gqa_c": _with_identity(jnp.concatenate([c64, c64], axis=1), 1.0),
        "gqa_s": _with_identity(jnp.concatenate([s64, s64], axis=1), 0.0),
    }


def _prep_even(w_in, q_norm, w_q_up, kv_norm, w_kv_up, wgf, bgf, wgb, bgb):
    sizes = [MLA_Q_RANK, MLA_KV_RANK, MLA_ROPE, GLA_HEADS * GLA_DK, GLA_HEADS * GLA_DK,
             GLA_HEADS * GLA_DV, GLA_HEADS * GLA_DV, GLA_GATE_RANK, GLA_GATE_RANK]
    cq, ckv, kpe, gq, gk, gv, gr, gaf, gab = jnp.split(w_in, [int(s) for s in np.cumsum(sizes)[:-1]], axis=1)
    pad = jnp.zeros((D_MODEL, LANES - 2 * MLA_ROPE - 2 * GLA_GATE_RANK), F32)
    win = jnp.concatenate([cq, ckv, gq, gk, gv, gr, kpe, _rot_cols(kpe, MLA_ROPE // 4), gaf, gab, pad], axis=1)

    wq = w_q_up.reshape(MLA_Q_RANK, MLA_HEADS, MLA_NOPE + MLA_ROPE)
    nope, pe = wq[..., :MLA_NOPE], wq[..., MLA_NOPE:]
    pe_rot = _rot_cols(pe.reshape(MLA_Q_RANK, MLA_HEADS * MLA_ROPE), MLA_ROPE // 4).reshape(pe.shape)
    zpad = jnp.zeros((MLA_Q_RANK, MLA_HEADS, MLA_HEAD_PAD - MLA_NOPE - MLA_ROPE), F32)
    wq_main = jnp.concatenate([nope, pe, zpad], axis=-1).reshape(MLA_Q_RANK, MLA_QK_W)
    wq_rot = jnp.concatenate([0 * nope, pe_rot, zpad], axis=-1).reshape(MLA_Q_RANK, MLA_QK_W)

    wkv = w_kv_up.reshape(MLA_KV_RANK, MLA_HEADS, MLA_NOPE + MLA_V)
    knope, vv = wkv[..., :MLA_NOPE], wkv[..., MLA_NOPE:]
    wkk = jnp.concatenate([knope, jnp.zeros((MLA_KV_RANK, MLA_HEADS, MLA_HEAD_PAD - MLA_NOPE), F32)],
                          axis=-1).reshape(MLA_KV_RANK, MLA_QK_W)
    vpair = vv.reshape(MLA_KV_RANK, MLA_HEADS // 2, 2 * MLA_V)
    wkv_ext = jnp.concatenate([vpair, jnp.zeros((MLA_KV_RANK, MLA_HEADS // 2, LANES), F32)],
                              axis=-1).reshape(MLA_KV_RANK, MLA_VEXT_W)
    vbias = jnp.tile(jnp.concatenate([jnp.zeros((LANES,), F32), jnp.ones((LANES,), F32)]),
                     MLA_HEADS // 2).reshape(1, MLA_VEXT_W)
    epl = jnp.tile(jnp.concatenate([jnp.zeros((MLA_ROPE, MLA_NOPE), F32), jnp.eye(MLA_ROPE, dtype=F32),
                                    jnp.zeros((MLA_ROPE, MLA_HEAD_PAD - MLA_NOPE - MLA_ROPE), F32)], axis=1),
                   (1, MLA_HEADS))

    def gate_w(w, off):
        return jnp.zeros((LANES, GLA_HEADS * GLA_DK), F32).at[off:off + GLA_GATE_RANK].set(w)

    r = np.arange(CUMSUM_BLOCK)
    same = (r[:, None] // GLA_CHUNK) == (r[None, :] // GLA_CHUNK)
    lmat = jnp.asarray(same & (r[:, None] >= r[None, :]), BF16)
    umat = jnp.asarray(same & (r[:, None] <= r[None, :]), BF16)
    return {
        "win": win.astype(BF16), "qn": q_norm.reshape(1, -1), "wq": jnp.concatenate([wq_main, wq_rot], axis=1).astype(BF16),
        "kvn": kv_norm.reshape(1, -1), "wkk": wkk.astype(BF16), "wkv": wkv_ext.astype(BF16), "vbias": vbias,
        "epl": epl.astype(BF16), "wgf": gate_w(wgf, _S_GAF).astype(BF16), "bgf": bgf.reshape(1, -1),
        "wgb": gate_w(wgb, _S_GAB).astype(BF16), "bgb": bgb.reshape(1, -1), "lmat": lmat, "umat": umat,
    }


def _gqa_head_perm():
    heads = []
    for p in range(GQA_KV_HEADS // 2):
        for i in range(GQA_GROUP):
            heads += [(2 * p) * GQA_GROUP + i, (2 * p + 1) * GQA_GROUP + i]
    return np.asarray(heads)


def _prep_odd(w_in, w_out):
    perm = _gqa_head_perm()
    wq = w_in[:, :GQA_Q_W].reshape(D_MODEL, GQA_HEADS, GQA_HEAD_DIM)[:, perm].reshape(D_MODEL, GQA_Q_W)
    wk = w_in[:, GQA_Q_W:GQA_Q_W + GQA_KV_W]
    wv = w_in[:, GQA_Q_W + GQA_KV_W:].reshape(D_MODEL, GQA_KV_HEADS // 2, 2 * GQA_HEAD_DIM)
    wv_ext = jnp.concatenate([wv, jnp.zeros((D_MODEL, GQA_KV_HEADS // 2, LANES), F32)], axis=-1).reshape(D_MODEL, GQA_VEXT_W)
    win = jnp.concatenate([wq, _rot_cols(wq, GQA_HEAD_DIM // 4), wk, _rot_cols(wk, GQA_HEAD_DIM // 4), wv_ext], axis=1)
    vbias = jnp.tile(jnp.concatenate([jnp.zeros((LANES,), F32), jnp.ones((LANES,), F32)]),
                     GQA_KV_HEADS // 2).reshape(1, GQA_VEXT_W)
    wo = w_out.reshape(GQA_HEADS, GQA_HEAD_DIM, D_MODEL)[perm].reshape(GQA_Q_W, D_MODEL)
    return win.astype(BF16), vbias, wo.astype(BF16)


def _ext_v(v):
    rows = v.shape[0]
    vp = v.reshape(rows, GQA_KV_HEADS // 2, 2 * GQA_HEAD_DIM)
    return jnp.concatenate([vp, jnp.ones((rows, GQA_KV_HEADS // 2, LANES), v.dtype)], axis=-1).reshape(rows, GQA_VEXT_W)


def kernel(x_prompt, x_sample, cache_mla_ckv, cache_mla_kpe, state_gla_fwd, state_gla_bwd, cache_gqa_k, cache_gqa_v, c, c_ctx, w_mod, b_mod, norm_g, w_in_ab, mla_q_norm, mla_w_q_up, mla_kv_norm, mla_w_kv_up, gla_w_gate_f, gla_b_gate_f, gla_w_gate_b, gla_b_gate_b, gla_norm, w_out_ab, ffn_w_gate, ffn_w_up, ffn_w_down, w_in_c, gqa_sink, w_out_c, moe_w_router, moe_b_router, moe_w_gate, moe_w_up, moe_w_down):
    x = jnp.concatenate([x_prompt.reshape(NP_TOK, D_MODEL), x_sample.reshape(NS_TOK, D_MODEL)], axis=0)
    cvec = jnp.concatenate([c_ctx[None, :], c, jnp.zeros((MOD_ROWS - N_GROUPS, D_MODEL), F32)], axis=0)
    mod5 = _modulation(cvec, w_mod, b_mod).reshape(DEPTH, MOD_ROWS, N_MOD, 1, D_MODEL)
    tabs = _prep_tables()
    gvec = lambda l, j: norm_g[l, j].reshape(1, D_MODEL)

    wts = _prep_even(w_in_ab[0], mla_q_norm[0], mla_w_q_up[0], mla_kv_norm[0], mla_w_kv_up[0],
                     gla_w_gate_f[0], gla_b_gate_f[0], gla_w_gate_b[0], gla_b_gate_b[0])
    (q, k, v, ckv, kpe, gq, gk, gv, gr, bf, bb) = _even_in_proj(x, mod5, 0, gvec(0, 0), wts, tabs)
    kc, vc = _cache_kv(cache_mla_ckv[:, 0].reshape(DEC_BATCH * PAST_LEN, MLA_KV_RANK),
                       cache_mla_kpe[:, 0].reshape(DEC_BATCH * PAST_LEN, MLA_ROPE), wts)
    oa_p = _mla_attention(q, [k], [v], n_batch=BATCH, seq_q=SEQ, q_tile=SEQ, tok_off=0, k_batch_rows=[SEQ])
    oa_s = _mla_attention(q, [k, kc], [v, vc], n_batch=DEC_BATCH, seq_q=DEC_SEQ, q_tile=MLA_Q_TILE,
                          tok_off=NP_TOK, k_batch_rows=[DEC_SEQ, PAST_LEN])
    gn = gla_norm[0].reshape(1, GLA_DV)
    zero_state = jnp.zeros((BATCH, GLA_HEADS, GLA_DK, GLA_DV), F32)
    ob_p, sf, sb = _gla(gq, gk, gv, gr, bf, bb, zero_state, zero_state, gn, n_batch=BATCH, seq=SEQ, tok_off=0)
    ob_s, _, _ = _gla(gq, gk, gv, gr, bf, bb, state_gla_fwd[:, 0], state_gla_bwd[:, 0], gn,
                      n_batch=DEC_BATCH, seq=DEC_SEQ, tok_off=NP_TOK)
    oa = jnp.concatenate([oa_p, oa_s], axis=0)
    ob = jnp.concatenate([ob_p, ob_s], axis=0)
    x, h = _out_proj([oa, ob], w_out_ab[0].astype(BF16), x, mod5, 0, gvec(0, 1), gvec(0, 2), BF16)
    x = _ffn(h, x, ffn_w_gate[0].astype(BF16), ffn_w_up[0].astype(BF16), ffn_w_down[0].astype(BF16),
             mod5, 0, gvec(0, 3))

    win_c, vbias_c, wo_c = _prep_odd(w_in_c[0], w_out_c[0])
    qg, kg, kgb, vg, vgb = _odd_in_proj(x, mod5, 1, gvec(1, 0), win_c, vbias_c, tabs["gqa_c"], tabs["gqa_s"])
    sink = gqa_sink[0]
    og_p = _gqa_attention(sink, qg, None, None, kgb, vgb, n_batch=BATCH, seq_q=SEQ, q_tile=SEQ, tok_off=0, n_ctx=SEQ, local=False)
    kc_g = cache_gqa_k[:, 0].reshape(DEC_BATCH * PAST_LEN, GQA_KV_W).astype(BF16)
    vc_g = _ext_v(cache_gqa_v[:, 0].reshape(DEC_BATCH * PAST_LEN, GQA_KV_W)).astype(BF16)
    og_s = _gqa_attention(sink, qg, kgb, vgb, kc_g, vc_g, n_batch=DEC_BATCH, seq_q=DEC_SEQ, q_tile=GQA_Q_TILE,
                          tok_off=NP_TOK, n_ctx=PAST_LEN, local=True)
    og = jnp.concatenate([og_p, og_s], axis=0)
    x, h = _out_proj([og], wo_c, x, mod5, 1, gvec(1, 1), gvec(1, 2), F32)
    w_r = jnp.zeros((D_MODEL, LANES), F32).at[:, :N_EXPERTS].set(moe_w_router[0]).astype(BF16)
    b_r = jnp.zeros((1, LANES), F32).at[0, :N_EXPERTS].set(moe_b_router[0])
    x = _moe(h, x, w_r, b_r, moe_w_gate[0].astype(BF16), moe_w_up[0].astype(BF16), moe_w_down[0].astype(BF16),
             mod5, 1, gvec(1, 3))

    y_prompt = x[:NP_TOK].reshape(BATCH, SEQ, D_MODEL)
    y_sample = x[NP_TOK:].reshape(DEC_BATCH, DEC_SEQ, D_MODEL)
    new_ckv = ckv[:NP_TOK].reshape(BATCH, 1, SEQ, MLA_KV_RANK)
    new_kpe = kpe[:NP_TOK].reshape(BATCH, 1, SEQ, MLA_ROPE)
    new_k = kg[:NP_TOK].reshape(BATCH, 1, SEQ, GQA_KV_HEADS, GQA_HEAD_DIM)
    new_v = vg[:NP_TOK].reshape(BATCH, 1, SEQ, GQA_KV_HEADS, GQA_HEAD_DIM)
    return (y_prompt, y_sample, new_ckv, new_kpe, sf[:, None], sb[:, None], new_k, new_v)
```

```python
import functools

import jax
import jax.numpy as jnp
import numpy as np
from jax import lax
from jax.experimental import pallas as pl
from jax.experimental.pallas import tpu as pltpu
from jax.experimental.pallas import tpu_sc as plsc

F32 = jnp.float32
BF16 = jnp.bfloat16

D_MODEL = 1024
BATCH = 16
SEQ = 256
DEPTH = 2
DEC_BATCH = 4
DEC_SEQ = 4096
PAST_LEN = 256
GRID_W = 64
N_MOD = 6
EPS = 1e-6
ROPE_BASE = 10000.0
NEG_INF = -1e30

MLA_HEADS = 8
MLA_NOPE = 64
MLA_ROPE = 32
MLA_V = 64
MLA_Q_RANK = 384
MLA_KV_RANK = 256
GLA_HEADS = 4
GLA_DK = 64
GLA_DV = 128
GLA_GATE_RANK = 16
GLA_GATE_NORM = 16.0
GLA_CHUNK = 64
GQA_HEADS = 16
GQA_KV_HEADS = 4
GQA_GROUP = GQA_HEADS // GQA_KV_HEADS
GQA_HEAD_DIM = 64
WINDOW = 128
D_FF = 2816
N_EXPERTS = 8
TOP_K = 2

NP_TOK = BATCH * SEQ
NS_TOK = DEC_BATCH * DEC_SEQ
N_TOK = NP_TOK + NS_TOK
N_GROUPS = 1 + DEC_BATCH
MOD_ROWS = 8

LANES = 128
MXU_COLS = 256
VMEM_LIMIT_BYTES = 56 * 1024 * 1024

TOK_TILE = 512
CUMSUM_BLOCK = 256
MLA_Q_TILE = 512
MLA_Q_SUB = 256
GQA_Q_TILE = 256
MOE_ROW_TILE = 512
MOE_ROWS = TOP_K * N_TOK + N_EXPERTS * MOE_ROW_TILE
SC_CORES = 2
SC_SUBCORES = 16
SC_WORKERS = SC_CORES * SC_SUBCORES
SC_INDEX_BLOCK = 128
SC_GATHER_WINDOW = 32
FF_CHUNK = 1408

_C_CQ = 0
_C_CKV = _C_CQ + MLA_Q_RANK
_C_GQ = _C_CKV + MLA_KV_RANK
_C_GK = _C_GQ + GLA_HEADS * GLA_DK
_C_GV = _C_GK + GLA_HEADS * GLA_DK
_C_GR = _C_GV + GLA_HEADS * GLA_DV
_C_SMALL = _C_GR + GLA_HEADS * GLA_DV
IN_AB_EXT = _C_SMALL + LANES
_S_KPE, _S_KPER, _S_GAF, _S_GAB = 0, MLA_ROPE, 2 * MLA_ROPE, 2 * MLA_ROPE + GLA_GATE_RANK
MLA_HEAD_PAD = LANES
MLA_QK_W = MLA_HEADS * MLA_HEAD_PAD
MLA_VEXT_W = (MLA_HEADS // 2) * 2 * LANES


def _cparams(semantics):
    return pltpu.CompilerParams(dimension_semantics=semantics, vmem_limit_bytes=VMEM_LIMIT_BYTES)


def _const_spec(shape):
    nd = len(shape)
    return pl.BlockSpec(shape, lambda *_: (0,) * nd, pipeline_mode=pl.Buffered(1))


def _log_sigmoid(x):
    return jnp.minimum(x, 0.0) - jnp.log1p(jnp.exp(-jnp.abs(x)))


def _rms(x, g):
    return (x * lax.rsqrt(jnp.mean(x * x, axis=-1, keepdims=True) + EPS)) * g


def _modulate(x, g, shift, scale):
    return _rms(x, g) * (1.0 + scale) + shift


def _dot(a, b):
    return jnp.dot(a, b, preferred_element_type=F32)


def _dot_nt(a, b):
    return lax.dot_general(a, b, (((1,), (1,)), ((), ())), preferred_element_type=F32)


def _dot_tn(a, b):
    return lax.dot_general(a, b, (((0,), (0,)), ((), ())), preferred_element_type=F32)


def _split3(x):
    hi = x.astype(BF16)
    r1 = x - hi.astype(F32)
    mid = r1.astype(BF16)
    lo = (r1 - mid.astype(F32)).astype(BF16)
    return hi, mid, lo


def _lane_tile(x, reps):
    return jnp.concatenate([x] * reps, axis=1)


def _mod_kernel(c_ref, w_ref, b_ref, o_ref):
    c = c_ref[...]
    s = c * jax.nn.sigmoid(c)
    o_ref[...] = _dot(s.astype(BF16), w_ref[...].astype(BF16)) + b_ref[...]


def _modulation(cvec, w_mod, b_mod):
    ncol = N_MOD * D_MODEL
    blk = 1536
    return pl.pallas_call(
        _mod_kernel,
        out_shape=jax.ShapeDtypeStruct((DEPTH, MOD_ROWS, ncol), F32),
        grid=(DEPTH, ncol // blk),
        in_specs=[
            pl.BlockSpec((MOD_ROWS, D_MODEL), lambda l, j: (0, 0)),
            pl.BlockSpec((None, D_MODEL, blk), lambda l, j: (l, 0, j)),
            pl.BlockSpec((None, 1, blk), lambda l, j: (l, 0, j)),
        ],
        out_specs=pl.BlockSpec((None, MOD_ROWS, blk), lambda l, j: (l, 0, j)),
        compiler_params=_cparams(("arbitrary", "arbitrary")),
        name="modulation",
    )(cvec, w_mod, b_mod.reshape(DEPTH, 1, ncol))


def _mod_spec(layer, j, tile):
    tpg = NP_TOK // tile
    return pl.BlockSpec((None, None, None, 1, D_MODEL), lambda i: (layer, i // tpg, j, 0, 0))


def _tok_spec(tile, width):
    return pl.BlockSpec((tile, width), lambda i: (i, 0))


def _split_specs(tile, width):
    npt = NP_TOK // tile
    return [pl.BlockSpec((tile, width), lambda i: (jnp.minimum(i, npt - 1), 0)),
            pl.BlockSpec((tile, width), lambda i: (jnp.maximum(i - npt, 0), 0))]


def _pick(tile, p_ref, s_ref):
    return jnp.where(pl.program_id(0) < NP_TOK // tile, p_ref[...], s_ref[...])


def _rope_row_spec(tile, width):
    npt = NP_TOK // tile
    spt = DEC_SEQ // tile
    return pl.BlockSpec((tile, width), lambda i: (jnp.where(i < npt, 0, 1 + (i - npt) % spt), 0))


def _even_in_kernel(xp_ref, xs_ref, g_ref, shift_ref, scale_ref, win_ref, qn_ref, wq_ref, kvn_ref, wkk_ref,
                    wkv_ref, vbias_ref, epl_ref, wgf_ref, bgf_ref, wgb_ref, bgb_ref, lmat_ref,
                    umat_ref, cq_ref, sq_ref, ck_ref, sk_ref,
                    q_out, k_out, v_out, ckv_out, kpe_out, gq_out, gk_out, gv_out, gr_out,
                    bf_out, bb_out):
    h = _modulate(_pick(TOK_TILE, xp_ref, xs_ref), g_ref[...], shift_ref[...], scale_ref[...])
    z = _dot(h.astype(BF16), win_ref[...])

    cqn = _rms(z[:, _C_CQ:_C_CQ + MLA_Q_RANK], qn_ref[...]).astype(BF16)
    qf = _dot(cqn, wq_ref[...])
    cq_t = _lane_tile(cq_ref[...], MLA_HEADS)
    sq_t = _lane_tile(sq_ref[...], MLA_HEADS)
    q_out[...] = (qf[:, :MLA_QK_W] * cq_t + qf[:, MLA_QK_W:] * sq_t).astype(BF16)

    ckvn = _rms(z[:, _C_CKV:_C_CKV + MLA_KV_RANK], kvn_ref[...])
    ckv_out[...] = ckvn
    small = z[:, _C_SMALL:_C_SMALL + LANES]
    kpe = (small[:, _S_KPE:_S_KPE + MLA_ROPE] * ck_ref[...]
           + small[:, _S_KPER:_S_KPER + MLA_ROPE] * sk_ref[...])
    kpe_out[...] = kpe
    ckvn_b = ckvn.astype(BF16)
    k_out[...] = (_dot(ckvn_b, wkk_ref[...]) + _dot(kpe.astype(BF16), epl_ref[...])).astype(BF16)
    v_out[...] = (_dot(ckvn_b, wkv_ref[...]) + vbias_ref[...]).astype(BF16)

    gq_out[...] = z[:, _C_GQ:_C_GQ + GLA_HEADS * GLA_DK] * (GLA_DK ** -0.5)
    gk_out[...] = z[:, _C_GK:_C_GK + GLA_HEADS * GLA_DK]
    gv_out[...] = z[:, _C_GV:_C_GV + GLA_HEADS * GLA_DV].astype(BF16)
    gr_out[...] = z[:, _C_GR:_C_GR + GLA_HEADS * GLA_DV]

    small_b = small.astype(BF16)
    la_f = _log_sigmoid(_dot(small_b, wgf_ref[...]) + bgf_ref[...]) * (1.0 / GLA_GATE_NORM)
    la_b = _log_sigmoid(_dot(small_b, wgb_ref[...]) + bgb_ref[...]) * (1.0 / GLA_GATE_NORM)
    lmat = lmat_ref[...]
    umat = umat_ref[...]
    for r in range(TOK_TILE // CUMSUM_BLOCK):
        rows = slice(r * CUMSUM_BLOCK, (r + 1) * CUMSUM_BLOCK)
        f_hi, f_mid, f_lo = _split3(la_f[rows])
        bf_out[rows, :] = _dot(lmat, f_hi) + _dot(lmat, f_mid) + _dot(lmat, f_lo)
        b_hi, b_mid, b_lo = _split3(la_b[rows])
        bb_out[rows, :] = _dot(umat, b_hi) + _dot(umat, b_mid) + _dot(umat, b_lo)


def _even_in_proj(xp, xs, mod5, layer, g, wts, tabs):
    t = TOK_TILE
    out_widths = [(MLA_QK_W, BF16), (MLA_QK_W, BF16), (MLA_VEXT_W, BF16), (MLA_KV_RANK, F32),
                  (MLA_ROPE, F32), (GLA_HEADS * GLA_DK, F32), (GLA_HEADS * GLA_DK, F32),
                  (GLA_HEADS * GLA_DV, BF16), (GLA_HEADS * GLA_DV, F32),
                  (GLA_HEADS * GLA_DK, F32), (GLA_HEADS * GLA_DK, F32)]
    const_names = ["win", "qn", "wq", "kvn", "wkk", "wkv", "vbias", "epl", "wgf", "bgf", "wgb",
                   "bgb", "lmat", "umat"]
    consts = [wts[n] for n in const_names]
    in_specs = (_split_specs(t, D_MODEL)
                + [_const_spec((1, D_MODEL)), _mod_spec(layer, 0, t), _mod_spec(layer, 1, t)]
                + [_const_spec(c.shape) for c in consts]
                + [_rope_row_spec(t, LANES), _rope_row_spec(t, LANES),
                   _rope_row_spec(t, MLA_ROPE), _rope_row_spec(t, MLA_ROPE)])
    return pl.pallas_call(
        _even_in_kernel,
        out_shape=[jax.ShapeDtypeStruct((N_TOK, w), dt) for w, dt in out_widths],
        grid=(N_TOK // t,),
        in_specs=in_specs,
        out_specs=[_tok_spec(t, w) for w, _ in out_widths],
        compiler_params=_cparams(("parallel",)),
        name="even_in_proj",
    )(xp, xs, g, mod5, mod5, *consts, tabs["mla_cq"], tabs["mla_sq"], tabs["mla_ck"], tabs["mla_sk"])


def _cache_kv_kernel(ckv_ref, kpe_ref, wkk_ref, wkv_ref, vbias_ref, epl_ref, k_out, v_out):
    ckv_b = ckv_ref[...].astype(BF16)
    k_out[...] = (_dot(ckv_b, wkk_ref[...]) + _dot(kpe_ref[...].astype(BF16), epl_ref[...])).astype(BF16)
    v_out[...] = (_dot(ckv_b, wkv_ref[...]) + vbias_ref[...]).astype(BF16)


def _cache_kv(ckv, kpe, wts):
    n = ckv.shape[0]
    consts = [wts[k] for k in ("wkk", "wkv", "vbias", "epl")]
    return pl.pallas_call(
        _cache_kv_kernel,
        out_shape=[jax.ShapeDtypeStruct((n, MLA_QK_W), BF16), jax.ShapeDtypeStruct((n, MLA_VEXT_W), BF16)],
        grid=(1,),
        in_specs=[_const_spec(ckv.shape), _const_spec(kpe.shape)] + [_const_spec(c.shape) for c in consts],
        out_specs=[_const_spec((n, MLA_QK_W)), _const_spec((n, MLA_VEXT_W))],
        compiler_params=_cparams(("arbitrary",)),
        name="mla_cache_kv",
    )(ckv, kpe, *consts)


def _mla_attn_kernel(*refs, n_seg):
    q_ref = refs[0]
    k_refs = refs[1:1 + n_seg]
    v_refs = refs[1 + n_seg:1 + 2 * n_seg]
    o_ref, s_scr = refs[1 + 2 * n_seg:]
    scale = (MLA_NOPE + MLA_ROPE) ** -0.5
    c = scale * float(np.log2(np.e))
    tq = q_ref.shape[0]
    q_sub = min(tq, MLA_Q_SUB)
    kt = MXU_COLS
    tiles = []
    for si, k in enumerate(k_refs):
        for r0 in range(0, k.shape[0], kt):
            tiles.append((si, r0, len(tiles) * kt))
    lane = lax.broadcasted_iota(jnp.int32, (q_sub, LANES), 1)
    for qs in range(tq // q_sub):
        rows = slice(qs * q_sub, (qs + 1) * q_sub)
        res = []
        for j in range(2):
            hl = slice(j * LANES, (j + 1) * LANES)
            qj = q_ref[rows, hl]
            macc = None
            for si, r0, c0 in tiles:
                s = _dot_nt(qj, k_refs[si][r0:r0 + kt, hl])
                s_scr[j, rows, c0:c0 + kt] = s
                mt = jnp.maximum(s[:, :LANES], s[:, LANES:])
                macc = mt if macc is None else jnp.maximum(macc, mt)
            m = macc.max(axis=-1, keepdims=True)
            r = None
            for si, r0, c0 in tiles:
                p = jnp.exp2((s_scr[j, rows, c0:c0 + kt] - m) * c).astype(BF16)
                rj = _dot(p, v_refs[si][r0:r0 + kt, :])
                r = rj if r is None else r + rj
            res.append(r[:, :LANES] / r[:, LANES:])
        o_ref[rows, :] = jnp.where(lane < MLA_V, res[0], res[1]).astype(BF16)


def _mla_attention(q, ks, vs, *, n_batch, seq_q, q_tile, tok_off, k_batch_rows):
    n_seg = len(ks)
    nq = seq_q // q_tile
    qoff = tok_off // q_tile
    grid = (n_batch, MLA_HEADS // 2, nq)
    in_specs = [pl.BlockSpec((q_tile, 2 * LANES), lambda b, hp, i: (qoff + b * nq + i, hp))]
    for s in range(n_seg):
        rows = k_batch_rows[s]
        off = (tok_off // rows) if s == 0 else 0
        in_specs.append(pl.BlockSpec((rows, 2 * LANES), functools.partial(lambda b, hp, i, off: (off + b, hp), off=off)))
    for s in range(n_seg):
        rows = k_batch_rows[s]
        off = (tok_off // rows) if s == 0 else 0
        in_specs.append(pl.BlockSpec((rows, 2 * LANES), functools.partial(lambda b, hp, i, off: (off + b, hp), off=off)))
    return pl.pallas_call(
        functools.partial(_mla_attn_kernel, n_seg=n_seg),
        out_shape=jax.ShapeDtypeStruct((n_batch * seq_q, MLA_HEADS * MLA_V), BF16),
        grid=grid,
        in_specs=in_specs,
        out_specs=pl.BlockSpec((q_tile, LANES), lambda b, hp, i: (b * nq + i, hp)),
        scratch_shapes=[pltpu.VMEM((2, q_tile, sum(k_batch_rows)), F32)],
        compiler_params=_cparams(("parallel", "parallel", "arbitrary")),
        name=f"mla_attention_{n_seg}seg",
    )(q, *ks, *vs)


def _gla_kernel(q_ref, k_ref, v_ref, gr_ref, bf_ref, bb_ref, s0f_ref, s0b_ref, gn_ref,
                o_ref, sf_ref, sb_ref, acc_ref, *, n_chunks):
    c = GLA_CHUNK
    lane = lax.broadcasted_iota(jnp.int32, (c, LANES), 1)
    lo = lane < GLA_DK
    row = lax.broadcasted_iota(jnp.int32, (c, c), 0)
    col = lax.broadcasted_iota(jnp.int32, (c, c), 1)
    tril = row >= col
    triu = row <= col
    zero_blk = jnp.zeros((GLA_DK, GLA_DV), F32)

    def pair_state_t(s_ref):
        blockdiag = jnp.concatenate(
            [jnp.concatenate([s_ref[0], zero_blk], axis=1),
             jnp.concatenate([zero_blk, s_ref[1]], axis=1)], axis=0)
        return blockdiag.T

    def chunk(ci, b_ref, st, mid_row, last_row, causal):
        rows = pl.ds(pl.multiple_of(ci * c, c), c)
        q = q_ref[rows, :]
        k = k_ref[rows, :]
        v = v_ref[rows, :]
        b = b_ref[rows, :]
        b_mid = b[mid_row:mid_row + 1, :]
        b_last = b[last_row:last_row + 1, :]
        qe = q * jnp.exp(b - b_mid)
        ke = (k * jnp.exp(b_mid - b)).astype(BF16)
        kd = (k * jnp.exp(b_last - b)).astype(BF16)
        qd = q * jnp.exp(b)
        st_b = st.astype(BF16)
        for j in range(2):
            sel = lo if j == 0 else jnp.logical_not(lo)
            a = _dot_nt(jnp.where(sel, qe, 0.0).astype(BF16), ke)
            a = jnp.where(causal, a, 0.0).astype(BF16)
            o = (_dot(a, v[:, j * GLA_DV:(j + 1) * GLA_DV])
                 + _dot_nt(jnp.where(sel, qd, 0.0).astype(BF16), st_b[j * GLA_DV:(j + 1) * GLA_DV, :]))
            acc_ref[rows, j * GLA_DV:(j + 1) * GLA_DV] += o
        return st * jnp.exp(b_last) + _dot_tn(v, kd)

    acc_ref[...] = jnp.zeros_like(acc_ref)

    def body(i, carry):
        st_f, st_b = carry
        st_f = chunk(i, bf_ref, st_f, c // 2 - 1, c - 1, tril)
        st_b = chunk(n_chunks - 1 - i, bb_ref, st_b, c // 2, 0, triu)
        return st_f, st_b

    st_f, st_b = lax.fori_loop(0, n_chunks, body, (pair_state_t(s0f_ref), pair_state_t(s0b_ref)))
    s_f = st_f.T
    s_b = st_b.T
    sf_ref[0] = s_f[:GLA_DK, :GLA_DV]
    sf_ref[1] = s_f[GLA_DK:, GLA_DV:]
    sb_ref[0] = s_b[:GLA_DK, :GLA_DV]
    sb_ref[1] = s_b[GLA_DK:, GLA_DV:]

    gn = gn_ref[...]
    for j in range(2):
        cols = slice(j * GLA_DV, (j + 1) * GLA_DV)
        gr = gr_ref[:, cols]
        o_ref[:, cols] = (_rms(acc_ref[:, cols], gn) * (gr * jax.nn.sigmoid(gr))).astype(BF16)


def _gla(gq, gk, gv, gr, bf, bb, s0f, s0b, gnorm, *, n_batch, seq, tok_off):
    n_chunks = seq // GLA_CHUNK
    boff = tok_off // seq
    hp = GLA_HEADS // 2
    tok = lambda w: pl.BlockSpec((seq, w), lambda b, p: (boff + b, p))
    st = pl.BlockSpec((None, 2, GLA_DK, GLA_DV), lambda b, p: (b, p, 0, 0))
    return pl.pallas_call(
        functools.partial(_gla_kernel, n_chunks=n_chunks),
        out_shape=[jax.ShapeDtypeStruct((n_batch * seq, GLA_HEADS * GLA_DV), BF16),
                   jax.ShapeDtypeStruct((n_batch, GLA_HEADS, GLA_DK, GLA_DV), F32),
                   jax.ShapeDtypeStruct((n_batch, GLA_HEADS, GLA_DK, GLA_DV), F32)],
        grid=(n_batch, hp),
        in_specs=[tok(2 * GLA_DK), tok(2 * GLA_DK), tok(2 * GLA_DV), tok(2 * GLA_DV),
                  tok(2 * GLA_DK), tok(2 * GLA_DK), st, st, _const_spec((1, GLA_DV))],
        out_specs=[pl.BlockSpec((seq, 2 * GLA_DV), lambda b, p: (b, p)), st, st],
        scratch_shapes=[pltpu.VMEM((seq, 2 * GLA_DV), F32)],
        compiler_params=_cparams(("parallel", "parallel")),
        name=f"gla_seq{seq}",
    )(gq, gk, gv, gr, bf, bb, s0f, s0b, gnorm)


def _out_proj_kernel(*refs, n_o, x_split):
    t = TOK_TILE
    o = [_pick(t, refs[2 * j], refs[2 * j + 1]) for j in range(n_o)]
    rest = refs[2 * n_o:]
    w_ref = rest[0]
    if x_split:
        x_in = _pick(t, rest[1], rest[2])
        rest = rest[3:]
    else:
        x_in = rest[1][...]
        rest = rest[2:]
    g1_ref, gate_ref, g2_ref, shift_ref, scale_ref, x_out, h_out = rest
    y = _dot(o[0] if n_o == 1 else jnp.concatenate(o, axis=1), w_ref[...])
    x = x_in + gate_ref[...] * _rms(y, g1_ref[...])
    x_out[...] = x
    h_out[...] = _modulate(x, g2_ref[...], shift_ref[...], scale_ref[...]).astype(h_out.dtype)


def _out_proj(os_, w, x, mod5, layer, g1, g2, h_dtype):
    t = TOK_TILE
    x_split = isinstance(x, tuple)
    in_specs, args = [], []
    for o_p, o_s in os_:
        in_specs += _split_specs(t, o_p.shape[1])
        args += [o_p, o_s]
    in_specs.append(_const_spec(w.shape))
    args.append(w)
    if x_split:
        in_specs += _split_specs(t, D_MODEL)
        args += list(x)
    else:
        in_specs.append(_tok_spec(t, D_MODEL))
        args.append(x)
    in_specs += [_const_spec((1, D_MODEL)), _mod_spec(layer, 2, t), _const_spec((1, D_MODEL)),
                 _mod_spec(layer, 3, t), _mod_spec(layer, 4, t)]
    args += [g1, mod5, g2, mod5, mod5]
    return pl.pallas_call(
        functools.partial(_out_proj_kernel, n_o=len(os_), x_split=x_split),
        out_shape=[jax.ShapeDtypeStruct((N_TOK, D_MODEL), F32), jax.ShapeDtypeStruct((N_TOK, D_MODEL), h_dtype)],
        grid=(N_TOK // t,),
        in_specs=in_specs,
        out_specs=[_tok_spec(t, D_MODEL), _tok_spec(t, D_MODEL)],
        compiler_params=_cparams(("parallel",)),
        name=f"out_proj_{len(os_)}",
    )(*args)


def _ffn_kernel(h_ref, x_ref, wg_ref, wu_ref, wd_ref, g_ref, gate_ref, x_out):
    h = h_ref[...]
    f = None
    for cidx in range(D_FF // FF_CHUNK):
        cols = slice(cidx * FF_CHUNK, (cidx + 1) * FF_CHUNK)
        a = _dot(h, wg_ref[:, cols].astype(BF16))
        u = _dot(h, wu_ref[:, cols].astype(BF16))
        fc = _dot(((a * jax.nn.sigmoid(a)) * u).astype(BF16), wd_ref[cols, :].astype(BF16))
        f = fc if f is None else f + fc
    x_out[...] = x_ref[...] + gate_ref[...] * _rms(f, g_ref[...])


def _ffn(h, x, wg, wu, wd, mod5, layer, g3):
    t = TOK_TILE
    return pl.pallas_call(
        _ffn_kernel,
        out_shape=jax.ShapeDtypeStruct((N_TOK, D_MODEL), F32),
        grid=(N_TOK // t,),
        in_specs=[_tok_spec(t, D_MODEL), _tok_spec(t, D_MODEL), _const_spec(wg.shape),
                  _const_spec(wu.shape), _const_spec(wd.shape), _const_spec((1, D_MODEL)),
                  _mod_spec(layer, 5, t)],
        out_specs=_tok_spec(t, D_MODEL),
        compiler_params=_cparams(("parallel",)),
        name="ffn_swiglu",
    )(h, x, wg, wu, wd, g3, mod5)


GQA_Q_W = GQA_HEADS * GQA_HEAD_DIM
GQA_KV_W = GQA_KV_HEADS * GQA_HEAD_DIM
GQA_VEXT_W = (GQA_KV_HEADS // 2) * 2 * LANES
_O_Q, _O_QR = 0, GQA_Q_W
_O_K, _O_KR = 2 * GQA_Q_W, 2 * GQA_Q_W + GQA_KV_W
_O_V = 2 * GQA_Q_W + 2 * GQA_KV_W
IN_C_EXT = _O_V + GQA_VEXT_W


def _odd_in_kernel(x_ref, g_ref, shift_ref, scale_ref, win_ref, vbias_ref, c_ref, s_ref,
                   q_out, k_out, kb_out, v_out, vb_out):
    h = _modulate(x_ref[...], g_ref[...], shift_ref[...], scale_ref[...])
    z = _dot(h.astype(BF16), win_ref[...])
    c_t = _lane_tile(c_ref[...], GQA_Q_W // LANES)
    s_t = _lane_tile(s_ref[...], GQA_Q_W // LANES)
    q = z[:, _O_Q:_O_Q + GQA_Q_W] * c_t + z[:, _O_QR:_O_QR + GQA_Q_W] * s_t
    q_out[...] = (q * (GQA_HEAD_DIM ** -0.5)).astype(BF16)
    k = (z[:, _O_K:_O_K + GQA_KV_W] * c_t[:, :GQA_KV_W]
         + z[:, _O_KR:_O_KR + GQA_KV_W] * s_t[:, :GQA_KV_W])
    k_out[...] = k
    kb_out[...] = k.astype(BF16)
    vext = z[:, _O_V:_O_V + GQA_VEXT_W] + vbias_ref[...]
    vb_out[...] = vext.astype(BF16)
    for p in range(GQA_KV_HEADS // 2):
        v_out[:, p * LANES:(p + 1) * LANES] = vext[:, 2 * p * LANES:(2 * p + 1) * LANES]


def _odd_in_proj(x, mod5, layer, g, win, vbias, tab_c, tab_s):
    t = TOK_TILE
    out_widths = [(GQA_Q_W, BF16), (GQA_KV_W, F32), (GQA_KV_W, BF16), (GQA_KV_W, F32), (GQA_VEXT_W, BF16)]
    return pl.pallas_call(
        _odd_in_kernel,
        out_shape=[jax.ShapeDtypeStruct((N_TOK, w), dt) for w, dt in out_widths],
        grid=(N_TOK // t,),
        in_specs=[_tok_spec(t, D_MODEL), _const_spec((1, D_MODEL)), _mod_spec(layer, 0, t),
                  _mod_spec(layer, 1, t), _const_spec(win.shape), _const_spec(vbias.shape),
                  _rope_row_spec(t, LANES), _rope_row_spec(t, LANES)],
        out_specs=[_tok_spec(t, w) for w, _ in out_widths],
        compiler_params=_cparams(("parallel",)),
        name="odd_in_proj",
    )(x, g, mod5, mod5, win, vbias, tab_c, tab_s)


def _gqa_kernel(sink_ref, q_ref, *refs, local_len):
    if local_len:
        kl_ref, vl_ref, kc_ref, vc_ref, o_ref = refs
    else:
        kc_ref, vc_ref, o_ref = refs
    tq = q_ref.shape[0]
    lane = lax.broadcasted_iota(jnp.int32, (tq, LANES), 1)
    lo = lane < GQA_HEAD_DIM
    if local_len:
        i = pl.program_id(1)
        q0 = i * tq
        seq = kl_ref.shape[0]
        kstart = pl.multiple_of(jnp.clip(q0 - WINDOW, 0, seq - local_len), LANES)
        qpos = q0 + lax.broadcasted_iota(jnp.int32, (tq, local_len), 0)
        kpos = kstart + lax.broadcasted_iota(jnp.int32, (tq, local_len), 1)
        band = jnp.abs(qpos - kpos) <= WINDOW
    for p in range(GQA_KV_HEADS // 2):
        kc = kc_ref[:, p * LANES:(p + 1) * LANES]
        vc = vc_ref[:, 2 * p * LANES:(2 * p + 2) * LANES]
        if local_len:
            kl = kl_ref[pl.ds(kstart, local_len), p * LANES:(p + 1) * LANES]
            vl = vl_ref[pl.ds(kstart, local_len), 2 * p * LANES:(2 * p + 2) * LANES]
        for blk in range(GQA_GROUP):
            cols = slice((p * GQA_GROUP + blk) * LANES, (p * GQA_GROUP + blk + 1) * LANES)
            qb = q_ref[:, cols]
            res = []
            for half in range(2):
                head = (2 * p + half) * GQA_GROUP + blk
                sink = sink_ref[head]
                qh = jnp.where(lo if half == 0 else jnp.logical_not(lo), qb, jnp.zeros_like(qb))
                s_c = _dot_nt(qh, kc)
                m = jnp.maximum(s_c.max(axis=-1, keepdims=True), sink)
                if local_len:
                    s_l = jnp.where(band, _dot_nt(qh, kl), NEG_INF)
                    m = jnp.maximum(m, s_l.max(axis=-1, keepdims=True))
                r = _dot(jnp.exp(s_c - m).astype(BF16), vc)
                if local_len:
                    r = r + _dot(jnp.exp(s_l - m).astype(BF16), vl)
                res.append(r[:, :LANES] / (r[:, LANES:] + jnp.exp(sink - m)))
            o_ref[:, cols] = jnp.where(lo, res[0], res[1]).astype(BF16)


def _gqa_attention(sink, q, k_loc, v_loc, k_ctx, v_ctx, *, n_batch, seq_q, q_tile, tok_off, n_ctx, local):
    nq = seq_q // q_tile
    qoff = tok_off // q_tile
    local_len = q_tile + 2 * WINDOW if local else 0
    in_specs = [pl.BlockSpec(memory_space=pltpu.SMEM),
                pl.BlockSpec((q_tile, GQA_Q_W), lambda b, i: (qoff + b * nq + i, 0))]
    args = [sink, q]
    if local:
        boff = tok_off // seq_q
        in_specs += [pl.BlockSpec((seq_q, GQA_KV_W), lambda b, i: (boff + b, 0)),
                     pl.BlockSpec((seq_q, GQA_VEXT_W), lambda b, i: (boff + b, 0))]
        args += [k_loc, v_loc]
    in_specs += [pl.BlockSpec((n_ctx, GQA_KV_W), lambda b, i: (b, 0)),
                 pl.BlockSpec((n_ctx, GQA_VEXT_W), lambda b, i: (b, 0))]
    args += [k_ctx, v_ctx]
    return pl.pallas_call(
        functools.partial(_gqa_kernel, local_len=local_len),
        out_shape=jax.ShapeDtypeStruct((n_batch * seq_q, GQA_Q_W), BF16),
        grid=(n_batch, nq),
        in_specs=in_specs,
        out_specs=pl.BlockSpec((q_tile, GQA_Q_W), lambda b, i: (b * nq + i, 0)),
        compiler_params=_cparams(("parallel", "arbitrary")),
        name="gqa_local" if local else "gqa_ctx",
    )(*args)


def _router_kernel(h_ref, w_ref, b_ref, ltri_ref, wsel_out, isel_out, rank_out, cnt_out, carry_ref):
    @pl.when(pl.program_id(0) == 0)
    def _():
        carry_ref[...] = jnp.zeros_like(carry_ref)

    logits = _dot(h_ref[...].astype(BF16), w_ref[...]) + b_ref[...]
    lane = lax.broadcasted_iota(jnp.int32, logits.shape, 1)
    neg = float(np.finfo(np.float32).min)
    lg = jnp.where(lane < N_EXPERTS, logits, neg)
    v1 = lg.max(axis=-1, keepdims=True)
    i1 = jnp.min(jnp.where(lg == v1, lane, LANES), axis=-1, keepdims=True)
    lg2 = jnp.where(lane == i1, neg, lg)
    v2 = lg2.max(axis=-1, keepdims=True)
    i2 = jnp.min(jnp.where(lg2 == v2, lane, LANES), axis=-1, keepdims=True)
    e2 = jnp.exp(v2 - v1)
    den = 1.0 + e2
    wsel_out[...] = jnp.where(lane == 0, 1.0 / den, jnp.where(lane == 1, e2 / den, 0.0))
    isel_out[...] = jnp.where(lane == 0, i1, jnp.where(lane == 1, i2, 0))
    hit = jnp.where(lane == i1, 1.0, jnp.where(lane == i2, 1.0, 0.0))
    carry = carry_ref[...]
    rank_out[...] = (_dot(ltri_ref[...], hit.astype(BF16)) + carry[0:1, :]).astype(jnp.int32)
    carry = carry + jnp.sum(hit, axis=0, keepdims=True)
    carry_ref[...] = carry
    cnt_out[...] = carry.astype(jnp.int32)


def _router(h, w, b):
    t = TOK_TILE
    r = np.arange(t)
    ltri = jnp.asarray(r[:, None] > r[None, :], BF16)
    return pl.pallas_call(
        _router_kernel,
        out_shape=[jax.ShapeDtypeStruct((N_TOK, LANES), F32), jax.ShapeDtypeStruct((N_TOK, LANES), jnp.int32),
                   jax.ShapeDtypeStruct((N_TOK, LANES), jnp.int32), jax.ShapeDtypeStruct((8, LANES), jnp.int32)],
        grid=(N_TOK // t,),
        in_specs=[_tok_spec(t, D_MODEL), _const_spec(w.shape), _const_spec(b.shape), _const_spec((t, t))],
        out_specs=[_tok_spec(t, LANES), _tok_spec(t, LANES), _tok_spec(t, LANES),
                   pl.BlockSpec((8, LANES), lambda i: (0, 0))],
        scratch_shapes=[pltpu.VMEM((8, LANES), F32)],
        compiler_params=_cparams(("arbitrary",)),
        name="moe_router",
    )(h, w, b, ltri)


def _route_tables(isel, rank, cnt):
    tm = MOE_ROW_TILE
    counts = cnt[0, :N_EXPERTS]
    padded = ((counts + tm - 1) // tm) * tm
    ends = jnp.cumsum(padded)
    base = ends - padded
    e_ids = jnp.arange(N_EXPERTS, dtype=jnp.int32)
    row = rank[:, :N_EXPERTS] + base[None, :]
    pos1 = jnp.sum(jnp.where(e_ids[None, :] == isel[:, 0:1], row, 0), axis=1)
    pos2 = jnp.sum(jnp.where(e_ids[None, :] == isel[:, 1:2], row, 0), axis=1)
    pos = jnp.concatenate([pos1, pos2])
    tok = jnp.arange(N_TOK, dtype=jnp.int32)
    src = jnp.zeros((MOE_ROWS,), jnp.int32).at[pos].set(jnp.concatenate([tok, tok]))
    tile_start = jnp.arange(MOE_ROWS // tm, dtype=jnp.int32) * tm
    tile_expert = jnp.minimum(jnp.sum(tile_start[:, None] >= ends[None, :], axis=1), N_EXPERTS - 1).astype(jnp.int32)
    n_used = (ends[-1] // tm).astype(jnp.int32).reshape(1)
    return pos, src, tile_expert, n_used


def _gather_rows(table, idx):
    n_idx = idx.shape[0]
    d = table.shape[1]
    per_w = n_idx // SC_WORKERS
    assert per_w * SC_WORKERS == n_idx and per_w % SC_INDEX_BLOCK == 0
    mesh = plsc.VectorSubcoreMesh(core_axis_name="core", subcore_axis_name="subcore")

    @functools.partial(
        pl.kernel, out_type=jax.ShapeDtypeStruct((n_idx, d), table.dtype), mesh=mesh,
        scratch_types=[pltpu.VMEM((SC_INDEX_BLOCK,), jnp.int32), pltpu.VMEM((SC_GATHER_WINDOW, d), table.dtype)],
        name="sc_gather_rows")
    def gather(x_hbm, i_hbm, o_hbm, idx_v, rows_v):
        wid = lax.axis_index("subcore") * SC_CORES + lax.axis_index("core")
        base = wid * per_w

        @pl.loop(0, per_w // SC_INDEX_BLOCK)
        def _(g):
            off = base + g * SC_INDEX_BLOCK
            pltpu.sync_copy(i_hbm.at[pl.ds(off, SC_INDEX_BLOCK)], idx_v)
            for s in range(SC_INDEX_BLOCK // SC_GATHER_WINDOW):
                sub = pl.ds(s * SC_GATHER_WINDOW, SC_GATHER_WINDOW)
                pltpu.sync_copy(x_hbm.at[idx_v.at[sub]], rows_v)
                pltpu.sync_copy(rows_v, o_hbm.at[pl.ds(off + s * SC_GATHER_WINDOW, SC_GATHER_WINDOW)])

    return gather(table, idx)


def _expert_ffn_kernel(te_ref, nu_ref, x_ref, wg_ref, wu_ref, wd_ref, y_out):
    @pl.when(pl.program_id(0) < nu_ref[0])
    def _():
        h = x_ref[...].astype(BF16)
        f = None
        for cidx in range(D_FF // FF_CHUNK):
            cols = slice(cidx * FF_CHUNK, (cidx + 1) * FF_CHUNK)
            a = _dot(h, wg_ref[:, cols].astype(BF16))
            u = _dot(h, wu_ref[:, cols].astype(BF16))
            fc = _dot(((a * jax.nn.sigmoid(a)) * u).astype(BF16), wd_ref[cols, :].astype(BF16))
            f = fc if f is None else f + fc
        y_out[...] = f

    @pl.when(pl.program_id(0) >= nu_ref[0])
    def _():
        y_out[...] = jnp.zeros_like(y_out)


def _expert_ffn(xs, tile_expert, n_used, wg, wu, wd):
    tm = MOE_ROW_TILE
    wspec = lambda shape: pl.BlockSpec((None,) + shape, lambda j, te, nu: (te[j], 0, 0),
                                       pipeline_mode=pl.Buffered(1))
    return pl.pallas_call(
        _expert_ffn_kernel,
        out_shape=jax.ShapeDtypeStruct((MOE_ROWS, D_MODEL), F32),
        grid_spec=pltpu.PrefetchScalarGridSpec(
            num_scalar_prefetch=2,
            grid=(MOE_ROWS // tm,),
            in_specs=[pl.BlockSpec((tm, D_MODEL), lambda j, te, nu: (j, 0)),
                      wspec((D_MODEL, D_FF)), wspec((D_MODEL, D_FF)), wspec((D_FF, D_MODEL))],
            out_specs=pl.BlockSpec((tm, D_MODEL), lambda j, te, nu: (j, 0)),
        ),
        compiler_params=_cparams(("arbitrary",)),
        name="moe_expert_ffn",
    )(tile_expert, n_used, xs, wg, wu, wd)


def _moe_combine_kernel(y1_ref, y2_ref, wsel_ref, x_ref, g_ref, gate_ref, x_out):
    w = wsel_ref[...]
    f = w[:, 0:1] * y1_ref[...] + w[:, 1:2] * y2_ref[...]
    x_out[...] = x_ref[...] + gate_ref[...] * _rms(f, g_ref[...])


def _moe_combine(yg, wsel, x, mod5, layer, g3, *, tok_off, n_tok):
    t = TOK_TILE
    nt = N_TOK // t
    off = tok_off // t
    tpg = NP_TOK // t
    tok = lambda w, shift: pl.BlockSpec((t, w), lambda i: (off + shift + i, 0))
    return pl.pallas_call(
        _moe_combine_kernel,
        out_shape=jax.ShapeDtypeStruct((n_tok, D_MODEL), F32),
        grid=(n_tok // t,),
        in_specs=[tok(D_MODEL, 0), tok(D_MODEL, nt), tok(LANES, 0), tok(D_MODEL, 0), _const_spec((1, D_MODEL)),
                  pl.BlockSpec((None, None, None, 1, D_MODEL), lambda i: (layer, (off + i) // tpg, 5, 0, 0))],
        out_specs=_tok_spec(t, D_MODEL),
        compiler_params=_cparams(("parallel",)),
        name="moe_combine",
    )(yg, yg, wsel, x, g3, mod5)


def _moe(h, x, w_router, b_router, wg, wu, wd, mod5, layer, g3):
    wsel, isel, rank, cnt = _router(h, w_router, b_router)
    pos, src, tile_expert, n_used = _route_tables(isel, rank, cnt)
    xs = _gather_rows(h, src)
    ys = _expert_ffn(xs, tile_expert, n_used, wg, wu, wd)
    yg = _gather_rows(ys, pos)
    return (_moe_combine(yg, wsel, x, mod5, layer, g3, tok_off=0, n_tok=NP_TOK),
            _moe_combine(yg, wsel, x, mod5, layer, g3, tok_off=NP_TOK, n_tok=NS_TOK))


def _rot_cols(w, half):
    k, n = w.shape
    wb = w.reshape(k, n // (2 * half), 2, half)
    return jnp.stack([-wb[:, :, 1], wb[:, :, 0]], axis=2).reshape(k, n)


def _axis_tables(r, pos):
    inv = ROPE_BASE ** (-jnp.arange(0, r, 2, dtype=F32) / r)
    ang = pos.astype(F32)[:, None] * inv[None, :]
    cos, sin = jnp.cos(ang), jnp.sin(ang)
    return jnp.concatenate([cos, cos], axis=1), jnp.concatenate([sin, sin], axis=1)


def _rope_tables(r):
    s = jnp.arange(DEC_SEQ)
    cr, sr = _axis_tables(r // 2, s // GRID_W)
    cc, sc = _axis_tables(r // 2, s % GRID_W)
    return jnp.concatenate([cr, cc], axis=1), jnp.concatenate([sr, sc], axis=1)


def _with_identity(tab, ident):
    return jnp.concatenate([jnp.full((TOK_TILE, tab.shape[1]), ident, F32), tab], axis=0)


def _prep_tables():
    c32, s32 = _rope_tables(MLA_ROPE)
    ones = jnp.ones((DEC_SEQ, MLA_NOPE), F32)
    pad1 = jnp.ones((DEC_SEQ, MLA_HEAD_PAD - MLA_NOPE - MLA_ROPE), F32)
    cq = jnp.concatenate([ones, c32, pad1], axis=1)
    sq = jnp.concatenate([0 * ones, s32, 0 * pad1], axis=1)
    c64, s64 = _rope_tables(GQA_HEAD_DIM)
    return {
        "mla_cq": _with_identity(cq, 1.0), "mla_sq": _with_identity(sq, 0.0),
        "mla_ck": _with_identity(c32, 1.0), "mla_sk": _with_identity(s32, 0.0),
        "gqa_c": _with_identity(jnp.concatenate([c64, c64], axis=1), 1.0),
        "gqa_s": _with_identity(jnp.concatenate([s64, s64], axis=1), 0.0),
    }


def _prep_even(w_in, q_norm, w_q_up, kv_norm, w_kv_up, wgf, bgf, wgb, bgb):
    sizes = [MLA_Q_RANK, MLA_KV_RANK, MLA_ROPE, GLA_HEADS * GLA_DK, GLA_HEADS * GLA_DK,
             GLA_HEADS * GLA_DV, GLA_HEADS * GLA_DV, GLA_GATE_RANK, GLA_GATE_RANK]
    cq, ckv, kpe, gq, gk, gv, gr, gaf, gab = jnp.split(w_in, [int(s) for s in np.cumsum(sizes)[:-1]], axis=1)
    pad = jnp.zeros((D_MODEL, LANES - 2 * MLA_ROPE - 2 * GLA_GATE_RANK), F32)
    win = jnp.concatenate([cq, ckv, gq, gk, gv, gr, kpe, _rot_cols(kpe, MLA_ROPE // 4), gaf, gab, pad], axis=1)

    wq = w_q_up.reshape(MLA_Q_RANK, MLA_HEADS, MLA_NOPE + MLA_ROPE)
    nope, pe = wq[..., :MLA_NOPE], wq[..., MLA_NOPE:]
    pe_rot = _rot_cols(pe.reshape(MLA_Q_RANK, MLA_HEADS * MLA_ROPE), MLA_ROPE // 4).reshape(pe.shape)
    zpad = jnp.zeros((MLA_Q_RANK, MLA_HEADS, MLA_HEAD_PAD - MLA_NOPE - MLA_ROPE), F32)
    wq_main = jnp.concatenate([nope, pe, zpad], axis=-1).reshape(MLA_Q_RANK, MLA_QK_W)
    wq_rot = jnp.concatenate([0 * nope, pe_rot, zpad], axis=-1).reshape(MLA_Q_RANK, MLA_QK_W)

    wkv = w_kv_up.reshape(MLA_KV_RANK, MLA_HEADS, MLA_NOPE + MLA_V)
    knope, vv = wkv[..., :MLA_NOPE], wkv[..., MLA_NOPE:]
    wkk = jnp.concatenate([knope, jnp.zeros((MLA_KV_RANK, MLA_HEADS, MLA_HEAD_PAD - MLA_NOPE), F32)],
                          axis=-1).reshape(MLA_KV_RANK, MLA_QK_W)
    vpair = vv.reshape(MLA_KV_RANK, MLA_HEADS // 2, 2 * MLA_V)
    wkv_ext = jnp.concatenate([vpair, jnp.zeros((MLA_KV_RANK, MLA_HEADS // 2, LANES), F32)],
                              axis=-1).reshape(MLA_KV_RANK, MLA_VEXT_W)
    vbias = jnp.tile(jnp.concatenate([jnp.zeros((LANES,), F32), jnp.ones((LANES,), F32)]),
                     MLA_HEADS // 2).reshape(1, MLA_VEXT_W)
    epl = jnp.tile(jnp.concatenate([jnp.zeros((MLA_ROPE, MLA_NOPE), F32), jnp.eye(MLA_ROPE, dtype=F32),
                                    jnp.zeros((MLA_ROPE, MLA_HEAD_PAD - MLA_NOPE - MLA_ROPE), F32)], axis=1),
                   (1, MLA_HEADS))

    def gate_w(w, off):
        return jnp.zeros((LANES, GLA_HEADS * GLA_DK), F32).at[off:off + GLA_GATE_RANK].set(w)

    r = np.arange(CUMSUM_BLOCK)
    same = (r[:, None] // GLA_CHUNK) == (r[None, :] // GLA_CHUNK)
    lmat = jnp.asarray(same & (r[:, None] >= r[None, :]), BF16)
    umat = jnp.asarray(same & (r[:, None] <= r[None, :]), BF16)
    return {
        "win": win.astype(BF16), "qn": q_norm.reshape(1, -1), "wq": jnp.concatenate([wq_main, wq_rot], axis=1).astype(BF16),
        "kvn": kv_norm.reshape(1, -1), "wkk": wkk.astype(BF16), "wkv": wkv_ext.astype(BF16), "vbias": vbias,
        "epl": epl.astype(BF16), "wgf": gate_w(wgf, _S_GAF).astype(BF16), "bgf": bgf.reshape(1, -1),
        "wgb": gate_w(wgb, _S_GAB).astype(BF16), "bgb": bgb.reshape(1, -1), "lmat": lmat, "umat": umat,
    }


def _gqa_head_perm():
    heads = []
    for p in range(GQA_KV_HEADS // 2):
        for i in range(GQA_GROUP):
            heads += [(2 * p) * GQA_GROUP + i, (2 * p + 1) * GQA_GROUP + i]
    return np.asarray(heads)


def _prep_odd(w_in, w_out):
    perm = _gqa_head_perm()
    wq = w_in[:, :GQA_Q_W].reshape(D_MODEL, GQA_HEADS, GQA_HEAD_DIM)[:, perm].reshape(D_MODEL, GQA_Q_W)
    wk = w_in[:, GQA_Q_W:GQA_Q_W + GQA_KV_W]
    wv = w_in[:, GQA_Q_W + GQA_KV_W:].reshape(D_MODEL, GQA_KV_HEADS // 2, 2 * GQA_HEAD_DIM)
    wv_ext = jnp.concatenate([wv, jnp.zeros((D_MODEL, GQA_KV_HEADS // 2, LANES), F32)], axis=-1).reshape(D_MODEL, GQA_VEXT_W)
    win = jnp.concatenate([wq, _rot_cols(wq, GQA_HEAD_DIM // 4), wk, _rot_cols(wk, GQA_HEAD_DIM // 4), wv_ext], axis=1)
    vbias = jnp.tile(jnp.concatenate([jnp.zeros((LANES,), F32), jnp.ones((LANES,), F32)]),
                     GQA_KV_HEADS // 2).reshape(1, GQA_VEXT_W)
    wo = w_out.reshape(GQA_HEADS, GQA_HEAD_DIM, D_MODEL)[perm].reshape(GQA_Q_W, D_MODEL)
    return win.astype(BF16), vbias, wo.astype(BF16)


def _ext_v(v):
    rows = v.shape[0]
    vp = v.reshape(rows, GQA_KV_HEADS // 2, 2 * GQA_HEAD_DIM)
    return jnp.concatenate([vp, jnp.ones((rows, GQA_KV_HEADS // 2, LANES), v.dtype)], axis=-1).reshape(rows, GQA_VEXT_W)


def kernel(x_prompt, x_sample, cache_mla_ckv, cache_mla_kpe, state_gla_fwd, state_gla_bwd, cache_gqa_k, cache_gqa_v, c, c_ctx, w_mod, b_mod, norm_g, w_in_ab, mla_q_norm, mla_w_q_up, mla_kv_norm, mla_w_kv_up, gla_w_gate_f, gla_b_gate_f, gla_w_gate_b, gla_b_gate_b, gla_norm, w_out_ab, ffn_w_gate, ffn_w_up, ffn_w_down, w_in_c, gqa_sink, w_out_c, moe_w_router, moe_b_router, moe_w_gate, moe_w_up, moe_w_down):
    x_in = (x_prompt.reshape(NP_TOK, D_MODEL), x_sample.reshape(NS_TOK, D_MODEL))
    cvec =jnp.concatenate([c_ctx[None, :], c, jnp.zeros((MOD_ROWS - N_GROUPS, D_MODEL), F32)], axis=0)
    mod5 = _modulation(cvec, w_mod, b_mod).reshape(DEPTH, MOD_ROWS, N_MOD, 1, D_MODEL)
    tabs = _prep_tables()
    gvec = lambda l, j: norm_g[l, j].reshape(1, D_MODEL)

    wts = _prep_even(w_in_ab[0], mla_q_norm[0], mla_w_q_up[0], mla_kv_norm[0], mla_w_kv_up[0],
                     gla_w_gate_f[0], gla_b_gate_f[0], gla_w_gate_b[0], gla_b_gate_b[0])
    (q, k, v, ckv, kpe, gq, gk, gv, gr, bf, bb) = _even_in_proj(*x_in, mod5, 0, gvec(0, 0), wts, tabs)
    kc, vc = _cache_kv(cache_mla_ckv[:, 0].reshape(DEC_BATCH * PAST_LEN, MLA_KV_RANK),
                       cache_mla_kpe[:, 0].reshape(DEC_BATCH * PAST_LEN, MLA_ROPE), wts)
    oa_p = _mla_attention(q, [k], [v], n_batch=BATCH, seq_q=SEQ, q_tile=SEQ, tok_off=0, k_batch_rows=[SEQ])
    oa_s = _mla_attention(q, [k, kc], [v, vc], n_batch=DEC_BATCH, seq_q=DEC_SEQ, q_tile=MLA_Q_TILE,
                          tok_off=NP_TOK, k_batch_rows=[DEC_SEQ, PAST_LEN])
    gn = gla_norm[0].reshape(1, GLA_DV)
    zero_state = jnp.zeros((BATCH, GLA_HEADS, GLA_DK, GLA_DV), F32)
    ob_p, sf, sb = _gla(gq, gk, gv, gr, bf, bb, zero_state, zero_state, gn, n_batch=BATCH, seq=SEQ, tok_off=0)
    ob_s, _, _ = _gla(gq, gk, gv, gr, bf, bb, state_gla_fwd[:, 0], state_gla_bwd[:, 0], gn,
                      n_batch=DEC_BATCH, seq=DEC_SEQ, tok_off=NP_TOK)
    x, h = _out_proj([(oa_p, oa_s), (ob_p, ob_s)], w_out_ab[0].astype(BF16), x_in, mod5, 0, gvec(0, 1),
                     gvec(0, 2), BF16)
    x = _ffn(h, x, ffn_w_gate[0], ffn_w_up[0], ffn_w_down[0], mod5, 0, gvec(0, 3))

    win_c, vbias_c, wo_c = _prep_odd(w_in_c[0], w_out_c[0])
    qg, kg, kgb, vg, vgb = _odd_in_proj(x, mod5, 1, gvec(1, 0), win_c, vbias_c, tabs["gqa_c"], tabs["gqa_s"])
    sink = gqa_sink[0]
    og_p = _gqa_attention(sink, qg, None, None, kgb, vgb, n_batch=BATCH, seq_q=SEQ, q_tile=SEQ, tok_off=0, n_ctx=SEQ, local=False)
    kc_g = cache_gqa_k[:, 0].reshape(DEC_BATCH * PAST_LEN, GQA_KV_W).astype(BF16)
    vc_g = _ext_v(cache_gqa_v[:, 0].reshape(DEC_BATCH * PAST_LEN, GQA_KV_W)).astype(BF16)
    og_s = _gqa_attention(sink, qg, kgb, vgb, kc_g, vc_g, n_batch=DEC_BATCH, seq_q=DEC_SEQ, q_tile=GQA_Q_TILE,
                          tok_off=NP_TOK, n_ctx=PAST_LEN, local=True)
    x, h = _out_proj([(og_p, og_s)], wo_c, x, mod5, 1, gvec(1, 1), gvec(1, 2), F32)
    w_r = jnp.zeros((D_MODEL, LANES), F32).at[:, :N_EXPERTS].set(moe_w_router[0]).astype(BF16)
    b_r = jnp.zeros((1, LANES), F32).at[0, :N_EXPERTS].set(moe_b_router[0])
    y_p, y_s = _moe(h, x, w_r, b_r, moe_w_gate[0], moe_w_up[0], moe_w_down[0], mod5, 1, gvec(1, 3))

    y_prompt = y_p.reshape(BATCH, SEQ, D_MODEL)
    y_sample = y_s.reshape(DEC_BATCH, DEC_SEQ, D_MODEL)
    new_ckv = ckv[:NP_TOK].reshape(BATCH, 1, SEQ, MLA_KV_RANK)
    new_kpe = kpe[:NP_TOK].reshape(BATCH, 1, SEQ, MLA_ROPE)
    new_k = kg[:NP_TOK].reshape(BATCH, 1, SEQ, GQA_KV_HEADS, GQA_HEAD_DIM)
    new_v = vg[:NP_TOK].reshape(BATCH, 1, SEQ, GQA_KV_HEADS, GQA_HEAD_DIM)
    return (y_prompt, y_sample, new_ckv, new_kpe, sf[:, None], sb[:, None], new_k, new_v)
```

```python
import functools

import jax
import jax.numpy as jnp
import numpy as np
from jax import lax
from jax.experimental import pallas as pl
from jax.experimental.pallas import tpu as pltpu
from jax.experimental.pallas import tpu_sc as plsc

F32 = jnp.float32
BF16 = jnp.bfloat16

D_MODEL = 1024
BATCH = 16
SEQ = 256
DEPTH = 2
DEC_BATCH = 4
DEC_SEQ = 4096
PAST_LEN = 256
GRID_W = 64
N_MOD = 6
EPS = 1e-6
ROPE_BASE = 10000.0
NEG_INF = -1e30

MLA_HEADS = 8
MLA_NOPE = 64
MLA_ROPE = 32
MLA_V = 64
MLA_Q_RANK = 384
MLA_KV_RANK = 256
GLA_HEADS = 4
GLA_DK = 64
GLA_DV = 128
GLA_GATE_RANK = 16
GLA_GATE_NORM = 16.0
GLA_CHUNK = 64
GQA_HEADS = 16
GQA_KV_HEADS = 4
GQA_GROUP = GQA_HEADS // GQA_KV_HEADS
GQA_HEAD_DIM = 64
WINDOW = 128
D_FF = 2816
N_EXPERTS = 8
TOP_K = 2

NP_TOK = BATCH * SEQ
NS_TOK = DEC_BATCH * DEC_SEQ
N_TOK = NP_TOK + NS_TOK
N_GROUPS = 1 + DEC_BATCH
MOD_ROWS = 8

LANES = 128
MXU_COLS = 256
VMEM_LIMIT_BYTES = 56 * 1024 * 1024

TOK_TILE = 512
CUMSUM_BLOCK = 256
MLA_Q_TILE = 512
MLA_Q_SUB = 256
GQA_Q_TILE = 256
MOE_ROW_TILE = 512
MOE_ROWS = TOP_K * N_TOK + N_EXPERTS * MOE_ROW_TILE
SC_CORES = 2
SC_SUBCORES = 16
SC_WORKERS = SC_CORES * SC_SUBCORES
SC_INDEX_BLOCK = 128
SC_GATHER_WINDOW = 32
FF_CHUNK = 1408

_C_CQ = 0
_C_CKV = _C_CQ + MLA_Q_RANK
_C_GQ = _C_CKV + MLA_KV_RANK
_C_GK = _C_GQ + GLA_HEADS * GLA_DK
_C_GV = _C_GK + GLA_HEADS * GLA_DK
_C_GR = _C_GV + GLA_HEADS * GLA_DV
_C_SMALL = _C_GR + GLA_HEADS * GLA_DV
IN_AB_EXT = _C_SMALL + LANES
_S_KPE, _S_KPER, _S_GAF, _S_GAB = 0, MLA_ROPE, 2 * MLA_ROPE, 2 * MLA_ROPE + GLA_GATE_RANK
MLA_HEAD_PAD = LANES
MLA_QK_W = MLA_HEADS * MLA_HEAD_PAD
MLA_VEXT_W = (MLA_HEADS // 2) * 2 * LANES


def _cparams(semantics):
    return pltpu.CompilerParams(dimension_semantics=semantics, vmem_limit_bytes=VMEM_LIMIT_BYTES)


def _const_spec(shape):
    nd = len(shape)
    return pl.BlockSpec(shape, lambda *_: (0,) * nd, pipeline_mode=pl.Buffered(1))


def _log_sigmoid(x):
    return jnp.minimum(x, 0.0) - jnp.log1p(jnp.exp(-jnp.abs(x)))


def _rms(x, g):
    return (x * lax.rsqrt(jnp.mean(x * x, axis=-1, keepdims=True) + EPS)) * g


def _modulate(x, g, shift, scale):
    return _rms(x, g) * (1.0 + scale) + shift


def _dot(a, b):
    return jnp.dot(a, b, preferred_element_type=F32)


def _dot_nt(a, b):
    return lax.dot_general(a, b, (((1,), (1,)), ((), ())), preferred_element_type=F32)


def _dot_tn(a, b):
    return lax.dot_general(a, b, (((0,), (0,)), ((), ())), preferred_element_type=F32)


def _split3(x):
    hi = x.astype(BF16)
    r1 = x - hi.astype(F32)
    mid = r1.astype(BF16)
    lo = (r1 - mid.astype(F32)).astype(BF16)
    return hi, mid, lo


def _lane_tile(x, reps):
    return jnp.concatenate([x] * reps, axis=1)


def _mod_kernel(c_ref, w_ref, b_ref, o_ref):
    c = c_ref[...]
    s = c * jax.nn.sigmoid(c)
    o_ref[...] = _dot(s.astype(BF16), w_ref[...].astype(BF16)) + b_ref[...]


def _modulation(cvec, w_mod, b_mod):
    ncol = N_MOD * D_MODEL
    blk = 1536
    return pl.pallas_call(
        _mod_kernel,
        out_shape=jax.ShapeDtypeStruct((DEPTH, MOD_ROWS, ncol), F32),
        grid=(DEPTH, ncol // blk),
        in_specs=[
            pl.BlockSpec((MOD_ROWS, D_MODEL), lambda l, j: (0, 0)),
            pl.BlockSpec((None, D_MODEL, blk), lambda l, j: (l, 0, j)),
            pl.BlockSpec((None, 1, blk), lambda l, j: (l, 0, j)),
        ],
        out_specs=pl.BlockSpec((None, MOD_ROWS, blk), lambda l, j: (l, 0, j)),
        compiler_params=_cparams(("arbitrary", "arbitrary")),
        name="modulation",
    )(cvec, w_mod, b_mod.reshape(DEPTH, 1, ncol))


def _mod_spec(layer, j, tile):
    tpg = NP_TOK // tile
    return pl.BlockSpec((None, None, None, 1, D_MODEL), lambda i: (layer, i // tpg, j, 0, 0))


def _tok_spec(tile, width):
    return pl.BlockSpec((tile, width), lambda i: (i, 0))


def _split_specs(tile, width):
    npt = NP_TOK // tile
    return [pl.BlockSpec((tile, width), lambda i: (jnp.minimum(i, npt - 1), 0)),
            pl.BlockSpec((tile, width), lambda i: (jnp.maximum(i - npt, 0), 0))]


def _pick(tile, p_ref, s_ref):
    return jnp.where(pl.program_id(0) < NP_TOK // tile, p_ref[...], s_ref[...])


def _rope_row_spec(tile, width):
    npt = NP_TOK // tile
    spt = DEC_SEQ // tile
    return pl.BlockSpec((tile, width), lambda i: (jnp.where(i < npt, 0, 1 + (i - npt) % spt), 0))


def _even_in_kernel(xp_ref, xs_ref, g_ref, shift_ref, scale_ref, win_ref, qn_ref, wq_ref, kvn_ref, wkk_ref,
                    wkv_ref, vbias_ref, epl_ref, wgf_ref, bgf_ref, wgb_ref, bgb_ref, lmat_ref,
                    umat_ref, cq_ref, sq_ref, ck_ref, sk_ref,
                    q_out, k_out, v_out, ckv_out, kpe_out, gq_out, gk_out, gv_out, gr_out,
                    bf_out, bb_out):
    h = _modulate(_pick(TOK_TILE, xp_ref, xs_ref), g_ref[...], shift_ref[...], scale_ref[...])
    z = _dot(h.astype(BF16), win_ref[...])

    cqn = _rms(z[:, _C_CQ:_C_CQ + MLA_Q_RANK], qn_ref[...]).astype(BF16)
    qf = _dot(cqn, wq_ref[...])
    cq_t = _lane_tile(cq_ref[...], MLA_HEADS)
    sq_t = _lane_tile(sq_ref[...], MLA_HEADS)
    q_out[...] = (qf[:, :MLA_QK_W] * cq_t + qf[:, MLA_QK_W:] * sq_t).astype(BF16)

    ckvn = _rms(z[:, _C_CKV:_C_CKV + MLA_KV_RANK], kvn_ref[...])
    ckv_out[...] = ckvn
    small = z[:, _C_SMALL:_C_SMALL + LANES]
    kpe = (small[:, _S_KPE:_S_KPE + MLA_ROPE] * ck_ref[...]
           + small[:, _S_KPER:_S_KPER + MLA_ROPE] * sk_ref[...])
    kpe_out[...] = kpe
    ckvn_b = ckvn.astype(BF16)
    k_out[...] = (_dot(ckvn_b, wkk_ref[...]) + _dot(kpe.astype(BF16), epl_ref[...])).astype(BF16)
    v_out[...] = (_dot(ckvn_b, wkv_ref[...]) + vbias_ref[...]).astype(BF16)

    gq_out[...] = z[:, _C_GQ:_C_GQ + GLA_HEADS * GLA_DK] * (GLA_DK ** -0.5)
    gk_out[...] = z[:, _C_GK:_C_GK + GLA_HEADS * GLA_DK]
    gv_out[...] = z[:, _C_GV:_C_GV + GLA_HEADS * GLA_DV].astype(BF16)
    gr_out[...] = z[:, _C_GR:_C_GR + GLA_HEADS * GLA_DV]

    small_b = small.astype(BF16)
    la_f = _log_sigmoid(_dot(small_b, wgf_ref[...]) + bgf_ref[...]) * (1.0 / GLA_GATE_NORM)
    la_b = _log_sigmoid(_dot(small_b, wgb_ref[...]) + bgb_ref[...]) * (1.0 / GLA_GATE_NORM)
    lmat = lmat_ref[...]
    umat = umat_ref[...]
    for r in range(TOK_TILE // CUMSUM_BLOCK):
        rows = slice(r * CUMSUM_BLOCK, (r + 1) * CUMSUM_BLOCK)
        f_hi, f_mid, f_lo = _split3(la_f[rows])
        bf_out[rows, :] = _dot(lmat, f_hi) + _dot(lmat, f_mid) + _dot(lmat, f_lo)
        b_hi, b_mid, b_lo = _split3(la_b[rows])
        bb_out[rows, :] = _dot(umat, b_hi) + _dot(umat, b_mid) + _dot(umat, b_lo)


def _even_in_proj(xp, xs, mod5, layer, g, wts, tabs):
    t = TOK_TILE
    out_widths = [(MLA_QK_W, BF16), (MLA_QK_W, BF16), (MLA_VEXT_W, BF16), (MLA_KV_RANK, F32),
                  (MLA_ROPE, F32), (GLA_HEADS * GLA_DK, F32), (GLA_HEADS * GLA_DK, F32),
                  (GLA_HEADS * GLA_DV, BF16), (GLA_HEADS * GLA_DV, F32),
                  (GLA_HEADS * GLA_DK, F32), (GLA_HEADS * GLA_DK, F32)]
    const_names = ["win", "qn", "wq", "kvn", "wkk", "wkv", "vbias", "epl", "wgf", "bgf", "wgb",
                   "bgb", "lmat", "umat"]
    consts = [wts[n] for n in const_names]
    in_specs = (_split_specs(t, D_MODEL)
                + [_const_spec((1, D_MODEL)), _mod_spec(layer, 0, t), _mod_spec(layer, 1, t)]
                + [_const_spec(c.shape) for c in consts]
                + [_rope_row_spec(t, LANES), _rope_row_spec(t, LANES),
                   _rope_row_spec(t, MLA_ROPE), _rope_row_spec(t, MLA_ROPE)])
    return pl.pallas_call(
        _even_in_kernel,
        out_shape=[jax.ShapeDtypeStruct((N_TOK, w), dt) for w, dt in out_widths],
        grid=(N_TOK // t,),
        in_specs=in_specs,
        out_specs=[_tok_spec(t, w) for w, _ in out_widths],
        compiler_params=_cparams(("parallel",)),
        name="even_in_proj",
    )(xp, xs, g, mod5, mod5, *consts, tabs["mla_cq"], tabs["mla_sq"], tabs["mla_ck"], tabs["mla_sk"])


def _cache_kv_kernel(ckv_ref, kpe_ref, wkk_ref, wkv_ref, vbias_ref, epl_ref, k_out, v_out):
    ckv_b = ckv_ref[...].astype(BF16)
    k_out[...] = (_dot(ckv_b, wkk_ref[...]) + _dot(kpe_ref[...].astype(BF16), epl_ref[...])).astype(BF16)
    v_out[...] = (_dot(ckv_b, wkv_ref[...]) + vbias_ref[...]).astype(BF16)


def _cache_kv(ckv, kpe, wts):
    n = ckv.shape[0]
    consts = [wts[k] for k in ("wkk", "wkv", "vbias", "epl")]
    return pl.pallas_call(
        _cache_kv_kernel,
        out_shape=[jax.ShapeDtypeStruct((n, MLA_QK_W), BF16), jax.ShapeDtypeStruct((n, MLA_VEXT_W), BF16)],
        grid=(1,),
        in_specs=[_const_spec(ckv.shape), _const_spec(kpe.shape)] + [_const_spec(c.shape) for c in consts],
        out_specs=[_const_spec((n, MLA_QK_W)), _const_spec((n, MLA_VEXT_W))],
        compiler_params=_cparams(("arbitrary",)),
        name="mla_cache_kv",
    )(ckv, kpe, *consts)


def _mla_attn_kernel(*refs, n_seg):
    q_ref = refs[0]
    k_refs = refs[1:1 + n_seg]
    v_refs = refs[1 + n_seg:1 + 2 * n_seg]
    o_ref, s_scr = refs[1 + 2 * n_seg:]
    scale = (MLA_NOPE + MLA_ROPE) ** -0.5
    c = scale * float(np.log2(np.e))
    tq = q_ref.shape[0]
    q_sub = min(tq, MLA_Q_SUB)
    kt = MXU_COLS
    tiles = []
    for si, k in enumerate(k_refs):
        for r0 in range(0, k.shape[0], kt):
            tiles.append((si, r0, len(tiles) * kt))
    lane = lax.broadcasted_iota(jnp.int32, (q_sub, LANES), 1)
    for qs in range(tq // q_sub):
        rows = slice(qs * q_sub, (qs + 1) * q_sub)
        res = []
        for j in range(2):
            hl = slice(j * LANES, (j + 1) * LANES)
            qj = q_ref[rows, hl]
            macc = None
            for si, r0, c0 in tiles:
                s = _dot_nt(qj, k_refs[si][r0:r0 + kt, hl])
                s_scr[j, rows, c0:c0 + kt] = s
                mt = jnp.maximum(s[:, :LANES], s[:, LANES:])
                macc = mt if macc is None else jnp.maximum(macc, mt)
            m = macc.max(axis=-1, keepdims=True)
            r = None
            for si, r0, c0 in tiles:
                p = jnp.exp2((s_scr[j, rows, c0:c0 + kt] - m) * c).astype(BF16)
                rj = _dot(p, v_refs[si][r0:r0 + kt, :])
                r = rj if r is None else r + rj
            res.append(r[:, :LANES] / r[:, LANES:])
        o_ref[rows, :] = jnp.where(lane < MLA_V, res[0], res[1]).astype(BF16)


def _mla_attention(q, ks, vs, *, n_batch, seq_q, q_tile, tok_off, k_batch_rows):
    n_seg = len(ks)
    nq = seq_q // q_tile
    qoff = tok_off // q_tile
    grid = (n_batch, MLA_HEADS // 2, nq)
    in_specs = [pl.BlockSpec((q_tile, 2 * LANES), lambda b, hp, i: (qoff + b * nq + i, hp))]
    for s in range(n_seg):
        rows = k_batch_rows[s]
        off = (tok_off // rows) if s == 0 else 0
        in_specs.append(pl.BlockSpec((rows, 2 * LANES), functools.partial(lambda b, hp, i, off: (off + b, hp), off=off)))
    for s in range(n_seg):
        rows = k_batch_rows[s]
        off = (tok_off // rows) if s == 0 else 0
        in_specs.append(pl.BlockSpec((rows, 2 * LANES), functools.partial(lambda b, hp, i, off: (off + b, hp), off=off)))
    return pl.pallas_call(
        functools.partial(_mla_attn_kernel, n_seg=n_seg),
        out_shape=jax.ShapeDtypeStruct((n_batch * seq_q, MLA_HEADS * MLA_V), BF16),
        grid=grid,
        in_specs=in_specs,
        out_specs=pl.BlockSpec((q_tile, LANES), lambda b, hp, i: (b * nq + i, hp)),
        scratch_shapes=[pltpu.VMEM((2, q_tile, sum(k_batch_rows)), F32)],
        compiler_params=_cparams(("parallel", "parallel", "arbitrary")),
        name=f"mla_attention_{n_seg}seg",
    )(q, *ks, *vs)


def _gla_kernel(q_ref, k_ref, v_ref, gr_ref, bf_ref, bb_ref, s0f_ref, s0b_ref, gn_ref,
                o_ref, sf_ref, sb_ref, acc_ref, *, n_chunks):
    c = GLA_CHUNK
    lane = lax.broadcasted_iota(jnp.int32, (c, LANES), 1)
    lo = lane < GLA_DK
    row = lax.broadcasted_iota(jnp.int32, (c, c), 0)
    col = lax.broadcasted_iota(jnp.int32, (c, c), 1)
    tril = row >= col
    triu = row <= col
    zero_blk = jnp.zeros((GLA_DK, GLA_DV), F32)

    def pair_state_t(s_ref):
        blockdiag = jnp.concatenate(
            [jnp.concatenate([s_ref[0], zero_blk], axis=1),
             jnp.concatenate([zero_blk, s_ref[1]], axis=1)], axis=0)
        return blockdiag.T

    def chunk(ci, b_ref, st, mid_row, last_row, causal):
        rows = pl.ds(pl.multiple_of(ci * c, c), c)
        q = q_ref[rows, :]
        k = k_ref[rows, :]
        v = v_ref[rows, :]
        b = b_ref[rows, :]
        b_mid = b[mid_row:mid_row + 1, :]
        b_last = b[last_row:last_row + 1, :]
        qe = q * jnp.exp(b - b_mid)
        ke = (k * jnp.exp(b_mid - b)).astype(BF16)
        kd = (k * jnp.exp(b_last - b)).astype(BF16)
        qd = q * jnp.exp(b)
        st_b = st.astype(BF16)
        for j in range(2):
            sel = lo if j == 0 else jnp.logical_not(lo)
            a = _dot_nt(jnp.where(sel, qe, 0.0).astype(BF16), ke)
            a = jnp.where(causal, a, 0.0).astype(BF16)
            o = (_dot(a, v[:, j * GLA_DV:(j + 1) * GLA_DV])
                 + _dot_nt(jnp.where(sel, qd, 0.0).astype(BF16), st_b[j * GLA_DV:(j + 1) * GLA_DV, :]))
            acc_ref[rows, j * GLA_DV:(j + 1) * GLA_DV] += o
        return st * jnp.exp(b_last) + _dot_tn(v, kd)

    acc_ref[...] = jnp.zeros_like(acc_ref)

    def body(i, carry):
        st_f, st_b = carry
        st_f = chunk(i, bf_ref, st_f, c // 2 - 1, c - 1, tril)
        st_b = chunk(n_chunks - 1 - i, bb_ref, st_b, c // 2, 0, triu)
        return st_f, st_b

    st_f, st_b = lax.fori_loop(0, n_chunks, body, (pair_state_t(s0f_ref), pair_state_t(s0b_ref)))
    s_f = st_f.T
    s_b = st_b.T
    sf_ref[0] = s_f[:GLA_DK, :GLA_DV]
    sf_ref[1] = s_f[GLA_DK:, GLA_DV:]
    sb_ref[0] = s_b[:GLA_DK, :GLA_DV]
    sb_ref[1] = s_b[GLA_DK:, GLA_DV:]

    gn = gn_ref[...]
    for j in range(2):
        cols = slice(j * GLA_DV, (j + 1) * GLA_DV)
        gr = gr_ref[:, cols]
        o_ref[:, cols] = (_rms(acc_ref[:, cols], gn) * (gr * jax.nn.sigmoid(gr))).astype(BF16)


def _gla(gq, gk, gv, gr, bf, bb, s0f, s0b, gnorm, *, n_batch, seq, tok_off):
    n_chunks = seq // GLA_CHUNK
    boff = tok_off // seq
    hp = GLA_HEADS // 2
    tok = lambda w: pl.BlockSpec((seq, w), lambda b, p: (boff + b, p))
    st = pl.BlockSpec((None, 2, GLA_DK, GLA_DV), lambda b, p: (b, p, 0, 0))
    return pl.pallas_call(
        functools.partial(_gla_kernel, n_chunks=n_chunks),
        out_shape=[jax.ShapeDtypeStruct((n_batch * seq, GLA_HEADS * GLA_DV), BF16),
                   jax.ShapeDtypeStruct((n_batch, GLA_HEADS, GLA_DK, GLA_DV), F32),
                   jax.ShapeDtypeStruct((n_batch, GLA_HEADS, GLA_DK, GLA_DV), F32)],
        grid=(n_batch, hp),
        in_specs=[tok(2 * GLA_DK), tok(2 * GLA_DK), tok(2 * GLA_DV), tok(2 * GLA_DV),
                  tok(2 * GLA_DK), tok(2 * GLA_DK), st, st, _const_spec((1, GLA_DV))],
        out_specs=[pl.BlockSpec((seq, 2 * GLA_DV), lambda b, p: (b, p)), st, st],
        scratch_shapes=[pltpu.VMEM((seq, 2 * GLA_DV), F32)],
        compiler_params=_cparams(("parallel", "parallel")),
        name=f"gla_seq{seq}",
    )(gq, gk, gv, gr, bf, bb, s0f, s0b, gnorm)


def _out_proj_kernel(*refs, n_o, x_split):
    t = TOK_TILE
    o = [_pick(t, refs[2 * j], refs[2 * j + 1]) for j in range(n_o)]
    rest = refs[2 * n_o:]
    w_ref = rest[0]
    if x_split:
        x_in = _pick(t, rest[1], rest[2])
        rest = rest[3:]
    else:
        x_in = rest[1][...]
        rest = rest[2:]
    g1_ref, gate_ref, g2_ref, shift_ref, scale_ref, x_out, h_out = rest
    y = _dot(o[0] if n_o == 1 else jnp.concatenate(o, axis=1), w_ref[...])
    x = x_in + gate_ref[...] * _rms(y, g1_ref[...])
    x_out[...] = x
    h_out[...] = _modulate(x, g2_ref[...], shift_ref[...], scale_ref[...]).astype(h_out.dtype)


def _out_proj(os_, w, x, mod5, layer, g1, g2, h_dtype):
    t = TOK_TILE
    x_split = isinstance(x, tuple)
    in_specs, args = [], []
    for o_p, o_s in os_:
        in_specs += _split_specs(t, o_p.shape[1])
        args += [o_p, o_s]
    in_specs.append(_const_spec(w.shape))
    args.append(w)
    if x_split:
        in_specs += _split_specs(t, D_MODEL)
        args += list(x)
    else:
        in_specs.append(_tok_spec(t, D_MODEL))
        args.append(x)
    in_specs += [_const_spec((1, D_MODEL)), _mod_spec(layer, 2, t), _const_spec((1, D_MODEL)),
                 _mod_spec(layer, 3, t), _mod_spec(layer, 4, t)]
    args += [g1, mod5, g2, mod5, mod5]
    return pl.pallas_call(
        functools.partial(_out_proj_kernel, n_o=len(os_), x_split=x_split),
        out_shape=[jax.ShapeDtypeStruct((N_TOK, D_MODEL), F32), jax.ShapeDtypeStruct((N_TOK, D_MODEL), h_dtype)],
        grid=(N_TOK // t,),
        in_specs=in_specs,
        out_specs=[_tok_spec(t, D_MODEL), _tok_spec(t, D_MODEL)],
        compiler_params=_cparams(("parallel",)),
        name=f"out_proj_{len(os_)}",
    )(*args)


def _ffn_kernel(h_ref, x_ref, wg_ref, wu_ref, wd_ref, g_ref, gate_ref, x_out):
    h = h_ref[...]
    f = None
    for cidx in range(D_FF // FF_CHUNK):
        cols = slice(cidx * FF_CHUNK, (cidx + 1) * FF_CHUNK)
        a = _dot(h, wg_ref[:, cols].astype(BF16))
        u = _dot(h, wu_ref[:, cols].astype(BF16))
        fc = _dot(((a * jax.nn.sigmoid(a)) * u).astype(BF16), wd_ref[cols, :].astype(BF16))
        f = fc if f is None else f + fc
    x_out[...] = x_ref[...] + gate_ref[...] * _rms(f, g_ref[...])


def _ffn(h, x, wg, wu, wd, mod5, layer, g3):
    t = TOK_TILE
    return pl.pallas_call(
        _ffn_kernel,
        out_shape=jax.ShapeDtypeStruct((N_TOK, D_MODEL), F32),
        grid=(N_TOK // t,),
        in_specs=[_tok_spec(t, D_MODEL), _tok_spec(t, D_MODEL), _const_spec(wg.shape),
                  _const_spec(wu.shape), _const_spec(wd.shape), _const_spec((1, D_MODEL)),
                  _mod_spec(layer, 5, t)],
        out_specs=_tok_spec(t, D_MODEL),
        compiler_params=_cparams(("parallel",)),
        name="ffn_swiglu",
    )(h, x, wg, wu, wd, g3, mod5)


GQA_Q_W = GQA_HEADS * GQA_HEAD_DIM
GQA_KV_W = GQA_KV_HEADS * GQA_HEAD_DIM
GQA_VEXT_W = (GQA_KV_HEADS // 2) * 2 * LANES
_O_Q, _O_QR = 0, GQA_Q_W
_O_K, _O_KR = 2 * GQA_Q_W, 2 * GQA_Q_W + GQA_KV_W
_O_V = 2 * GQA_Q_W + 2 * GQA_KV_W
IN_C_EXT = _O_V + GQA_VEXT_W


def _odd_in_kernel(x_ref, g_ref, shift_ref, scale_ref, win_ref, vbias_ref, c_ref, s_ref,
                   q_out, k_out, kb_out, v_out, vb_out):
    h = _modulate(x_ref[...], g_ref[...], shift_ref[...], scale_ref[...])
    z = _dot(h.astype(BF16), win_ref[...])
    c_t = _lane_tile(c_ref[...], GQA_Q_W // LANES)
    s_t = _lane_tile(s_ref[...], GQA_Q_W // LANES)
    q = z[:, _O_Q:_O_Q + GQA_Q_W] * c_t + z[:, _O_QR:_O_QR + GQA_Q_W] * s_t
    q_out[...] = (q * (GQA_HEAD_DIM ** -0.5)).astype(BF16)
    k = (z[:, _O_K:_O_K + GQA_KV_W] * c_t[:, :GQA_KV_W]
         + z[:, _O_KR:_O_KR + GQA_KV_W] * s_t[:, :GQA_KV_W])
    k_out[...] = k
    kb_out[...] = k.astype(BF16)
    vext = z[:, _O_V:_O_V + GQA_VEXT_W] + vbias_ref[...]
    vb_out[...] = vext.astype(BF16)
    for p in range(GQA_KV_HEADS // 2):
        v_out[:, p * LANES:(p + 1) * LANES] = vext[:, 2 * p * LANES:(2 * p + 1) * LANES]


def _odd_in_proj(x, mod5, layer, g, win, vbias, tab_c, tab_s):
    t = TOK_TILE
    out_widths = [(GQA_Q_W, BF16), (GQA_KV_W, F32), (GQA_KV_W, BF16), (GQA_KV_W, F32), (GQA_VEXT_W, BF16)]
    return pl.pallas_call(
        _odd_in_kernel,
        out_shape=[jax.ShapeDtypeStruct((N_TOK, w), dt) for w, dt in out_widths],
        grid=(N_TOK // t,),
        in_specs=[_tok_spec(t, D_MODEL), _const_spec((1, D_MODEL)), _mod_spec(layer, 0, t),
                  _mod_spec(layer, 1, t), _const_spec(win.shape), _const_spec(vbias.shape),
                  _rope_row_spec(t, LANES), _rope_row_spec(t, LANES)],
        out_specs=[_tok_spec(t, w) for w, _ in out_widths],
        compiler_params=_cparams(("parallel",)),
        name="odd_in_proj",
    )(x, g, mod5, mod5, win, vbias, tab_c, tab_s)


def _gqa_kernel(sink_ref, q_ref, *refs, local_len):
    if local_len:
        kl_ref, vl_ref, kc_ref, vc_ref, o_ref = refs
    else:
        kc_ref, vc_ref, o_ref = refs
    tq = q_ref.shape[0]
    lane = lax.broadcasted_iota(jnp.int32, (tq, LANES), 1)
    lo = lane < GQA_HEAD_DIM
    if local_len:
        i = pl.program_id(1)
        q0 = i * tq
        seq = kl_ref.shape[0]
        kstart = pl.multiple_of(jnp.clip(q0 - WINDOW, 0, seq - local_len), LANES)
        qpos = q0 + lax.broadcasted_iota(jnp.int32, (tq, local_len), 0)
        kpos = kstart + lax.broadcasted_iota(jnp.int32, (tq, local_len), 1)
        band = jnp.abs(qpos - kpos) <= WINDOW
    for p in range(GQA_KV_HEADS // 2):
        kc = kc_ref[:, p * LANES:(p + 1) * LANES]
        vc = vc_ref[:, 2 * p * LANES:(2 * p + 2) * LANES]
        if local_len:
            kl = kl_ref[pl.ds(kstart, local_len), p * LANES:(p + 1) * LANES]
            vl = vl_ref[pl.ds(kstart, local_len), 2 * p * LANES:(2 * p + 2) * LANES]
        for blk in range(GQA_GROUP):
            cols = slice((p * GQA_GROUP + blk) * LANES, (p * GQA_GROUP + blk + 1) * LANES)
            qb = q_ref[:, cols]
            res = []
            for half in range(2):
                head = (2 * p + half) * GQA_GROUP + blk
                sink = sink_ref[head]
                qh = jnp.where(lo if half == 0 else jnp.logical_not(lo), qb, jnp.zeros_like(qb))
                s_c = _dot_nt(qh, kc)
                m = jnp.maximum(s_c.max(axis=-1, keepdims=True), sink)
                if local_len:
                    s_l = jnp.where(band, _dot_nt(qh, kl), NEG_INF)
                    m = jnp.maximum(m, s_l.max(axis=-1, keepdims=True))
                r = _dot(jnp.exp(s_c - m).astype(BF16), vc)
                if local_len:
                    r = r + _dot(jnp.exp(s_l - m).astype(BF16), vl)
                res.append(r[:, :LANES] / (r[:, LANES:] + jnp.exp(sink - m)))
            o_ref[:, cols] = jnp.where(lo, res[0], res[1]).astype(BF16)


def _gqa_attention(sink, q, k_loc, v_loc, k_ctx, v_ctx, *, n_batch, seq_q, q_tile, tok_off, n_ctx, local):
    nq = seq_q // q_tile
    qoff = tok_off // q_tile
    local_len = q_tile + 2 * WINDOW if local else 0
    in_specs = [pl.BlockSpec(memory_space=pltpu.SMEM),
                pl.BlockSpec((q_tile, GQA_Q_W), lambda b, i: (qoff + b * nq + i, 0))]
    args = [sink, q]
    if local:
        boff = tok_off // seq_q
        in_specs += [pl.BlockSpec((seq_q, GQA_KV_W), lambda b, i: (boff + b, 0)),
                     pl.BlockSpec((seq_q, GQA_VEXT_W), lambda b, i: (boff + b, 0))]
        args += [k_loc, v_loc]
    in_specs += [pl.BlockSpec((n_ctx, GQA_KV_W), lambda b, i: (b, 0)),
                 pl.BlockSpec((n_ctx, GQA_VEXT_W), lambda b, i: (b, 0))]
    args += [k_ctx, v_ctx]
    return pl.pallas_call(
        functools.partial(_gqa_kernel, local_len=local_len),
        out_shape=jax.ShapeDtypeStruct((n_batch * seq_q, GQA_Q_W), BF16),
        grid=(n_batch, nq),
        in_specs=in_specs,
        out_specs=pl.BlockSpec((q_tile, GQA_Q_W), lambda b, i: (b * nq + i, 0)),
        compiler_params=_cparams(("parallel", "arbitrary")),
        name="gqa_local" if local else "gqa_ctx",
    )(*args)


def _router_kernel(h_ref, w_ref, b_ref, ltri_ref, wsel_out, isel_out, rank_out, cnt_out, carry_ref):
    @pl.when(pl.program_id(0) == 0)
    def _():
        carry_ref[...] = jnp.zeros_like(carry_ref)

    logits = _dot(h_ref[...].astype(BF16), w_ref[...]) + b_ref[...]
    lane = lax.broadcasted_iota(jnp.int32, logits.shape, 1)
    neg = float(np.finfo(np.float32).min)
    lg = jnp.where(lane < N_EXPERTS, logits, neg)
    v1 = lg.max(axis=-1, keepdims=True)
    i1 = jnp.min(jnp.where(lg == v1, lane, LANES), axis=-1, keepdims=True)
    lg2 = jnp.where(lane == i1, neg, lg)
    v2 = lg2.max(axis=-1, keepdims=True)
    i2 = jnp.min(jnp.where(lg2 == v2, lane, LANES), axis=-1, keepdims=True)
    e2 = jnp.exp(v2 - v1)
    den = 1.0 + e2
    wsel_out[...] = jnp.where(lane == 0, 1.0 / den, jnp.where(lane == 1, e2 / den, 0.0))
    isel_out[...] = jnp.where(lane == 0, i1, jnp.where(lane == 1, i2, 0))
    hit = jnp.where(lane == i1, 1.0, jnp.where(lane == i2, 1.0, 0.0))
    carry = carry_ref[...]
    rank_out[...] = (_dot(ltri_ref[...], hit.astype(BF16)) + carry[0:1, :]).astype(jnp.int32)
    carry = carry + jnp.sum(hit, axis=0, keepdims=True)
    carry_ref[...] = carry
    cnt_out[...] = carry.astype(jnp.int32)


def _router(h, w, b):
    t = TOK_TILE
    r = np.arange(t)
    ltri = jnp.asarray(r[:, None] > r[None, :], BF16)
    return pl.pallas_call(
        _router_kernel,
        out_shape=[jax.ShapeDtypeStruct((N_TOK, LANES), F32), jax.ShapeDtypeStruct((N_TOK, LANES), jnp.int32),
                   jax.ShapeDtypeStruct((N_TOK, LANES), jnp.int32), jax.ShapeDtypeStruct((8, LANES), jnp.int32)],
        grid=(N_TOK // t,),
        in_specs=[_tok_spec(t, D_MODEL), _const_spec(w.shape), _const_spec(b.shape), _const_spec((t, t))],
        out_specs=[_tok_spec(t, LANES), _tok_spec(t, LANES), _tok_spec(t, LANES),
                   pl.BlockSpec((8, LANES), lambda i: (0, 0))],
        scratch_shapes=[pltpu.VMEM((8, LANES), F32)],
        compiler_params=_cparams(("arbitrary",)),
        name="moe_router",
    )(h, w, b, ltri)


def _route_tables(isel, rank, cnt):
    tm = MOE_ROW_TILE
    counts = cnt[0, :N_EXPERTS]
    padded = ((counts + tm - 1) // tm) * tm
    ends = jnp.cumsum(padded)
    base = ends - padded
    e_ids = jnp.arange(N_EXPERTS, dtype=jnp.int32)
    row = rank[:, :N_EXPERTS] + base[None, :]
    pos1 = jnp.sum(jnp.where(e_ids[None, :] == isel[:, 0:1], row, 0), axis=1)
    pos2 = jnp.sum(jnp.where(e_ids[None, :] == isel[:, 1:2], row, 0), axis=1)
    tile_start = jnp.arange(MOE_ROWS // tm, dtype=jnp.int32) * tm
    tile_expert = jnp.minimum(jnp.sum(tile_start[:, None] >= ends[None, :], axis=1), N_EXPERTS - 1).astype(jnp.int32)
    tile_valid = jnp.clip((base + counts)[tile_expert] - tile_start, 0, tm).astype(jnp.int32)
    tile_valid = jnp.where(tile_start < ends[-1], tile_valid, 0)
    return pos1, pos2, tile_expert, tile_valid


def _scatter_rows(rows, pos1, pos2, n_out):
    n_tok, d = rows.shape
    per_w = n_tok // SC_WORKERS
    w = SC_GATHER_WINDOW
    assert per_w * SC_WORKERS == n_tok and per_w % w == 0
    mesh = plsc.VectorSubcoreMesh(core_axis_name="core", subcore_axis_name="subcore")

    @functools.partial(
        pl.kernel, out_type=jax.ShapeDtypeStruct((n_out, d), rows.dtype), mesh=mesh,
        scratch_types=[pltpu.VMEM((w,), jnp.int32), pltpu.VMEM((w,), jnp.int32), pltpu.VMEM((w, d), rows.dtype)],
        name="sc_scatter_rows")
    def scatter(x_hbm, p1_hbm, p2_hbm, o_hbm, i1_v, i2_v, rows_v):
        wid = lax.axis_index("subcore") * SC_CORES + lax.axis_index("core")
        base = wid * per_w

        @pl.loop(0, per_w // w)
        def _(g):
            off = base + g * w
            pltpu.sync_copy(p1_hbm.at[pl.ds(off, w)], i1_v)
            pltpu.sync_copy(p2_hbm.at[pl.ds(off, w)], i2_v)
            pltpu.sync_copy(x_hbm.at[pl.ds(off, w)], rows_v)
            pltpu.sync_copy(rows_v, o_hbm.at[i1_v])
            pltpu.sync_copy(rows_v, o_hbm.at[i2_v])

    return scatter(rows, pos1, pos2)


def _gather_rows(table, idx):
    n_idx = idx.shape[0]
    d = table.shape[1]
    per_w = n_idx // SC_WORKERS
    assert per_w * SC_WORKERS == n_idx and per_w % SC_INDEX_BLOCK == 0
    mesh = plsc.VectorSubcoreMesh(core_axis_name="core", subcore_axis_name="subcore")

    @functools.partial(
        pl.kernel, out_type=jax.ShapeDtypeStruct((n_idx, d), table.dtype), mesh=mesh,
        scratch_types=[pltpu.VMEM((SC_INDEX_BLOCK,), jnp.int32), pltpu.VMEM((SC_GATHER_WINDOW, d), table.dtype)],
        name="sc_gather_rows")
    def gather(x_hbm, i_hbm, o_hbm, idx_v, rows_v):
        wid = lax.axis_index("subcore") * SC_CORES + lax.axis_index("core")
        base = wid * per_w

        @pl.loop(0, per_w // SC_INDEX_BLOCK)
        def _(g):
            off = base + g * SC_INDEX_BLOCK
            pltpu.sync_copy(i_hbm.at[pl.ds(off, SC_INDEX_BLOCK)], idx_v)
            for s in range(SC_INDEX_BLOCK // SC_GATHER_WINDOW):
                sub = pl.ds(s * SC_GATHER_WINDOW, SC_GATHER_WINDOW)
                pltpu.sync_copy(x_hbm.at[idx_v.at[sub]], rows_v)
                pltpu.sync_copy(rows_v, o_hbm.at[pl.ds(off + s * SC_GATHER_WINDOW, SC_GATHER_WINDOW)])

    return gather(table, idx)


def _expert_ffn_kernel(te_ref, nv_ref, x_ref, wg_ref, wu_ref, wd_ref, y_out):
    n_valid = nv_ref[pl.program_id(0)]

    @pl.when(n_valid > 0)
    def _():
        row = lax.broadcasted_iota(jnp.int32, x_ref.shape, 0)
        h = jnp.where(row < n_valid, x_ref[...], 0.0).astype(BF16)
        f = None
        for cidx in range(D_FF // FF_CHUNK):
            cols = slice(cidx * FF_CHUNK, (cidx + 1) * FF_CHUNK)
            a = _dot(h, wg_ref[:, cols].astype(BF16))
            u = _dot(h, wu_ref[:, cols].astype(BF16))
            fc = _dot(((a * jax.nn.sigmoid(a)) * u).astype(BF16), wd_ref[cols, :].astype(BF16))
            f = fc if f is None else f + fc
        y_out[...] = f

    @pl.when(n_valid == 0)
    def _():
        y_out[...] = jnp.zeros_like(y_out)


def _expert_ffn(xs, tile_expert, tile_valid, wg, wu, wd):
    tm = MOE_ROW_TILE
    wspec = lambda shape: pl.BlockSpec((None,) + shape, lambda j, te, nu: (te[j], 0, 0),
                                       pipeline_mode=pl.Buffered(1))
    return pl.pallas_call(
        _expert_ffn_kernel,
        out_shape=jax.ShapeDtypeStruct((MOE_ROWS, D_MODEL), F32),
        grid_spec=pltpu.PrefetchScalarGridSpec(
            num_scalar_prefetch=2,
            grid=(MOE_ROWS // tm,),
            in_specs=[pl.BlockSpec((tm, D_MODEL), lambda j, te, nu: (j, 0)),
                      wspec((D_MODEL, D_FF)), wspec((D_MODEL, D_FF)), wspec((D_FF, D_MODEL))],
            out_specs=pl.BlockSpec((tm, D_MODEL), lambda j, te, nu: (j, 0)),
        ),
        compiler_params=_cparams(("arbitrary",)),
        name="moe_expert_ffn",
    )(tile_expert, tile_valid, xs, wg, wu, wd)


def _moe_combine_kernel(y1_ref, y2_ref, wsel_ref, x_ref, g_ref, gate_ref, x_out):
    w = wsel_ref[...]
    f = w[:, 0:1] * y1_ref[...] + w[:, 1:2] * y2_ref[...]
    x_out[...] = x_ref[...] + gate_ref[...] * _rms(f, g_ref[...])


def _moe_combine(yg, wsel, x, mod5, layer, g3, *, tok_off, n_tok):
    t = TOK_TILE
    nt = N_TOK // t
    off = tok_off // t
    tpg = NP_TOK // t
    tok = lambda w, shift: pl.BlockSpec((t, w), lambda i: (off + shift + i, 0))
    return pl.pallas_call(
        _moe_combine_kernel,
        out_shape=jax.ShapeDtypeStruct((n_tok, D_MODEL), F32),
        grid=(n_tok // t,),
        in_specs=[tok(D_MODEL, 0), tok(D_MODEL, nt), tok(LANES, 0), tok(D_MODEL, 0), _const_spec((1, D_MODEL)),
                  pl.BlockSpec((None, None, None, 1, D_MODEL), lambda i: (layer, (off + i) // tpg, 5, 0, 0))],
        out_specs=_tok_spec(t, D_MODEL),
        compiler_params=_cparams(("parallel",)),
        name="moe_combine",
    )(yg, yg, wsel, x, g3, mod5)


def _moe(h, x, w_router, b_router, wg, wu, wd, mod5, layer, g3):
    wsel, isel, rank, cnt = _router(h, w_router, b_router)
    pos1, pos2, tile_expert, tile_valid = _route_tables(isel, rank, cnt)
    xs = _scatter_rows(h, pos1, pos2, MOE_ROWS)
    ys = _expert_ffn(xs, tile_expert, tile_valid, wg, wu, wd)
    yg = _gather_rows(ys, jnp.concatenate([pos1, pos2]))
    return (_moe_combine(yg, wsel, x, mod5, layer, g3, tok_off=0, n_tok=NP_TOK),
            _moe_combine(yg, wsel, x, mod5, layer, g3, tok_off=NP_TOK, n_tok=NS_TOK))


def _rot_cols(w, half):
    k, n = w.shape
    wb = w.reshape(k, n // (2 * half), 2, half)
    return jnp.stack([-wb[:, :, 1], wb[:, :, 0]], axis=2).reshape(k, n)


def _axis_tables(r, pos):
    inv = np.float32(ROPE_BASE) ** (-np.arange(0, r, 2, dtype=np.float32) / np.float32(r))
    ang = pos.astype(np.float32)[:, None] * inv[None, :]
    cos, sin = np.cos(ang), np.sin(ang)
    return np.concatenate([cos, cos], axis=1), np.concatenate([sin, sin], axis=1)


def _rope_tables(r):
    s = np.arange(DEC_SEQ)
    cr, sr = _axis_tables(r // 2, s // GRID_W)
    cc, sc = _axis_tables(r // 2, s % GRID_W)
    return np.concatenate([cr, cc], axis=1), np.concatenate([sr, sc], axis=1)


def _with_identity(tab, ident):
    return np.concatenate([np.full((TOK_TILE, tab.shape[1]), ident, np.float32), tab], axis=0)


@functools.lru_cache(maxsize=None)
def _rope_constants():
    c32, s32 = _rope_tables(MLA_ROPE)
    ones = np.ones((DEC_SEQ, MLA_NOPE), np.float32)
    pad1 = np.ones((DEC_SEQ, MLA_HEAD_PAD - MLA_NOPE - MLA_ROPE), np.float32)
    cq = np.concatenate([ones, c32, pad1], axis=1)
    sq = np.concatenate([0 * ones, s32, 0 * pad1], axis=1)
    c64, s64 = _rope_tables(GQA_HEAD_DIM)
    return {
        "mla_cq": _with_identity(cq, 1.0), "mla_sq": _with_identity(sq, 0.0),
        "mla_ck": _with_identity(c32, 1.0), "mla_sk": _with_identity(s32, 0.0),
        "gqa_c": _with_identity(np.concatenate([c64, c64], axis=1), 1.0),
        "gqa_s": _with_identity(np.concatenate([s64, s64], axis=1), 0.0),
    }


def _prep_tables():
    return {k: jnp.asarray(v, F32) for k, v in _rope_constants().items()}


def _prep_even(w_in, q_norm, w_q_up, kv_norm, w_kv_up, wgf, bgf, wgb, bgb):
    sizes = [MLA_Q_RANK, MLA_KV_RANK, MLA_ROPE, GLA_HEADS * GLA_DK, GLA_HEADS * GLA_DK,
             GLA_HEADS * GLA_DV, GLA_HEADS * GLA_DV, GLA_GATE_RANK, GLA_GATE_RANK]
    cq, ckv, kpe, gq, gk, gv, gr, gaf, gab = jnp.split(w_in, [int(s) for s in np.cumsum(sizes)[:-1]], axis=1)
    pad = jnp.zeros((D_MODEL, LANES - 2 * MLA_ROPE - 2 * GLA_GATE_RANK), F32)
    win = jnp.concatenate([cq, ckv, gq, gk, gv, gr, kpe, _rot_cols(kpe, MLA_ROPE // 4), gaf, gab, pad], axis=1)

    wq = w_q_up.reshape(MLA_Q_RANK, MLA_HEADS, MLA_NOPE + MLA_ROPE)
    nope, pe = wq[..., :MLA_NOPE], wq[..., MLA_NOPE:]
    pe_rot = _rot_cols(pe.reshape(MLA_Q_RANK, MLA_HEADS * MLA_ROPE), MLA_ROPE // 4).reshape(pe.shape)
    zpad = jnp.zeros((MLA_Q_RANK, MLA_HEADS, MLA_HEAD_PAD - MLA_NOPE - MLA_ROPE), F32)
    wq_main = jnp.concatenate([nope, pe, zpad], axis=-1).reshape(MLA_Q_RANK, MLA_QK_W)
    wq_rot = jnp.concatenate([0 * nope, pe_rot, zpad], axis=-1).reshape(MLA_Q_RANK, MLA_QK_W)

    wkv = w_kv_up.reshape(MLA_KV_RANK, MLA_HEADS, MLA_NOPE + MLA_V)
    knope, vv = wkv[..., :MLA_NOPE], wkv[..., MLA_NOPE:]
    wkk = jnp.concatenate([knope, jnp.zeros((MLA_KV_RANK, MLA_HEADS, MLA_HEAD_PAD - MLA_NOPE), F32)],
                          axis=-1).reshape(MLA_KV_RANK, MLA_QK_W)
    vpair = vv.reshape(MLA_KV_RANK, MLA_HEADS // 2, 2 * MLA_V)
    wkv_ext = jnp.concatenate([vpair, jnp.zeros((MLA_KV_RANK, MLA_HEADS // 2, LANES), F32)],
                              axis=-1).reshape(MLA_KV_RANK, MLA_VEXT_W)
    vbias = jnp.tile(jnp.concatenate([jnp.zeros((LANES,), F32), jnp.ones((LANES,), F32)]),
                     MLA_HEADS // 2).reshape(1, MLA_VEXT_W)
    epl = jnp.tile(jnp.concatenate([jnp.zeros((MLA_ROPE, MLA_NOPE), F32), jnp.eye(MLA_ROPE, dtype=F32),
                                    jnp.zeros((MLA_ROPE, MLA_HEAD_PAD - MLA_NOPE - MLA_ROPE), F32)], axis=1),
                   (1, MLA_HEADS))

    def gate_w(w, off):
        return jnp.zeros((LANES, GLA_HEADS * GLA_DK), F32).at[off:off + GLA_GATE_RANK].set(w)

    r = np.arange(CUMSUM_BLOCK)
    same = (r[:, None] // GLA_CHUNK) == (r[None, :] // GLA_CHUNK)
    lmat = jnp.asarray(same & (r[:, None] >= r[None, :]), BF16)
    umat = jnp.asarray(same & (r[:, None] <= r[None, :]), BF16)
    return {
        "win": win.astype(BF16), "qn": q_norm.reshape(1, -1), "wq": jnp.concatenate([wq_main, wq_rot], axis=1).astype(BF16),
        "kvn": kv_norm.reshape(1, -1), "wkk": wkk.astype(BF16), "wkv": wkv_ext.astype(BF16), "vbias": vbias,
        "epl": epl.astype(BF16), "wgf": gate_w(wgf, _S_GAF).astype(BF16), "bgf": bgf.reshape(1, -1),
        "wgb": gate_w(wgb, _S_GAB).astype(BF16), "bgb": bgb.reshape(1, -1), "lmat": lmat, "umat": umat,
    }


def _gqa_head_perm():
    heads = []
    for p in range(GQA_KV_HEADS // 2):
        for i in range(GQA_GROUP):
            heads += [(2 * p) * GQA_GROUP + i, (2 * p + 1) * GQA_GROUP + i]
    return np.asarray(heads)


def _prep_odd(w_in, w_out):
    perm = _gqa_head_perm()
    wq = w_in[:, :GQA_Q_W].reshape(D_MODEL, GQA_HEADS, GQA_HEAD_DIM)[:, perm].reshape(D_MODEL, GQA_Q_W)
    wk = w_in[:, GQA_Q_W:GQA_Q_W + GQA_KV_W]
    wv = w_in[:, GQA_Q_W + GQA_KV_W:].reshape(D_MODEL, GQA_KV_HEADS // 2, 2 * GQA_HEAD_DIM)
    wv_ext = jnp.concatenate([wv, jnp.zeros((D_MODEL, GQA_KV_HEADS // 2, LANES), F32)], axis=-1).reshape(D_MODEL, GQA_VEXT_W)
    win = jnp.concatenate([wq, _rot_cols(wq, GQA_HEAD_DIM // 4), wk, _rot_cols(wk, GQA_HEAD_DIM // 4), wv_ext], axis=1)
    vbias = jnp.tile(jnp.concatenate([jnp.zeros((LANES,), F32), jnp.ones((LANES,), F32)]),
                     GQA_KV_HEADS // 2).reshape(1, GQA_VEXT_W)
    wo = w_out.reshape(GQA_HEADS, GQA_HEAD_DIM, D_MODEL)[perm].reshape(GQA_Q_W, D_MODEL)
    return win.astype(BF16), vbias, wo.astype(BF16)


def _ext_v(v):
    rows = v.shape[0]
    vp = v.reshape(rows, GQA_KV_HEADS // 2, 2 * GQA_HEAD_DIM)
    return jnp.concatenate([vp, jnp.ones((rows, GQA_KV_HEADS // 2, LANES), v.dtype)], axis=-1).reshape(rows, GQA_VEXT_W)


def kernel(x_prompt, x_sample, cache_mla_ckv, cache_mla_kpe, state_gla_fwd, state_gla_bwd, cache_gqa_k, cache_gqa_v, c, c_ctx, w_mod, b_mod, norm_g, w_in_ab, mla_q_norm, mla_w_q_up, mla_kv_norm, mla_w_kv_up, gla_w_gate_f, gla_b_gate_f, gla_w_gate_b, gla_b_gate_b, gla_norm, w_out_ab, ffn_w_gate, ffn_w_up, ffn_w_down, w_in_c, gqa_sink, w_out_c, moe_w_router, moe_b_router, moe_w_gate, moe_w_up, moe_w_down):
    x_in = (x_prompt.reshape(NP_TOK, D_MODEL), x_sample.reshape(NS_TOK, D_MODEL))
    cvec =jnp.concatenate([c_ctx[None, :], c, jnp.zeros((MOD_ROWS - N_GROUPS, D_MODEL), F32)], axis=0)
    mod5 = _modulation(cvec, w_mod, b_mod).reshape(DEPTH, MOD_ROWS, N_MOD, 1, D_MODEL)
    tabs = _prep_tables()
    gvec = lambda l, j: norm_g[l, j].reshape(1, D_MODEL)

    wts = _prep_even(w_in_ab[0], mla_q_norm[0], mla_w_q_up[0], mla_kv_norm[0], mla_w_kv_up[0],
                     gla_w_gate_f[0], gla_b_gate_f[0], gla_w_gate_b[0], gla_b_gate_b[0])
    (q, k, v, ckv, kpe, gq, gk, gv, gr, bf, bb) = _even_in_proj(*x_in, mod5, 0, gvec(0, 0), wts, tabs)
    kc, vc = _cache_kv(cache_mla_ckv[:, 0].reshape(DEC_BATCH * PAST_LEN, MLA_KV_RANK),
                       cache_mla_kpe[:, 0].reshape(DEC_BATCH * PAST_LEN, MLA_ROPE), wts)
    oa_p = _mla_attention(q, [k], [v], n_batch=BATCH, seq_q=SEQ, q_tile=SEQ, tok_off=0, k_batch_rows=[SEQ])
    oa_s = _mla_attention(q, [k, kc], [v, vc], n_batch=DEC_BATCH, seq_q=DEC_SEQ, q_tile=MLA_Q_TILE,
                          tok_off=NP_TOK, k_batch_rows=[DEC_SEQ, PAST_LEN])
    gn = gla_norm[0].reshape(1, GLA_DV)
    zero_state = jnp.zeros((BATCH, GLA_HEADS, GLA_DK, GLA_DV), F32)
    ob_p, sf, sb = _gla(gq, gk, gv, gr, bf, bb, zero_state, zero_state, gn, n_batch=BATCH, seq=SEQ, tok_off=0)
    ob_s, _, _ = _gla(gq, gk, gv, gr, bf, bb, state_gla_fwd[:, 0], state_gla_bwd[:, 0], gn,
                      n_batch=DEC_BATCH, seq=DEC_SEQ, tok_off=NP_TOK)
    x, h = _out_proj([(oa_p, oa_s), (ob_p, ob_s)], w_out_ab[0].astype(BF16), x_in, mod5, 0, gvec(0, 1),
                     gvec(0, 2), BF16)
    x = _ffn(h, x, ffn_w_gate[0], ffn_w_up[0], ffn_w_down[0], mod5, 0, gvec(0, 3))

    win_c, vbias_c, wo_c = _prep_odd(w_in_c[0], w_out_c[0])
    qg, kg, kgb, vg, vgb = _odd_in_proj(x, mod5, 1, gvec(1, 0), win_c, vbias_c, tabs["gqa_c"], tabs["gqa_s"])
    sink = gqa_sink[0]
    og_p = _gqa_attention(sink, qg, None, None, kgb, vgb, n_batch=BATCH, seq_q=SEQ, q_tile=SEQ, tok_off=0, n_ctx=SEQ, local=False)
    kc_g = cache_gqa_k[:, 0].reshape(DEC_BATCH * PAST_LEN, GQA_KV_W).astype(BF16)
    vc_g = _ext_v(cache_gqa_v[:, 0].reshape(DEC_BATCH * PAST_LEN, GQA_KV_W)).astype(BF16)
    og_s = _gqa_attention(sink, qg, kgb, vgb, kc_g, vc_g, n_batch=DEC_BATCH, seq_q=DEC_SEQ, q_tile=GQA_Q_TILE,
                          tok_off=NP_TOK, n_ctx=PAST_LEN, local=True)
    x, h = _out_proj([(og_p, og_s)], wo_c, x, mod5, 1, gvec(1, 1), gvec(1, 2), F32)
    w_r = jnp.zeros((D_MODEL, LANES), F32).at[:, :N_EXPERTS].set(moe_w_router[0]).astype(BF16)
    b_r = jnp.zeros((1, LANES), F32).at[0, :N_EXPERTS].set(moe_b_router[0])
    y_p, y_s = _moe(h, x, w_r, b_r, moe_w_gate[0], moe_w_up[0], moe_w_down[0], mod5, 1, gvec(1, 3))

    y_prompt = y_p.reshape(BATCH, SEQ, D_MODEL)
    y_sample = y_s.reshape(DEC_BATCH, DEC_SEQ, D_MODEL)
    new_ckv = ckv[:NP_TOK].reshape(BATCH, 1, SEQ, MLA_KV_RANK)
    new_kpe = kpe[:NP_TOK].reshape(BATCH, 1, SEQ, MLA_ROPE)
    new_k = kg[:NP_TOK].reshape(BATCH, 1, SEQ, GQA_KV_HEADS, GQA_HEAD_DIM)
    new_v = vg[:NP_TOK].reshape(BATCH, 1, SEQ, GQA_KV_HEADS, GQA_HEAD_DIM)
    return (y_prompt, y_sample, new_ckv, new_kpe, sf[:, None], sb[:, None], new_k, new_v)
```

```python
import functools

import jax
import jax.numpy as jnp
import numpy as np
from jax import lax
from jax.experimental import pallas as pl
from jax.experimental.pallas import tpu as pltpu
from jax.experimental.pallas import tpu_sc as plsc

F32 = jnp.float32
BF16 = jnp.bfloat16

D_MODEL = 1024
BATCH = 16
SEQ = 256
DEPTH = 2
DEC_BATCH = 4
DEC_SEQ = 4096
PAST_LEN = 256
GRID_W = 64
N_MOD = 6
EPS = 1e-6
ROPE_BASE = 10000.0
NEG_INF = -1e30

MLA_HEADS = 8
MLA_NOPE = 64
MLA_ROPE = 32
MLA_V = 64
MLA_Q_RANK = 384
MLA_KV_RANK = 256
GLA_HEADS = 4
GLA_DK = 64
GLA_DV = 128
GLA_GATE_RANK = 16
GLA_GATE_NORM = 16.0
GLA_CHUNK = 64
GQA_HEADS = 16
GQA_KV_HEADS = 4
GQA_GROUP = GQA_HEADS // GQA_KV_HEADS
GQA_HEAD_DIM = 64
WINDOW = 128
D_FF = 2816
N_EXPERTS = 8
TOP_K = 2

NP_TOK = BATCH * SEQ
NS_TOK = DEC_BATCH * DEC_SEQ
N_TOK = NP_TOK + NS_TOK
N_GROUPS = 1 + DEC_BATCH
MOD_ROWS = 8

LANES = 128
MXU_COLS = 256
VMEM_LIMIT_BYTES = 56 * 1024 * 1024

TOK_TILE = 512
CUMSUM_BLOCK = 256
GLA_BLOCK_CHUNKS = MXU_COLS // GLA_CHUNK
GLA_CHUNK_UNROLL = 4
MLA_Q_TILE = 1024
MLA_Q_SUB = 256
GQA_Q_TILE = 256
MOE_ROW_TILE = 512
MOE_ROWS = TOP_K * N_TOK + N_EXPERTS * MOE_ROW_TILE
SC_CORES = 2
SC_SUBCORES = 16
SC_WORKERS = SC_CORES * SC_SUBCORES
SC_INDEX_BLOCK = 128
SC_GATHER_WINDOW = 32
FF_CHUNK = 1408

_C_CQ = 0
_C_CKV = _C_CQ + MLA_Q_RANK
_C_GQ = _C_CKV + MLA_KV_RANK
_C_GK = _C_GQ + GLA_HEADS * GLA_DK
_C_GV = _C_GK + GLA_HEADS * GLA_DK
_C_GR = _C_GV + GLA_HEADS * GLA_DV
_C_SMALL = _C_GR + GLA_HEADS * GLA_DV
IN_AB_EXT = _C_SMALL + LANES
_S_KPE, _S_KPER, _S_GAF, _S_GAB = 0, MLA_ROPE, 2 * MLA_ROPE, 2 * MLA_ROPE + GLA_GATE_RANK
MLA_HEAD_PAD = LANES
MLA_QK_W = MLA_HEADS * MLA_HEAD_PAD
MLA_VEXT_W = (MLA_HEADS // 2) * 2 * LANES


def _cparams(semantics):
    return pltpu.CompilerParams(dimension_semantics=semantics, vmem_limit_bytes=VMEM_LIMIT_BYTES)


def _const_spec(shape):
    nd = len(shape)
    return pl.BlockSpec(shape, lambda *_: (0,) * nd, pipeline_mode=pl.Buffered(1))


def _log_sigmoid(x):
    return jnp.minimum(x, 0.0) - jnp.log1p(jnp.exp(-jnp.abs(x)))


def _rms(x, g):
    return (x * lax.rsqrt(jnp.mean(x * x, axis=-1, keepdims=True) + EPS)) * g


def _modulate(x, g, shift, scale):
    return _rms(x, g) * (1.0 + scale) + shift


def _dot(a, b):
    return jnp.dot(a, b, preferred_element_type=F32)


def _dot_nt(a, b):
    return lax.dot_general(a, b, (((1,), (1,)), ((), ())), preferred_element_type=F32)


def _dot_tn(a, b):
    return lax.dot_general(a, b, (((0,), (0,)), ((), ())), preferred_element_type=F32)


def _split3(x):
    hi = x.astype(BF16)
    r1 = x - hi.astype(F32)
    mid = r1.astype(BF16)
    lo = (r1 - mid.astype(F32)).astype(BF16)
    return hi, mid, lo


def _lane_tile(x, reps):
    return jnp.concatenate([x] * reps, axis=1)


def _mod_kernel(c_ref, w_ref, b_ref, o_ref):
    c = c_ref[...]
    s = c * jax.nn.sigmoid(c)
    o_ref[...] = _dot(s.astype(BF16), w_ref[...].astype(BF16)) + b_ref[...]


def _modulation(cvec, w_mod, b_mod):
    ncol = N_MOD * D_MODEL
    blk = 1536
    return pl.pallas_call(
        _mod_kernel,
        out_shape=jax.ShapeDtypeStruct((DEPTH, MOD_ROWS, ncol), F32),
        grid=(DEPTH, ncol // blk),
        in_specs=[
            pl.BlockSpec((MOD_ROWS, D_MODEL), lambda l, j: (0, 0)),
            pl.BlockSpec((None, D_MODEL, blk), lambda l, j: (l, 0, j)),
            pl.BlockSpec((None, 1, blk), lambda l, j: (l, 0, j)),
        ],
        out_specs=pl.BlockSpec((None, MOD_ROWS, blk), lambda l, j: (l, 0, j)),
        compiler_params=_cparams(("arbitrary", "arbitrary")),
        name="modulation",
    )(cvec, w_mod, b_mod.reshape(DEPTH, 1, ncol))


def _mod_spec(layer, j, tile):
    tpg = NP_TOK // tile
    return pl.BlockSpec((None, None, None, 1, D_MODEL), lambda i: (layer, i // tpg, j, 0, 0))


def _tok_spec(tile, width):
    return pl.BlockSpec((tile, width), lambda i: (i, 0))


def _split_specs(tile, width):
    npt = NP_TOK // tile
    return [pl.BlockSpec((tile, width), lambda i: (jnp.minimum(i, npt - 1), 0)),
            pl.BlockSpec((tile, width), lambda i: (jnp.maximum(i - npt, 0), 0))]


def _pick(tile, p_ref, s_ref):
    return jnp.where(pl.program_id(0) < NP_TOK // tile, p_ref[...], s_ref[...])


def _rope_row_spec(tile, width):
    npt = NP_TOK // tile
    spt = DEC_SEQ // tile
    return pl.BlockSpec((tile, width), lambda i: (jnp.where(i < npt, 0, 1 + (i - npt) % spt), 0))


def _even_in_kernel(xp_ref, xs_ref, g_ref, shift_ref, scale_ref, win_ref, qn_ref, wq_ref, kvn_ref, wkk_ref,
                    wkv_ref, vbias_ref, epl_ref, wgf_ref, bgf_ref, wgb_ref, bgb_ref, lmat_ref,
                    umat_ref, cq_ref, sq_ref, ck_ref, sk_ref,
                    q_out, k_out, v_out, ckv_out, kpe_out, gq_out, gk_out, gv_out, gr_out,
                    bf_out, bb_out):
    h = _modulate(_pick(TOK_TILE, xp_ref, xs_ref), g_ref[...], shift_ref[...], scale_ref[...])
    z = _dot(h.astype(BF16), win_ref[...])

    cqn = _rms(z[:, _C_CQ:_C_CQ + MLA_Q_RANK], qn_ref[...]).astype(BF16)
    qf = _dot(cqn, wq_ref[...])
    cq_t = _lane_tile(cq_ref[...], MLA_HEADS)
    sq_t = _lane_tile(sq_ref[...], MLA_HEADS)
    q_out[...] = (qf[:, :MLA_QK_W] * cq_t + qf[:, MLA_QK_W:] * sq_t).astype(BF16)

    ckvn = _rms(z[:, _C_CKV:_C_CKV + MLA_KV_RANK], kvn_ref[...])
    ckv_out[...] = ckvn
    small = z[:, _C_SMALL:_C_SMALL + LANES]
    kpe = (small[:, _S_KPE:_S_KPE + MLA_ROPE] * ck_ref[...]
           + small[:, _S_KPER:_S_KPER + MLA_ROPE] * sk_ref[...])
    kpe_out[...] = kpe
    ckvn_b = ckvn.astype(BF16)
    k_out[...] = (_dot(ckvn_b, wkk_ref[...]) + _dot(kpe.astype(BF16), epl_ref[...])).astype(BF16)
    v_out[...] = (_dot(ckvn_b, wkv_ref[...]) + vbias_ref[...]).astype(BF16)

    gq_out[...] = z[:, _C_GQ:_C_GQ + GLA_HEADS * GLA_DK] * (GLA_DK ** -0.5)
    gk_out[...] = z[:, _C_GK:_C_GK + GLA_HEADS * GLA_DK]
    gv_out[...] = z[:, _C_GV:_C_GV + GLA_HEADS * GLA_DV].astype(BF16)
    gr_out[...] = z[:, _C_GR:_C_GR + GLA_HEADS * GLA_DV]

    small_b = small.astype(BF16)
    la_f = _log_sigmoid(_dot(small_b, wgf_ref[...]) + bgf_ref[...]) * (1.0 / GLA_GATE_NORM)
    la_b = _log_sigmoid(_dot(small_b, wgb_ref[...]) + bgb_ref[...]) * (1.0 / GLA_GATE_NORM)
    lmat = lmat_ref[...]
    umat = umat_ref[...]
    for r in range(TOK_TILE // CUMSUM_BLOCK):
        rows = slice(r * CUMSUM_BLOCK, (r + 1) * CUMSUM_BLOCK)
        f_hi, f_mid, f_lo = _split3(la_f[rows])
        bf_out[rows, :] = _dot(lmat, f_hi) + _dot(lmat, f_mid) + _dot(lmat, f_lo)
        b_hi, b_mid, b_lo = _split3(la_b[rows])
        bb_out[rows, :] = _dot(umat, b_hi) + _dot(umat, b_mid) + _dot(umat, b_lo)


def _even_in_proj(xp, xs, mod5, layer, g, wts, tabs):
    t = TOK_TILE
    out_widths = [(MLA_QK_W, BF16), (MLA_QK_W, BF16), (MLA_VEXT_W, BF16), (MLA_KV_RANK, F32),
                  (MLA_ROPE, F32), (GLA_HEADS * GLA_DK, F32), (GLA_HEADS * GLA_DK, F32),
                  (GLA_HEADS * GLA_DV, BF16), (GLA_HEADS * GLA_DV, F32),
                  (GLA_HEADS * GLA_DK, F32), (GLA_HEADS * GLA_DK, F32)]
    const_names = ["win", "qn", "wq", "kvn", "wkk", "wkv", "vbias", "epl", "wgf", "bgf", "wgb",
                   "bgb", "lmat", "umat"]
    consts = [wts[n] for n in const_names]
    in_specs = (_split_specs(t, D_MODEL)
                + [_const_spec((1, D_MODEL)), _mod_spec(layer, 0, t), _mod_spec(layer, 1, t)]
                + [_const_spec(c.shape) for c in consts]
                + [_rope_row_spec(t, LANES), _rope_row_spec(t, LANES),
                   _rope_row_spec(t, MLA_ROPE), _rope_row_spec(t, MLA_ROPE)])
    return pl.pallas_call(
        _even_in_kernel,
        out_shape=[jax.ShapeDtypeStruct((N_TOK, w), dt) for w, dt in out_widths],
        grid=(N_TOK // t,),
        in_specs=in_specs,
        out_specs=[_tok_spec(t, w) for w, _ in out_widths],
        compiler_params=_cparams(("parallel",)),
        name="even_in_proj",
    )(xp, xs, g, mod5, mod5, *consts, tabs["mla_cq"], tabs["mla_sq"], tabs["mla_ck"], tabs["mla_sk"])


def _cache_kv_kernel(ckv_ref, kpe_ref, wkk_ref, wkv_ref, vbias_ref, epl_ref, k_out, v_out):
    ckv_b = ckv_ref[...].astype(BF16)
    k_out[...] = (_dot(ckv_b, wkk_ref[...]) + _dot(kpe_ref[...].astype(BF16), epl_ref[...])).astype(BF16)
    v_out[...] = (_dot(ckv_b, wkv_ref[...]) + vbias_ref[...]).astype(BF16)


def _cache_kv(ckv, kpe, wts):
    n = ckv.shape[0]
    consts = [wts[k] for k in ("wkk", "wkv", "vbias", "epl")]
    return pl.pallas_call(
        _cache_kv_kernel,
        out_shape=[jax.ShapeDtypeStruct((n, MLA_QK_W), BF16), jax.ShapeDtypeStruct((n, MLA_VEXT_W), BF16)],
        grid=(1,),
        in_specs=[_const_spec(ckv.shape), _const_spec(kpe.shape)] + [_const_spec(c.shape) for c in consts],
        out_specs=[_const_spec((n, MLA_QK_W)), _const_spec((n, MLA_VEXT_W))],
        compiler_params=_cparams(("arbitrary",)),
        name="mla_cache_kv",
    )(ckv, kpe, *consts)


def _mla_attn_kernel(*refs, n_seg):
    q_ref = refs[0]
    k_refs = refs[1:1 + n_seg]
    v_refs = refs[1 + n_seg:1 + 2 * n_seg]
    o_ref, s_scr = refs[1 + 2 * n_seg:]
    scale = (MLA_NOPE + MLA_ROPE) ** -0.5
    c = scale * float(np.log2(np.e))
    tq = q_ref.shape[0]
    q_sub = min(tq, MLA_Q_SUB)
    kt = MXU_COLS
    tiles = []
    for si, k in enumerate(k_refs):
        for r0 in range(0, k.shape[0], kt):
            tiles.append((si, r0, len(tiles) * kt))
    lane = lax.broadcasted_iota(jnp.int32, (q_sub, LANES), 1)
    for qs in range(tq // q_sub):
        rows = slice(qs * q_sub, (qs + 1) * q_sub)
        res = []
        for j in range(2):
            hl = slice(j * LANES, (j + 1) * LANES)
            qj = q_ref[rows, hl]
            macc = None
            for si, r0, c0 in tiles:
                s = _dot_nt(qj, k_refs[si][r0:r0 + kt, hl])
                s_scr[j, rows, c0:c0 + kt] = s
                mt = jnp.maximum(s[:, :LANES], s[:, LANES:])
                macc = mt if macc is None else jnp.maximum(macc, mt)
            m = macc.max(axis=-1, keepdims=True)
            r = None
            for si, r0, c0 in tiles:
                p = jnp.exp2((s_scr[j, rows, c0:c0 + kt] - m) * c).astype(BF16)
                rj = _dot(p, v_refs[si][r0:r0 + kt, :])
                r = rj if r is None else r + rj
            res.append(r[:, :LANES] / r[:, LANES:])
        o_ref[rows, :] = jnp.where(lane < MLA_V, res[0], res[1]).astype(BF16)


def _mla_attention(q, ks, vs, *, n_batch, seq_q, q_tile, tok_off, k_batch_rows):
    n_seg = len(ks)
    nq = seq_q // q_tile
    qoff = tok_off // q_tile
    grid = (n_batch, MLA_HEADS // 2, nq)
    in_specs = [pl.BlockSpec((q_tile, 2 * LANES), lambda b, hp, i: (qoff + b * nq + i, hp))]
    for s in range(n_seg):
        rows = k_batch_rows[s]
        off = (tok_off // rows) if s == 0 else 0
        in_specs.append(pl.BlockSpec((rows, 2 * LANES), functools.partial(lambda b, hp, i, off: (off + b, hp), off=off)))
    for s in range(n_seg):
        rows = k_batch_rows[s]
        off = (tok_off // rows) if s == 0 else 0
        in_specs.append(pl.BlockSpec((rows, 2 * LANES), functools.partial(lambda b, hp, i, off: (off + b, hp), off=off)))
    return pl.pallas_call(
        functools.partial(_mla_attn_kernel, n_seg=n_seg),
        out_shape=jax.ShapeDtypeStruct((n_batch * seq_q, MLA_HEADS * MLA_V), BF16),
        grid=grid,
        in_specs=in_specs,
        out_specs=pl.BlockSpec((q_tile, LANES), lambda b, hp, i: (b * nq + i, hp)),
        scratch_shapes=[pltpu.VMEM((2, q_tile, sum(k_batch_rows)), F32)],
        compiler_params=_cparams(("parallel", "parallel", "arbitrary")),
        name=f"mla_attention_{n_seg}seg",
    )(q, *ks, *vs)


def _gla_kernel(q_ref, k_ref, v_ref, gr_ref, bf_ref, bb_ref, s0f_ref, s0b_ref, gn_ref,
                o_ref, sf_ref, sb_ref, acc_ref, kdf_ref, kdb_ref, qdf_ref, qdb_ref, hist_ref, *, n_chunks):
    c = GLA_CHUNK
    cpb = min(n_chunks, GLA_BLOCK_CHUNKS)
    blk = cpb * c
    lane = lax.broadcasted_iota(jnp.int32, (blk, LANES), 1)
    lo = lane < GLA_DK
    row = lax.broadcasted_iota(jnp.int32, (blk, blk), 0)
    col = lax.broadcasted_iota(jnp.int32, (blk, blk), 1)
    chunk_bits = c.bit_length() - 1
    same_chunk = jnp.right_shift(row, chunk_bits) == jnp.right_shift(col, chunk_bits)
    tril = same_chunk & (row >= col)
    triu = same_chunk & (row <= col)
    zero_blk = jnp.zeros((GLA_DK, GLA_DV), F32)

    def pair_state_t(s_ref):
        blockdiag = jnp.concatenate(
            [jnp.concatenate([s_ref[0], zero_blk], axis=1),
             jnp.concatenate([zero_blk, s_ref[1]], axis=1)], axis=0)
        return blockdiag.T

    dirs = ((bf_ref, c // 2 - 1, c - 1, tril, kdf_ref, qdf_ref),
            (bb_ref, c // 2, 0, triu, kdb_ref, qdb_ref))
    sels = (lo, jnp.logical_not(lo))
    lo_c = lax.broadcasted_iota(jnp.int32, (c, LANES), 1) < GLA_DK
    sels_c = (lo_c, jnp.logical_not(lo_c))
    hcols = (slice(0, GLA_DV), slice(GLA_DV, 2 * GLA_DV))

    def block(r, carry):
        rows = pl.ds(pl.multiple_of(r * blk, blk), blk)
        q = q_ref[rows, :]
        k = k_ref[rows, :]
        v = v_ref[rows, :]

        def chunk_row(b, r):
            return jnp.concatenate([jnp.broadcast_to(b[ch * c + r:ch * c + r + 1, :], (c, LANES))
                                    for ch in range(cpb)], axis=0)

        for d, (b_ref, mid_row, last_row, causal, kd_ref, qd_ref) in enumerate(dirs):
            b = b_ref[rows, :]
            b_mid = chunk_row(b, mid_row)
            b_last = chunk_row(b, last_row)
            qe = q * jnp.exp(b - b_mid)
            ke = (k * jnp.exp(b_mid - b)).astype(BF16)
            kd_ref[rows, :] = (k * jnp.exp(b_last - b)).astype(BF16)
            qd_ref[rows, :] = (q * jnp.exp(b)).astype(BF16)
            for j in range(2):
                a = _dot_nt(jnp.where(sels[j], qe, 0.0).astype(BF16), ke)
                o = _dot(jnp.where(causal, a, 0.0).astype(BF16), v[:, hcols[j]])
                if d == 0:
                    acc_ref[rows, hcols[j]] = o
                else:
                    acc_ref[rows, hcols[j]] += o
        return carry

    lax.fori_loop(0, n_chunks // cpb, block, 0, unroll=min(2, n_chunks // cpb))

    def scan(i, carry):
        new = []
        for d, (b_ref, _, last_row, _, kd_ref, _) in enumerate(dirs):
            ci = i if d == 0 else n_chunks - 1 - i
            rows = pl.ds(pl.multiple_of(ci * c, c), c)
            hist_ref[d, ci] = carry[d].astype(BF16)
            grp = b_ref[pl.ds(pl.multiple_of(ci * c + (last_row // 8) * 8, 8), 8), :]
            b_last = grp[last_row % 8:last_row % 8 + 1, :]
            new.append(carry[d] * jnp.exp(b_last) + _dot_tn(v_ref[rows, :], kd_ref[rows, :]))
        return tuple(new)

    st_f, st_b = lax.fori_loop(0, n_chunks, scan, (pair_state_t(s0f_ref), pair_state_t(s0b_ref)),
                               unroll=GLA_CHUNK_UNROLL)

    def inter(ci, carry):
        rows = pl.ds(pl.multiple_of(ci * c, c), c)
        for d, (_, _, _, _, _, qd_ref) in enumerate(dirs):
            qd = qd_ref[rows, :]
            st = hist_ref[d, ci]
            for j in range(2):
                acc_ref[rows, hcols[j]] += _dot_nt(jnp.where(sels_c[j], qd, jnp.zeros_like(qd)), st[hcols[j], :])
        return carry

    lax.fori_loop(0, n_chunks, inter, 0, unroll=GLA_CHUNK_UNROLL)
    s_f = st_f.T
    s_b = st_b.T
    sf_ref[0] = s_f[:GLA_DK, :GLA_DV]
    sf_ref[1] = s_f[GLA_DK:, GLA_DV:]
    sb_ref[0] = s_b[:GLA_DK, :GLA_DV]
    sb_ref[1] = s_b[GLA_DK:, GLA_DV:]

    gn = gn_ref[...]
    for j in range(2):
        cols = slice(j * GLA_DV, (j + 1) * GLA_DV)
        gr = gr_ref[:, cols]
        o_ref[:, cols] = (_rms(acc_ref[:, cols], gn) * (gr * jax.nn.sigmoid(gr))).astype(BF16)


def _gla(gq, gk, gv, gr, bf, bb, s0f, s0b, gnorm, *, n_batch, seq, tok_off):
    n_chunks = seq // GLA_CHUNK
    boff = tok_off // seq
    hp = GLA_HEADS // 2
    tok = lambda w: pl.BlockSpec((seq, w), lambda b, p: (boff + b, p))
    st = pl.BlockSpec((None, 2, GLA_DK, GLA_DV), lambda b, p: (b, p, 0, 0))
    return pl.pallas_call(
        functools.partial(_gla_kernel, n_chunks=n_chunks),
        out_shape=[jax.ShapeDtypeStruct((n_batch * seq, GLA_HEADS * GLA_DV), BF16),
                   jax.ShapeDtypeStruct((n_batch, GLA_HEADS, GLA_DK, GLA_DV), F32),
                   jax.ShapeDtypeStruct((n_batch, GLA_HEADS, GLA_DK, GLA_DV), F32)],
        grid=(n_batch, hp),
        in_specs=[tok(2 * GLA_DK), tok(2 * GLA_DK), tok(2 * GLA_DV), tok(2 * GLA_DV),
                  tok(2 * GLA_DK), tok(2 * GLA_DK), st, st, _const_spec((1, GLA_DV))],
        out_specs=[pl.BlockSpec((seq, 2 * GLA_DV), lambda b, p: (b, p)), st, st],
        scratch_shapes=[pltpu.VMEM((seq, 2 * GLA_DV), F32)]
                       + [pltpu.VMEM((seq, 2 * GLA_DK), BF16)] * 4
                       + [pltpu.VMEM((2, n_chunks, 2 * GLA_DV, 2 * GLA_DK), BF16)],
        compiler_params=_cparams(("parallel", "parallel")),
        name=f"gla_seq{seq}",
    )(gq, gk, gv, gr, bf, bb, s0f, s0b, gnorm)


def _out_proj_kernel(*refs, n_o, x_split):
    t = TOK_TILE
    o = [_pick(t, refs[2 * j], refs[2 * j + 1]) for j in range(n_o)]
    rest = refs[2 * n_o:]
    w_ref = rest[0]
    if x_split:
        x_in = _pick(t, rest[1], rest[2])
        rest = rest[3:]
    else:
        x_in = rest[1][...]
        rest = rest[2:]
    g1_ref, gate_ref, g2_ref, shift_ref, scale_ref, x_out, h_out = rest
    y = _dot(o[0] if n_o == 1 else jnp.concatenate(o, axis=1), w_ref[...])
    x = x_in + gate_ref[...] * _rms(y, g1_ref[...])
    x_out[...] = x
    h_out[...] = _modulate(x, g2_ref[...], shift_ref[...], scale_ref[...]).astype(h_out.dtype)


def _out_proj(os_, w, x, mod5, layer, g1, g2, h_dtype):
    t = TOK_TILE
    x_split = isinstance(x, tuple)
    in_specs, args = [], []
    for o_p, o_s in os_:
        in_specs += _split_specs(t, o_p.shape[1])
        args += [o_p, o_s]
    in_specs.append(_const_spec(w.shape))
    args.append(w)
    if x_split:
        in_specs += _split_specs(t, D_MODEL)
        args += list(x)
    else:
        in_specs.append(_tok_spec(t, D_MODEL))
        args.append(x)
    in_specs += [_const_spec((1, D_MODEL)), _mod_spec(layer, 2, t), _const_spec((1, D_MODEL)),
                 _mod_spec(layer, 3, t), _mod_spec(layer, 4, t)]
    args += [g1, mod5, g2, mod5, mod5]
    return pl.pallas_call(
        functools.partial(_out_proj_kernel, n_o=len(os_), x_split=x_split),
        out_shape=[jax.ShapeDtypeStruct((N_TOK, D_MODEL), F32), jax.ShapeDtypeStruct((N_TOK, D_MODEL), h_dtype)],
        grid=(N_TOK // t,),
        in_specs=in_specs,
        out_specs=[_tok_spec(t, D_MODEL), _tok_spec(t, D_MODEL)],
        compiler_params=_cparams(("parallel",)),
        name=f"out_proj_{len(os_)}",
    )(*args)


def _ffn_kernel(h_ref, x_ref, wg_ref, wu_ref, wd_ref, g_ref, gate_ref, x_out):
    h = h_ref[...]
    f = None
    for cidx in range(D_FF // FF_CHUNK):
        cols = slice(cidx * FF_CHUNK, (cidx + 1) * FF_CHUNK)
        a = _dot(h, wg_ref[:, cols].astype(BF16))
        u = _dot(h, wu_ref[:, cols].astype(BF16))
        fc = _dot(((a * jax.nn.sigmoid(a)) * u).astype(BF16), wd_ref[cols, :].astype(BF16))
        f = fc if f is None else f + fc
    x_out[...] = x_ref[...] + gate_ref[...] * _rms(f, g_ref[...])


def _ffn(h, x, wg, wu, wd, mod5, layer, g3):
    t = TOK_TILE
    return pl.pallas_call(
        _ffn_kernel,
        out_shape=jax.ShapeDtypeStruct((N_TOK, D_MODEL), F32),
        grid=(N_TOK // t,),
        in_specs=[_tok_spec(t, D_MODEL), _tok_spec(t, D_MODEL), _const_spec(wg.shape),
                  _const_spec(wu.shape), _const_spec(wd.shape), _const_spec((1, D_MODEL)),
                  _mod_spec(layer, 5, t)],
        out_specs=_tok_spec(t, D_MODEL),
        compiler_params=_cparams(("parallel",)),
        name="ffn_swiglu",
    )(h, x, wg, wu, wd, g3, mod5)


GQA_Q_W = GQA_HEADS * GQA_HEAD_DIM
GQA_KV_W = GQA_KV_HEADS * GQA_HEAD_DIM
GQA_VEXT_W = (GQA_KV_HEADS // 2) * 2 * LANES
_O_Q, _O_QR = 0, GQA_Q_W
_O_K, _O_KR = 2 * GQA_Q_W, 2 * GQA_Q_W + GQA_KV_W
_O_V = 2 * GQA_Q_W + 2 * GQA_KV_W
IN_C_EXT = _O_V + GQA_VEXT_W


def _odd_in_kernel(x_ref, g_ref, shift_ref, scale_ref, win_ref, vbias_ref, c_ref, s_ref,
                   q_out, k_out, kb_out, v_out, vb_out):
    h = _modulate(x_ref[...], g_ref[...], shift_ref[...], scale_ref[...])
    z = _dot(h.astype(BF16), win_ref[...])
    c_t = _lane_tile(c_ref[...], GQA_Q_W // LANES)
    s_t = _lane_tile(s_ref[...], GQA_Q_W // LANES)
    q = z[:, _O_Q:_O_Q + GQA_Q_W] * c_t + z[:, _O_QR:_O_QR + GQA_Q_W] * s_t
    q_out[...] = (q * (GQA_HEAD_DIM ** -0.5)).astype(BF16)
    k = (z[:, _O_K:_O_K + GQA_KV_W] * c_t[:, :GQA_KV_W]
         + z[:, _O_KR:_O_KR + GQA_KV_W] * s_t[:, :GQA_KV_W])
    k_out[...] = k
    kb_out[...] = k.astype(BF16)
    vext = z[:, _O_V:_O_V + GQA_VEXT_W] + vbias_ref[...]
    vb_out[...] = vext.astype(BF16)
    for p in range(GQA_KV_HEADS // 2):
        v_out[:, p * LANES:(p + 1) * LANES] = vext[:, 2 * p * LANES:(2 * p + 1) * LANES]


def _odd_in_proj(x, mod5, layer, g, win, vbias, tab_c, tab_s):
    t = TOK_TILE
    out_widths = [(GQA_Q_W, BF16), (GQA_KV_W, F32), (GQA_KV_W, BF16), (GQA_KV_W, F32), (GQA_VEXT_W, BF16)]
    return pl.pallas_call(
        _odd_in_kernel,
        out_shape=[jax.ShapeDtypeStruct((N_TOK, w), dt) for w, dt in out_widths],
        grid=(N_TOK // t,),
        in_specs=[_tok_spec(t, D_MODEL), _const_spec((1, D_MODEL)), _mod_spec(layer, 0, t),
                  _mod_spec(layer, 1, t), _const_spec(win.shape), _const_spec(vbias.shape),
                  _rope_row_spec(t, LANES), _rope_row_spec(t, LANES)],
        out_specs=[_tok_spec(t, w) for w, _ in out_widths],
        compiler_params=_cparams(("parallel",)),
        name="odd_in_proj",
    )(x, g, mod5, mod5, win, vbias, tab_c, tab_s)


def _gqa_kernel(sink_ref, q_ref, *refs, local_len):
    if local_len:
        kl_ref, vl_ref, kc_ref, vc_ref, o_ref = refs
    else:
        kc_ref, vc_ref, o_ref = refs
    tq = q_ref.shape[0]
    lane = lax.broadcasted_iota(jnp.int32, (tq, LANES), 1)
    lo = lane < GQA_HEAD_DIM
    if local_len:
        i = pl.program_id(1)
        q0 = i * tq
        seq = kl_ref.shape[0]
        kstart = pl.multiple_of(jnp.clip(q0 - WINDOW, 0, seq - local_len), LANES)
        qpos = q0 + lax.broadcasted_iota(jnp.int32, (tq, local_len), 0)
        kpos = kstart + lax.broadcasted_iota(jnp.int32, (tq, local_len), 1)
        band = jnp.abs(qpos - kpos) <= WINDOW
    for p in range(GQA_KV_HEADS // 2):
        kc = kc_ref[:, p * LANES:(p + 1) * LANES]
        vc = vc_ref[:, 2 * p * LANES:(2 * p + 2) * LANES]
        if local_len:
            kl = kl_ref[pl.ds(kstart, local_len), p * LANES:(p + 1) * LANES]
            vl = vl_ref[pl.ds(kstart, local_len), 2 * p * LANES:(2 * p + 2) * LANES]
        for blk in range(GQA_GROUP):
            cols = slice((p * GQA_GROUP + blk) * LANES, (p * GQA_GROUP + blk + 1) * LANES)
            qb = q_ref[:, cols]
            res = []
            for half in range(2):
                head = (2 * p + half) * GQA_GROUP + blk
                sink = sink_ref[head]
                qh = jnp.where(lo if half == 0 else jnp.logical_not(lo), qb, jnp.zeros_like(qb))
                s_c = _dot_nt(qh, kc)
                m = jnp.maximum(s_c.max(axis=-1, keepdims=True), sink)
                if local_len:
                    s_l = jnp.where(band, _dot_nt(qh, kl), NEG_INF)
                    m = jnp.maximum(m, s_l.max(axis=-1, keepdims=True))
                r = _dot(jnp.exp(s_c - m).astype(BF16), vc)
                if local_len:
                    r = r + _dot(jnp.exp(s_l - m).astype(BF16), vl)
                res.append(r[:, :LANES] / (r[:, LANES:] + jnp.exp(sink - m)))
            o_ref[:, cols] = jnp.where(lo, res[0], res[1]).astype(BF16)


def _gqa_attention(sink, q, k_loc, v_loc, k_ctx, v_ctx, *, n_batch, seq_q, q_tile, tok_off, n_ctx, local):
    nq = seq_q // q_tile
    qoff = tok_off // q_tile
    local_len = q_tile + 2 * WINDOW if local else 0
    in_specs = [pl.BlockSpec(memory_space=pltpu.SMEM),
                pl.BlockSpec((q_tile, GQA_Q_W), lambda b, i: (qoff + b * nq + i, 0))]
    args = [sink, q]
    if local:
        boff = tok_off // seq_q
        in_specs += [pl.BlockSpec((seq_q, GQA_KV_W), lambda b, i: (boff + b, 0)),
                     pl.BlockSpec((seq_q, GQA_VEXT_W), lambda b, i: (boff + b, 0))]
        args += [k_loc, v_loc]
    in_specs += [pl.BlockSpec((n_ctx, GQA_KV_W), lambda b, i: (b, 0)),
                 pl.BlockSpec((n_ctx, GQA_VEXT_W), lambda b, i: (b, 0))]
    args += [k_ctx, v_ctx]
    return pl.pallas_call(
        functools.partial(_gqa_kernel, local_len=local_len),
        out_shape=jax.ShapeDtypeStruct((n_batch * seq_q, GQA_Q_W), BF16),
        grid=(n_batch, nq),
        in_specs=in_specs,
        out_specs=pl.BlockSpec((q_tile, GQA_Q_W), lambda b, i: (b * nq + i, 0)),
        compiler_params=_cparams(("parallel", "arbitrary")),
        name="gqa_local" if local else "gqa_ctx",
    )(*args)


def _router_kernel(h_ref, w_ref, b_ref, ltri_ref, wsel_out, isel_out, rank_out, cnt_out, carry_ref):
    @pl.when(pl.program_id(0) == 0)
    def _():
        carry_ref[...] = jnp.zeros_like(carry_ref)

    logits = _dot(h_ref[...].astype(BF16), w_ref[...]) + b_ref[...]
    lane = lax.broadcasted_iota(jnp.int32, logits.shape, 1)
    neg = float(np.finfo(np.float32).min)
    lg = jnp.where(lane < N_EXPERTS, logits, neg)
    v1 = lg.max(axis=-1, keepdims=True)
    i1 = jnp.min(jnp.where(lg == v1, lane, LANES), axis=-1, keepdims=True)
    lg2 = jnp.where(lane == i1, neg, lg)
    v2 = lg2.max(axis=-1, keepdims=True)
    i2 = jnp.min(jnp.where(lg2 == v2, lane, LANES), axis=-1, keepdims=True)
    e2 = jnp.exp(v2 - v1)
    den = 1.0 + e2
    wsel_out[...] = jnp.where(lane == 0, 1.0 / den, jnp.where(lane == 1, e2 / den, 0.0))
    isel_out[...] = jnp.where(lane == 0, i1, jnp.where(lane == 1, i2, 0))
    hit = jnp.where(lane == i1, 1.0, jnp.where(lane == i2, 1.0, 0.0))
    carry = carry_ref[...]
    rank_out[...] = (_dot(ltri_ref[...], hit.astype(BF16)) + carry[0:1, :]).astype(jnp.int32)
    carry = carry + jnp.sum(hit, axis=0, keepdims=True)
    carry_ref[...] = carry
    cnt_out[...] = carry.astype(jnp.int32)


def _router(h, w, b):
    t = TOK_TILE
    r = np.arange(t)
    ltri = jnp.asarray(r[:, None] > r[None, :], BF16)
    return pl.pallas_call(
        _router_kernel,
        out_shape=[jax.ShapeDtypeStruct((N_TOK, LANES), F32), jax.ShapeDtypeStruct((N_TOK, LANES), jnp.int32),
                   jax.ShapeDtypeStruct((N_TOK, LANES), jnp.int32), jax.ShapeDtypeStruct((8, LANES), jnp.int32)],
        grid=(N_TOK // t,),
        in_specs=[_tok_spec(t, D_MODEL), _const_spec(w.shape), _const_spec(b.shape), _const_spec((t, t))],
        out_specs=[_tok_spec(t, LANES), _tok_spec(t, LANES), _tok_spec(t, LANES),
                   pl.BlockSpec((8, LANES), lambda i: (0, 0))],
        scratch_shapes=[pltpu.VMEM((8, LANES), F32)],
        compiler_params=_cparams(("arbitrary",)),
        name="moe_router",
    )(h, w, b, ltri)


def _route_tables(isel, rank, cnt):
    tm = MOE_ROW_TILE
    counts = cnt[0, :N_EXPERTS]
    padded = ((counts + tm - 1) // tm) * tm
    ends = jnp.cumsum(padded)
    base = ends - padded
    e_ids = jnp.arange(N_EXPERTS, dtype=jnp.int32)
    row = rank[:, :N_EXPERTS] + base[None, :]
    pos1 = jnp.sum(jnp.where(e_ids[None, :] == isel[:, 0:1], row, 0), axis=1)
    pos2 = jnp.sum(jnp.where(e_ids[None, :] == isel[:, 1:2], row, 0), axis=1)
    tile_start = jnp.arange(MOE_ROWS // tm, dtype=jnp.int32) * tm
    tile_expert = jnp.minimum(jnp.sum(tile_start[:, None] >= ends[None, :], axis=1), N_EXPERTS - 1).astype(jnp.int32)
    tile_valid = jnp.clip((base + counts)[tile_expert] - tile_start, 0, tm).astype(jnp.int32)
    tile_valid = jnp.where(tile_start < ends[-1], tile_valid, 0)
    return pos1, pos2, tile_expert, tile_valid


def _scatter_rows(rows, pos1, pos2, n_out):
    n_tok, d = rows.shape
    per_w = n_tok // SC_WORKERS
    w = SC_GATHER_WINDOW
    assert per_w * SC_WORKERS == n_tok and per_w % w == 0
    mesh = plsc.VectorSubcoreMesh(core_axis_name="core", subcore_axis_name="subcore")

    @functools.partial(
        pl.kernel, out_type=jax.ShapeDtypeStruct((n_out, d), rows.dtype), mesh=mesh,
        scratch_types=[pltpu.VMEM((w,), jnp.int32), pltpu.VMEM((w,), jnp.int32), pltpu.VMEM((w, d), rows.dtype)],
        name="sc_scatter_rows")
    def scatter(x_hbm, p1_hbm, p2_hbm, o_hbm, i1_v, i2_v, rows_v):
        wid = lax.axis_index("subcore") * SC_CORES + lax.axis_index("core")
        base = wid * per_w

        @pl.loop(0, per_w // w)
        def _(g):
            off = base + g * w
            pltpu.sync_copy(p1_hbm.at[pl.ds(off, w)], i1_v)
            pltpu.sync_copy(p2_hbm.at[pl.ds(off, w)], i2_v)
            pltpu.sync_copy(x_hbm.at[pl.ds(off, w)], rows_v)
            pltpu.sync_copy(rows_v, o_hbm.at[i1_v])
            pltpu.sync_copy(rows_v, o_hbm.at[i2_v])

    return scatter(rows, pos1, pos2)


def _gather_rows(table, idx):
    n_idx = idx.shape[0]
    d = table.shape[1]
    per_w = n_idx // SC_WORKERS
    assert per_w * SC_WORKERS == n_idx and per_w % SC_INDEX_BLOCK == 0
    mesh = plsc.VectorSubcoreMesh(core_axis_name="core", subcore_axis_name="subcore")

    @functools.partial(
        pl.kernel, out_type=jax.ShapeDtypeStruct((n_idx, d), table.dtype), mesh=mesh,
        scratch_types=[pltpu.VMEM((SC_INDEX_BLOCK,), jnp.int32), pltpu.VMEM((SC_GATHER_WINDOW, d), table.dtype)],
        name="sc_gather_rows")
    def gather(x_hbm, i_hbm, o_hbm, idx_v, rows_v):
        wid = lax.axis_index("subcore") * SC_CORES + lax.axis_index("core")
        base = wid * per_w

        @pl.loop(0, per_w // SC_INDEX_BLOCK)
        def _(g):
            off = base + g * SC_INDEX_BLOCK
            pltpu.sync_copy(i_hbm.at[pl.ds(off, SC_INDEX_BLOCK)], idx_v)
            for s in range(SC_INDEX_BLOCK // SC_GATHER_WINDOW):
                sub = pl.ds(s * SC_GATHER_WINDOW, SC_GATHER_WINDOW)
                pltpu.sync_copy(x_hbm.at[idx_v.at[sub]], rows_v)
                pltpu.sync_copy(rows_v, o_hbm.at[pl.ds(off + s * SC_GATHER_WINDOW, SC_GATHER_WINDOW)])

    return gather(table, idx)


def _expert_ffn_kernel(te_ref, nv_ref, x_ref, wg_ref, wu_ref, wd_ref, y_out):
    n_valid = nv_ref[pl.program_id(0)]

    @pl.when(n_valid > 0)
    def _():
        row = lax.broadcasted_iota(jnp.int32, x_ref.shape, 0)
        h = jnp.where(row < n_valid, x_ref[...], 0.0).astype(BF16)
        f = None
        for cidx in range(D_FF // FF_CHUNK):
            cols = slice(cidx * FF_CHUNK, (cidx + 1) * FF_CHUNK)
            a = _dot(h, wg_ref[:, cols].astype(BF16))
            u = _dot(h, wu_ref[:, cols].astype(BF16))
            fc = _dot(((a * jax.nn.sigmoid(a)) * u).astype(BF16), wd_ref[cols, :].astype(BF16))
            f = fc if f is None else f + fc
        y_out[...] = f

    @pl.when(n_valid == 0)
    def _():
        y_out[...] = jnp.zeros_like(y_out)


def _expert_ffn(xs, tile_expert, tile_valid, wg, wu, wd):
    tm = MOE_ROW_TILE
    wspec = lambda shape: pl.BlockSpec((None,) + shape, lambda j, te, nu: (te[j], 0, 0),
                                       pipeline_mode=pl.Buffered(1))
    return pl.pallas_call(
        _expert_ffn_kernel,
        out_shape=jax.ShapeDtypeStruct((MOE_ROWS, D_MODEL), F32),
        grid_spec=pltpu.PrefetchScalarGridSpec(
            num_scalar_prefetch=2,
            grid=(MOE_ROWS // tm,),
            in_specs=[pl.BlockSpec((tm, D_MODEL), lambda j, te, nu: (j, 0)),
                      wspec((D_MODEL, D_FF)), wspec((D_MODEL, D_FF)), wspec((D_FF, D_MODEL))],
            out_specs=pl.BlockSpec((tm, D_MODEL), lambda j, te, nu: (j, 0)),
        ),
        compiler_params=_cparams(("arbitrary",)),
        name="moe_expert_ffn",
    )(tile_expert, tile_valid, xs, wg, wu, wd)


def _moe_combine_kernel(y1_ref, y2_ref, wsel_ref, x_ref, g_ref, gate_ref, x_out):
    w = wsel_ref[...]
    f = w[:, 0:1] * y1_ref[...] + w[:, 1:2] * y2_ref[...]
    x_out[...] = x_ref[...] + gate_ref[...] * _rms(f, g_ref[...])


def _moe_combine(yg, wsel, x, mod5, layer, g3, *, tok_off, n_tok):
    t = TOK_TILE
    nt = N_TOK // t
    off = tok_off // t
    tpg = NP_TOK // t
    tok = lambda w, shift: pl.BlockSpec((t, w), lambda i: (off + shift + i, 0))
    return pl.pallas_call(
        _moe_combine_kernel,
        out_shape=jax.ShapeDtypeStruct((n_tok, D_MODEL), F32),
        grid=(n_tok // t,),
        in_specs=[tok(D_MODEL, 0), tok(D_MODEL, nt), tok(LANES, 0), tok(D_MODEL, 0), _const_spec((1, D_MODEL)),
                  pl.BlockSpec((None, None, None, 1, D_MODEL), lambda i: (layer, (off + i) // tpg, 5, 0, 0))],
        out_specs=_tok_spec(t, D_MODEL),
        compiler_params=_cparams(("parallel",)),
        name="moe_combine",
    )(yg, yg, wsel, x, g3, mod5)


def _moe(h, x, w_router, b_router, wg, wu, wd, mod5, layer, g3):
    wsel, isel, rank, cnt = _router(h, w_router, b_router)
    pos1, pos2, tile_expert, tile_valid = _route_tables(isel, rank, cnt)
    xs = _scatter_rows(h, pos1, pos2, MOE_ROWS)
    ys = _expert_ffn(xs, tile_expert, tile_valid, wg, wu, wd)
    yg = _gather_rows(ys, jnp.concatenate([pos1, pos2]))
    return (_moe_combine(yg, wsel, x, mod5, layer, g3, tok_off=0, n_tok=NP_TOK),
            _moe_combine(yg, wsel, x, mod5, layer, g3, tok_off=NP_TOK, n_tok=NS_TOK))


def _rot_cols(w, half):
    k, n = w.shape
    wb = w.reshape(k, n // (2 * half), 2, half)
    return jnp.stack([-wb[:, :, 1], wb[:, :, 0]], axis=2).reshape(k, n)


def _axis_tables(r, pos):
    inv = np.float32(ROPE_BASE) ** (-np.arange(0, r, 2, dtype=np.float32) / np.float32(r))
    ang = pos.astype(np.float32)[:, None] * inv[None, :]
    cos, sin = np.cos(ang), np.sin(ang)
    return np.concatenate([cos, cos], axis=1), np.concatenate([sin, sin], axis=1)


def _rope_tables(r):
    s = np.arange(DEC_SEQ)
    cr, sr = _axis_tables(r // 2, s // GRID_W)
    cc, sc = _axis_tables(r // 2, s % GRID_W)
    return np.concatenate([cr, cc], axis=1), np.concatenate([sr, sc], axis=1)


def _with_identity(tab, ident):
    return np.concatenate([np.full((TOK_TILE, tab.shape[1]), ident, np.float32), tab], axis=0)


@functools.lru_cache(maxsize=None)
def _rope_constants():
    c32, s32 = _rope_tables(MLA_ROPE)
    ones = np.ones((DEC_SEQ, MLA_NOPE), np.float32)
    pad1 = np.ones((DEC_SEQ, MLA_HEAD_PAD - MLA_NOPE - MLA_ROPE), np.float32)
    cq = np.concatenate([ones, c32, pad1], axis=1)
    sq = np.concatenate([0 * ones, s32, 0 * pad1], axis=1)
    c64, s64 = _rope_tables(GQA_HEAD_DIM)
    return {
        "mla_cq": _with_identity(cq, 1.0), "mla_sq": _with_identity(sq, 0.0),
        "mla_ck": _with_identity(c32, 1.0), "mla_sk": _with_identity(s32, 0.0),
        "gqa_c": _with_identity(np.concatenate([c64, c64], axis=1), 1.0),
        "gqa_s": _with_identity(np.concatenate([s64, s64], axis=1), 0.0),
    }


def _prep_tables():
    return {k: jnp.asarray(v, F32) for k, v in _rope_constants().items()}


def _prep_even(w_in, q_norm, w_q_up, kv_norm, w_kv_up, wgf, bgf, wgb, bgb):
    sizes = [MLA_Q_RANK, MLA_KV_RANK, MLA_ROPE, GLA_HEADS * GLA_DK, GLA_HEADS * GLA_DK,
             GLA_HEADS * GLA_DV, GLA_HEADS * GLA_DV, GLA_GATE_RANK, GLA_GATE_RANK]
    cq, ckv, kpe, gq, gk, gv, gr, gaf, gab = jnp.split(w_in, [int(s) for s in np.cumsum(sizes)[:-1]], axis=1)
    pad = jnp.zeros((D_MODEL, LANES - 2 * MLA_ROPE - 2 * GLA_GATE_RANK), F32)
    win = jnp.concatenate([cq, ckv, gq, gk, gv, gr, kpe, _rot_cols(kpe, MLA_ROPE // 4), gaf, gab, pad], axis=1)

    wq = w_q_up.reshape(MLA_Q_RANK, MLA_HEADS, MLA_NOPE + MLA_ROPE)
    nope, pe = wq[..., :MLA_NOPE], wq[..., MLA_NOPE:]
    pe_rot = _rot_cols(pe.reshape(MLA_Q_RANK, MLA_HEADS * MLA_ROPE), MLA_ROPE // 4).reshape(pe.shape)
    zpad = jnp.zeros((MLA_Q_RANK, MLA_HEADS, MLA_HEAD_PAD - MLA_NOPE - MLA_ROPE), F32)
    wq_main = jnp.concatenate([nope, pe, zpad], axis=-1).reshape(MLA_Q_RANK, MLA_QK_W)
    wq_rot = jnp.concatenate([0 * nope, pe_rot, zpad], axis=-1).reshape(MLA_Q_RANK, MLA_QK_W)

    wkv = w_kv_up.reshape(MLA_KV_RANK, MLA_HEADS, MLA_NOPE + MLA_V)
    knope, vv = wkv[..., :MLA_NOPE], wkv[..., MLA_NOPE:]
    wkk = jnp.concatenate([knope, jnp.zeros((MLA_KV_RANK, MLA_HEADS, MLA_HEAD_PAD - MLA_NOPE), F32)],
                          axis=-1).reshape(MLA_KV_RANK, MLA_QK_W)
    vpair = vv.reshape(MLA_KV_RANK, MLA_HEADS // 2, 2 * MLA_V)
    wkv_ext = jnp.concatenate([vpair, jnp.zeros((MLA_KV_RANK, MLA_HEADS // 2, LANES), F32)],
                              axis=-1).reshape(MLA_KV_RANK, MLA_VEXT_W)
    vbias = jnp.tile(jnp.concatenate([jnp.zeros((LANES,), F32), jnp.ones((LANES,), F32)]),
                     MLA_HEADS // 2).reshape(1, MLA_VEXT_W)
    epl = jnp.tile(jnp.concatenate([jnp.zeros((MLA_ROPE, MLA_NOPE), F32), jnp.eye(MLA_ROPE, dtype=F32),
                                    jnp.zeros((MLA_ROPE, MLA_HEAD_PAD - MLA_NOPE - MLA_ROPE), F32)], axis=1),
                   (1, MLA_HEADS))

    def gate_w(w, off):
        return jnp.zeros((LANES, GLA_HEADS * GLA_DK), F32).at[off:off + GLA_GATE_RANK].set(w)

    r = np.arange(CUMSUM_BLOCK)
    same = (r[:, None] // GLA_CHUNK) == (r[None, :] // GLA_CHUNK)
    lmat = jnp.asarray(same & (r[:, None] >= r[None, :]), BF16)
    umat = jnp.asarray(same & (r[:, None] <= r[None, :]), BF16)
    return {
        "win": win.astype(BF16), "qn": q_norm.reshape(1, -1), "wq": jnp.concatenate([wq_main, wq_rot], axis=1).astype(BF16),
        "kvn": kv_norm.reshape(1, -1), "wkk": wkk.astype(BF16), "wkv": wkv_ext.astype(BF16), "vbias": vbias,
        "epl": epl.astype(BF16), "wgf": gate_w(wgf, _S_GAF).astype(BF16), "bgf": bgf.reshape(1, -1),
        "wgb": gate_w(wgb, _S_GAB).astype(BF16), "bgb": bgb.reshape(1, -1), "lmat": lmat, "umat": umat,
    }


def _gqa_head_perm():
    heads = []
    for p in range(GQA_KV_HEADS // 2):
        for i in range(GQA_GROUP):
            heads += [(2 * p) * GQA_GROUP + i, (2 * p + 1) * GQA_GROUP + i]
    return np.asarray(heads)


def _prep_odd(w_in, w_out):
    perm = _gqa_head_perm()
    wq = w_in[:, :GQA_Q_W].reshape(D_MODEL, GQA_HEADS, GQA_HEAD_DIM)[:, perm].reshape(D_MODEL, GQA_Q_W)
    wk = w_in[:, GQA_Q_W:GQA_Q_W + GQA_KV_W]
    wv = w_in[:, GQA_Q_W + GQA_KV_W:].reshape(D_MODEL, GQA_KV_HEADS // 2, 2 * GQA_HEAD_DIM)
    wv_ext = jnp.concatenate([wv, jnp.zeros((D_MODEL, GQA_KV_HEADS // 2, LANES), F32)], axis=-1).reshape(D_MODEL, GQA_VEXT_W)
    win = jnp.concatenate([wq, _rot_cols(wq, GQA_HEAD_DIM // 4), wk, _rot_cols(wk, GQA_HEAD_DIM // 4), wv_ext], axis=1)
    vbias = jnp.tile(jnp.concatenate([jnp.zeros((LANES,), F32), jnp.ones((LANES,), F32)]),
                     GQA_KV_HEADS // 2).reshape(1, GQA_VEXT_W)
    wo = w_out.reshape(GQA_HEADS, GQA_HEAD_DIM, D_MODEL)[perm].reshape(GQA_Q_W, D_MODEL)
    return win.astype(BF16), vbias, wo.astype(BF16)


def _ext_v(v):
    rows = v.shape[0]
    vp = v.reshape(rows, GQA_KV_HEADS // 2, 2 * GQA_HEAD_DIM)
    return jnp.concatenate([vp, jnp.ones((rows, GQA_KV_HEADS // 2, LANES), v.dtype)], axis=-1).reshape(rows, GQA_VEXT_W)


def kernel(x_prompt, x_sample, cache_mla_ckv, cache_mla_kpe, state_gla_fwd, state_gla_bwd, cache_gqa_k, cache_gqa_v, c, c_ctx, w_mod, b_mod, norm_g, w_in_ab, mla_q_norm, mla_w_q_up, mla_kv_norm, mla_w_kv_up, gla_w_gate_f, gla_b_gate_f, gla_w_gate_b, gla_b_gate_b, gla_norm, w_out_ab, ffn_w_gate, ffn_w_up, ffn_w_down, w_in_c, gqa_sink, w_out_c, moe_w_router, moe_b_router, moe_w_gate, moe_w_up, moe_w_down):
    x_in = (x_prompt.reshape(NP_TOK, D_MODEL), x_sample.reshape(NS_TOK, D_MODEL))
    cvec =jnp.concatenate([c_ctx[None, :], c, jnp.zeros((MOD_ROWS - N_GROUPS, D_MODEL), F32)], axis=0)
    mod5 = _modulation(cvec, w_mod, b_mod).reshape(DEPTH, MOD_ROWS, N_MOD, 1, D_MODEL)
    tabs = _prep_tables()
    gvec = lambda l, j: norm_g[l, j].reshape(1, D_MODEL)

    wts = _prep_even(w_in_ab[0], mla_q_norm[0], mla_w_q_up[0], mla_kv_norm[0], mla_w_kv_up[0],
                     gla_w_gate_f[0], gla_b_gate_f[0], gla_w_gate_b[0], gla_b_gate_b[0])
    (q, k, v, ckv, kpe, gq, gk, gv, gr, bf, bb) = _even_in_proj(*x_in, mod5, 0, gvec(0, 0), wts, tabs)
    kc, vc = _cache_kv(cache_mla_ckv[:, 0].reshape(DEC_BATCH * PAST_LEN, MLA_KV_RANK),
                       cache_mla_kpe[:, 0].reshape(DEC_BATCH * PAST_LEN, MLA_ROPE), wts)
    oa_p = _mla_attention(q, [k], [v], n_batch=BATCH, seq_q=SEQ, q_tile=SEQ, tok_off=0, k_batch_rows=[SEQ])
    oa_s = _mla_attention(q, [k, kc], [v, vc], n_batch=DEC_BATCH, seq_q=DEC_SEQ, q_tile=MLA_Q_TILE,
                          tok_off=NP_TOK, k_batch_rows=[DEC_SEQ, PAST_LEN])
    gn = gla_norm[0].reshape(1, GLA_DV)
    zero_state = jnp.zeros((BATCH, GLA_HEADS, GLA_DK, GLA_DV), F32)
    ob_p, sf, sb = _gla(gq, gk, gv, gr, bf, bb, zero_state, zero_state, gn, n_batch=BATCH, seq=SEQ, tok_off=0)
    ob_s, _, _ = _gla(gq, gk, gv, gr, bf, bb, state_gla_fwd[:, 0], state_gla_bwd[:, 0], gn,
                      n_batch=DEC_BATCH, seq=DEC_SEQ, tok_off=NP_TOK)
    x, h = _out_proj([(oa_p, oa_s), (ob_p, ob_s)], w_out_ab[0].astype(BF16), x_in, mod5, 0, gvec(0, 1),
                     gvec(0, 2), BF16)
    x = _ffn(h, x, ffn_w_gate[0], ffn_w_up[0], ffn_w_down[0], mod5, 0, gvec(0, 3))

    win_c, vbias_c, wo_c = _prep_odd(w_in_c[0], w_out_c[0])
    qg, kg, kgb, vg, vgb = _odd_in_proj(x, mod5, 1, gvec(1, 0), win_c, vbias_c, tabs["gqa_c"], tabs["gqa_s"])
    sink = gqa_sink[0]
    og_p = _gqa_attention(sink, qg, None, None, kgb, vgb, n_batch=BATCH, seq_q=SEQ, q_tile=SEQ, tok_off=0, n_ctx=SEQ, local=False)
    kc_g = cache_gqa_k[:, 0].reshape(DEC_BATCH * PAST_LEN, GQA_KV_W).astype(BF16)
    vc_g = _ext_v(cache_gqa_v[:, 0].reshape(DEC_BATCH * PAST_LEN, GQA_KV_W)).astype(BF16)
    og_s = _gqa_attention(sink, qg, kgb, vgb, kc_g, vc_g, n_batch=DEC_BATCH, seq_q=DEC_SEQ, q_tile=GQA_Q_TILE,
                          tok_off=NP_TOK, n_ctx=PAST_LEN, local=True)
    x, h = _out_proj([(og_p, og_s)], wo_c, x, mod5, 1, gvec(1, 1), gvec(1, 2), F32)
    w_r = jnp.zeros((D_MODEL, LANES), F32).at[:, :N_EXPERTS].set(moe_w_router[0]).astype(BF16)
    b_r = jnp.zeros((1, LANES), F32).at[0, :N_EXPERTS].set(moe_b_router[0])
    y_p, y_s = _moe(h, x, w_r, b_r, moe_w_gate[0], moe_w_up[0], moe_w_down[0], mod5, 1, gvec(1, 3))

    y_prompt = y_p.reshape(BATCH, SEQ, D_MODEL)
    y_sample = y_s.reshape(DEC_BATCH, DEC_SEQ, D_MODEL)
    new_ckv = ckv[:NP_TOK].reshape(BATCH, 1, SEQ, MLA_KV_RANK)
    new_kpe = kpe[:NP_TOK].reshape(BATCH, 1, SEQ, MLA_ROPE)
    new_k = kg[:NP_TOK].reshape(BATCH, 1, SEQ, GQA_KV_HEADS, GQA_HEAD_DIM)
    new_v = vg[:NP_TOK].reshape(BATCH, 1, SEQ, GQA_KV_HEADS, GQA_HEAD_DIM)
    return (y_prompt, y_sample, new_ckv, new_kpe, sf[:, None], sb[:, None], new_k, new_v)
```

```python
import functools

import jax
import jax.numpy as jnp
import numpy as np
from jax import lax
from jax.experimental import pallas as pl
from jax.experimental.pallas import tpu as pltpu
from jax.experimental.pallas import tpu_sc as plsc

F32 = jnp.float32
BF16 = jnp.bfloat16

D_MODEL = 1024
BATCH = 16
SEQ = 256
DEPTH = 2
DEC_BATCH = 4
DEC_SEQ = 4096
PAST_LEN = 256
GRID_W = 64
N_MOD = 6
EPS = 1e-6
ROPE_BASE = 10000.0
NEG_INF = -1e30

MLA_HEADS = 8
MLA_NOPE = 64
MLA_ROPE = 32
MLA_V = 64
MLA_Q_RANK = 384
MLA_KV_RANK = 256
GLA_HEADS = 4
GLA_DK = 64
GLA_DV = 128
GLA_GATE_RANK = 16
GLA_GATE_NORM = 16.0
GLA_CHUNK = 64
GQA_HEADS = 16
GQA_KV_HEADS = 4
GQA_GROUP = GQA_HEADS // GQA_KV_HEADS
GQA_HEAD_DIM = 64
WINDOW = 128
D_FF = 2816
N_EXPERTS = 8
TOP_K = 2

NP_TOK = BATCH * SEQ
NS_TOK = DEC_BATCH * DEC_SEQ
N_TOK = NP_TOK + NS_TOK
N_GROUPS = 1 + DEC_BATCH
MOD_ROWS = 8

LANES = 128
MXU_COLS = 256
VMEM_LIMIT_BYTES = 56 * 1024 * 1024

TOK_TILE = 512
CUMSUM_BLOCK = 256
GLA_BLOCK_CHUNKS = MXU_COLS // GLA_CHUNK
GLA_CHUNK_UNROLL = 4
MLA_Q_TILE = 1024
MLA_Q_SUB = 256
GQA_Q_TILE = 256
MOE_ROW_TILE = 512
MOE_ROWS = TOP_K * N_TOK + N_EXPERTS * MOE_ROW_TILE
SC_CORES = 2
SC_SUBCORES = 16
SC_WORKERS = SC_CORES * SC_SUBCORES
SC_INDEX_BLOCK = 128
SC_GATHER_WINDOW = 32
FF_CHUNK = 1408

_C_CQ = 0
_C_CKV = _C_CQ + MLA_Q_RANK
_C_GQ = _C_CKV + MLA_KV_RANK
_C_GK = _C_GQ + GLA_HEADS * GLA_DK
_C_GV = _C_GK + GLA_HEADS * GLA_DK
_C_GR = _C_GV + GLA_HEADS * GLA_DV
_C_SMALL = _C_GR + GLA_HEADS * GLA_DV
IN_AB_EXT = _C_SMALL + LANES
_S_KPE, _S_KPER, _S_GAF, _S_GAB = 0, MLA_ROPE, 2 * MLA_ROPE, 2 * MLA_ROPE + GLA_GATE_RANK
MLA_HEAD_PAD = LANES
MLA_QK_W = MLA_HEADS * MLA_HEAD_PAD
MLA_VEXT_W = (MLA_HEADS // 2) * 2 * LANES


def _cparams(semantics):
    return pltpu.CompilerParams(dimension_semantics=semantics, vmem_limit_bytes=VMEM_LIMIT_BYTES)


def _const_spec(shape):
    nd = len(shape)
    return pl.BlockSpec(shape, lambda *_: (0,) * nd, pipeline_mode=pl.Buffered(1))


def _log_sigmoid(x):
    return jnp.minimum(x, 0.0) - jnp.log1p(jnp.exp(-jnp.abs(x)))


def _rms(x, g):
    return (x * lax.rsqrt(jnp.mean(x * x, axis=-1, keepdims=True) + EPS)) * g


def _modulate(x, g, shift, scale):
    return _rms(x, g) * (1.0 + scale) + shift


def _dot(a, b):
    return jnp.dot(a, b, preferred_element_type=F32)


def _dot_nt(a, b):
    return lax.dot_general(a, b, (((1,), (1,)), ((), ())), preferred_element_type=F32)


def _dot_tn(a, b):
    return lax.dot_general(a, b, (((0,), (0,)), ((), ())), preferred_element_type=F32)


def _split3(x):
    hi = x.astype(BF16)
    r1 = x - hi.astype(F32)
    mid = r1.astype(BF16)
    lo = (r1 - mid.astype(F32)).astype(BF16)
    return hi, mid, lo


def _lane_tile(x, reps):
    return jnp.concatenate([x] * reps, axis=1)


def _mod_kernel(c_ref, w_ref, b_ref, o_ref):
    c = c_ref[...]
    s = c * jax.nn.sigmoid(c)
    o_ref[...] = _dot(s.astype(BF16), w_ref[...].astype(BF16)) + b_ref[...]


def _modulation(cvec, w_mod, b_mod):
    ncol = N_MOD * D_MODEL
    blk = 1536
    return pl.pallas_call(
        _mod_kernel,
        out_shape=jax.ShapeDtypeStruct((DEPTH, MOD_ROWS, ncol), F32),
        grid=(DEPTH, ncol // blk),
        in_specs=[
            pl.BlockSpec((MOD_ROWS, D_MODEL), lambda l, j: (0, 0)),
            pl.BlockSpec((None, D_MODEL, blk), lambda l, j: (l, 0, j)),
            pl.BlockSpec((None, 1, blk), lambda l, j: (l, 0, j)),
        ],
        out_specs=pl.BlockSpec((None, MOD_ROWS, blk), lambda l, j: (l, 0, j)),
        compiler_params=_cparams(("arbitrary", "arbitrary")),
        name="modulation",
    )(cvec, w_mod, b_mod.reshape(DEPTH, 1, ncol))


def _mod_spec(layer, j, tile):
    tpg = NP_TOK // tile
    return pl.BlockSpec((None, None, None, 1, D_MODEL), lambda i: (layer, i // tpg, j, 0, 0))


def _tok_spec(tile, width):
    return pl.BlockSpec((tile, width), lambda i: (i, 0))


def _split_specs(tile, width):
    npt = NP_TOK // tile
    return [pl.BlockSpec((tile, width), lambda i: (jnp.minimum(i, npt - 1), 0)),
            pl.BlockSpec((tile, width), lambda i: (jnp.maximum(i - npt, 0), 0))]


def _pick(tile, p_ref, s_ref):
    return jnp.where(pl.program_id(0) < NP_TOK // tile, p_ref[...], s_ref[...])


def _rope_row_spec(tile, width):
    npt = NP_TOK // tile
    spt = DEC_SEQ // tile
    return pl.BlockSpec((tile, width), lambda i: (jnp.where(i < npt, 0, 1 + (i - npt) % spt), 0))


def _even_in_kernel(xp_ref, xs_ref, g_ref, shift_ref, scale_ref, win_ref, qn_ref, wq_ref, kvn_ref, wkk_ref,
                    wkv_ref, vbias_ref, epl_ref, wgf_ref, bgf_ref, wgb_ref, bgb_ref, lmat_ref,
                    umat_ref, cq_ref, sq_ref, ck_ref, sk_ref,
                    q_out, k_out, v_out, ckv_out, kpe_out, gq_out, gk_out, gv_out, gr_out,
                    bf_out, bb_out):
    h = _modulate(_pick(TOK_TILE, xp_ref, xs_ref), g_ref[...], shift_ref[...], scale_ref[...])
    z = _dot(h.astype(BF16), win_ref[...])

    cqn = _rms(z[:, _C_CQ:_C_CQ + MLA_Q_RANK], qn_ref[...]).astype(BF16)
    qf = _dot(cqn, wq_ref[...])
    cq_t = _lane_tile(cq_ref[...], MLA_HEADS)
    sq_t = _lane_tile(sq_ref[...], MLA_HEADS)
    q_out[...] = (qf[:, :MLA_QK_W] * cq_t + qf[:, MLA_QK_W:] * sq_t).astype(BF16)

    ckvn = _rms(z[:, _C_CKV:_C_CKV + MLA_KV_RANK], kvn_ref[...])
    ckv_out[...] = ckvn
    small = z[:, _C_SMALL:_C_SMALL + LANES]
    kpe = (small[:, _S_KPE:_S_KPE + MLA_ROPE] * ck_ref[...]
           + small[:, _S_KPER:_S_KPER + MLA_ROPE] * sk_ref[...])
    kpe_out[...] = kpe
    ckvn_b = ckvn.astype(BF16)
    k_out[...] = (_dot(ckvn_b, wkk_ref[...]) + _dot(kpe.astype(BF16), epl_ref[...])).astype(BF16)
    v_out[...] = (_dot(ckvn_b, wkv_ref[...]) + vbias_ref[...]).astype(BF16)

    gq_out[...] = z[:, _C_GQ:_C_GQ + GLA_HEADS * GLA_DK] * (GLA_DK ** -0.5)
    gk_out[...] = z[:, _C_GK:_C_GK + GLA_HEADS * GLA_DK]
    gv_out[...] = z[:, _C_GV:_C_GV + GLA_HEADS * GLA_DV].astype(BF16)
    gr_out[...] = z[:, _C_GR:_C_GR + GLA_HEADS * GLA_DV]

    small_b = small.astype(BF16)
    la_f = _log_sigmoid(_dot(small_b, wgf_ref[...]) + bgf_ref[...]) * (1.0 / GLA_GATE_NORM)
    la_b = _log_sigmoid(_dot(small_b, wgb_ref[...]) + bgb_ref[...]) * (1.0 / GLA_GATE_NORM)
    lmat = lmat_ref[...]
    umat = umat_ref[...]
    for r in range(TOK_TILE // CUMSUM_BLOCK):
        rows = slice(r * CUMSUM_BLOCK, (r + 1) * CUMSUM_BLOCK)
        f_hi, f_mid, f_lo = _split3(la_f[rows])
        bf_out[rows, :] = _dot(lmat, f_hi) + _dot(lmat, f_mid) + _dot(lmat, f_lo)
        b_hi, b_mid, b_lo = _split3(la_b[rows])
        bb_out[rows, :] = _dot(umat, b_hi) + _dot(umat, b_mid) + _dot(umat, b_lo)


def _even_in_proj(xp, xs, mod5, layer, g, wts, tabs):
    t = TOK_TILE
    out_widths = [(MLA_QK_W, BF16), (MLA_QK_W, BF16), (MLA_VEXT_W, BF16), (MLA_KV_RANK, F32),
                  (MLA_ROPE, F32), (GLA_HEADS * GLA_DK, F32), (GLA_HEADS * GLA_DK, F32),
                  (GLA_HEADS * GLA_DV, BF16), (GLA_HEADS * GLA_DV, F32),
                  (GLA_HEADS * GLA_DK, F32), (GLA_HEADS * GLA_DK, F32)]
    const_names = ["win", "qn", "wq", "kvn", "wkk", "wkv", "vbias", "epl", "wgf", "bgf", "wgb",
                   "bgb", "lmat", "umat"]
    consts = [wts[n] for n in const_names]
    in_specs = (_split_specs(t, D_MODEL)
                + [_const_spec((1, D_MODEL)), _mod_spec(layer, 0, t), _mod_spec(layer, 1, t)]
                + [_const_spec(c.shape) for c in consts]
                + [_rope_row_spec(t, LANES), _rope_row_spec(t, LANES),
                   _rope_row_spec(t, MLA_ROPE), _rope_row_spec(t, MLA_ROPE)])
    return pl.pallas_call(
        _even_in_kernel,
        out_shape=[jax.ShapeDtypeStruct((N_TOK, w), dt) for w, dt in out_widths],
        grid=(N_TOK // t,),
        in_specs=in_specs,
        out_specs=[_tok_spec(t, w) for w, _ in out_widths],
        compiler_params=_cparams(("parallel",)),
        name="even_in_proj",
    )(xp, xs, g, mod5, mod5, *consts, tabs["mla_cq"], tabs["mla_sq"], tabs["mla_ck"], tabs["mla_sk"])


def _cache_kv_kernel(ckv_ref, kpe_ref, wkk_ref, wkv_ref, vbias_ref, epl_ref, k_out, v_out):
    ckv_b = ckv_ref[...].astype(BF16)
    k_out[...] = (_dot(ckv_b, wkk_ref[...]) + _dot(kpe_ref[...].astype(BF16), epl_ref[...])).astype(BF16)
    v_out[...] = (_dot(ckv_b, wkv_ref[...]) + vbias_ref[...]).astype(BF16)


def _cache_kv(ckv, kpe, wts):
    n = ckv.shape[0]
    consts = [wts[k] for k in ("wkk", "wkv", "vbias", "epl")]
    return pl.pallas_call(
        _cache_kv_kernel,
        out_shape=[jax.ShapeDtypeStruct((n, MLA_QK_W), BF16), jax.ShapeDtypeStruct((n, MLA_VEXT_W), BF16)],
        grid=(1,),
        in_specs=[_const_spec(ckv.shape), _const_spec(kpe.shape)] + [_const_spec(c.shape) for c in consts],
        out_specs=[_const_spec((n, MLA_QK_W)), _const_spec((n, MLA_VEXT_W))],
        compiler_params=_cparams(("arbitrary",)),
        name="mla_cache_kv",
    )(ckv, kpe, *consts)


def _mla_attn_kernel(*refs, n_seg):
    q_ref = refs[0]
    k_refs = refs[1:1 + n_seg]
    v_refs = refs[1 + n_seg:1 + 2 * n_seg]
    o_ref = refs[1 + 2 * n_seg]
    scale = (MLA_NOPE + MLA_ROPE) ** -0.5
    c = scale * float(np.log2(np.e))
    tq = q_ref.shape[0]
    q_sub = min(tq, MLA_Q_SUB)
    kt = MXU_COLS
    tiles = [(si, r0) for si, k in enumerate(k_refs) for r0 in range(0, k.shape[0], kt)]
    lane = lax.broadcasted_iota(jnp.int32, (q_sub, LANES), 1)
    for qs in range(tq // q_sub):
        rows = slice(qs * q_sub, (qs + 1) * q_sub)
        res = []
        for j in range(2):
            hl = slice(j * LANES, (j + 1) * LANES)
            qj = q_ref[rows, hl]
            macc = None
            s_tiles = []
            for si, r0 in tiles:
                s = _dot_nt(qj, k_refs[si][r0:r0 + kt, hl])
                s_tiles.append(s)
                mt = jnp.maximum(s[:, :LANES], s[:, LANES:])
                macc = mt if macc is None else jnp.maximum(macc, mt)
            m = macc.max(axis=-1, keepdims=True)
            r = None
            for (si, r0), s in zip(tiles, s_tiles):
                p = jnp.exp2((s - m) * c).astype(BF16)
                rj = _dot(p, v_refs[si][r0:r0 + kt, :])
                r = rj if r is None else r + rj
            res.append(r[:, :LANES] / r[:, LANES:])
        o_ref[rows, :] = jnp.where(lane < MLA_V, res[0], res[1]).astype(BF16)


def _mla_attention(q, ks, vs, *, n_batch, seq_q, q_tile, tok_off, k_batch_rows):
    n_seg = len(ks)
    nq = seq_q // q_tile
    qoff = tok_off // q_tile
    grid = (n_batch, MLA_HEADS // 2, nq)
    in_specs = [pl.BlockSpec((q_tile, 2 * LANES), lambda b, hp, i: (qoff + b * nq + i, hp))]
    for s in range(n_seg):
        rows = k_batch_rows[s]
        off = (tok_off // rows) if s == 0 else 0
        in_specs.append(pl.BlockSpec((rows, 2 * LANES), functools.partial(lambda b, hp, i, off: (off + b, hp), off=off)))
    for s in range(n_seg):
        rows = k_batch_rows[s]
        off = (tok_off // rows) if s == 0 else 0
        in_specs.append(pl.BlockSpec((rows, 2 * LANES), functools.partial(lambda b, hp, i, off: (off + b, hp), off=off)))
    return pl.pallas_call(
        functools.partial(_mla_attn_kernel, n_seg=n_seg),
        out_shape=jax.ShapeDtypeStruct((n_batch * seq_q, MLA_HEADS * MLA_V), BF16),
        grid=grid,
        in_specs=in_specs,
        out_specs=pl.BlockSpec((q_tile, LANES), lambda b, hp, i: (b * nq + i, hp)),
        compiler_params=_cparams(("parallel", "parallel", "arbitrary")),
        name=f"mla_attention_{n_seg}seg",
    )(q, *ks, *vs)


def _gla_kernel(q_ref, k_ref, v_ref, gr_ref, bf_ref, bb_ref, s0f_ref, s0b_ref, gn_ref,
                o_ref, sf_ref, sb_ref, acc_ref, kdf_ref, kdb_ref, qdf_ref, qdb_ref, hist_ref, *, n_chunks):
    c = GLA_CHUNK
    cpb = min(n_chunks, GLA_BLOCK_CHUNKS)
    blk = cpb * c
    lane = lax.broadcasted_iota(jnp.int32, (blk, LANES), 1)
    lo = lane < GLA_DK
    row = lax.broadcasted_iota(jnp.int32, (blk, blk), 0)
    col = lax.broadcasted_iota(jnp.int32, (blk, blk), 1)
    chunk_bits = c.bit_length() - 1
    same_chunk = jnp.right_shift(row, chunk_bits) == jnp.right_shift(col, chunk_bits)
    tril = same_chunk & (row >= col)
    triu = same_chunk & (row <= col)
    zero_blk = jnp.zeros((GLA_DK, GLA_DV), F32)

    def pair_state_t(s_ref):
        blockdiag = jnp.concatenate(
            [jnp.concatenate([s_ref[0], zero_blk], axis=1),
             jnp.concatenate([zero_blk, s_ref[1]], axis=1)], axis=0)
        return blockdiag.T

    dirs = ((bf_ref, c // 2 - 1, c - 1, tril, kdf_ref, qdf_ref),
            (bb_ref, c // 2, 0, triu, kdb_ref, qdb_ref))
    sels = (lo, jnp.logical_not(lo))
    lo_c = lax.broadcasted_iota(jnp.int32, (c, LANES), 1) < GLA_DK
    sels_c = (lo_c, jnp.logical_not(lo_c))
    hcols = (slice(0, GLA_DV), slice(GLA_DV, 2 * GLA_DV))

    def block(r, carry):
        rows = pl.ds(pl.multiple_of(r * blk, blk), blk)
        q = q_ref[rows, :]
        k = k_ref[rows, :]
        v = v_ref[rows, :]

        def chunk_row(b, r):
            return jnp.concatenate([jnp.broadcast_to(b[ch * c + r:ch * c + r + 1, :], (c, LANES))
                                    for ch in range(cpb)], axis=0)

        for d, (b_ref, mid_row, last_row, causal, kd_ref, qd_ref) in enumerate(dirs):
            b = b_ref[rows, :]
            b_mid = chunk_row(b, mid_row)
            b_last = chunk_row(b, last_row)
            qe = q * jnp.exp(b - b_mid)
            ke = (k * jnp.exp(b_mid - b)).astype(BF16)
            kd_ref[rows, :] = (k * jnp.exp(b_last - b)).astype(BF16)
            qd_ref[rows, :] = (q * jnp.exp(b)).astype(BF16)
            for j in range(2):
                a = _dot_nt(jnp.where(sels[j], qe, 0.0).astype(BF16), ke)
                o = _dot(jnp.where(causal, a, 0.0).astype(BF16), v[:, hcols[j]])
                if d == 0:
                    acc_ref[rows, hcols[j]] = o
                else:
                    acc_ref[rows, hcols[j]] += o
        return carry

    lax.fori_loop(0, n_chunks // cpb, block, 0, unroll=min(2, n_chunks // cpb))

    def scan(i, carry):
        new = []
        for d, (b_ref, _, last_row, _, kd_ref, _) in enumerate(dirs):
            ci = i if d == 0 else n_chunks - 1 - i
            rows = pl.ds(pl.multiple_of(ci * c, c), c)
            hist_ref[d, ci] = carry[d].astype(BF16)
            grp = b_ref[pl.ds(pl.multiple_of(ci * c + (last_row // 8) * 8, 8), 8), :]
            b_last = grp[last_row % 8:last_row % 8 + 1, :]
            new.append(carry[d] * jnp.exp(b_last) + _dot_tn(v_ref[rows, :], kd_ref[rows, :]))
        return tuple(new)

    st_f, st_b = lax.fori_loop(0, n_chunks, scan, (pair_state_t(s0f_ref), pair_state_t(s0b_ref)),
                               unroll=GLA_CHUNK_UNROLL)

    def inter(ci, carry):
        rows = pl.ds(pl.multiple_of(ci * c, c), c)
        for d, (_, _, _, _, _, qd_ref) in enumerate(dirs):
            qd = qd_ref[rows, :]
            st = hist_ref[d, ci]
            for j in range(2):
                acc_ref[rows, hcols[j]] += _dot_nt(jnp.where(sels_c[j], qd, jnp.zeros_like(qd)), st[hcols[j], :])
        return carry

    lax.fori_loop(0, n_chunks, inter, 0, unroll=GLA_CHUNK_UNROLL)
    s_f = st_f.T
    s_b = st_b.T
    sf_ref[0] = s_f[:GLA_DK, :GLA_DV]
    sf_ref[1] = s_f[GLA_DK:, GLA_DV:]
    sb_ref[0] = s_b[:GLA_DK, :GLA_DV]
    sb_ref[1] = s_b[GLA_DK:, GLA_DV:]

    gn = gn_ref[...]
    for j in range(2):
        cols = slice(j * GLA_DV, (j + 1) * GLA_DV)
        gr = gr_ref[:, cols]
        o_ref[:, cols] = (_rms(acc_ref[:, cols], gn) * (gr * jax.nn.sigmoid(gr))).astype(BF16)


def _gla(gq, gk, gv, gr, bf, bb, s0f, s0b, gnorm, *, n_batch, seq, tok_off):
    n_chunks = seq // GLA_CHUNK
    boff = tok_off // seq
    hp = GLA_HEADS // 2
    tok = lambda w: pl.BlockSpec((seq, w), lambda b, p: (boff + b, p))
    st = pl.BlockSpec((None, 2, GLA_DK, GLA_DV), lambda b, p: (b, p, 0, 0))
    return pl.pallas_call(
        functools.partial(_gla_kernel, n_chunks=n_chunks),
        out_shape=[jax.ShapeDtypeStruct((n_batch * seq, GLA_HEADS * GLA_DV), BF16),
                   jax.ShapeDtypeStruct((n_batch, GLA_HEADS, GLA_DK, GLA_DV), F32),
                   jax.ShapeDtypeStruct((n_batch, GLA_HEADS, GLA_DK, GLA_DV), F32)],
        grid=(n_batch, hp),
        in_specs=[tok(2 * GLA_DK), tok(2 * GLA_DK), tok(2 * GLA_DV), tok(2 * GLA_DV),
                  tok(2 * GLA_DK), tok(2 * GLA_DK), st, st, _const_spec((1, GLA_DV))],
        out_specs=[pl.BlockSpec((seq, 2 * GLA_DV), lambda b, p: (b, p)), st, st],
        scratch_shapes=[pltpu.VMEM((seq, 2 * GLA_DV), F32)]
                       + [pltpu.VMEM((seq, 2 * GLA_DK), BF16)] * 4
                       + [pltpu.VMEM((2, n_chunks, 2 * GLA_DV, 2 * GLA_DK), BF16)],
        compiler_params=_cparams(("parallel", "parallel")),
        name=f"gla_seq{seq}",
    )(gq, gk, gv, gr, bf, bb, s0f, s0b, gnorm)


def _out_proj_kernel(*refs, n_o, x_split, route):
    t = TOK_TILE
    o = [_pick(t, refs[2 * j], refs[2 * j + 1]) for j in range(n_o)]
    rest = refs[2 * n_o:]
    w_ref = rest[0]
    if x_split:
        x_in = _pick(t, rest[1], rest[2])
        rest = rest[3:]
    else:
        x_in = rest[1][...]
        rest = rest[2:]
    g1_ref, gate_ref, g2_ref, shift_ref, scale_ref = rest[:5]
    rest = rest[5:]
    if route:
        router_in, rest = rest[:3], rest[3:]
    x_out, h_out = rest[:2]
    y = _dot(o[0] if n_o == 1 else jnp.concatenate(o, axis=1), w_ref[...])
    x = x_in + gate_ref[...] * _rms(y, g1_ref[...])
    x_out[...] = x
    h = _modulate(x, g2_ref[...], shift_ref[...], scale_ref[...])
    h_out[...] = h.astype(h_out.dtype)
    if route:
        _route(h.astype(BF16), *router_in, *rest[2:])


def _out_proj(os_, w, x, mod5, layer, g1, g2, h_dtype, router=None):
    t = TOK_TILE
    x_split = isinstance(x, tuple)
    in_specs, args = [], []
    for o_p, o_s in os_:
        in_specs += _split_specs(t, o_p.shape[1])
        args += [o_p, o_s]
    in_specs.append(_const_spec(w.shape))
    args.append(w)
    if x_split:
        in_specs += _split_specs(t, D_MODEL)
        args += list(x)
    else:
        in_specs.append(_tok_spec(t, D_MODEL))
        args.append(x)
    in_specs += [_const_spec((1, D_MODEL)), _mod_spec(layer, 2, t), _const_spec((1, D_MODEL)),
                 _mod_spec(layer, 3, t), _mod_spec(layer, 4, t)]
    args += [g1, mod5, g2, mod5, mod5]
    out_shape = [jax.ShapeDtypeStruct((N_TOK, D_MODEL), F32), jax.ShapeDtypeStruct((N_TOK, D_MODEL), h_dtype)]
    out_specs = [_tok_spec(t, D_MODEL), _tok_spec(t, D_MODEL)]
    scratch = []
    if router is not None:
        r = np.arange(t)
        ltri = jnp.asarray(r[:, None] > r[None, :], BF16)
        args += [router[0], router[1], ltri]
        in_specs += [_const_spec(router[0].shape), _const_spec(router[1].shape), _const_spec((t, t))]
        out_shape += [jax.ShapeDtypeStruct((N_TOK, LANES), F32), jax.ShapeDtypeStruct((N_TOK, LANES), jnp.int32),
                      jax.ShapeDtypeStruct((N_TOK, LANES), jnp.int32), jax.ShapeDtypeStruct((8, LANES), jnp.int32)]
        out_specs += [_tok_spec(t, LANES), _tok_spec(t, LANES), _tok_spec(t, LANES),
                      pl.BlockSpec((8, LANES), lambda i: (0, 0))]
        scratch = [pltpu.VMEM((8, LANES), F32)]
    return pl.pallas_call(
        functools.partial(_out_proj_kernel, n_o=len(os_), x_split=x_split, route=router is not None),
        out_shape=out_shape,
        grid=(N_TOK // t,),
        in_specs=in_specs,
        out_specs=out_specs,
        scratch_shapes=scratch,
        compiler_params=_cparams(("arbitrary",) if router is not None else ("parallel",)),
        name=f"out_proj_{len(os_)}",
    )(*args)


def _ffn_kernel(h_ref, x_ref, wg_ref, wu_ref, wd_ref, g_ref, gate_ref, x_out):
    h = h_ref[...]
    f = None
    for cidx in range(D_FF // FF_CHUNK):
        cols = slice(cidx * FF_CHUNK, (cidx + 1) * FF_CHUNK)
        a = _dot(h, wg_ref[:, cols].astype(BF16))
        u = _dot(h, wu_ref[:, cols].astype(BF16))
        fc = _dot(((a * jax.nn.sigmoid(a)) * u).astype(BF16), wd_ref[cols, :].astype(BF16))
        f = fc if f is None else f + fc
    x_out[...] = x_ref[...] + gate_ref[...] * _rms(f, g_ref[...])


def _ffn(h, x, wg, wu, wd, mod5, layer, g3):
    t = TOK_TILE
    return pl.pallas_call(
        _ffn_kernel,
        out_shape=jax.ShapeDtypeStruct((N_TOK, D_MODEL), F32),
        grid=(N_TOK // t,),
        in_specs=[_tok_spec(t, D_MODEL), _tok_spec(t, D_MODEL), _const_spec(wg.shape),
                  _const_spec(wu.shape), _const_spec(wd.shape), _const_spec((1, D_MODEL)),
                  _mod_spec(layer, 5, t)],
        out_specs=_tok_spec(t, D_MODEL),
        compiler_params=_cparams(("parallel",)),
        name="ffn_swiglu",
    )(h, x, wg, wu, wd, g3, mod5)


GQA_Q_W = GQA_HEADS * GQA_HEAD_DIM
GQA_KV_W = GQA_KV_HEADS * GQA_HEAD_DIM
GQA_VEXT_W = (GQA_KV_HEADS // 2) * 2 * LANES
_O_Q, _O_QR = 0, GQA_Q_W
_O_K, _O_KR = 2 * GQA_Q_W, 2 * GQA_Q_W + GQA_KV_W
_O_V = 2 * GQA_Q_W + 2 * GQA_KV_W
IN_C_EXT = _O_V + GQA_VEXT_W


def _odd_in_kernel(x_ref, g_ref, shift_ref, scale_ref, win_ref, vbias_ref, c_ref, s_ref,
                   q_out, k_out, kb_out, v_out, vb_out):
    h = _modulate(x_ref[...], g_ref[...], shift_ref[...], scale_ref[...])
    z = _dot(h.astype(BF16), win_ref[...])
    c_t = _lane_tile(c_ref[...], GQA_Q_W // LANES)
    s_t = _lane_tile(s_ref[...], GQA_Q_W // LANES)
    q = z[:, _O_Q:_O_Q + GQA_Q_W] * c_t + z[:, _O_QR:_O_QR + GQA_Q_W] * s_t
    q_out[...] = (q * (GQA_HEAD_DIM ** -0.5)).astype(BF16)
    k = (z[:, _O_K:_O_K + GQA_KV_W] * c_t[:, :GQA_KV_W]
         + z[:, _O_KR:_O_KR + GQA_KV_W] * s_t[:, :GQA_KV_W])
    k_out[...] = k
    kb_out[...] = k.astype(BF16)
    vext = z[:, _O_V:_O_V + GQA_VEXT_W] + vbias_ref[...]
    vb_out[...] = vext.astype(BF16)
    for p in range(GQA_KV_HEADS // 2):
        v_out[:, p * LANES:(p + 1) * LANES] = vext[:, 2 * p * LANES:(2 * p + 1) * LANES]


def _odd_in_proj(x, mod5, layer, g, win, vbias, tab_c, tab_s):
    t = TOK_TILE
    out_widths = [(GQA_Q_W, BF16), (GQA_KV_W, F32), (GQA_KV_W, BF16), (GQA_KV_W, F32), (GQA_VEXT_W, BF16)]
    return pl.pallas_call(
        _odd_in_kernel,
        out_shape=[jax.ShapeDtypeStruct((N_TOK, w), dt) for w, dt in out_widths],
        grid=(N_TOK // t,),
        in_specs=[_tok_spec(t, D_MODEL), _const_spec((1, D_MODEL)), _mod_spec(layer, 0, t),
                  _mod_spec(layer, 1, t), _const_spec(win.shape), _const_spec(vbias.shape),
                  _rope_row_spec(t, LANES), _rope_row_spec(t, LANES)],
        out_specs=[_tok_spec(t, w) for w, _ in out_widths],
        compiler_params=_cparams(("parallel",)),
        name="odd_in_proj",
    )(x, g, mod5, mod5, win, vbias, tab_c, tab_s)


def _gqa_kernel(sink_ref, q_ref, *refs, local_len):
    if local_len:
        kl_ref, vl_ref, kc_ref, vc_ref, o_ref = refs
    else:
        kc_ref, vc_ref, o_ref = refs
    tq = q_ref.shape[0]
    lane = lax.broadcasted_iota(jnp.int32, (tq, LANES), 1)
    lo = lane < GQA_HEAD_DIM
    if local_len:
        i = pl.program_id(1)
        q0 = i * tq
        seq = kl_ref.shape[0]
        kstart = pl.multiple_of(jnp.clip(q0 - WINDOW, 0, seq - local_len), LANES)
        qpos = q0 + lax.broadcasted_iota(jnp.int32, (tq, local_len), 0)
        kpos = kstart + lax.broadcasted_iota(jnp.int32, (tq, local_len), 1)
        band = jnp.abs(qpos - kpos) <= WINDOW
    for p in range(GQA_KV_HEADS // 2):
        kc = kc_ref[:, p * LANES:(p + 1) * LANES]
        vc = vc_ref[:, 2 * p * LANES:(2 * p + 2) * LANES]
        if local_len:
            kl = kl_ref[pl.ds(kstart, local_len), p * LANES:(p + 1) * LANES]
            vl = vl_ref[pl.ds(kstart, local_len), 2 * p * LANES:(2 * p + 2) * LANES]
        for blk in range(GQA_GROUP):
            cols = slice((p * GQA_GROUP + blk) * LANES, (p * GQA_GROUP + blk + 1) * LANES)
            qb = q_ref[:, cols]
            res = []
            for half in range(2):
                head = (2 * p + half) * GQA_GROUP + blk
                sink = sink_ref[head]
                qh = jnp.where(lo if half == 0 else jnp.logical_not(lo), qb, jnp.zeros_like(qb))
                s_c = _dot_nt(qh, kc)
                m = jnp.maximum(s_c.max(axis=-1, keepdims=True), sink)
                if local_len:
                    s_l = jnp.where(band, _dot_nt(qh, kl), NEG_INF)
                    m = jnp.maximum(m, s_l.max(axis=-1, keepdims=True))
                r = _dot(jnp.exp(s_c - m).astype(BF16), vc)
                if local_len:
                    r = r + _dot(jnp.exp(s_l - m).astype(BF16), vl)
                res.append(r[:, :LANES] / (r[:, LANES:] + jnp.exp(sink - m)))
            o_ref[:, cols] = jnp.where(lo, res[0], res[1]).astype(BF16)


def _gqa_attention(sink, q, k_loc, v_loc, k_ctx, v_ctx, *, n_batch, seq_q, q_tile, tok_off, n_ctx, local):
    nq = seq_q // q_tile
    qoff = tok_off // q_tile
    local_len = q_tile + 2 * WINDOW if local else 0
    in_specs = [pl.BlockSpec(memory_space=pltpu.SMEM),
                pl.BlockSpec((q_tile, GQA_Q_W), lambda b, i: (qoff + b * nq + i, 0))]
    args = [sink, q]
    if local:
        boff = tok_off // seq_q
        in_specs += [pl.BlockSpec((seq_q, GQA_KV_W), lambda b, i: (boff + b, 0)),
                     pl.BlockSpec((seq_q, GQA_VEXT_W), lambda b, i: (boff + b, 0))]
        args += [k_loc, v_loc]
    in_specs += [pl.BlockSpec((n_ctx, GQA_KV_W), lambda b, i: (b, 0)),
                 pl.BlockSpec((n_ctx, GQA_VEXT_W), lambda b, i: (b, 0))]
    args += [k_ctx, v_ctx]
    return pl.pallas_call(
        functools.partial(_gqa_kernel, local_len=local_len),
        out_shape=jax.ShapeDtypeStruct((n_batch * seq_q, GQA_Q_W), BF16),
        grid=(n_batch, nq),
        in_specs=in_specs,
        out_specs=pl.BlockSpec((q_tile, GQA_Q_W), lambda b, i: (b * nq + i, 0)),
        compiler_params=_cparams(("parallel", "arbitrary")),
        name="gqa_local" if local else "gqa_ctx",
    )(*args)


def _route(h, w_ref, b_ref, ltri_ref, wsel_out, isel_out, rank_out, cnt_out, carry_ref):
    @pl.when(pl.program_id(0) == 0)
    def _():
        carry_ref[...] = jnp.zeros_like(carry_ref)

    logits = _dot(h, w_ref[...]) + b_ref[...]
    lane = lax.broadcasted_iota(jnp.int32, logits.shape, 1)
    neg = float(np.finfo(np.float32).min)
    lg = jnp.where(lane < N_EXPERTS, logits, neg)
    v1 = lg.max(axis=-1, keepdims=True)
    i1 = jnp.min(jnp.where(lg == v1, lane, LANES), axis=-1, keepdims=True)
    lg2 = jnp.where(lane == i1, neg, lg)
    v2 = lg2.max(axis=-1, keepdims=True)
    i2 = jnp.min(jnp.where(lg2 == v2, lane, LANES), axis=-1, keepdims=True)
    e2 = jnp.exp(v2 - v1)
    den = 1.0 + e2
    wsel_out[...] = jnp.where(lane == 0, 1.0 / den, jnp.where(lane == 1, e2 / den, 0.0))
    isel_out[...] = jnp.where(lane == 0, i1, jnp.where(lane == 1, i2, 0))
    hit = jnp.where(lane == i1, 1.0, jnp.where(lane == i2, 1.0, 0.0))
    carry = carry_ref[...]
    rank_out[...] = (_dot(ltri_ref[...], hit.astype(BF16)) + carry[0:1, :]).astype(jnp.int32)
    carry = carry + jnp.sum(hit, axis=0, keepdims=True)
    carry_ref[...] = carry
    cnt_out[...] = carry.astype(jnp.int32)


def _route_tables(isel, rank, cnt):
    tm = MOE_ROW_TILE
    counts = cnt[0, :N_EXPERTS]
    padded = ((counts + tm - 1) // tm) * tm
    ends = jnp.cumsum(padded)
    base = ends - padded
    e_ids = jnp.arange(N_EXPERTS, dtype=jnp.int32)
    row = rank[:, :N_EXPERTS] + base[None, :]
    pos1 = jnp.sum(jnp.where(e_ids[None, :] == isel[:, 0:1], row, 0), axis=1)
    pos2 = jnp.sum(jnp.where(e_ids[None, :] == isel[:, 1:2], row, 0), axis=1)
    tile_start = jnp.arange(MOE_ROWS // tm, dtype=jnp.int32) * tm
    tile_expert = jnp.minimum(jnp.sum(tile_start[:, None] >= ends[None, :], axis=1), N_EXPERTS - 1).astype(jnp.int32)
    tile_valid = jnp.clip((base + counts)[tile_expert] - tile_start, 0, tm).astype(jnp.int32)
    tile_valid = jnp.where(tile_start < ends[-1], tile_valid, 0)
    return pos1, pos2, tile_expert, tile_valid


def _scatter_rows(rows, pos1, pos2, n_out):
    n_tok, d = rows.shape
    per_w = n_tok // SC_WORKERS
    w = SC_GATHER_WINDOW
    assert per_w * SC_WORKERS == n_tok and per_w % w == 0
    mesh = plsc.VectorSubcoreMesh(core_axis_name="core", subcore_axis_name="subcore")

    assert (per_w // w) % 2 == 0
    slot_types = [pltpu.VMEM((w,), jnp.int32), pltpu.VMEM((w,), jnp.int32), pltpu.VMEM((w, d), rows.dtype),
                  pltpu.SemaphoreType.DMA, pltpu.SemaphoreType.DMA]

    @functools.partial(
        pl.kernel, out_type=jax.ShapeDtypeStruct((n_out, d), rows.dtype), mesh=mesh,
        scratch_types=slot_types * 2, name="sc_scatter_rows")
    def scatter(x_hbm, p1_hbm, p2_hbm, o_hbm, *scratch):
        wid = lax.axis_index("subcore") * SC_CORES + lax.axis_index("core")
        base = wid * per_w
        slots = (scratch[:5], scratch[5:])

        @pl.loop(0, per_w // (2 * w))
        def _(g):
            loads = []
            for s, (i1_v, i2_v, rows_v, sem_in, _) in enumerate(slots):
                off = base + (2 * g + s) * w
                loads.append([pltpu.async_copy(p1_hbm.at[pl.ds(off, w)], i1_v, sem_in),
                              pltpu.async_copy(p2_hbm.at[pl.ds(off, w)], i2_v, sem_in),
                              pltpu.async_copy(x_hbm.at[pl.ds(off, w)], rows_v, sem_in)])
            stores = []
            for s, (i1_v, i2_v, rows_v, _, sem_out) in enumerate(slots):
                for cp in loads[s]:
                    cp.wait()
                stores += [pltpu.async_copy(rows_v, o_hbm.at[i1_v], sem_out),
                           pltpu.async_copy(rows_v, o_hbm.at[i2_v], sem_out)]
            for cp in stores:
                cp.wait()

    return scatter(rows, pos1, pos2)


def _gather_rows(table, idx):
    n_idx = idx.shape[0]
    d = table.shape[1]
    per_w = n_idx // SC_WORKERS
    assert per_w * SC_WORKERS == n_idx and per_w % SC_INDEX_BLOCK == 0
    mesh = plsc.VectorSubcoreMesh(core_axis_name="core", subcore_axis_name="subcore")

    w = SC_GATHER_WINDOW
    n_sub = SC_INDEX_BLOCK // w
    slot_types = [pltpu.VMEM((w, d), table.dtype), pltpu.SemaphoreType.DMA, pltpu.SemaphoreType.DMA]

    @functools.partial(
        pl.kernel, out_type=jax.ShapeDtypeStruct((n_idx, d), table.dtype), mesh=mesh,
        scratch_types=[pltpu.VMEM((SC_INDEX_BLOCK,), jnp.int32)] + slot_types * 2,
        name="sc_gather_rows")
    def gather(x_hbm, i_hbm, o_hbm, idx_v, *scratch):
        wid = lax.axis_index("subcore") * SC_CORES + lax.axis_index("core")
        base = wid * per_w
        slots = (scratch[:3], scratch[3:])

        @pl.loop(0, per_w // SC_INDEX_BLOCK)
        def _(g):
            off = base + g * SC_INDEX_BLOCK
            pltpu.sync_copy(i_hbm.at[pl.ds(off, SC_INDEX_BLOCK)], idx_v)

            def start_gather(s):
                rows_v, sem_in, _ = slots[s % 2]
                return pltpu.async_copy(x_hbm.at[idx_v.at[pl.ds(s * w, w)]], rows_v, sem_in)

            gathers = {0: start_gather(0)}
            writes = {}
            for s in range(n_sub):
                if s + 1 < n_sub:
                    if s >= 1:
                        writes[s - 1].wait()
                    gathers[s + 1] = start_gather(s + 1)
                gathers[s].wait()
                rows_v, _, sem_out = slots[s % 2]
                writes[s] = pltpu.async_copy(rows_v, o_hbm.at[pl.ds(off + s * w, w)], sem_out)
            writes[n_sub - 2].wait()
            writes[n_sub - 1].wait()

    return gather(table, idx)


def _expert_ffn_kernel(te_ref, nv_ref, x_ref, wg_ref, wu_ref, wd_ref, y_out):
    n_valid = nv_ref[pl.program_id(0)]

    @pl.when(n_valid > 0)
    def _():
        row = lax.broadcasted_iota(jnp.int32, x_ref.shape, 0)
        h = jnp.where(row < n_valid, x_ref[...], 0.0).astype(BF16)
        f = None
        for cidx in range(D_FF // FF_CHUNK):
            cols = slice(cidx * FF_CHUNK, (cidx + 1) * FF_CHUNK)
            a = _dot(h, wg_ref[:, cols].astype(BF16))
            u = _dot(h, wu_ref[:, cols].astype(BF16))
            fc = _dot(((a * jax.nn.sigmoid(a)) * u).astype(BF16), wd_ref[cols, :].astype(BF16))
            f = fc if f is None else f + fc
        y_out[...] = f

    @pl.when(n_valid == 0)
    def _():
        y_out[...] = jnp.zeros_like(y_out)


def _expert_ffn(xs, tile_expert, tile_valid, wg, wu, wd):
    tm = MOE_ROW_TILE
    wspec = lambda shape: pl.BlockSpec((None,) + shape, lambda j, te, nu: (te[j], 0, 0),
                                       pipeline_mode=pl.Buffered(1))
    return pl.pallas_call(
        _expert_ffn_kernel,
        out_shape=jax.ShapeDtypeStruct((MOE_ROWS, D_MODEL), F32),
        grid_spec=pltpu.PrefetchScalarGridSpec(
            num_scalar_prefetch=2,
            grid=(MOE_ROWS // tm,),
            in_specs=[pl.BlockSpec((tm, D_MODEL), lambda j, te, nu: (j, 0)),
                      wspec((D_MODEL, D_FF)), wspec((D_MODEL, D_FF)), wspec((D_FF, D_MODEL))],
            out_specs=pl.BlockSpec((tm, D_MODEL), lambda j, te, nu: (j, 0)),
        ),
        compiler_params=_cparams(("arbitrary",)),
        name="moe_expert_ffn",
    )(tile_expert, tile_valid, xs, wg, wu, wd)


def _moe_combine_kernel(y1_ref, y2_ref, wsel_ref, x_ref, g_ref, gate_ref, x_out):
    w = wsel_ref[...]
    f = w[:, 0:1] * y1_ref[...] + w[:, 1:2] * y2_ref[...]
    x_out[...] = x_ref[...] + gate_ref[...] * _rms(f, g_ref[...])


def _moe_combine(yg, wsel, x, mod5, layer, g3, *, tok_off, n_tok):
    t = TOK_TILE
    nt = N_TOK // t
    off = tok_off // t
    tpg = NP_TOK // t
    tok = lambda w, shift: pl.BlockSpec((t, w), lambda i: (off + shift + i, 0))
    return pl.pallas_call(
        _moe_combine_kernel,
        out_shape=jax.ShapeDtypeStruct((n_tok, D_MODEL), F32),
        grid=(n_tok // t,),
        in_specs=[tok(D_MODEL, 0), tok(D_MODEL, nt), tok(LANES, 0), tok(D_MODEL, 0), _const_spec((1, D_MODEL)),
                  pl.BlockSpec((None, None, None, 1, D_MODEL), lambda i: (layer, (off + i) // tpg, 5, 0, 0))],
        out_specs=_tok_spec(t, D_MODEL),
        compiler_params=_cparams(("parallel",)),
        name="moe_combine",
    )(yg, yg, wsel, x, g3, mod5)


def _moe(h, x, routing, wg, wu, wd, mod5, layer, g3):
    wsel, isel, rank, cnt = routing
    pos1, pos2, tile_expert, tile_valid = _route_tables(isel, rank, cnt)
    xs = _scatter_rows(h, pos1, pos2, MOE_ROWS)
    ys = _expert_ffn(xs, tile_expert, tile_valid, wg, wu, wd)
    yg = _gather_rows(ys, jnp.concatenate([pos1, pos2]))
    return (_moe_combine(yg, wsel, x, mod5, layer, g3, tok_off=0, n_tok=NP_TOK),
            _moe_combine(yg, wsel, x, mod5, layer, g3, tok_off=NP_TOK, n_tok=NS_TOK))


def _rot_cols(w, half):
    k, n = w.shape
    wb = w.reshape(k, n // (2 * half), 2, half)
    return jnp.stack([-wb[:, :, 1], wb[:, :, 0]], axis=2).reshape(k, n)


def _axis_tables(r, pos):
    inv = np.float32(ROPE_BASE) ** (-np.arange(0, r, 2, dtype=np.float32) / np.float32(r))
    ang = pos.astype(np.float32)[:, None] * inv[None, :]
    cos, sin = np.cos(ang), np.sin(ang)
    return np.concatenate([cos, cos], axis=1), np.concatenate([sin, sin], axis=1)


def _rope_tables(r):
    s = np.arange(DEC_SEQ)
    cr, sr = _axis_tables(r // 2, s // GRID_W)
    cc, sc = _axis_tables(r // 2, s % GRID_W)
    return np.concatenate([cr, cc], axis=1), np.concatenate([sr, sc], axis=1)


def _with_identity(tab, ident):
    return np.concatenate([np.full((TOK_TILE, tab.shape[1]), ident, np.float32), tab], axis=0)


@functools.lru_cache(maxsize=None)
def _rope_constants():
    c32, s32 = _rope_tables(MLA_ROPE)
    ones = np.ones((DEC_SEQ, MLA_NOPE), np.float32)
    pad1 = np.ones((DEC_SEQ, MLA_HEAD_PAD - MLA_NOPE - MLA_ROPE), np.float32)
    cq = np.concatenate([ones, c32, pad1], axis=1)
    sq = np.concatenate([0 * ones, s32, 0 * pad1], axis=1)
    c64, s64 = _rope_tables(GQA_HEAD_DIM)
    return {
        "mla_cq": _with_identity(cq, 1.0), "mla_sq": _with_identity(sq, 0.0),
        "mla_ck": _with_identity(c32, 1.0), "mla_sk": _with_identity(s32, 0.0),
        "gqa_c": _with_identity(np.concatenate([c64, c64], axis=1), 1.0),
        "gqa_s": _with_identity(np.concatenate([s64, s64], axis=1), 0.0),
    }


def _prep_tables():
    return {k: jnp.asarray(v, F32) for k, v in _rope_constants().items()}


def _prep_even(w_in, q_norm, w_q_up, kv_norm, w_kv_up, wgf, bgf, wgb, bgb):
    sizes = [MLA_Q_RANK, MLA_KV_RANK, MLA_ROPE, GLA_HEADS * GLA_DK, GLA_HEADS * GLA_DK,
             GLA_HEADS * GLA_DV, GLA_HEADS * GLA_DV, GLA_GATE_RANK, GLA_GATE_RANK]
    cq, ckv, kpe, gq, gk, gv, gr, gaf, gab = jnp.split(w_in, [int(s) for s in np.cumsum(sizes)[:-1]], axis=1)
    pad = jnp.zeros((D_MODEL, LANES - 2 * MLA_ROPE - 2 * GLA_GATE_RANK), F32)
    win = jnp.concatenate([cq, ckv, gq, gk, gv, gr, kpe, _rot_cols(kpe, MLA_ROPE // 4), gaf, gab, pad], axis=1)

    wq = w_q_up.reshape(MLA_Q_RANK, MLA_HEADS, MLA_NOPE + MLA_ROPE)
    nope, pe = wq[..., :MLA_NOPE], wq[..., MLA_NOPE:]
    pe_rot = _rot_cols(pe.reshape(MLA_Q_RANK, MLA_HEADS * MLA_ROPE), MLA_ROPE // 4).reshape(pe.shape)
    zpad = jnp.zeros((MLA_Q_RANK, MLA_HEADS, MLA_HEAD_PAD - MLA_NOPE - MLA_ROPE), F32)
    wq_main = jnp.concatenate([nope, pe, zpad], axis=-1).reshape(MLA_Q_RANK, MLA_QK_W)
    wq_rot = jnp.concatenate([0 * nope, pe_rot, zpad], axis=-1).reshape(MLA_Q_RANK, MLA_QK_W)

    wkv = w_kv_up.reshape(MLA_KV_RANK, MLA_HEADS, MLA_NOPE + MLA_V)
    knope, vv = wkv[..., :MLA_NOPE], wkv[..., MLA_NOPE:]
    wkk = jnp.concatenate([knope, jnp.zeros((MLA_KV_RANK, MLA_HEADS, MLA_HEAD_PAD - MLA_NOPE), F32)],
                          axis=-1).reshape(MLA_KV_RANK, MLA_QK_W)
    vpair = vv.reshape(MLA_KV_RANK, MLA_HEADS // 2, 2 * MLA_V)
    wkv_ext = jnp.concatenate([vpair, jnp.zeros((MLA_KV_RANK, MLA_HEADS // 2, LANES), F32)],
                              axis=-1).reshape(MLA_KV_RANK, MLA_VEXT_W)
    vbias = jnp.tile(jnp.concatenate([jnp.zeros((LANES,), F32), jnp.ones((LANES,), F32)]),
                     MLA_HEADS // 2).reshape(1, MLA_VEXT_W)
    epl = jnp.tile(jnp.concatenate([jnp.zeros((MLA_ROPE, MLA_NOPE), F32), jnp.eye(MLA_ROPE, dtype=F32),
                                    jnp.zeros((MLA_ROPE, MLA_HEAD_PAD - MLA_NOPE - MLA_ROPE), F32)], axis=1),
                   (1, MLA_HEADS))

    def gate_w(w, off):
        return jnp.zeros((LANES, GLA_HEADS * GLA_DK), F32).at[off:off + GLA_GATE_RANK].set(w)

    r = np.arange(CUMSUM_BLOCK)
    same = (r[:, None] // GLA_CHUNK) == (r[None, :] // GLA_CHUNK)
    lmat = jnp.asarray(same & (r[:, None] >= r[None, :]), BF16)
    umat = jnp.asarray(same & (r[:, None] <= r[None, :]), BF16)
    return {
        "win": win.astype(BF16), "qn": q_norm.reshape(1, -1), "wq": jnp.concatenate([wq_main, wq_rot], axis=1).astype(BF16),
        "kvn": kv_norm.reshape(1, -1), "wkk": wkk.astype(BF16), "wkv": wkv_ext.astype(BF16), "vbias": vbias,
        "epl": epl.astype(BF16), "wgf": gate_w(wgf, _S_GAF).astype(BF16), "bgf": bgf.reshape(1, -1),
        "wgb": gate_w(wgb, _S_GAB).astype(BF16), "bgb": bgb.reshape(1, -1), "lmat": lmat, "umat": umat,
    }


def _gqa_head_perm():
    heads = []
    for p in range(GQA_KV_HEADS // 2):
        for i in range(GQA_GROUP):
            heads += [(2 * p) * GQA_GROUP + i, (2 * p + 1) * GQA_GROUP + i]
    return np.asarray(heads)


def _prep_odd(w_in, w_out):
    perm = _gqa_head_perm()
    wq = w_in[:, :GQA_Q_W].reshape(D_MODEL, GQA_HEADS, GQA_HEAD_DIM)[:, perm].reshape(D_MODEL, GQA_Q_W)
    wk = w_in[:, GQA_Q_W:GQA_Q_W + GQA_KV_W]
    wv = w_in[:, GQA_Q_W + GQA_KV_W:].reshape(D_MODEL, GQA_KV_HEADS // 2, 2 * GQA_HEAD_DIM)
    wv_ext = jnp.concatenate([wv, jnp.zeros((D_MODEL, GQA_KV_HEADS // 2, LANES), F32)], axis=-1).reshape(D_MODEL, GQA_VEXT_W)
    win = jnp.concatenate([wq, _rot_cols(wq, GQA_HEAD_DIM // 4), wk, _rot_cols(wk, GQA_HEAD_DIM // 4), wv_ext], axis=1)
    vbias = jnp.tile(jnp.concatenate([jnp.zeros((LANES,), F32), jnp.ones((LANES,), F32)]),
                     GQA_KV_HEADS // 2).reshape(1, GQA_VEXT_W)
    wo = w_out.reshape(GQA_HEADS, GQA_HEAD_DIM, D_MODEL)[perm].reshape(GQA_Q_W, D_MODEL)
    return win.astype(BF16), vbias, wo.astype(BF16)


def _ext_v(v):
    rows = v.shape[0]
    vp = v.reshape(rows, GQA_KV_HEADS // 2, 2 * GQA_HEAD_DIM)
    return jnp.concatenate([vp, jnp.ones((rows, GQA_KV_HEADS // 2, LANES), v.dtype)], axis=-1).reshape(rows, GQA_VEXT_W)


def kernel(x_prompt, x_sample, cache_mla_ckv, cache_mla_kpe, state_gla_fwd, state_gla_bwd, cache_gqa_k, cache_gqa_v, c, c_ctx, w_mod, b_mod, norm_g, w_in_ab, mla_q_norm, mla_w_q_up, mla_kv_norm, mla_w_kv_up, gla_w_gate_f, gla_b_gate_f, gla_w_gate_b, gla_b_gate_b, gla_norm, w_out_ab, ffn_w_gate, ffn_w_up, ffn_w_down, w_in_c, gqa_sink, w_out_c, moe_w_router, moe_b_router, moe_w_gate, moe_w_up, moe_w_down):
    x_in = (x_prompt.reshape(NP_TOK, D_MODEL), x_sample.reshape(NS_TOK, D_MODEL))
    cvec =jnp.concatenate([c_ctx[None, :], c, jnp.zeros((MOD_ROWS - N_GROUPS, D_MODEL), F32)], axis=0)
    mod5 = _modulation(cvec, w_mod, b_mod).reshape(DEPTH, MOD_ROWS, N_MOD, 1, D_MODEL)
    tabs = _prep_tables()
    gvec = lambda l, j: norm_g[l, j].reshape(1, D_MODEL)

    wts = _prep_even(w_in_ab[0], mla_q_norm[0], mla_w_q_up[0], mla_kv_norm[0], mla_w_kv_up[0],
                     gla_w_gate_f[0], gla_b_gate_f[0], gla_w_gate_b[0], gla_b_gate_b[0])
    (q, k, v, ckv, kpe, gq, gk, gv, gr, bf, bb) = _even_in_proj(*x_in, mod5, 0, gvec(0, 0), wts, tabs)
    kc, vc = _cache_kv(cache_mla_ckv[:, 0].reshape(DEC_BATCH * PAST_LEN, MLA_KV_RANK),
                       cache_mla_kpe[:, 0].reshape(DEC_BATCH * PAST_LEN, MLA_ROPE), wts)
    oa_p = _mla_attention(q, [k], [v], n_batch=BATCH, seq_q=SEQ, q_tile=SEQ, tok_off=0, k_batch_rows=[SEQ])
    oa_s = _mla_attention(q, [k, kc], [v, vc], n_batch=DEC_BATCH, seq_q=DEC_SEQ, q_tile=MLA_Q_TILE,
                          tok_off=NP_TOK, k_batch_rows=[DEC_SEQ, PAST_LEN])
    gn = gla_norm[0].reshape(1, GLA_DV)
    zero_state = jnp.zeros((BATCH, GLA_HEADS, GLA_DK, GLA_DV), F32)
    ob_p, sf, sb = _gla(gq, gk, gv, gr, bf, bb, zero_state, zero_state, gn, n_batch=BATCH, seq=SEQ, tok_off=0)
    ob_s, _, _ = _gla(gq, gk, gv, gr, bf, bb, state_gla_fwd[:, 0], state_gla_bwd[:, 0], gn,
                      n_batch=DEC_BATCH, seq=DEC_SEQ, tok_off=NP_TOK)
    x, h = _out_proj([(oa_p, oa_s), (ob_p, ob_s)], w_out_ab[0].astype(BF16), x_in, mod5, 0, gvec(0, 1),
                     gvec(0, 2), BF16)
    x = _ffn(h, x, ffn_w_gate[0], ffn_w_up[0], ffn_w_down[0], mod5, 0, gvec(0, 3))

    win_c, vbias_c, wo_c = _prep_odd(w_in_c[0], w_out_c[0])
    qg, kg, kgb, vg, vgb = _odd_in_proj(x, mod5, 1, gvec(1, 0), win_c, vbias_c, tabs["gqa_c"], tabs["gqa_s"])
    sink = gqa_sink[0]
    og_p = _gqa_attention(sink, qg, None, None, kgb, vgb, n_batch=BATCH, seq_q=SEQ, q_tile=SEQ, tok_off=0, n_ctx=SEQ, local=False)
    kc_g = cache_gqa_k[:, 0].reshape(DEC_BATCH * PAST_LEN, GQA_KV_W).astype(BF16)
    vc_g = _ext_v(cache_gqa_v[:, 0].reshape(DEC_BATCH * PAST_LEN, GQA_KV_W)).astype(BF16)
    og_s = _gqa_attention(sink, qg, kgb, vgb, kc_g, vc_g, n_batch=DEC_BATCH, seq_q=DEC_SEQ, q_tile=GQA_Q_TILE,
                          tok_off=NP_TOK, n_ctx=PAST_LEN, local=True)
    w_r = jnp.zeros((D_MODEL, LANES), F32).at[:, :N_EXPERTS].set(moe_w_router[0]).astype(BF16)
    b_r = jnp.zeros((1, LANES), F32).at[0, :N_EXPERTS].set(moe_b_router[0])
    x, h, *routing = _out_proj([(og_p, og_s)], wo_c, x, mod5, 1, gvec(1, 1), gvec(1, 2), F32, router=(w_r, b_r))
    y_p, y_s = _moe(h, x, routing, moe_w_gate[0], moe_w_up[0], moe_w_down[0], mod5, 1, gvec(1, 3))

    y_prompt = y_p.reshape(BATCH, SEQ, D_MODEL)
    y_sample = y_s.reshape(DEC_BATCH, DEC_SEQ, D_MODEL)
    new_ckv = ckv[:NP_TOK].reshape(BATCH, 1, SEQ, MLA_KV_RANK)
    new_kpe = kpe[:NP_TOK].reshape(BATCH, 1, SEQ, MLA_ROPE)
    new_k = kg[:NP_TOK].reshape(BATCH, 1, SEQ, GQA_KV_HEADS, GQA_HEAD_DIM)
    new_v = vg[:NP_TOK].reshape(BATCH, 1, SEQ, GQA_KV_HEADS, GQA_HEAD_DIM)
    return (y_prompt, y_sample, new_ckv, new_kpe, sf[:, None], sb[:, None], new_k, new_v)
```

```python
import functools

import jax
import jax.numpy as jnp
import numpy as np
from jax import lax
from jax.experimental import pallas as pl
from jax.experimental.pallas import tpu as pltpu
from jax.experimental.pallas import tpu_sc as plsc

F32 = jnp.float32
BF16 = jnp.bfloat16

D_MODEL = 1024
BATCH = 16
SEQ = 256
DEPTH = 2
DEC_BATCH = 4
DEC_SEQ = 4096
PAST_LEN = 256
GRID_W = 64
N_MOD = 6
EPS = 1e-6
ROPE_BASE = 10000.0
NEG_INF = -1e30

MLA_HEADS = 8
MLA_NOPE = 64
MLA_ROPE = 32
MLA_V = 64
MLA_Q_RANK = 384
MLA_KV_RANK = 256
GLA_HEADS = 4
GLA_DK = 64
GLA_DV = 128
GLA_GATE_RANK = 16
GLA_GATE_NORM = 16.0
GLA_CHUNK = 64
GQA_HEADS = 16
GQA_KV_HEADS = 4
GQA_GROUP = GQA_HEADS // GQA_KV_HEADS
GQA_HEAD_DIM = 64
WINDOW = 128
D_FF = 2816
N_EXPERTS = 8
TOP_K = 2

NP_TOK = BATCH * SEQ
NS_TOK = DEC_BATCH * DEC_SEQ
N_TOK = NP_TOK + NS_TOK
N_GROUPS = 1 + DEC_BATCH
MOD_ROWS = 8

LANES = 128
MXU_COLS = 256
VMEM_LIMIT_BYTES = 56 * 1024 * 1024

TOK_TILE = 512
CUMSUM_BLOCK = 256
GLA_BLOCK_CHUNKS = MXU_COLS // GLA_CHUNK
GLA_CHUNK_UNROLL = 4
MLA_Q_TILE = 1024
MLA_Q_SUB = 256
GQA_Q_TILE = 256
MOE_ROW_TILE = 512
MOE_ROWS = TOP_K * N_TOK + N_EXPERTS * MOE_ROW_TILE
SC_CORES = 2
SC_SUBCORES = 16
SC_WORKERS = SC_CORES * SC_SUBCORES
SC_INDEX_BLOCK = 128
SC_GATHER_WINDOW = 32
FF_CHUNK = 1408

_C_CQ = 0
_C_CKV = _C_CQ + MLA_Q_RANK
_C_GQ = _C_CKV + MLA_KV_RANK
_C_GK = _C_GQ + GLA_HEADS * GLA_DK
_C_GV = _C_GK + GLA_HEADS * GLA_DK
_C_GR = _C_GV + GLA_HEADS * GLA_DV
_C_SMALL = _C_GR + GLA_HEADS * GLA_DV
IN_AB_EXT = _C_SMALL + LANES
_S_KPE, _S_KPER, _S_GAF, _S_GAB = 0, MLA_ROPE, 2 * MLA_ROPE, 2 * MLA_ROPE + GLA_GATE_RANK
MLA_HEAD_PAD = LANES
MLA_QK_W = MLA_HEADS * MLA_HEAD_PAD
MLA_VEXT_W = (MLA_HEADS // 2) * 2 * LANES


def _cparams(semantics):
    return pltpu.CompilerParams(dimension_semantics=semantics, vmem_limit_bytes=VMEM_LIMIT_BYTES)


def _const_spec(shape):
    nd = len(shape)
    return pl.BlockSpec(shape, lambda *_: (0,) * nd, pipeline_mode=pl.Buffered(1))


def _log_sigmoid(x):
    return jnp.minimum(x, 0.0) - jnp.log1p(jnp.exp(-jnp.abs(x)))


def _rms(x, g):
    return (x * lax.rsqrt(jnp.mean(x * x, axis=-1, keepdims=True) + EPS)) * g


def _modulate(x, g, shift, scale):
    return _rms(x, g) * (1.0 + scale) + shift


def _dot(a, b):
    return jnp.dot(a, b, preferred_element_type=F32)


def _dot_nt(a, b):
    return lax.dot_general(a, b, (((1,), (1,)), ((), ())), preferred_element_type=F32)


def _dot_tn(a, b):
    return lax.dot_general(a, b, (((0,), (0,)), ((), ())), preferred_element_type=F32)


def _split3(x):
    hi = x.astype(BF16)
    r1 = x - hi.astype(F32)
    mid = r1.astype(BF16)
    lo = (r1 - mid.astype(F32)).astype(BF16)
    return hi, mid, lo


def _lane_tile(x, reps):
    return jnp.concatenate([x] * reps, axis=1)


def _mod_kernel(c_ref, w_ref, b_ref, o_ref):
    c = c_ref[...]
    s = c * jax.nn.sigmoid(c)
    o_ref[...] = _dot(s.astype(BF16), w_ref[...].astype(BF16)) + b_ref[...]


def _modulation(cvec, w_mod, b_mod):
    ncol = N_MOD * D_MODEL
    blk = 1536
    return pl.pallas_call(
        _mod_kernel,
        out_shape=jax.ShapeDtypeStruct((DEPTH, MOD_ROWS, ncol), F32),
        grid=(DEPTH, ncol // blk),
        in_specs=[
            pl.BlockSpec((MOD_ROWS, D_MODEL), lambda l, j: (0, 0)),
            pl.BlockSpec((None, D_MODEL, blk), lambda l, j: (l, 0, j)),
            pl.BlockSpec((None, 1, blk), lambda l, j: (l, 0, j)),
        ],
        out_specs=pl.BlockSpec((None, MOD_ROWS, blk), lambda l, j: (l, 0, j)),
        compiler_params=_cparams(("arbitrary", "arbitrary")),
        name="modulation",
    )(cvec, w_mod, b_mod.reshape(DEPTH, 1, ncol))


def _mod_spec(layer, j, tile):
    tpg = NP_TOK // tile
    return pl.BlockSpec((None, None, None, 1, D_MODEL), lambda i: (layer, i // tpg, j, 0, 0))


def _tok_spec(tile, width):
    return pl.BlockSpec((tile, width), lambda i: (i, 0))


def _split_specs(tile, width):
    npt = NP_TOK // tile
    return [pl.BlockSpec((tile, width), lambda i: (jnp.minimum(i, npt - 1), 0)),
            pl.BlockSpec((tile, width), lambda i: (jnp.maximum(i - npt, 0), 0))]


def _pick(tile, p_ref, s_ref):
    return jnp.where(pl.program_id(0) < NP_TOK // tile, p_ref[...], s_ref[...])


def _rope_row_spec(tile, width):
    npt = NP_TOK // tile
    spt = DEC_SEQ // tile
    return pl.BlockSpec((tile, width), lambda i: (jnp.where(i < npt, 0, 1 + (i - npt) % spt), 0))


def _even_in_kernel(xp_ref, xs_ref, g_ref, shift_ref, scale_ref, win_ref, qn_ref, wq_ref, kvn_ref, wkk_ref,
                    wkv_ref, vbias_ref, epl_ref, wgf_ref, bgf_ref, wgb_ref, bgb_ref, lmat_ref,
                    umat_ref, cq_ref, sq_ref, ck_ref, sk_ref,
                    q_out, k_out, v_out, ckv_out, kpe_out, gq_out, gk_out, gv_out, gr_out,
                    bf_out, bb_out):
    h = _modulate(_pick(TOK_TILE, xp_ref, xs_ref), g_ref[...], shift_ref[...], scale_ref[...])
    z = _dot(h.astype(BF16), win_ref[...])

    cqn = _rms(z[:, _C_CQ:_C_CQ + MLA_Q_RANK], qn_ref[...]).astype(BF16)
    qf = _dot(cqn, wq_ref[...])
    cq_t = _lane_tile(cq_ref[...], MLA_HEADS)
    sq_t = _lane_tile(sq_ref[...], MLA_HEADS)
    q_out[...] = (qf[:, :MLA_QK_W] * cq_t + qf[:, MLA_QK_W:] * sq_t).astype(BF16)

    ckvn = _rms(z[:, _C_CKV:_C_CKV + MLA_KV_RANK], kvn_ref[...])
    ckv_out[...] = ckvn
    small = z[:, _C_SMALL:_C_SMALL + LANES]
    kpe = (small[:, _S_KPE:_S_KPE + MLA_ROPE] * ck_ref[...]
           + small[:, _S_KPER:_S_KPER + MLA_ROPE] * sk_ref[...])
    kpe_out[...] = kpe
    ckvn_b = ckvn.astype(BF16)
    k_out[...] = (_dot(ckvn_b, wkk_ref[...]) + _dot(kpe.astype(BF16), epl_ref[...])).astype(BF16)
    v_out[...] = (_dot(ckvn_b, wkv_ref[...]) + vbias_ref[...]).astype(BF16)

    gq_out[...] = z[:, _C_GQ:_C_GQ + GLA_HEADS * GLA_DK] * (GLA_DK ** -0.5)
    gk_out[...] = z[:, _C_GK:_C_GK + GLA_HEADS * GLA_DK]
    gv_out[...] = z[:, _C_GV:_C_GV + GLA_HEADS * GLA_DV].astype(BF16)
    gr_out[...] = z[:, _C_GR:_C_GR + GLA_HEADS * GLA_DV]

    small_b = small.astype(BF16)
    la_f = _log_sigmoid(_dot(small_b, wgf_ref[...]) + bgf_ref[...]) * (1.0 / GLA_GATE_NORM)
    la_b = _log_sigmoid(_dot(small_b, wgb_ref[...]) + bgb_ref[...]) * (1.0 / GLA_GATE_NORM)
    lmat = lmat_ref[...]
    umat = umat_ref[...]
    for r in range(TOK_TILE // CUMSUM_BLOCK):
        rows = slice(r * CUMSUM_BLOCK, (r + 1) * CUMSUM_BLOCK)
        f_hi, f_mid, f_lo = _split3(la_f[rows])
        bf_out[rows, :] = _dot(lmat, f_hi) + _dot(lmat, f_mid) + _dot(lmat, f_lo)
        b_hi, b_mid, b_lo = _split3(la_b[rows])
        bb_out[rows, :] = _dot(umat, b_hi) + _dot(umat, b_mid) + _dot(umat, b_lo)


def _even_in_proj(xp, xs, mod5, layer, g, wts, tabs):
    t = TOK_TILE
    out_widths = [(MLA_QK_W, BF16), (MLA_QK_W, BF16), (MLA_VEXT_W, BF16), (MLA_KV_RANK, F32),
                  (MLA_ROPE, F32), (GLA_HEADS * GLA_DK, F32), (GLA_HEADS * GLA_DK, F32),
                  (GLA_HEADS * GLA_DV, BF16), (GLA_HEADS * GLA_DV, F32),
                  (GLA_HEADS * GLA_DK, F32), (GLA_HEADS * GLA_DK, F32)]
    const_names = ["win", "qn", "wq", "kvn", "wkk", "wkv", "vbias", "epl", "wgf", "bgf", "wgb",
                   "bgb", "lmat", "umat"]
    consts = [wts[n] for n in const_names]
    in_specs = (_split_specs(t, D_MODEL)
                + [_const_spec((1, D_MODEL)), _mod_spec(layer, 0, t), _mod_spec(layer, 1, t)]
                + [_const_spec(c.shape) for c in consts]
                + [_rope_row_spec(t, LANES), _rope_row_spec(t, LANES),
                   _rope_row_spec(t, MLA_ROPE), _rope_row_spec(t, MLA_ROPE)])
    return pl.pallas_call(
        _even_in_kernel,
        out_shape=[jax.ShapeDtypeStruct((N_TOK, w), dt) for w, dt in out_widths],
        grid=(N_TOK // t,),
        in_specs=in_specs,
        out_specs=[_tok_spec(t, w) for w, _ in out_widths],
        compiler_params=_cparams(("parallel",)),
        name="even_in_proj",
    )(xp, xs, g, mod5, mod5, *consts, tabs["mla_cq"], tabs["mla_sq"], tabs["mla_ck"], tabs["mla_sk"])


def _cache_kv_kernel(ckv_ref, kpe_ref, wkk_ref, wkv_ref, vbias_ref, epl_ref, k_out, v_out):
    ckv_b = ckv_ref[...].astype(BF16)
    k_out[...] = (_dot(ckv_b, wkk_ref[...]) + _dot(kpe_ref[...].astype(BF16), epl_ref[...])).astype(BF16)
    v_out[...] = (_dot(ckv_b, wkv_ref[...]) + vbias_ref[...]).astype(BF16)


def _cache_kv(ckv, kpe, wts):
    n = ckv.shape[0]
    consts = [wts[k] for k in ("wkk", "wkv", "vbias", "epl")]
    return pl.pallas_call(
        _cache_kv_kernel,
        out_shape=[jax.ShapeDtypeStruct((n, MLA_QK_W), BF16), jax.ShapeDtypeStruct((n, MLA_VEXT_W), BF16)],
        grid=(1,),
        in_specs=[_const_spec(ckv.shape), _const_spec(kpe.shape)] + [_const_spec(c.shape) for c in consts],
        out_specs=[_const_spec((n, MLA_QK_W)), _const_spec((n, MLA_VEXT_W))],
        compiler_params=_cparams(("arbitrary",)),
        name="mla_cache_kv",
    )(ckv, kpe, *consts)


def _mla_attn_kernel(*refs, n_seg):
    q_ref = refs[0]
    k_refs = refs[1:1 + n_seg]
    v_refs = refs[1 + n_seg:1 + 2 * n_seg]
    o_ref = refs[1 + 2 * n_seg]
    scale = (MLA_NOPE + MLA_ROPE) ** -0.5
    c = scale * float(np.log2(np.e))
    tq = q_ref.shape[0]
    q_sub = min(tq, MLA_Q_SUB)
    kt = MXU_COLS
    tiles = [(si, r0) for si, k in enumerate(k_refs) for r0 in range(0, k.shape[0], kt)]
    lane = lax.broadcasted_iota(jnp.int32, (q_sub, LANES), 1)
    n_pairs = o_ref.shape[1] // LANES
    for pi, qs in [(pi, qs) for pi in range(n_pairs) for qs in range(tq // q_sub)]:
        rows = slice(qs * q_sub, (qs + 1) * q_sub)
        vl = slice(pi * 2 * LANES, (pi + 1) * 2 * LANES)
        res = []
        for j in range(2):
            hl = slice((2 * pi + j) * LANES, (2 * pi + j + 1) * LANES)
            qj = q_ref[rows, hl]
            macc = None
            s_tiles = []
            for si, r0 in tiles:
                s = _dot_nt(qj, k_refs[si][r0:r0 + kt, hl])
                s_tiles.append(s)
                mt = jnp.maximum(s[:, :LANES], s[:, LANES:])
                macc = mt if macc is None else jnp.maximum(macc, mt)
            m = macc.max(axis=-1, keepdims=True)
            r = None
            for (si, r0), s in zip(tiles, s_tiles):
                p = jnp.exp2((s - m) * c).astype(BF16)
                rj = _dot(p, v_refs[si][r0:r0 + kt, vl])
                r = rj if r is None else r + rj
            res.append(r[:, :LANES] / r[:, LANES:])
        o_ref[rows, pi * LANES:(pi + 1) * LANES] = jnp.where(lane < MLA_V, res[0], res[1]).astype(BF16)


def _mla_attention(q, ks, vs, *, n_batch, seq_q, q_tile, tok_off, k_batch_rows, pairs):
    n_seg = len(ks)
    nq = seq_q // q_tile
    qoff = tok_off // q_tile
    grid = (n_batch, MLA_HEADS // 2 // pairs, nq)
    wq = pairs * 2 * LANES
    in_specs = [pl.BlockSpec((q_tile, wq), lambda b, hp, i: (qoff + b * nq + i, hp))]
    for s in range(n_seg):
        rows = k_batch_rows[s]
        off = (tok_off // rows) if s == 0 else 0
        in_specs.append(pl.BlockSpec((rows, wq), functools.partial(lambda b, hp, i, off: (off + b, hp), off=off)))
    for s in range(n_seg):
        rows = k_batch_rows[s]
        off = (tok_off // rows) if s == 0 else 0
        in_specs.append(pl.BlockSpec((rows, wq), functools.partial(lambda b, hp, i, off: (off + b, hp), off=off)))
    return pl.pallas_call(
        functools.partial(_mla_attn_kernel, n_seg=n_seg),
        out_shape=jax.ShapeDtypeStruct((n_batch * seq_q, MLA_HEADS * MLA_V), BF16),
        grid=grid,
        in_specs=in_specs,
        out_specs=pl.BlockSpec((q_tile, pairs * LANES), lambda b, hp, i: (b * nq + i, hp)),
        compiler_params=_cparams(("parallel", "parallel", "arbitrary")),
        name=f"mla_attention_{n_seg}seg",
    )(q, *ks, *vs)


def _gla_kernel(q_ref, k_ref, v_ref, gr_ref, bf_ref, bb_ref, s0f_ref, s0b_ref, gn_ref,
                o_ref, sf_ref, sb_ref, acc_ref, kdf_ref, kdb_ref, qdf_ref, qdb_ref, hist_ref, *, n_chunks):
    c = GLA_CHUNK
    cpb = min(n_chunks, GLA_BLOCK_CHUNKS)
    blk = cpb * c
    lane = lax.broadcasted_iota(jnp.int32, (blk, LANES), 1)
    lo = lane < GLA_DK
    row = lax.broadcasted_iota(jnp.int32, (blk, blk), 0)
    col = lax.broadcasted_iota(jnp.int32, (blk, blk), 1)
    chunk_bits = c.bit_length() - 1
    same_chunk = jnp.right_shift(row, chunk_bits) == jnp.right_shift(col, chunk_bits)
    tril = same_chunk & (row >= col)
    triu = same_chunk & (row <= col)
    zero_blk = jnp.zeros((GLA_DK, GLA_DV), F32)

    def pair_state_t(s_ref):
        blockdiag = jnp.concatenate(
            [jnp.concatenate([s_ref[0], zero_blk], axis=1),
             jnp.concatenate([zero_blk, s_ref[1]], axis=1)], axis=0)
        return blockdiag.T

    dirs = ((bf_ref, c // 2 - 1, c - 1, tril, kdf_ref, qdf_ref),
            (bb_ref, c // 2, 0, triu, kdb_ref, qdb_ref))
    sels = (lo, jnp.logical_not(lo))
    lo_c = lax.broadcasted_iota(jnp.int32, (c, LANES), 1) < GLA_DK
    sels_c = (lo_c, jnp.logical_not(lo_c))
    hcols = (slice(0, GLA_DV), slice(GLA_DV, 2 * GLA_DV))

    def block(r, carry):
        rows = pl.ds(pl.multiple_of(r * blk, blk), blk)
        q = q_ref[rows, :]
        k = k_ref[rows, :]
        v = v_ref[rows, :]

        def chunk_row(b, r):
            return jnp.concatenate([jnp.broadcast_to(b[ch * c + r:ch * c + r + 1, :], (c, LANES))
                                    for ch in range(cpb)], axis=0)

        for d, (b_ref, mid_row, last_row, causal, kd_ref, qd_ref) in enumerate(dirs):
            b = b_ref[rows, :]
            b_mid = chunk_row(b, mid_row)
            b_last = chunk_row(b, last_row)
            qe = q * jnp.exp(b - b_mid)
            ke = (k * jnp.exp(b_mid - b)).astype(BF16)
            kd_ref[rows, :] = (k * jnp.exp(b_last - b)).astype(BF16)
            qd_ref[rows, :] = (q * jnp.exp(b)).astype(BF16)
            for j in range(2):
                a = _dot_nt(jnp.where(sels[j], qe, 0.0).astype(BF16), ke)
                o = _dot(jnp.where(causal, a, 0.0).astype(BF16), v[:, hcols[j]])
                if d == 0:
                    acc_ref[rows, hcols[j]] = o
                else:
                    acc_ref[rows, hcols[j]] += o
        return carry

    lax.fori_loop(0, n_chunks // cpb, block, 0, unroll=min(2, n_chunks // cpb))

    def scan(i, carry):
        new = []
        for d, (b_ref, _, last_row, _, kd_ref, _) in enumerate(dirs):
            ci = i if d == 0 else n_chunks - 1 - i
            rows = pl.ds(pl.multiple_of(ci * c, c), c)
            hist_ref[d, ci] = carry[d].astype(BF16)
            grp = b_ref[pl.ds(pl.multiple_of(ci * c + (last_row // 8) * 8, 8), 8), :]
            b_last = grp[last_row % 8:last_row % 8 + 1, :]
            new.append(carry[d] * jnp.exp(b_last) + _dot_tn(v_ref[rows, :], kd_ref[rows, :]))
        return tuple(new)

    st_f, st_b = lax.fori_loop(0, n_chunks, scan, (pair_state_t(s0f_ref), pair_state_t(s0b_ref)),
                               unroll=GLA_CHUNK_UNROLL)

    def inter(ci, carry):
        rows = pl.ds(pl.multiple_of(ci * c, c), c)
        for d, (_, _, _, _, _, qd_ref) in enumerate(dirs):
            qd = qd_ref[rows, :]
            st = hist_ref[d, ci]
            for j in range(2):
                acc_ref[rows, hcols[j]] += _dot_nt(jnp.where(sels_c[j], qd, jnp.zeros_like(qd)), st[hcols[j], :])
        return carry

    lax.fori_loop(0, n_chunks, inter, 0, unroll=GLA_CHUNK_UNROLL)
    s_f = st_f.T
    s_b = st_b.T
    sf_ref[0] = s_f[:GLA_DK, :GLA_DV]
    sf_ref[1] = s_f[GLA_DK:, GLA_DV:]
    sb_ref[0] = s_b[:GLA_DK, :GLA_DV]
    sb_ref[1] = s_b[GLA_DK:, GLA_DV:]

    gn = gn_ref[...]
    for j in range(2):
        cols = slice(j * GLA_DV, (j + 1) * GLA_DV)
        gr = gr_ref[:, cols]
        o_ref[:, cols] = (_rms(acc_ref[:, cols], gn) * (gr * jax.nn.sigmoid(gr))).astype(BF16)


def _gla(gq, gk, gv, gr, bf, bb, s0f, s0b, gnorm, *, n_batch, seq, tok_off):
    n_chunks = seq // GLA_CHUNK
    boff = tok_off // seq
    hp = GLA_HEADS // 2
    tok = lambda w: pl.BlockSpec((seq, w), lambda b, p: (boff + b, p))
    st = pl.BlockSpec((None, 2, GLA_DK, GLA_DV), lambda b, p: (b, p, 0, 0))
    return pl.pallas_call(
        functools.partial(_gla_kernel, n_chunks=n_chunks),
        out_shape=[jax.ShapeDtypeStruct((n_batch * seq, GLA_HEADS * GLA_DV), BF16),
                   jax.ShapeDtypeStruct((n_batch, GLA_HEADS, GLA_DK, GLA_DV), F32),
                   jax.ShapeDtypeStruct((n_batch, GLA_HEADS, GLA_DK, GLA_DV), F32)],
        grid=(n_batch, hp),
        in_specs=[tok(2 * GLA_DK), tok(2 * GLA_DK), tok(2 * GLA_DV), tok(2 * GLA_DV),
                  tok(2 * GLA_DK), tok(2 * GLA_DK), st, st, _const_spec((1, GLA_DV))],
        out_specs=[pl.BlockSpec((seq, 2 * GLA_DV), lambda b, p: (b, p)), st, st],
        scratch_shapes=[pltpu.VMEM((seq, 2 * GLA_DV), F32)]
                       + [pltpu.VMEM((seq, 2 * GLA_DK), BF16)] * 4
                       + [pltpu.VMEM((2, n_chunks, 2 * GLA_DV, 2 * GLA_DK), BF16)],
        compiler_params=_cparams(("parallel", "parallel")),
        name=f"gla_seq{seq}",
    )(gq, gk, gv, gr, bf, bb, s0f, s0b, gnorm)


def _out_proj_kernel(*refs, n_o, x_split, route):
    t = TOK_TILE
    o = [_pick(t, refs[2 * j], refs[2 * j + 1]) for j in range(n_o)]
    rest = refs[2 * n_o:]
    w_ref = rest[0]
    if x_split:
        x_in = _pick(t, rest[1], rest[2])
        rest = rest[3:]
    else:
        x_in = rest[1][...]
        rest = rest[2:]
    g1_ref, gate_ref, g2_ref, shift_ref, scale_ref = rest[:5]
    rest = rest[5:]
    if route:
        router_in, rest = rest[:3], rest[3:]
    x_out, h_out = rest[:2]
    y = _dot(o[0] if n_o == 1 else jnp.concatenate(o, axis=1), w_ref[...])
    x = x_in + gate_ref[...] * _rms(y, g1_ref[...])
    x_out[...] = x
    h = _modulate(x, g2_ref[...], shift_ref[...], scale_ref[...])
    h_out[...] = h.astype(h_out.dtype)
    if route:
        _route(h.astype(BF16), *router_in, *rest[2:])


def _out_proj(os_, w, x, mod5, layer, g1, g2, h_dtype, router=None):
    t = TOK_TILE
    x_split = isinstance(x, tuple)
    in_specs, args = [], []
    for o_p, o_s in os_:
        in_specs += _split_specs(t, o_p.shape[1])
        args += [o_p, o_s]
    in_specs.append(_const_spec(w.shape))
    args.append(w)
    if x_split:
        in_specs += _split_specs(t, D_MODEL)
        args += list(x)
    else:
        in_specs.append(_tok_spec(t, D_MODEL))
        args.append(x)
    in_specs += [_const_spec((1, D_MODEL)), _mod_spec(layer, 2, t), _const_spec((1, D_MODEL)),
                 _mod_spec(layer, 3, t), _mod_spec(layer, 4, t)]
    args += [g1, mod5, g2, mod5, mod5]
    out_shape = [jax.ShapeDtypeStruct((N_TOK, D_MODEL), F32), jax.ShapeDtypeStruct((N_TOK, D_MODEL), h_dtype)]
    out_specs = [_tok_spec(t, D_MODEL), _tok_spec(t, D_MODEL)]
    scratch = []
    if router is not None:
        r = np.arange(t)
        ltri = jnp.asarray(r[:, None] > r[None, :], BF16)
        args += [router[0], router[1], ltri]
        in_specs += [_const_spec(router[0].shape), _const_spec(router[1].shape), _const_spec((t, t))]
        out_shape += [jax.ShapeDtypeStruct((N_TOK, LANES), F32), jax.ShapeDtypeStruct((N_TOK, LANES), jnp.int32),
                      jax.ShapeDtypeStruct((N_TOK, LANES), jnp.int32), jax.ShapeDtypeStruct((8, LANES), jnp.int32)]
        out_specs += [_tok_spec(t, LANES), _tok_spec(t, LANES), _tok_spec(t, LANES),
                      pl.BlockSpec((8, LANES), lambda i: (0, 0))]
        scratch = [pltpu.VMEM((8, LANES), F32)]
    return pl.pallas_call(
        functools.partial(_out_proj_kernel, n_o=len(os_), x_split=x_split, route=router is not None),
        out_shape=out_shape,
        grid=(N_TOK // t,),
        in_specs=in_specs,
        out_specs=out_specs,
        scratch_shapes=scratch,
        compiler_params=_cparams(("arbitrary",) if router is not None else ("parallel",)),
        name=f"out_proj_{len(os_)}",
    )(*args)


def _ffn_kernel(h_ref, x_ref, wg_ref, wu_ref, wd_ref, g_ref, gate_ref, x_out):
    h = h_ref[...]
    f = None
    for cidx in range(D_FF // FF_CHUNK):
        cols = slice(cidx * FF_CHUNK, (cidx + 1) * FF_CHUNK)
        a = _dot(h, wg_ref[:, cols].astype(BF16))
        u = _dot(h, wu_ref[:, cols].astype(BF16))
        fc = _dot(((a * jax.nn.sigmoid(a)) * u).astype(BF16), wd_ref[cols, :].astype(BF16))
        f = fc if f is None else f + fc
    x_out[...] = x_ref[...] + gate_ref[...] * _rms(f, g_ref[...])


def _ffn(h, x, wg, wu, wd, mod5, layer, g3):
    t = TOK_TILE
    return pl.pallas_call(
        _ffn_kernel,
        out_shape=jax.ShapeDtypeStruct((N_TOK, D_MODEL), F32),
        grid=(N_TOK // t,),
        in_specs=[_tok_spec(t, D_MODEL), _tok_spec(t, D_MODEL), _const_spec(wg.shape),
                  _const_spec(wu.shape), _const_spec(wd.shape), _const_spec((1, D_MODEL)),
                  _mod_spec(layer, 5, t)],
        out_specs=_tok_spec(t, D_MODEL),
        compiler_params=_cparams(("parallel",)),
        name="ffn_swiglu",
    )(h, x, wg, wu, wd, g3, mod5)


GQA_Q_W = GQA_HEADS * GQA_HEAD_DIM
GQA_KV_W = GQA_KV_HEADS * GQA_HEAD_DIM
GQA_VEXT_W = (GQA_KV_HEADS // 2) * 2 * LANES
_O_Q, _O_QR = 0, GQA_Q_W
_O_K, _O_KR = 2 * GQA_Q_W, 2 * GQA_Q_W + GQA_KV_W
_O_V = 2 * GQA_Q_W + 2 * GQA_KV_W
IN_C_EXT = _O_V + GQA_VEXT_W


def _odd_in_kernel(x_ref, g_ref, shift_ref, scale_ref, win_ref, vbias_ref, c_ref, s_ref,
                   q_out, kb_out, vb_out, kt_out, vt_out):
    h = _modulate(x_ref[...], g_ref[...], shift_ref[...], scale_ref[...])
    z = _dot(h.astype(BF16), win_ref[...])
    c_t = _lane_tile(c_ref[...], GQA_Q_W // LANES)
    s_t = _lane_tile(s_ref[...], GQA_Q_W // LANES)
    q = z[:, _O_Q:_O_Q + GQA_Q_W] * c_t + z[:, _O_QR:_O_QR + GQA_Q_W] * s_t
    q_out[...] = (q * (GQA_HEAD_DIM ** -0.5)).astype(BF16)
    k = (z[:, _O_K:_O_K + GQA_KV_W] * c_t[:, :GQA_KV_W]
         + z[:, _O_KR:_O_KR + GQA_KV_W] * s_t[:, :GQA_KV_W])
    kb_out[...] = k.astype(BF16)
    vext = z[:, _O_V:_O_V + GQA_VEXT_W] + vbias_ref[...]
    vb_out[...] = vext.astype(BF16)

    @pl.when(pl.program_id(0) < NP_TOK // TOK_TILE)
    def _():
        v = jnp.concatenate([vext[:, 2 * p * LANES:(2 * p + 1) * LANES] for p in range(GQA_KV_HEADS // 2)], axis=1)
        for b in range(TOK_TILE // SEQ):
            kt_out[b] = k[b * SEQ:(b + 1) * SEQ, :].T
            vt_out[b] = v[b * SEQ:(b + 1) * SEQ, :].T


def _odd_in_proj(x, mod5, layer, g, win, vbias, tab_c, tab_s):
    t = TOK_TILE
    npt = NP_TOK // t
    out_widths = [(GQA_Q_W, BF16), (GQA_KV_W, BF16), (GQA_VEXT_W, BF16)]
    cache_shape = jax.ShapeDtypeStruct((BATCH, GQA_KV_W, SEQ), F32)
    cache_spec = pl.BlockSpec((t // SEQ, GQA_KV_W, SEQ), lambda i: (jnp.minimum(i, npt - 1), 0, 0))
    return pl.pallas_call(
        _odd_in_kernel,
        out_shape=[jax.ShapeDtypeStruct((N_TOK, w), dt) for w, dt in out_widths] + [cache_shape, cache_shape],
        grid=(N_TOK // t,),
        in_specs=[_tok_spec(t, D_MODEL), _const_spec((1, D_MODEL)), _mod_spec(layer, 0, t),
                  _mod_spec(layer, 1, t), _const_spec(win.shape), _const_spec(vbias.shape),
                  _rope_row_spec(t, LANES), _rope_row_spec(t, LANES)],
        out_specs=[_tok_spec(t, w) for w, _ in out_widths] + [cache_spec, cache_spec],
        compiler_params=_cparams(("arbitrary",)),
        name="odd_in_proj",
    )(x, g, mod5, mod5, win, vbias, tab_c, tab_s)


def _gqa_kernel(sink_ref, q_ref, *refs, local_len):
    if local_len:
        kl_ref, vl_ref, kc_ref, vc_ref, o_ref = refs
    else:
        kc_ref, vc_ref, o_ref = refs
    tq = q_ref.shape[0]
    lane = lax.broadcasted_iota(jnp.int32, (tq, LANES), 1)
    lo = lane < GQA_HEAD_DIM
    if local_len:
        i = pl.program_id(1)
        q0 = i * tq
        seq = kl_ref.shape[0]
        kstart = pl.multiple_of(jnp.clip(q0 - WINDOW, 0, seq - local_len), LANES)
        qpos = q0 + lax.broadcasted_iota(jnp.int32, (tq, local_len), 0)
        kpos = kstart + lax.broadcasted_iota(jnp.int32, (tq, local_len), 1)
        band = jnp.abs(qpos - kpos) <= WINDOW
    for p in range(GQA_KV_HEADS // 2):
        kc = kc_ref[:, p * LANES:(p + 1) * LANES]
        vc = vc_ref[:, 2 * p * LANES:(2 * p + 2) * LANES]
        if local_len:
            kl = kl_ref[pl.ds(kstart, local_len), p * LANES:(p + 1) * LANES]
            vl = vl_ref[pl.ds(kstart, local_len), 2 * p * LANES:(2 * p + 2) * LANES]
        for blk in range(GQA_GROUP):
            cols = slice((p * GQA_GROUP + blk) * LANES, (p * GQA_GROUP + blk + 1) * LANES)
            qb = q_ref[:, cols]
            res = []
            for half in range(2):
                head = (2 * p + half) * GQA_GROUP + blk
                sink = sink_ref[head]
                qh = jnp.where(lo if half == 0 else jnp.logical_not(lo), qb, jnp.zeros_like(qb))
                s_c = _dot_nt(qh, kc)
                m = jnp.maximum(s_c.max(axis=-1, keepdims=True), sink)
                if local_len:
                    s_l = jnp.where(band, _dot_nt(qh, kl), NEG_INF)
                    m = jnp.maximum(m, s_l.max(axis=-1, keepdims=True))
                r = _dot(jnp.exp(s_c - m).astype(BF16), vc)
                if local_len:
                    r = r + _dot(jnp.exp(s_l - m).astype(BF16), vl)
                res.append(r[:, :LANES] / (r[:, LANES:] + jnp.exp(sink - m)))
            o_ref[:, cols] = jnp.where(lo, res[0], res[1]).astype(BF16)


def _gqa_attention(sink, q, k_loc, v_loc, k_ctx, v_ctx, *, n_batch, seq_q, q_tile, tok_off, n_ctx, local):
    nq = seq_q // q_tile
    qoff = tok_off // q_tile
    local_len = q_tile + 2 * WINDOW if local else 0
    in_specs = [pl.BlockSpec(memory_space=pltpu.SMEM),
                pl.BlockSpec((q_tile, GQA_Q_W), lambda b, i: (qoff + b * nq + i, 0))]
    args = [sink, q]
    if local:
        boff = tok_off // seq_q
        in_specs += [pl.BlockSpec((seq_q, GQA_KV_W), lambda b, i: (boff + b, 0)),
                     pl.BlockSpec((seq_q, GQA_VEXT_W), lambda b, i: (boff + b, 0))]
        args += [k_loc, v_loc]
    in_specs += [pl.BlockSpec((n_ctx, GQA_KV_W), lambda b, i: (b, 0)),
                 pl.BlockSpec((n_ctx, GQA_VEXT_W), lambda b, i: (b, 0))]
    args += [k_ctx, v_ctx]
    return pl.pallas_call(
        functools.partial(_gqa_kernel, local_len=local_len),
        out_shape=jax.ShapeDtypeStruct((n_batch * seq_q, GQA_Q_W), BF16),
        grid=(n_batch, nq),
        in_specs=in_specs,
        out_specs=pl.BlockSpec((q_tile, GQA_Q_W), lambda b, i: (b * nq + i, 0)),
        compiler_params=_cparams(("parallel", "arbitrary")),
        name="gqa_local" if local else "gqa_ctx",
    )(*args)


def _route(h, w_ref, b_ref, ltri_ref, wsel_out, isel_out, rank_out, cnt_out, carry_ref):
    @pl.when(pl.program_id(0) == 0)
    def _():
        carry_ref[...] = jnp.zeros_like(carry_ref)

    logits = _dot(h, w_ref[...]) + b_ref[...]
    lane = lax.broadcasted_iota(jnp.int32, logits.shape, 1)
    neg = float(np.finfo(np.float32).min)
    lg = jnp.where(lane < N_EXPERTS, logits, neg)
    v1 = lg.max(axis=-1, keepdims=True)
    i1 = jnp.min(jnp.where(lg == v1, lane, LANES), axis=-1, keepdims=True)
    lg2 = jnp.where(lane == i1, neg, lg)
    v2 = lg2.max(axis=-1, keepdims=True)
    i2 = jnp.min(jnp.where(lg2 == v2, lane, LANES), axis=-1, keepdims=True)
    e2 = jnp.exp(v2 - v1)
    den = 1.0 + e2
    wsel_out[...] = jnp.where(lane == 0, 1.0 / den, jnp.where(lane == 1, e2 / den, 0.0))
    isel_out[...] = jnp.where(lane == 0, i1, jnp.where(lane == 1, i2, 0))
    hit = jnp.where(lane == i1, 1.0, jnp.where(lane == i2, 1.0, 0.0))
    carry = carry_ref[...]
    rank_out[...] = (_dot(ltri_ref[...], hit.astype(BF16)) + carry[0:1, :]).astype(jnp.int32)
    carry = carry + jnp.sum(hit, axis=0, keepdims=True)
    carry_ref[...] = carry
    cnt_out[...] = carry.astype(jnp.int32)


def _route_tables(isel, rank, cnt):
    tm = MOE_ROW_TILE
    counts = cnt[0, :N_EXPERTS]
    padded = ((counts + tm - 1) // tm) * tm
    ends = jnp.cumsum(padded)
    base = ends - padded
    e_ids = jnp.arange(N_EXPERTS, dtype=jnp.int32)
    row = rank[:, :N_EXPERTS] + base[None, :]
    pos1 = jnp.sum(jnp.where(e_ids[None, :] == isel[:, 0:1], row, 0), axis=1)
    pos2 = jnp.sum(jnp.where(e_ids[None, :] == isel[:, 1:2], row, 0), axis=1)
    tile_start = jnp.arange(MOE_ROWS // tm, dtype=jnp.int32) * tm
    tile_expert = jnp.minimum(jnp.sum(tile_start[:, None] >= ends[None, :], axis=1), N_EXPERTS - 1).astype(jnp.int32)
    tile_valid = jnp.clip((base + counts)[tile_expert] - tile_start, 0, tm).astype(jnp.int32)
    tile_valid = jnp.where(tile_start < ends[-1], tile_valid, 0)
    return pos1, pos2, tile_expert, tile_valid


def _scatter_rows(rows, pos1, pos2, n_out):
    n_tok, d = rows.shape
    per_w = n_tok // SC_WORKERS
    w = SC_GATHER_WINDOW
    assert per_w * SC_WORKERS == n_tok and per_w % w == 0
    mesh = plsc.VectorSubcoreMesh(core_axis_name="core", subcore_axis_name="subcore")

    assert (per_w // w) % 2 == 0
    slot_types = [pltpu.VMEM((w,), jnp.int32), pltpu.VMEM((w,), jnp.int32), pltpu.VMEM((w, d), rows.dtype),
                  pltpu.SemaphoreType.DMA, pltpu.SemaphoreType.DMA]

    @functools.partial(
        pl.kernel, out_type=jax.ShapeDtypeStruct((n_out, d), rows.dtype), mesh=mesh,
        scratch_types=slot_types * 2, name="sc_scatter_rows")
    def scatter(x_hbm, p1_hbm, p2_hbm, o_hbm, *scratch):
        wid = lax.axis_index("subcore") * SC_CORES + lax.axis_index("core")
        base = wid * per_w
        slots = (scratch[:5], scratch[5:])

        @pl.loop(0, per_w // (2 * w))
        def _(g):
            loads = []
            for s, (i1_v, i2_v, rows_v, sem_in, _) in enumerate(slots):
                off = base + (2 * g + s) * w
                loads.append([pltpu.async_copy(p1_hbm.at[pl.ds(off, w)], i1_v, sem_in),
                              pltpu.async_copy(p2_hbm.at[pl.ds(off, w)], i2_v, sem_in),
                              pltpu.async_copy(x_hbm.at[pl.ds(off, w)], rows_v, sem_in)])
            stores = []
            for s, (i1_v, i2_v, rows_v, _, sem_out) in enumerate(slots):
                for cp in loads[s]:
                    cp.wait()
                stores += [pltpu.async_copy(rows_v, o_hbm.at[i1_v], sem_out),
                           pltpu.async_copy(rows_v, o_hbm.at[i2_v], sem_out)]
            for cp in stores:
                cp.wait()

    return scatter(rows, pos1, pos2)


def _gather_rows(table, idx):
    n_idx = idx.shape[0]
    d = table.shape[1]
    per_w = n_idx // SC_WORKERS
    assert per_w * SC_WORKERS == n_idx and per_w % SC_INDEX_BLOCK == 0
    mesh = plsc.VectorSubcoreMesh(core_axis_name="core", subcore_axis_name="subcore")

    w = SC_GATHER_WINDOW
    n_sub = SC_INDEX_BLOCK // w
    slot_types = [pltpu.VMEM((w, d), table.dtype), pltpu.SemaphoreType.DMA, pltpu.SemaphoreType.DMA]

    @functools.partial(
        pl.kernel, out_type=jax.ShapeDtypeStruct((n_idx, d), table.dtype), mesh=mesh,
        scratch_types=[pltpu.VMEM((SC_INDEX_BLOCK,), jnp.int32)] + slot_types * 2,
        name="sc_gather_rows")
    def gather(x_hbm, i_hbm, o_hbm, idx_v, *scratch):
        wid = lax.axis_index("subcore") * SC_CORES + lax.axis_index("core")
        base = wid * per_w
        slots = (scratch[:3], scratch[3:])

        @pl.loop(0, per_w // SC_INDEX_BLOCK)
        def _(g):
            off = base + g * SC_INDEX_BLOCK
            pltpu.sync_copy(i_hbm.at[pl.ds(off, SC_INDEX_BLOCK)], idx_v)

            def start_gather(s):
                rows_v, sem_in, _ = slots[s % 2]
                return pltpu.async_copy(x_hbm.at[idx_v.at[pl.ds(s * w, w)]], rows_v, sem_in)

            gathers = {0: start_gather(0)}
            writes = {}
            for s in range(n_sub):
                if s + 1 < n_sub:
                    if s >= 1:
                        writes[s - 1].wait()
                    gathers[s + 1] = start_gather(s + 1)
                gathers[s].wait()
                rows_v, _, sem_out = slots[s % 2]
                writes[s] = pltpu.async_copy(rows_v, o_hbm.at[pl.ds(off + s * w, w)], sem_out)
            writes[n_sub - 2].wait()
            writes[n_sub - 1].wait()

    return gather(table, idx)


def _expert_ffn_kernel(te_ref, nv_ref, x_ref, wg_ref, wu_ref, wd_ref, y_out):
    n_valid = nv_ref[pl.program_id(0)]

    @pl.when(n_valid > 0)
    def _():
        row = lax.broadcasted_iota(jnp.int32, x_ref.shape, 0)
        h = jnp.where(row < n_valid, x_ref[...], 0.0).astype(BF16)
        f = None
        for cidx in range(D_FF // FF_CHUNK):
            cols = slice(cidx * FF_CHUNK, (cidx + 1) * FF_CHUNK)
            a = _dot(h, wg_ref[:, cols].astype(BF16))
            u = _dot(h, wu_ref[:, cols].astype(BF16))
            fc = _dot(((a * jax.nn.sigmoid(a)) * u).astype(BF16), wd_ref[cols, :].astype(BF16))
            f = fc if f is None else f + fc
        y_out[...] = f

    @pl.when(n_valid == 0)
    def _():
        y_out[...] = jnp.zeros_like(y_out)


def _expert_ffn(xs, tile_expert, tile_valid, wg, wu, wd):
    tm = MOE_ROW_TILE
    wspec = lambda shape: pl.BlockSpec((None,) + shape, lambda j, te, nu: (te[j], 0, 0),
                                       pipeline_mode=pl.Buffered(1))
    return pl.pallas_call(
        _expert_ffn_kernel,
        out_shape=jax.ShapeDtypeStruct((MOE_ROWS, D_MODEL), F32),
        grid_spec=pltpu.PrefetchScalarGridSpec(
            num_scalar_prefetch=2,
            grid=(MOE_ROWS // tm,),
            in_specs=[pl.BlockSpec((tm, D_MODEL), lambda j, te, nu: (j, 0)),
                      wspec((D_MODEL, D_FF)), wspec((D_MODEL, D_FF)), wspec((D_FF, D_MODEL))],
            out_specs=pl.BlockSpec((tm, D_MODEL), lambda j, te, nu: (j, 0)),
        ),
        compiler_params=_cparams(("arbitrary",)),
        name="moe_expert_ffn",
    )(tile_expert, tile_valid, xs, wg, wu, wd)


def _moe_combine_kernel(y1_ref, y2_ref, wsel_ref, x_ref, g_ref, gate_ref, x_out):
    w = wsel_ref[...]
    f = w[:, 0:1] * y1_ref[...] + w[:, 1:2] * y2_ref[...]
    x_out[...] = x_ref[...] + gate_ref[...] * _rms(f, g_ref[...])


def _moe_combine(yg, wsel, x, mod5, layer, g3, *, tok_off, n_tok):
    t = TOK_TILE
    nt = N_TOK // t
    off = tok_off // t
    tpg = NP_TOK // t
    tok = lambda w, shift: pl.BlockSpec((t, w), lambda i: (off + shift + i, 0))
    return pl.pallas_call(
        _moe_combine_kernel,
        out_shape=jax.ShapeDtypeStruct((n_tok, D_MODEL), F32),
        grid=(n_tok // t,),
        in_specs=[tok(D_MODEL, 0), tok(D_MODEL, nt), tok(LANES, 0), tok(D_MODEL, 0), _const_spec((1, D_MODEL)),
                  pl.BlockSpec((None, None, None, 1, D_MODEL), lambda i: (layer, (off + i) // tpg, 5, 0, 0))],
        out_specs=_tok_spec(t, D_MODEL),
        compiler_params=_cparams(("parallel",)),
        name="moe_combine",
    )(yg, yg, wsel, x, g3, mod5)


def _moe(h, x, routing, wg, wu, wd, mod5, layer, g3):
    wsel, isel, rank, cnt = routing
    pos1, pos2, tile_expert, tile_valid = _route_tables(isel, rank, cnt)
    xs = _scatter_rows(h, pos1, pos2, MOE_ROWS)
    ys = _expert_ffn(xs, tile_expert, tile_valid, wg, wu, wd)
    yg = _gather_rows(ys, jnp.concatenate([pos1, pos2]))
    return (_moe_combine(yg, wsel, x, mod5, layer, g3, tok_off=0, n_tok=NP_TOK),
            _moe_combine(yg, wsel, x, mod5, layer, g3, tok_off=NP_TOK, n_tok=NS_TOK))


def _rot_cols(w, half):
    k, n = w.shape
    wb = w.reshape(k, n // (2 * half), 2, half)
    return jnp.stack([-wb[:, :, 1], wb[:, :, 0]], axis=2).reshape(k, n)


def _axis_tables(r, pos):
    inv = np.float32(ROPE_BASE) ** (-np.arange(0, r, 2, dtype=np.float32) / np.float32(r))
    ang = pos.astype(np.float32)[:, None] * inv[None, :]
    cos, sin = np.cos(ang), np.sin(ang)
    return np.concatenate([cos, cos], axis=1), np.concatenate([sin, sin], axis=1)


def _rope_tables(r):
    s = np.arange(DEC_SEQ)
    cr, sr = _axis_tables(r // 2, s // GRID_W)
    cc, sc = _axis_tables(r // 2, s % GRID_W)
    return np.concatenate([cr, cc], axis=1), np.concatenate([sr, sc], axis=1)


def _with_identity(tab, ident):
    return np.concatenate([np.full((TOK_TILE, tab.shape[1]), ident, np.float32), tab], axis=0)


@functools.lru_cache(maxsize=None)
def _rope_constants():
    c32, s32 = _rope_tables(MLA_ROPE)
    ones = np.ones((DEC_SEQ, MLA_NOPE), np.float32)
    pad1 = np.ones((DEC_SEQ, MLA_HEAD_PAD - MLA_NOPE - MLA_ROPE), np.float32)
    cq = np.concatenate([ones, c32, pad1], axis=1)
    sq = np.concatenate([0 * ones, s32, 0 * pad1], axis=1)
    c64, s64 = _rope_tables(GQA_HEAD_DIM)
    return {
        "mla_cq": _with_identity(cq, 1.0), "mla_sq": _with_identity(sq, 0.0),
        "mla_ck": _with_identity(c32, 1.0), "mla_sk": _with_identity(s32, 0.0),
        "gqa_c": _with_identity(np.concatenate([c64, c64], axis=1), 1.0),
        "gqa_s": _with_identity(np.concatenate([s64, s64], axis=1), 0.0),
    }


def _prep_tables():
    return {k: jnp.asarray(v, F32) for k, v in _rope_constants().items()}


def _prep_even(w_in, q_norm, w_q_up, kv_norm, w_kv_up, wgf, bgf, wgb, bgb):
    sizes = [MLA_Q_RANK, MLA_KV_RANK, MLA_ROPE, GLA_HEADS * GLA_DK, GLA_HEADS * GLA_DK,
             GLA_HEADS * GLA_DV, GLA_HEADS * GLA_DV, GLA_GATE_RANK, GLA_GATE_RANK]
    cq, ckv, kpe, gq, gk, gv, gr, gaf, gab = jnp.split(w_in, [int(s) for s in np.cumsum(sizes)[:-1]], axis=1)
    pad = jnp.zeros((D_MODEL, LANES - 2 * MLA_ROPE - 2 * GLA_GATE_RANK), F32)
    win = jnp.concatenate([cq, ckv, gq, gk, gv, gr, kpe, _rot_cols(kpe, MLA_ROPE // 4), gaf, gab, pad], axis=1)

    wq = w_q_up.reshape(MLA_Q_RANK, MLA_HEADS, MLA_NOPE + MLA_ROPE)
    nope, pe = wq[..., :MLA_NOPE], wq[..., MLA_NOPE:]
    pe_rot = _rot_cols(pe.reshape(MLA_Q_RANK, MLA_HEADS * MLA_ROPE), MLA_ROPE // 4).reshape(pe.shape)
    zpad = jnp.zeros((MLA_Q_RANK, MLA_HEADS, MLA_HEAD_PAD - MLA_NOPE - MLA_ROPE), F32)
    wq_main = jnp.concatenate([nope, pe, zpad], axis=-1).reshape(MLA_Q_RANK, MLA_QK_W)
    wq_rot = jnp.concatenate([0 * nope, pe_rot, zpad], axis=-1).reshape(MLA_Q_RANK, MLA_QK_W)

    wkv = w_kv_up.reshape(MLA_KV_RANK, MLA_HEADS, MLA_NOPE + MLA_V)
    knope, vv = wkv[..., :MLA_NOPE], wkv[..., MLA_NOPE:]
    wkk = jnp.concatenate([knope, jnp.zeros((MLA_KV_RANK, MLA_HEADS, MLA_HEAD_PAD - MLA_NOPE), F32)],
                          axis=-1).reshape(MLA_KV_RANK, MLA_QK_W)
    vpair = vv.reshape(MLA_KV_RANK, MLA_HEADS // 2, 2 * MLA_V)
    wkv_ext = jnp.concatenate([vpair, jnp.zeros((MLA_KV_RANK, MLA_HEADS // 2, LANES), F32)],
                              axis=-1).reshape(MLA_KV_RANK, MLA_VEXT_W)
    vbias = jnp.tile(jnp.concatenate([jnp.zeros((LANES,), F32), jnp.ones((LANES,), F32)]),
                     MLA_HEADS // 2).reshape(1, MLA_VEXT_W)
    epl = jnp.tile(jnp.concatenate([jnp.zeros((MLA_ROPE, MLA_NOPE), F32), jnp.eye(MLA_ROPE, dtype=F32),
                                    jnp.zeros((MLA_ROPE, MLA_HEAD_PAD - MLA_NOPE - MLA_ROPE), F32)], axis=1),
                   (1, MLA_HEADS))

    def gate_w(w, off):
        return jnp.zeros((LANES, GLA_HEADS * GLA_DK), F32).at[off:off + GLA_GATE_RANK].set(w)

    r = np.arange(CUMSUM_BLOCK)
    same = (r[:, None] // GLA_CHUNK) == (r[None, :] // GLA_CHUNK)
    lmat = jnp.asarray(same & (r[:, None] >= r[None, :]), BF16)
    umat = jnp.asarray(same & (r[:, None] <= r[None, :]), BF16)
    return {
        "win": win.astype(BF16), "qn": q_norm.reshape(1, -1), "wq": jnp.concatenate([wq_main, wq_rot], axis=1).astype(BF16),
        "kvn": kv_norm.reshape(1, -1), "wkk": wkk.astype(BF16), "wkv": wkv_ext.astype(BF16), "vbias": vbias,
        "epl": epl.astype(BF16), "wgf": gate_w(wgf, _S_GAF).astype(BF16), "bgf": bgf.reshape(1, -1),
        "wgb": gate_w(wgb, _S_GAB).astype(BF16), "bgb": bgb.reshape(1, -1), "lmat": lmat, "umat": umat,
    }


def _gqa_head_perm():
    heads = []
    for p in range(GQA_KV_HEADS // 2):
        for i in range(GQA_GROUP):
            heads += [(2 * p) * GQA_GROUP + i, (2 * p + 1) * GQA_GROUP + i]
    return np.asarray(heads)


def _prep_odd(w_in, w_out):
    perm = _gqa_head_perm()
    wq = w_in[:, :GQA_Q_W].reshape(D_MODEL, GQA_HEADS, GQA_HEAD_DIM)[:, perm].reshape(D_MODEL, GQA_Q_W)
    wk = w_in[:, GQA_Q_W:GQA_Q_W + GQA_KV_W]
    wv = w_in[:, GQA_Q_W + GQA_KV_W:].reshape(D_MODEL, GQA_KV_HEADS // 2, 2 * GQA_HEAD_DIM)
    wv_ext = jnp.concatenate([wv, jnp.zeros((D_MODEL, GQA_KV_HEADS // 2, LANES), F32)], axis=-1).reshape(D_MODEL, GQA_VEXT_W)
    win = jnp.concatenate([wq, _rot_cols(wq, GQA_HEAD_DIM // 4), wk, _rot_cols(wk, GQA_HEAD_DIM // 4), wv_ext], axis=1)
    vbias = jnp.tile(jnp.concatenate([jnp.zeros((LANES,), F32), jnp.ones((LANES,), F32)]),
                     GQA_KV_HEADS // 2).reshape(1, GQA_VEXT_W)
    wo = w_out.reshape(GQA_HEADS, GQA_HEAD_DIM, D_MODEL)[perm].reshape(GQA_Q_W, D_MODEL)
    return win.astype(BF16), vbias, wo.astype(BF16)


def _ext_v(v):
    rows = v.shape[0]
    vp = v.reshape(rows, GQA_KV_HEADS // 2, 2 * GQA_HEAD_DIM)
    return jnp.concatenate([vp, jnp.ones((rows, GQA_KV_HEADS // 2, LANES), v.dtype)], axis=-1).reshape(rows, GQA_VEXT_W)


def kernel(x_prompt, x_sample, cache_mla_ckv, cache_mla_kpe, state_gla_fwd, state_gla_bwd, cache_gqa_k, cache_gqa_v, c, c_ctx, w_mod, b_mod, norm_g, w_in_ab, mla_q_norm, mla_w_q_up, mla_kv_norm, mla_w_kv_up, gla_w_gate_f, gla_b_gate_f, gla_w_gate_b, gla_b_gate_b, gla_norm, w_out_ab, ffn_w_gate, ffn_w_up, ffn_w_down, w_in_c, gqa_sink, w_out_c, moe_w_router, moe_b_router, moe_w_gate, moe_w_up, moe_w_down):
    x_in = (x_prompt.reshape(NP_TOK, D_MODEL), x_sample.reshape(NS_TOK, D_MODEL))
    cvec =jnp.concatenate([c_ctx[None, :], c, jnp.zeros((MOD_ROWS - N_GROUPS, D_MODEL), F32)], axis=0)
    mod5 = _modulation(cvec, w_mod, b_mod).reshape(DEPTH, MOD_ROWS, N_MOD, 1, D_MODEL)
    tabs = _prep_tables()
    gvec = lambda l, j: norm_g[l, j].reshape(1, D_MODEL)

    wts = _prep_even(w_in_ab[0], mla_q_norm[0], mla_w_q_up[0], mla_kv_norm[0], mla_w_kv_up[0],
                     gla_w_gate_f[0], gla_b_gate_f[0], gla_w_gate_b[0], gla_b_gate_b[0])
    (q, k, v, ckv, kpe, gq, gk, gv, gr, bf, bb) = _even_in_proj(*x_in, mod5, 0, gvec(0, 0), wts, tabs)
    kc, vc = _cache_kv(cache_mla_ckv[:, 0].reshape(DEC_BATCH * PAST_LEN, MLA_KV_RANK),
                       cache_mla_kpe[:, 0].reshape(DEC_BATCH * PAST_LEN, MLA_ROPE), wts)
    oa_p = _mla_attention(q, [k], [v], n_batch=BATCH, seq_q=SEQ, q_tile=SEQ, tok_off=0, k_batch_rows=[SEQ],
                          pairs=MLA_HEADS // 2)
    oa_s = _mla_attention(q, [k, kc], [v, vc], n_batch=DEC_BATCH, seq_q=DEC_SEQ, q_tile=MLA_Q_TILE,
                          tok_off=NP_TOK, k_batch_rows=[DEC_SEQ, PAST_LEN], pairs=1)
    gn = gla_norm[0].reshape(1, GLA_DV)
    zero_state = jnp.zeros((BATCH, GLA_HEADS, GLA_DK, GLA_DV), F32)
    ob_p, sf, sb = _gla(gq, gk, gv, gr, bf, bb, zero_state, zero_state, gn, n_batch=BATCH, seq=SEQ, tok_off=0)
    ob_s, _, _ = _gla(gq, gk, gv, gr, bf, bb, state_gla_fwd[:, 0], state_gla_bwd[:, 0], gn,
                      n_batch=DEC_BATCH, seq=DEC_SEQ, tok_off=NP_TOK)
    x, h = _out_proj([(oa_p, oa_s), (ob_p, ob_s)], w_out_ab[0].astype(BF16), x_in, mod5, 0, gvec(0, 1),
                     gvec(0, 2), BF16)
    x = _ffn(h, x, ffn_w_gate[0], ffn_w_up[0], ffn_w_down[0], mod5, 0, gvec(0, 3))

    win_c, vbias_c, wo_c = _prep_odd(w_in_c[0], w_out_c[0])
    qg, kgb, vgb, kg_t, vg_t = _odd_in_proj(x, mod5, 1, gvec(1, 0), win_c, vbias_c, tabs["gqa_c"], tabs["gqa_s"])
    sink = gqa_sink[0]
    og_p = _gqa_attention(sink, qg, None, None, kgb, vgb, n_batch=BATCH, seq_q=SEQ, q_tile=SEQ, tok_off=0, n_ctx=SEQ, local=False)
    kc_g = cache_gqa_k[:, 0].reshape(DEC_BATCH * PAST_LEN, GQA_KV_W).astype(BF16)
    vc_g = _ext_v(cache_gqa_v[:, 0].reshape(DEC_BATCH * PAST_LEN, GQA_KV_W)).astype(BF16)
    og_s = _gqa_attention(sink, qg, kgb, vgb, kc_g, vc_g, n_batch=DEC_BATCH, seq_q=DEC_SEQ, q_tile=GQA_Q_TILE,
                          tok_off=NP_TOK, n_ctx=PAST_LEN, local=True)
    w_r = jnp.zeros((D_MODEL, LANES), F32).at[:, :N_EXPERTS].set(moe_w_router[0]).astype(BF16)
    b_r = jnp.zeros((1, LANES), F32).at[0, :N_EXPERTS].set(moe_b_router[0])
    x, h, *routing = _out_proj([(og_p, og_s)], wo_c, x, mod5, 1, gvec(1, 1), gvec(1, 2), F32, router=(w_r, b_r))
    y_p, y_s = _moe(h, x, routing, moe_w_gate[0], moe_w_up[0], moe_w_down[0], mod5, 1, gvec(1, 3))

    y_prompt = y_p.reshape(BATCH, SEQ, D_MODEL)
    y_sample = y_s.reshape(DEC_BATCH, DEC_SEQ, D_MODEL)
    new_ckv = ckv[:NP_TOK].reshape(BATCH, 1, SEQ, MLA_KV_RANK)
    new_kpe = kpe[:NP_TOK].reshape(BATCH, 1, SEQ, MLA_ROPE)
    as_cache = lambda a: jnp.transpose(a.reshape(BATCH, 1, GQA_KV_HEADS, GQA_HEAD_DIM, SEQ), (0, 1, 4, 2, 3))
    new_k = as_cache(kg_t)
    new_v = as_cache(vg_t)
    return (y_prompt, y_sample, new_ckv, new_kpe, sf[:, None], sb[:, None], new_k, new_v)
```

```python
import functools

import jax
import jax.numpy as jnp
import numpy as np
from jax import lax
from jax.experimental import pallas as pl
from jax.experimental.pallas import tpu as pltpu
from jax.experimental.pallas import tpu_sc as plsc

F32 = jnp.float32
BF16 = jnp.bfloat16

D_MODEL = 1024
BATCH = 16
SEQ = 256
DEPTH = 2
DEC_BATCH = 4
DEC_SEQ = 4096
PAST_LEN = 256
GRID_W = 64
N_MOD = 6
EPS = 1e-6
ROPE_BASE = 10000.0
NEG_INF = -1e30

MLA_HEADS = 8
MLA_NOPE = 64
MLA_ROPE = 32
MLA_V = 64
MLA_Q_RANK = 384
MLA_KV_RANK = 256
GLA_HEADS = 4
GLA_DK = 64
GLA_DV = 128
GLA_GATE_RANK = 16
GLA_GATE_NORM = 16.0
GLA_CHUNK = 64
GQA_HEADS = 16
GQA_KV_HEADS = 4
GQA_GROUP = GQA_HEADS // GQA_KV_HEADS
GQA_HEAD_DIM = 64
WINDOW = 128
D_FF = 2816
N_EXPERTS = 8
TOP_K = 2

NP_TOK = BATCH * SEQ
NS_TOK = DEC_BATCH * DEC_SEQ
N_TOK = NP_TOK + NS_TOK
N_GROUPS = 1 + DEC_BATCH
MOD_ROWS = 8

LANES = 128
MXU_COLS = 256
VMEM_LIMIT_BYTES = 56 * 1024 * 1024

TOK_TILE = 512
CUMSUM_BLOCK = 256
GLA_BLOCK_CHUNKS = MXU_COLS // GLA_CHUNK
GLA_CHUNK_UNROLL = 4
MLA_Q_TILE = 1024
MLA_Q_SUB = 256
GQA_Q_TILE = 256
MOE_ROW_TILE = 512
MOE_ROWS = TOP_K * N_TOK + N_EXPERTS * MOE_ROW_TILE
MOE_SEG = 4096
SC_CORES = 2
SC_SUBCORES = 16
SC_WORKERS = SC_CORES * SC_SUBCORES
SC_INDEX_BLOCK = 128
SC_GATHER_WINDOW = 32
FF_CHUNK = 1408

_C_CQ = 0
_C_CKV = _C_CQ + MLA_Q_RANK
_C_GQ = _C_CKV + MLA_KV_RANK
_C_GK = _C_GQ + GLA_HEADS * GLA_DK
_C_GV = _C_GK + GLA_HEADS * GLA_DK
_C_GR = _C_GV + GLA_HEADS * GLA_DV
_C_SMALL = _C_GR + GLA_HEADS * GLA_DV
IN_AB_EXT = _C_SMALL + LANES
_S_KPE, _S_KPER, _S_GAF, _S_GAB = 0, MLA_ROPE, 2 * MLA_ROPE, 2 * MLA_ROPE + GLA_GATE_RANK
MLA_HEAD_PAD = LANES
MLA_QK_W = MLA_HEADS * MLA_HEAD_PAD
MLA_VEXT_W = (MLA_HEADS // 2) * 2 * LANES


def _cparams(semantics):
    return pltpu.CompilerParams(dimension_semantics=semantics, vmem_limit_bytes=VMEM_LIMIT_BYTES)


def _const_spec(shape):
    nd = len(shape)
    return pl.BlockSpec(shape, lambda *_: (0,) * nd, pipeline_mode=pl.Buffered(1))


def _log_sigmoid(x):
    return jnp.minimum(x, 0.0) - jnp.log1p(jnp.exp(-jnp.abs(x)))


def _rms(x, g):
    return (x * lax.rsqrt(jnp.mean(x * x, axis=-1, keepdims=True) + EPS)) * g


def _modulate(x, g, shift, scale):
    return _rms(x, g) * (1.0 + scale) + shift


def _dot(a, b):
    return jnp.dot(a, b, preferred_element_type=F32)


def _dot_nt(a, b):
    return lax.dot_general(a, b, (((1,), (1,)), ((), ())), preferred_element_type=F32)


def _dot_tn(a, b):
    return lax.dot_general(a, b, (((0,), (0,)), ((), ())), preferred_element_type=F32)


def _split3(x):
    hi = x.astype(BF16)
    r1 = x - hi.astype(F32)
    mid = r1.astype(BF16)
    lo = (r1 - mid.astype(F32)).astype(BF16)
    return hi, mid, lo


def _lane_tile(x, reps):
    return jnp.concatenate([x] * reps, axis=1)


_HI16 = np.uint32(0xFFFF0000)


def _pack_pairs(xb):
    w = xb.shape[1] // 2
    xf = xb.astype(F32)
    lo = lax.bitcast_convert_type(xf[:, :w], jnp.uint32) >> 16
    hi = lax.bitcast_convert_type(xf[:, w:], jnp.uint32) & _HI16
    return lax.bitcast_convert_type(lo | hi, F32)


def _unpack_pairs(words):
    u = lax.bitcast_convert_type(words, jnp.uint32)
    lo = lax.bitcast_convert_type(u << 16, F32)
    hi = lax.bitcast_convert_type(u & _HI16, F32)
    return jnp.concatenate([lo, hi], axis=1).astype(BF16)


def _mod_kernel(c_ref, w_ref, b_ref, o_ref):
    c = c_ref[...]
    s = c * jax.nn.sigmoid(c)
    o_ref[...] = _dot(s.astype(BF16), w_ref[...].astype(BF16)) + b_ref[...]


def _modulation(cvec, w_mod, b_mod):
    ncol = N_MOD * D_MODEL
    blk = 1536
    return pl.pallas_call(
        _mod_kernel,
        out_shape=jax.ShapeDtypeStruct((DEPTH, MOD_ROWS, ncol), F32),
        grid=(DEPTH, ncol // blk),
        in_specs=[
            pl.BlockSpec((MOD_ROWS, D_MODEL), lambda l, j: (0, 0)),
            pl.BlockSpec((None, D_MODEL, blk), lambda l, j: (l, 0, j)),
            pl.BlockSpec((None, 1, blk), lambda l, j: (l, 0, j)),
        ],
        out_specs=pl.BlockSpec((None, MOD_ROWS, blk), lambda l, j: (l, 0, j)),
        compiler_params=_cparams(("arbitrary", "arbitrary")),
        name="modulation",
    )(cvec, w_mod, b_mod.reshape(DEPTH, 1, ncol))


def _mod_spec(layer, j, tile):
    tpg = NP_TOK // tile
    return pl.BlockSpec((None, None, None, 1, D_MODEL), lambda i: (layer, i // tpg, j, 0, 0))


def _tok_spec(tile, width):
    return pl.BlockSpec((tile, width), lambda i: (i, 0))


def _split_specs(tile, width):
    npt = NP_TOK // tile
    return [pl.BlockSpec((tile, width), lambda i: (jnp.minimum(i, npt - 1), 0)),
            pl.BlockSpec((tile, width), lambda i: (jnp.maximum(i - npt, 0), 0))]


def _pick(tile, p_ref, s_ref):
    return jnp.where(pl.program_id(0) < NP_TOK // tile, p_ref[...], s_ref[...])


def _rope_row_spec(tile, width):
    npt = NP_TOK // tile
    spt = DEC_SEQ // tile
    return pl.BlockSpec((tile, width), lambda i: (jnp.where(i < npt, 0, 1 + (i - npt) % spt), 0))


def _even_in_kernel(xp_ref, xs_ref, g_ref, shift_ref, scale_ref, win_ref, qn_ref, wq_ref, kvn_ref, wkk_ref,
                    wkv_ref, vbias_ref, epl_ref, wgf_ref, bgf_ref, wgb_ref, bgb_ref, lmat_ref,
                    umat_ref, cq_ref, sq_ref, ck_ref, sk_ref,
                    q_out, k_out, v_out, ckv_out, kpe_out, gq_out, gk_out, gv_out, gr_out,
                    bf_out, bb_out):
    h = _modulate(_pick(TOK_TILE, xp_ref, xs_ref), g_ref[...], shift_ref[...], scale_ref[...])
    z = _dot(h.astype(BF16), win_ref[...])

    cqn = _rms(z[:, _C_CQ:_C_CQ + MLA_Q_RANK], qn_ref[...]).astype(BF16)
    qf = _dot(cqn, wq_ref[...])
    cq_t = _lane_tile(cq_ref[...], MLA_HEADS)
    sq_t = _lane_tile(sq_ref[...], MLA_HEADS)
    q_out[...] = (qf[:, :MLA_QK_W] * cq_t + qf[:, MLA_QK_W:] * sq_t).astype(BF16)

    ckvn = _rms(z[:, _C_CKV:_C_CKV + MLA_KV_RANK], kvn_ref[...])
    ckv_out[...] = ckvn
    small = z[:, _C_SMALL:_C_SMALL + LANES]
    kpe = (small[:, _S_KPE:_S_KPE + MLA_ROPE] * ck_ref[...]
           + small[:, _S_KPER:_S_KPER + MLA_ROPE] * sk_ref[...])
    kpe_out[...] = kpe
    ckvn_b = ckvn.astype(BF16)
    k_out[...] = (_dot(ckvn_b, wkk_ref[...]) + _dot(kpe.astype(BF16), epl_ref[...])).astype(BF16)
    v_out[...] = (_dot(ckvn_b, wkv_ref[...]) + vbias_ref[...]).astype(BF16)

    gq_out[...] = z[:, _C_GQ:_C_GQ + GLA_HEADS * GLA_DK] * (GLA_DK ** -0.5)
    gk_out[...] = z[:, _C_GK:_C_GK + GLA_HEADS * GLA_DK]
    gv_out[...] = z[:, _C_GV:_C_GV + GLA_HEADS * GLA_DV].astype(BF16)
    gr_out[...] = z[:, _C_GR:_C_GR + GLA_HEADS * GLA_DV]

    small_b = small.astype(BF16)
    la_f = _log_sigmoid(_dot(small_b, wgf_ref[...]) + bgf_ref[...]) * (1.0 / GLA_GATE_NORM)
    la_b = _log_sigmoid(_dot(small_b, wgb_ref[...]) + bgb_ref[...]) * (1.0 / GLA_GATE_NORM)
    lmat = lmat_ref[...]
    umat = umat_ref[...]
    for r in range(TOK_TILE // CUMSUM_BLOCK):
        rows = slice(r * CUMSUM_BLOCK, (r + 1) * CUMSUM_BLOCK)
        f_hi, f_mid, f_lo = _split3(la_f[rows])
        bf_out[rows, :] = _dot(lmat, f_hi) + _dot(lmat, f_mid) + _dot(lmat, f_lo)
        b_hi, b_mid, b_lo = _split3(la_b[rows])
        bb_out[rows, :] = _dot(umat, b_hi) + _dot(umat, b_mid) + _dot(umat, b_lo)


def _even_in_proj(xp, xs, mod5, layer, g, wts, tabs):
    t = TOK_TILE
    out_widths = [(MLA_QK_W, BF16), (MLA_QK_W, BF16), (MLA_VEXT_W, BF16), (MLA_KV_RANK, F32),
                  (MLA_ROPE, F32), (GLA_HEADS * GLA_DK, F32), (GLA_HEADS * GLA_DK, F32),
                  (GLA_HEADS * GLA_DV, BF16), (GLA_HEADS * GLA_DV, F32),
                  (GLA_HEADS * GLA_DK, F32), (GLA_HEADS * GLA_DK, F32)]
    const_names = ["win", "qn", "wq", "kvn", "wkk", "wkv", "vbias", "epl", "wgf", "bgf", "wgb",
                   "bgb", "lmat", "umat"]
    consts = [wts[n] for n in const_names]
    in_specs = (_split_specs(t, D_MODEL)
                + [_const_spec((1, D_MODEL)), _mod_spec(layer, 0, t), _mod_spec(layer, 1, t)]
                + [_const_spec(c.shape) for c in consts]
                + [_rope_row_spec(t, LANES), _rope_row_spec(t, LANES),
                   _rope_row_spec(t, MLA_ROPE), _rope_row_spec(t, MLA_ROPE)])
    return pl.pallas_call(
        _even_in_kernel,
        out_shape=[jax.ShapeDtypeStruct((N_TOK, w), dt) for w, dt in out_widths],
        grid=(N_TOK // t,),
        in_specs=in_specs,
        out_specs=[_tok_spec(t, w) for w, _ in out_widths],
        compiler_params=_cparams(("parallel",)),
        name="even_in_proj",
    )(xp, xs, g, mod5, mod5, *consts, tabs["mla_cq"], tabs["mla_sq"], tabs["mla_ck"], tabs["mla_sk"])


def _cache_kv_kernel(ckv_ref, kpe_ref, wkk_ref, wkv_ref, vbias_ref, epl_ref, k_out, v_out):
    ckv_b = ckv_ref[...].astype(BF16)
    k_out[...] = (_dot(ckv_b, wkk_ref[...]) + _dot(kpe_ref[...].astype(BF16), epl_ref[...])).astype(BF16)
    v_out[...] = (_dot(ckv_b, wkv_ref[...]) + vbias_ref[...]).astype(BF16)


def _cache_kv(ckv, kpe, wts):
    n = ckv.shape[0]
    consts = [wts[k] for k in ("wkk", "wkv", "vbias", "epl")]
    return pl.pallas_call(
        _cache_kv_kernel,
        out_shape=[jax.ShapeDtypeStruct((n, MLA_QK_W), BF16), jax.ShapeDtypeStruct((n, MLA_VEXT_W), BF16)],
        grid=(1,),
        in_specs=[_const_spec(ckv.shape), _const_spec(kpe.shape)] + [_const_spec(c.shape) for c in consts],
        out_specs=[_const_spec((n, MLA_QK_W)), _const_spec((n, MLA_VEXT_W))],
        compiler_params=_cparams(("arbitrary",)),
        name="mla_cache_kv",
    )(ckv, kpe, *consts)


def _mla_attn_kernel(*refs, n_seg):
    q_ref = refs[0]
    k_refs = refs[1:1 + n_seg]
    v_refs = refs[1 + n_seg:1 + 2 * n_seg]
    o_ref = refs[1 + 2 * n_seg]
    scale = (MLA_NOPE + MLA_ROPE) ** -0.5
    c = scale * float(np.log2(np.e))
    tq = q_ref.shape[0]
    q_sub = min(tq, MLA_Q_SUB)
    kt = MXU_COLS
    tiles = [(si, r0) for si, k in enumerate(k_refs) for r0 in range(0, k.shape[0], kt)]
    lane = lax.broadcasted_iota(jnp.int32, (q_sub, LANES), 1)
    n_pairs = o_ref.shape[1] // LANES
    for pi, qs in [(pi, qs) for pi in range(n_pairs) for qs in range(tq // q_sub)]:
        rows = slice(qs * q_sub, (qs + 1) * q_sub)
        vl = slice(pi * 2 * LANES, (pi + 1) * 2 * LANES)
        res = []
        for j in range(2):
            hl = slice((2 * pi + j) * LANES, (2 * pi + j + 1) * LANES)
            qj = q_ref[rows, hl]
            macc = None
            s_tiles = []
            for si, r0 in tiles:
                s = _dot_nt(qj, k_refs[si][r0:r0 + kt, hl])
                s_tiles.append(s)
                mt = jnp.maximum(s[:, :LANES], s[:, LANES:])
                macc = mt if macc is None else jnp.maximum(macc, mt)
            m = macc.max(axis=-1, keepdims=True)
            r = None
            for (si, r0), s in zip(tiles, s_tiles):
                p = jnp.exp2((s - m) * c).astype(BF16)
                rj = _dot(p, v_refs[si][r0:r0 + kt, vl])
                r = rj if r is None else r + rj
            res.append(r[:, :LANES] / r[:, LANES:])
        o_ref[rows, pi * LANES:(pi + 1) * LANES] = jnp.where(lane < MLA_V, res[0], res[1]).astype(BF16)


def _mla_attention(q, ks, vs, *, n_batch, seq_q, q_tile, tok_off, k_batch_rows, pairs):
    n_seg = len(ks)
    nq = seq_q // q_tile
    qoff = tok_off // q_tile
    grid = (n_batch, MLA_HEADS // 2 // pairs, nq)
    wq = pairs * 2 * LANES
    in_specs = [pl.BlockSpec((q_tile, wq), lambda b, hp, i: (qoff + b * nq + i, hp))]
    for s in range(n_seg):
        rows = k_batch_rows[s]
        off = (tok_off // rows) if s == 0 else 0
        in_specs.append(pl.BlockSpec((rows, wq), functools.partial(lambda b, hp, i, off: (off + b, hp), off=off)))
    for s in range(n_seg):
        rows = k_batch_rows[s]
        off = (tok_off // rows) if s == 0 else 0
        in_specs.append(pl.BlockSpec((rows, wq), functools.partial(lambda b, hp, i, off: (off + b, hp), off=off)))
    return pl.pallas_call(
        functools.partial(_mla_attn_kernel, n_seg=n_seg),
        out_shape=jax.ShapeDtypeStruct((n_batch * seq_q, MLA_HEADS * MLA_V), BF16),
        grid=grid,
        in_specs=in_specs,
        out_specs=pl.BlockSpec((q_tile, pairs * LANES), lambda b, hp, i: (b * nq + i, hp)),
        compiler_params=_cparams(("parallel", "parallel", "arbitrary")),
        name=f"mla_attention_{n_seg}seg",
    )(q, *ks, *vs)


def _gla_kernel(q_ref, k_ref, v_ref, gr_ref, bf_ref, bb_ref, s0f_ref, s0b_ref, gn_ref,
                o_ref, sf_ref, sb_ref, acc_ref, kdf_ref, kdb_ref, qdf_ref, qdb_ref, hist_ref, *, n_chunks):
    c = GLA_CHUNK
    cpb = min(n_chunks, GLA_BLOCK_CHUNKS)
    blk = cpb * c
    lane = lax.broadcasted_iota(jnp.int32, (blk, LANES), 1)
    lo = lane < GLA_DK
    row = lax.broadcasted_iota(jnp.int32, (blk, blk), 0)
    col = lax.broadcasted_iota(jnp.int32, (blk, blk), 1)
    chunk_bits = c.bit_length() - 1
    same_chunk = jnp.right_shift(row, chunk_bits) == jnp.right_shift(col, chunk_bits)
    tril = same_chunk & (row >= col)
    triu = same_chunk & (row <= col)
    zero_blk = jnp.zeros((GLA_DK, GLA_DV), F32)

    def pair_state_t(s_ref):
        blockdiag = jnp.concatenate(
            [jnp.concatenate([s_ref[0], zero_blk], axis=1),
             jnp.concatenate([zero_blk, s_ref[1]], axis=1)], axis=0)
        return blockdiag.T

    dirs = ((bf_ref, c // 2 - 1, c - 1, tril, kdf_ref, qdf_ref),
            (bb_ref, c // 2, 0, triu, kdb_ref, qdb_ref))
    sels = (lo, jnp.logical_not(lo))
    lo_c = lax.broadcasted_iota(jnp.int32, (c, LANES), 1) < GLA_DK
    sels_c = (lo_c, jnp.logical_not(lo_c))
    hcols = (slice(0, GLA_DV), slice(GLA_DV, 2 * GLA_DV))

    def block(r, carry):
        rows = pl.ds(pl.multiple_of(r * blk, blk), blk)
        q = q_ref[rows, :]
        k = k_ref[rows, :]
        v = v_ref[rows, :]

        def chunk_row(b, r):
            return jnp.concatenate([jnp.broadcast_to(b[ch * c + r:ch * c + r + 1, :], (c, LANES))
                                    for ch in range(cpb)], axis=0)

        for d, (b_ref, mid_row, last_row, causal, kd_ref, qd_ref) in enumerate(dirs):
            b = b_ref[rows, :]
            b_mid = chunk_row(b, mid_row)
            b_last = chunk_row(b, last_row)
            qe = q * jnp.exp(b - b_mid)
            ke = (k * jnp.exp(b_mid - b)).astype(BF16)
            kd_ref[rows, :] = (k * jnp.exp(b_last - b)).astype(BF16)
            qd_ref[rows, :] = (q * jnp.exp(b)).astype(BF16)
            for j in range(2):
                a = _dot_nt(jnp.where(sels[j], qe, 0.0).astype(BF16), ke)
                o = _dot(jnp.where(causal, a, 0.0).astype(BF16), v[:, hcols[j]])
                if d == 0:
                    acc_ref[rows, hcols[j]] = o
                else:
                    acc_ref[rows, hcols[j]] += o
        return carry

    lax.fori_loop(0, n_chunks // cpb, block, 0, unroll=min(2, n_chunks // cpb))

    def scan(i, carry):
        new = []
        for d, (b_ref, _, last_row, _, kd_ref, _) in enumerate(dirs):
            ci = i if d == 0 else n_chunks - 1 - i
            rows = pl.ds(pl.multiple_of(ci * c, c), c)
            hist_ref[d, ci] = carry[d].astype(BF16)
            grp = b_ref[pl.ds(pl.multiple_of(ci * c + (last_row // 8) * 8, 8), 8), :]
            b_last = grp[last_row % 8:last_row % 8 + 1, :]
            new.append(carry[d] * jnp.exp(b_last) + _dot_tn(v_ref[rows, :], kd_ref[rows, :]))
        return tuple(new)

    st_f, st_b = lax.fori_loop(0, n_chunks, scan, (pair_state_t(s0f_ref), pair_state_t(s0b_ref)),
                               unroll=GLA_CHUNK_UNROLL)

    def inter(ci, carry):
        rows = pl.ds(pl.multiple_of(ci * c, c), c)
        for d, (_, _, _, _, _, qd_ref) in enumerate(dirs):
            qd = qd_ref[rows, :]
            st = hist_ref[d, ci]
            for j in range(2):
                acc_ref[rows, hcols[j]] += _dot_nt(jnp.where(sels_c[j], qd, jnp.zeros_like(qd)), st[hcols[j], :])
        return carry

    lax.fori_loop(0, n_chunks, inter, 0, unroll=GLA_CHUNK_UNROLL)
    s_f = st_f.T
    s_b = st_b.T
    sf_ref[0] = s_f[:GLA_DK, :GLA_DV]
    sf_ref[1] = s_f[GLA_DK:, GLA_DV:]
    sb_ref[0] = s_b[:GLA_DK, :GLA_DV]
    sb_ref[1] = s_b[GLA_DK:, GLA_DV:]

    gn = gn_ref[...]
    for j in range(2):
        cols = slice(j * GLA_DV, (j + 1) * GLA_DV)
        gr = gr_ref[:, cols]
        o_ref[:, cols] = (_rms(acc_ref[:, cols], gn) * (gr * jax.nn.sigmoid(gr))).astype(BF16)


def _gla(gq, gk, gv, gr, bf, bb, s0f, s0b, gnorm, *, n_batch, seq, tok_off):
    n_chunks = seq // GLA_CHUNK
    boff = tok_off // seq
    hp = GLA_HEADS // 2
    tok = lambda w: pl.BlockSpec((seq, w), lambda b, p: (boff + b, p))
    st = pl.BlockSpec((None, 2, GLA_DK, GLA_DV), lambda b, p: (b, p, 0, 0))
    return pl.pallas_call(
        functools.partial(_gla_kernel, n_chunks=n_chunks),
        out_shape=[jax.ShapeDtypeStruct((n_batch * seq, GLA_HEADS * GLA_DV), BF16),
                   jax.ShapeDtypeStruct((n_batch, GLA_HEADS, GLA_DK, GLA_DV), F32),
                   jax.ShapeDtypeStruct((n_batch, GLA_HEADS, GLA_DK, GLA_DV), F32)],
        grid=(n_batch, hp),
        in_specs=[tok(2 * GLA_DK), tok(2 * GLA_DK), tok(2 * GLA_DV), tok(2 * GLA_DV),
                  tok(2 * GLA_DK), tok(2 * GLA_DK), st, st, _const_spec((1, GLA_DV))],
        out_specs=[pl.BlockSpec((seq, 2 * GLA_DV), lambda b, p: (b, p)), st, st],
        scratch_shapes=[pltpu.VMEM((seq, 2 * GLA_DV), F32)]
                       + [pltpu.VMEM((seq, 2 * GLA_DK), BF16)] * 4
                       + [pltpu.VMEM((2, n_chunks, 2 * GLA_DV, 2 * GLA_DK), BF16)],
        compiler_params=_cparams(("parallel", "parallel")),
        name=f"gla_seq{seq}",
    )(gq, gk, gv, gr, bf, bb, s0f, s0b, gnorm)


def _out_proj_kernel(*refs, n_o, x_split, route):
    t = TOK_TILE
    o = [_pick(t, refs[2 * j], refs[2 * j + 1]) for j in range(n_o)]
    rest = refs[2 * n_o:]
    w_ref = rest[0]
    if x_split:
        x_in = _pick(t, rest[1], rest[2])
        rest = rest[3:]
    else:
        x_in = rest[1][...]
        rest = rest[2:]
    g1_ref, gate_ref, g2_ref, shift_ref, scale_ref = rest[:5]
    rest = rest[5:]
    if route:
        router_in, rest = rest[:3], rest[3:]
    x_out, h_out = rest[:2]
    y = _dot(o[0] if n_o == 1 else jnp.concatenate(o, axis=1), w_ref[...])
    x = x_in + gate_ref[...] * _rms(y, g1_ref[...])
    x_out[...] = x
    h = _modulate(x, g2_ref[...], shift_ref[...], scale_ref[...])
    if route:
        hb = h.astype(BF16)
        h_out[...] = _pack_pairs(hb)
        _route(hb, *router_in, *rest[2:])
    else:
        h_out[...] = h.astype(h_out.dtype)


def _out_proj(os_, w, x, mod5, layer, g1, g2, h_dtype, router=None):
    t = TOK_TILE
    x_split = isinstance(x, tuple)
    in_specs, args = [], []
    for o_p, o_s in os_:
        in_specs += _split_specs(t, o_p.shape[1])
        args += [o_p, o_s]
    in_specs.append(_const_spec(w.shape))
    args.append(w)
    if x_split:
        in_specs += _split_specs(t, D_MODEL)
        args += list(x)
    else:
        in_specs.append(_tok_spec(t, D_MODEL))
        args.append(x)
    in_specs += [_const_spec((1, D_MODEL)), _mod_spec(layer, 2, t), _const_spec((1, D_MODEL)),
                 _mod_spec(layer, 3, t), _mod_spec(layer, 4, t)]
    args += [g1, mod5, g2, mod5, mod5]
    h_width = D_MODEL // 2 if router is not None else D_MODEL
    out_shape = [jax.ShapeDtypeStruct((N_TOK, D_MODEL), F32), jax.ShapeDtypeStruct((N_TOK, h_width), h_dtype)]
    out_specs = [_tok_spec(t, D_MODEL), _tok_spec(t, h_width)]
    scratch = []
    if router is not None:
        r = np.arange(t)
        ltri = jnp.asarray(r[:, None] > r[None, :], BF16)
        args += [router[0], router[1], ltri]
        in_specs += [_const_spec(router[0].shape), _const_spec(router[1].shape), _const_spec((t, t))]
        out_shape += [jax.ShapeDtypeStruct((N_TOK, LANES), F32), jax.ShapeDtypeStruct((N_TOK, LANES), jnp.int32),
                      jax.ShapeDtypeStruct((N_TOK, LANES), jnp.int32), jax.ShapeDtypeStruct((8, LANES), jnp.int32)]
        out_specs += [_tok_spec(t, LANES), _tok_spec(t, LANES), _tok_spec(t, LANES),
                      pl.BlockSpec((8, LANES), lambda i: (0, 0))]
        scratch = [pltpu.VMEM((8, LANES), F32)]
    return pl.pallas_call(
        functools.partial(_out_proj_kernel, n_o=len(os_), x_split=x_split, route=router is not None),
        out_shape=out_shape,
        grid=(N_TOK // t,),
        in_specs=in_specs,
        out_specs=out_specs,
        scratch_shapes=scratch,
        compiler_params=_cparams(("arbitrary",) if router is not None else ("parallel",)),
        name=f"out_proj_{len(os_)}",
    )(*args)


def _ffn_kernel(h_ref, x_ref, wg_ref, wu_ref, wd_ref, g_ref, gate_ref, x_out):
    h = h_ref[...]
    f = None
    for cidx in range(D_FF // FF_CHUNK):
        cols = slice(cidx * FF_CHUNK, (cidx + 1) * FF_CHUNK)
        a = _dot(h, wg_ref[:, cols].astype(BF16))
        u = _dot(h, wu_ref[:, cols].astype(BF16))
        fc = _dot(((a * jax.nn.sigmoid(a)) * u).astype(BF16), wd_ref[cols, :].astype(BF16))
        f = fc if f is None else f + fc
    x_out[...] = x_ref[...] + gate_ref[...] * _rms(f, g_ref[...])


def _ffn(h, x, wg, wu, wd, mod5, layer, g3):
    t = TOK_TILE
    return pl.pallas_call(
        _ffn_kernel,
        out_shape=jax.ShapeDtypeStruct((N_TOK, D_MODEL), F32),
        grid=(N_TOK // t,),
        in_specs=[_tok_spec(t, D_MODEL), _tok_spec(t, D_MODEL), _const_spec(wg.shape),
                  _const_spec(wu.shape), _const_spec(wd.shape), _const_spec((1, D_MODEL)),
                  _mod_spec(layer, 5, t)],
        out_specs=_tok_spec(t, D_MODEL),
        compiler_params=_cparams(("parallel",)),
        name="ffn_swiglu",
    )(h, x, wg, wu, wd, g3, mod5)


GQA_Q_W = GQA_HEADS * GQA_HEAD_DIM
GQA_KV_W = GQA_KV_HEADS * GQA_HEAD_DIM
GQA_VEXT_W = (GQA_KV_HEADS // 2) * 2 * LANES
_O_Q, _O_QR = 0, GQA_Q_W
_O_K, _O_KR = 2 * GQA_Q_W, 2 * GQA_Q_W + GQA_KV_W
_O_V = 2 * GQA_Q_W + 2 * GQA_KV_W
IN_C_EXT = _O_V + GQA_VEXT_W


def _odd_in_kernel(x_ref, g_ref, shift_ref, scale_ref, win_ref, vbias_ref, c_ref, s_ref,
                   q_out, kb_out, vb_out, kt_out, vt_out):
    h = _modulate(x_ref[...], g_ref[...], shift_ref[...], scale_ref[...])
    z = _dot(h.astype(BF16), win_ref[...])
    c_t = _lane_tile(c_ref[...], GQA_Q_W // LANES)
    s_t = _lane_tile(s_ref[...], GQA_Q_W // LANES)
    q = z[:, _O_Q:_O_Q + GQA_Q_W] * c_t + z[:, _O_QR:_O_QR + GQA_Q_W] * s_t
    q_out[...] = (q * (GQA_HEAD_DIM ** -0.5)).astype(BF16)
    k = (z[:, _O_K:_O_K + GQA_KV_W] * c_t[:, :GQA_KV_W]
         + z[:, _O_KR:_O_KR + GQA_KV_W] * s_t[:, :GQA_KV_W])
    kb_out[...] = k.astype(BF16)
    vext = z[:, _O_V:_O_V + GQA_VEXT_W] + vbias_ref[...]
    vb_out[...] = vext.astype(BF16)

    @pl.when(pl.program_id(0) < NP_TOK // TOK_TILE)
    def _():
        v = jnp.concatenate([vext[:, 2 * p * LANES:(2 * p + 1) * LANES] for p in range(GQA_KV_HEADS // 2)], axis=1)
        for b in range(TOK_TILE // SEQ):
            kt_out[b] = k[b * SEQ:(b + 1) * SEQ, :].T
            vt_out[b] = v[b * SEQ:(b + 1) * SEQ, :].T


def _odd_in_proj(x, mod5, layer, g, win, vbias, tab_c, tab_s):
    t = TOK_TILE
    npt = NP_TOK // t
    out_widths = [(GQA_Q_W, BF16), (GQA_KV_W, BF16), (GQA_VEXT_W, BF16)]
    cache_shape = jax.ShapeDtypeStruct((BATCH, GQA_KV_W, SEQ), F32)
    cache_spec = pl.BlockSpec((t // SEQ, GQA_KV_W, SEQ), lambda i: (jnp.minimum(i, npt - 1), 0, 0))
    return pl.pallas_call(
        _odd_in_kernel,
        out_shape=[jax.ShapeDtypeStruct((N_TOK, w), dt) for w, dt in out_widths] + [cache_shape, cache_shape],
        grid=(N_TOK // t,),
        in_specs=[_tok_spec(t, D_MODEL), _const_spec((1, D_MODEL)), _mod_spec(layer, 0, t),
                  _mod_spec(layer, 1, t), _const_spec(win.shape), _const_spec(vbias.shape),
                  _rope_row_spec(t, LANES), _rope_row_spec(t, LANES)],
        out_specs=[_tok_spec(t, w) for w, _ in out_widths] + [cache_spec, cache_spec],
        compiler_params=_cparams(("arbitrary",)),
        name="odd_in_proj",
    )(x, g, mod5, mod5, win, vbias, tab_c, tab_s)


def _gqa_kernel(sink_ref, q_ref, *refs, local_len):
    if local_len:
        kl_ref, vl_ref, kc_ref, vc_ref, o_ref = refs
    else:
        kc_ref, vc_ref, o_ref = refs
    tq = q_ref.shape[0]
    lane = lax.broadcasted_iota(jnp.int32, (tq, LANES), 1)
    lo = lane < GQA_HEAD_DIM
    if local_len:
        i = pl.program_id(1)
        q0 = i * tq
        seq = kl_ref.shape[0]
        kstart = pl.multiple_of(jnp.clip(q0 - WINDOW, 0, seq - local_len), LANES)
        qpos = q0 + lax.broadcasted_iota(jnp.int32, (tq, local_len), 0)
        kpos = kstart + lax.broadcasted_iota(jnp.int32, (tq, local_len), 1)
        band = jnp.abs(qpos - kpos) <= WINDOW
    for p in range(GQA_KV_HEADS // 2):
        kc = kc_ref[:, p * LANES:(p + 1) * LANES]
        vc = vc_ref[:, 2 * p * LANES:(2 * p + 2) * LANES]
        if local_len:
            kl = kl_ref[pl.ds(kstart, local_len), p * LANES:(p + 1) * LANES]
            vl = vl_ref[pl.ds(kstart, local_len), 2 * p * LANES:(2 * p + 2) * LANES]
        for blk in range(GQA_GROUP):
            cols = slice((p * GQA_GROUP + blk) * LANES, (p * GQA_GROUP + blk + 1) * LANES)
            qb = q_ref[:, cols]
            res = []
            for half in range(2):
                head = (2 * p + half) * GQA_GROUP + blk
                sink = sink_ref[head]
                qh = jnp.where(lo if half == 0 else jnp.logical_not(lo), qb, jnp.zeros_like(qb))
                s_c = _dot_nt(qh, kc)
                m = jnp.maximum(s_c.max(axis=-1, keepdims=True), sink)
                if local_len:
                    s_l = jnp.where(band, _dot_nt(qh, kl), NEG_INF)
                    m = jnp.maximum(m, s_l.max(axis=-1, keepdims=True))
                r = _dot(jnp.exp(s_c - m).astype(BF16), vc)
                if local_len:
                    r = r + _dot(jnp.exp(s_l - m).astype(BF16), vl)
                res.append(r[:, :LANES] / (r[:, LANES:] + jnp.exp(sink - m)))
            o_ref[:, cols] = jnp.where(lo, res[0], res[1]).astype(BF16)


def _gqa_attention(sink, q, k_loc, v_loc, k_ctx, v_ctx, *, n_batch, seq_q, q_tile, tok_off, n_ctx, local):
    nq = seq_q // q_tile
    qoff = tok_off // q_tile
    local_len = q_tile + 2 * WINDOW if local else 0
    in_specs = [pl.BlockSpec(memory_space=pltpu.SMEM),
                pl.BlockSpec((q_tile, GQA_Q_W), lambda b, i: (qoff + b * nq + i, 0))]
    args = [sink, q]
    if local:
        boff = tok_off // seq_q
        in_specs += [pl.BlockSpec((seq_q, GQA_KV_W), lambda b, i: (boff + b, 0)),
                     pl.BlockSpec((seq_q, GQA_VEXT_W), lambda b, i: (boff + b, 0))]
        args += [k_loc, v_loc]
    in_specs += [pl.BlockSpec((n_ctx, GQA_KV_W), lambda b, i: (b, 0)),
                 pl.BlockSpec((n_ctx, GQA_VEXT_W), lambda b, i: (b, 0))]
    args += [k_ctx, v_ctx]
    return pl.pallas_call(
        functools.partial(_gqa_kernel, local_len=local_len),
        out_shape=jax.ShapeDtypeStruct((n_batch * seq_q, GQA_Q_W), BF16),
        grid=(n_batch, nq),
        in_specs=in_specs,
        out_specs=pl.BlockSpec((q_tile, GQA_Q_W), lambda b, i: (b * nq + i, 0)),
        compiler_params=_cparams(("parallel", "arbitrary")),
        name="gqa_local" if local else "gqa_ctx",
    )(*args)


def _route(h, w_ref, b_ref, ltri_ref, wsel_out, isel_out, rank_out, cnt_out, carry_ref):
    @pl.when(pl.program_id(0) == 0)
    def _():
        carry_ref[...] = jnp.zeros_like(carry_ref)

    logits = _dot(h, w_ref[...]) + b_ref[...]
    lane = lax.broadcasted_iota(jnp.int32, logits.shape, 1)
    neg = float(np.finfo(np.float32).min)
    lg = jnp.where(lane < N_EXPERTS, logits, neg)
    v1 = lg.max(axis=-1, keepdims=True)
    i1 = jnp.min(jnp.where(lg == v1, lane, LANES), axis=-1, keepdims=True)
    lg2 = jnp.where(lane == i1, neg, lg)
    v2 = lg2.max(axis=-1, keepdims=True)
    i2 = jnp.min(jnp.where(lg2 == v2, lane, LANES), axis=-1, keepdims=True)
    e2 = jnp.exp(v2 - v1)
    den = 1.0 + e2
    wsel_out[...] = jnp.where(lane == 0, 1.0 / den, jnp.where(lane == 1, e2 / den, 0.0))
    isel_out[...] = jnp.where(lane == 0, i1, jnp.where(lane == 1, i2, 0))
    hit = jnp.where(lane == i1, 1.0, jnp.where(lane == i2, 1.0, 0.0))
    carry = carry_ref[...]
    rank_out[...] = (_dot(ltri_ref[...], hit.astype(BF16)) + carry[0:1, :]).astype(jnp.int32)
    carry = carry + jnp.sum(hit, axis=0, keepdims=True)
    carry_ref[...] = carry
    cnt_out[...] = carry.astype(jnp.int32)


def _route_tables(isel, rank, cnt):
    tm = MOE_ROW_TILE
    counts = cnt[0, :N_EXPERTS]
    padded = ((counts + tm - 1) // tm) * tm
    ends = jnp.cumsum(padded)
    base = ends - padded
    e_ids = jnp.arange(N_EXPERTS, dtype=jnp.int32)
    row = rank[:, :N_EXPERTS] + base[None, :]
    pos1 = jnp.sum(jnp.where(e_ids[None, :] == isel[:, 0:1], row, 0), axis=1)
    pos2 = jnp.sum(jnp.where(e_ids[None, :] == isel[:, 1:2], row, 0), axis=1)
    tile_start = jnp.arange(MOE_ROWS // tm, dtype=jnp.int32) * tm
    tile_expert = jnp.minimum(jnp.sum(tile_start[:, None] >= ends[None, :], axis=1), N_EXPERTS - 1).astype(jnp.int32)
    tile_valid = jnp.clip((base + counts)[tile_expert] - tile_start, 0, tm).astype(jnp.int32)
    tile_valid = jnp.where(tile_start < ends[-1], tile_valid, 0)
    return pos1, pos2, tile_expert, tile_valid


def _scatter_rows(rows, pos1, pos2, n_out):
    n_tok, d = rows.shape
    per_w = n_tok // SC_WORKERS
    w = SC_GATHER_WINDOW * D_MODEL // d
    assert per_w * SC_WORKERS == n_tok and per_w % w == 0
    mesh = plsc.VectorSubcoreMesh(core_axis_name="core", subcore_axis_name="subcore")

    assert (per_w // w) % 2 == 0
    slot_types = [pltpu.VMEM((w,), jnp.int32), pltpu.VMEM((w,), jnp.int32), pltpu.VMEM((w, d), rows.dtype),
                  pltpu.SemaphoreType.DMA, pltpu.SemaphoreType.DMA]

    @functools.partial(
        pl.kernel, out_type=jax.ShapeDtypeStruct((n_out, d), rows.dtype), mesh=mesh,
        scratch_types=slot_types * 2, name="sc_scatter_rows")
    def scatter(x_hbm, p1_hbm, p2_hbm, o_hbm, *scratch):
        wid = lax.axis_index("subcore") * SC_CORES + lax.axis_index("core")
        base = wid * per_w
        slots = (scratch[:5], scratch[5:])

        @pl.loop(0, per_w // (2 * w))
        def _(g):
            loads = []
            for s, (i1_v, i2_v, rows_v, sem_in, _) in enumerate(slots):
                off = base + (2 * g + s) * w
                loads.append([pltpu.async_copy(p1_hbm.at[pl.ds(off, w)], i1_v, sem_in),
                              pltpu.async_copy(p2_hbm.at[pl.ds(off, w)], i2_v, sem_in),
                              pltpu.async_copy(x_hbm.at[pl.ds(off, w)], rows_v, sem_in)])
            stores = []
            for s, (i1_v, i2_v, rows_v, _, sem_out) in enumerate(slots):
                for cp in loads[s]:
                    cp.wait()
                stores += [pltpu.async_copy(rows_v, o_hbm.at[i1_v], sem_out),
                           pltpu.async_copy(rows_v, o_hbm.at[i2_v], sem_out)]
            for cp in stores:
                cp.wait()

    return scatter(rows, pos1, pos2)


def _gather_rows(table, idx):
    n_idx = idx.shape[0]
    d = table.shape[1]
    per_w = n_idx // SC_WORKERS
    assert per_w * SC_WORKERS == n_idx and per_w % SC_INDEX_BLOCK == 0
    mesh = plsc.VectorSubcoreMesh(core_axis_name="core", subcore_axis_name="subcore")

    w = SC_GATHER_WINDOW
    n_sub = SC_INDEX_BLOCK // w
    slot_types = [pltpu.VMEM((w, d), table.dtype), pltpu.SemaphoreType.DMA, pltpu.SemaphoreType.DMA]

    @functools.partial(
        pl.kernel, out_type=jax.ShapeDtypeStruct((n_idx, d), table.dtype), mesh=mesh,
        scratch_types=[pltpu.VMEM((SC_INDEX_BLOCK,), jnp.int32)] + slot_types * 2,
        name="sc_gather_rows")
    def gather(x_hbm, i_hbm, o_hbm, idx_v, *scratch):
        wid = lax.axis_index("subcore") * SC_CORES + lax.axis_index("core")
        base = wid * per_w
        slots = (scratch[:3], scratch[3:])

        @pl.loop(0, per_w // SC_INDEX_BLOCK)
        def _(g):
            off = base + g * SC_INDEX_BLOCK
            pltpu.sync_copy(i_hbm.at[pl.ds(off, SC_INDEX_BLOCK)], idx_v)

            def start_gather(s):
                rows_v, sem_in, _ = slots[s % 2]
                return pltpu.async_copy(x_hbm.at[idx_v.at[pl.ds(s * w, w)]], rows_v, sem_in)

            gathers = {0: start_gather(0)}
            writes = {}
            for s in range(n_sub):
                if s + 1 < n_sub:
                    if s >= 1:
                        writes[s - 1].wait()
                    gathers[s + 1] = start_gather(s + 1)
                gathers[s].wait()
                rows_v, _, sem_out = slots[s % 2]
                writes[s] = pltpu.async_copy(rows_v, o_hbm.at[pl.ds(off + s * w, w)], sem_out)
            writes[n_sub - 2].wait()
            writes[n_sub - 1].wait()

    return gather(table, idx)


def _expert_ffn_kernel(te_ref, nv_ref, x_ref, wg_ref, wu_ref, wd_ref, y_out):
    n_valid = nv_ref[pl.program_id(0)]

    @pl.when(n_valid > 0)
    def _():
        row = lax.broadcasted_iota(jnp.int32, x_ref.shape, 0)
        h = _unpack_pairs(jnp.where(row < n_valid, x_ref[...], 0.0))
        f = None
        for cidx in range(D_FF // FF_CHUNK):
            cols = slice(cidx * FF_CHUNK, (cidx + 1) * FF_CHUNK)
            a = _dot(h, wg_ref[:, cols].astype(BF16))
            u = _dot(h, wu_ref[:, cols].astype(BF16))
            fc = _dot(((a * jax.nn.sigmoid(a)) * u).astype(BF16), wd_ref[cols, :].astype(BF16))
            f = fc if f is None else f + fc
        y_out[...] = f

    @pl.when(n_valid == 0)
    def _():
        y_out[...] = jnp.zeros_like(y_out)


def _expert_ffn(xs, tile_expert, tile_valid, wg, wu, wd):
    tm = MOE_ROW_TILE
    wspec = lambda shape: pl.BlockSpec((None,) + shape, lambda j, te, nu: (te[j], 0, 0),
                                       pipeline_mode=pl.Buffered(1))
    return pl.pallas_call(
        _expert_ffn_kernel,
        out_shape=jax.ShapeDtypeStruct((MOE_ROWS, D_MODEL), F32),
        grid_spec=pltpu.PrefetchScalarGridSpec(
            num_scalar_prefetch=2,
            grid=(MOE_ROWS // tm,),
            in_specs=[pl.BlockSpec((tm, xs.shape[1]), lambda j, te, nu: (j, 0)),
                      wspec((D_MODEL, D_FF)), wspec((D_MODEL, D_FF)), wspec((D_FF, D_MODEL))],
            out_specs=pl.BlockSpec((tm, D_MODEL), lambda j, te, nu: (j, 0)),
        ),
        compiler_params=_cparams(("arbitrary",)),
        name="moe_expert_ffn",
    )(tile_expert, tile_valid, xs, wg, wu, wd)


def _moe_combine_kernel(y1_ref, y2_ref, wsel_ref, x_ref, g_ref, gate_ref, *rest):
    x_out = rest[-1]
    w = wsel_ref[...]
    f = w[:, 0:1] * y1_ref[...] + w[:, 1:2] * y2_ref[...]
    x_out[...] = x_ref[...] + gate_ref[...] * _rms(f, g_ref[...])


def _moe_combine(yg, wsel, x, mod5, layer, g3, *, tok_off, out_rows, out_off, y_prev=None):
    t = TOK_TILE
    nt = MOE_SEG // t
    off = tok_off // t
    ooff = out_off // t
    tpg = NP_TOK // t
    tok = lambda w: pl.BlockSpec((t, w), lambda i: (off + i, 0))
    in_specs = [_tok_spec(t, D_MODEL), pl.BlockSpec((t, D_MODEL), lambda i: (nt + i, 0)), tok(LANES), tok(D_MODEL),
                _const_spec((1, D_MODEL)),
                pl.BlockSpec((None, None, None, 1, D_MODEL), lambda i: (layer, (off + i) // tpg, 5, 0, 0))]
    args = [yg, yg, wsel, x, g3, mod5]
    aliases = {}
    if y_prev is not None:
        in_specs.append(pl.BlockSpec(memory_space=pl.ANY))
        args.append(y_prev)
        aliases = {len(args) - 1: 0}
    return pl.pallas_call(
        _moe_combine_kernel,
        out_shape=jax.ShapeDtypeStruct((out_rows, D_MODEL), F32),
        grid=(nt,),
        in_specs=in_specs,
        out_specs=pl.BlockSpec((t, D_MODEL), lambda i: (ooff + i, 0)),
        input_output_aliases=aliases,
        compiler_params=_cparams(("parallel",)),
        name="moe_combine",
    )(*args)


def _moe(h, x, routing, wg, wu, wd, mod5, layer, g3):
    wsel, isel, rank, cnt = routing
    pos1, pos2, tile_expert, tile_valid = _route_tables(isel, rank, cnt)
    xs = _scatter_rows(h, pos1, pos2, MOE_ROWS)
    ys = _expert_ffn(xs, tile_expert, tile_valid, wg, wu, wd)
    y_p = y_s = None
    for seg in range(N_TOK // MOE_SEG):
        rows = slice(seg * MOE_SEG, (seg + 1) * MOE_SEG)
        yg = _gather_rows(ys, jnp.concatenate([pos1[rows], pos2[rows]]))
        common = dict(tok_off=seg * MOE_SEG)
        if seg * MOE_SEG < NP_TOK:
            y_p = _moe_combine(yg, wsel, x, mod5, layer, g3, out_rows=NP_TOK, out_off=seg * MOE_SEG, y_prev=y_p, **common)
        else:
            y_s = _moe_combine(yg, wsel, x, mod5, layer, g3, out_rows=NS_TOK, out_off=seg * MOE_SEG - NP_TOK,
                               y_prev=y_s, **common)
    return y_p, y_s


def _rot_cols(w, half):
    k, n = w.shape
    wb = w.reshape(k, n // (2 * half), 2, half)
    return jnp.stack([-wb[:, :, 1], wb[:, :, 0]], axis=2).reshape(k, n)


def _axis_tables(r, pos):
    inv = np.float32(ROPE_BASE) ** (-np.arange(0, r, 2, dtype=np.float32) / np.float32(r))
    ang = pos.astype(np.float32)[:, None] * inv[None, :]
    cos, sin = np.cos(ang), np.sin(ang)
    return np.concatenate([cos, cos], axis=1), np.concatenate([sin, sin], axis=1)


def _rope_tables(r):
    s = np.arange(DEC_SEQ)
    cr, sr = _axis_tables(r // 2, s // GRID_W)
    cc, sc = _axis_tables(r // 2, s % GRID_W)
    return np.concatenate([cr, cc], axis=1), np.concatenate([sr, sc], axis=1)


def _with_identity(tab, ident):
    return np.concatenate([np.full((TOK_TILE, tab.shape[1]), ident, np.float32), tab], axis=0)


@functools.lru_cache(maxsize=None)
def _rope_constants():
    c32, s32 = _rope_tables(MLA_ROPE)
    ones = np.ones((DEC_SEQ, MLA_NOPE), np.float32)
    pad1 = np.ones((DEC_SEQ, MLA_HEAD_PAD - MLA_NOPE - MLA_ROPE), np.float32)
    cq = np.concatenate([ones, c32, pad1], axis=1)
    sq = np.concatenate([0 * ones, s32, 0 * pad1], axis=1)
    c64, s64 = _rope_tables(GQA_HEAD_DIM)
    return {
        "mla_cq": _with_identity(cq, 1.0), "mla_sq": _with_identity(sq, 0.0),
        "mla_ck": _with_identity(c32, 1.0), "mla_sk": _with_identity(s32, 0.0),
        "gqa_c": _with_identity(np.concatenate([c64, c64], axis=1), 1.0),
        "gqa_s": _with_identity(np.concatenate([s64, s64], axis=1), 0.0),
    }


def _prep_tables():
    return {k: jnp.asarray(v, F32) for k, v in _rope_constants().items()}


def _prep_even(w_in, q_norm, w_q_up, kv_norm, w_kv_up, wgf, bgf, wgb, bgb):
    sizes = [MLA_Q_RANK, MLA_KV_RANK, MLA_ROPE, GLA_HEADS * GLA_DK, GLA_HEADS * GLA_DK,
             GLA_HEADS * GLA_DV, GLA_HEADS * GLA_DV, GLA_GATE_RANK, GLA_GATE_RANK]
    cq, ckv, kpe, gq, gk, gv, gr, gaf, gab = jnp.split(w_in, [int(s) for s in np.cumsum(sizes)[:-1]], axis=1)
    pad = jnp.zeros((D_MODEL, LANES - 2 * MLA_ROPE - 2 * GLA_GATE_RANK), F32)
    win = jnp.concatenate([cq, ckv, gq, gk, gv, gr, kpe, _rot_cols(kpe, MLA_ROPE // 4), gaf, gab, pad], axis=1)

    wq = w_q_up.reshape(MLA_Q_RANK, MLA_HEADS, MLA_NOPE + MLA_ROPE)
    nope, pe = wq[..., :MLA_NOPE], wq[..., MLA_NOPE:]
    pe_rot = _rot_cols(pe.reshape(MLA_Q_RANK, MLA_HEADS * MLA_ROPE), MLA_ROPE // 4).reshape(pe.shape)
    zpad = jnp.zeros((MLA_Q_RANK, MLA_HEADS, MLA_HEAD_PAD - MLA_NOPE - MLA_ROPE), F32)
    wq_main = jnp.concatenate([nope, pe, zpad], axis=-1).reshape(MLA_Q_RANK, MLA_QK_W)
    wq_rot = jnp.concatenate([0 * nope, pe_rot, zpad], axis=-1).reshape(MLA_Q_RANK, MLA_QK_W)

    wkv = w_kv_up.reshape(MLA_KV_RANK, MLA_HEADS, MLA_NOPE + MLA_V)
    knope, vv = wkv[..., :MLA_NOPE], wkv[..., MLA_NOPE:]
    wkk = jnp.concatenate([knope, jnp.zeros((MLA_KV_RANK, MLA_HEADS, MLA_HEAD_PAD - MLA_NOPE), F32)],
                          axis=-1).reshape(MLA_KV_RANK, MLA_QK_W)
    vpair = vv.reshape(MLA_KV_RANK, MLA_HEADS // 2, 2 * MLA_V)
    wkv_ext = jnp.concatenate([vpair, jnp.zeros((MLA_KV_RANK, MLA_HEADS // 2, LANES), F32)],
                              axis=-1).reshape(MLA_KV_RANK, MLA_VEXT_W)
    vbias = jnp.tile(jnp.concatenate([jnp.zeros((LANES,), F32), jnp.ones((LANES,), F32)]),
                     MLA_HEADS // 2).reshape(1, MLA_VEXT_W)
    epl = jnp.tile(jnp.concatenate([jnp.zeros((MLA_ROPE, MLA_NOPE), F32), jnp.eye(MLA_ROPE, dtype=F32),
                                    jnp.zeros((MLA_ROPE, MLA_HEAD_PAD - MLA_NOPE - MLA_ROPE), F32)], axis=1),
                   (1, MLA_HEADS))

    def gate_w(w, off):
        return jnp.zeros((LANES, GLA_HEADS * GLA_DK), F32).at[off:off + GLA_GATE_RANK].set(w)

    r = np.arange(CUMSUM_BLOCK)
    same = (r[:, None] // GLA_CHUNK) == (r[None, :] // GLA_CHUNK)
    lmat = jnp.asarray(same & (r[:, None] >= r[None, :]), BF16)
    umat = jnp.asarray(same & (r[:, None] <= r[None, :]), BF16)
    return {
        "win": win.astype(BF16), "qn": q_norm.reshape(1, -1), "wq": jnp.concatenate([wq_main, wq_rot], axis=1).astype(BF16),
        "kvn": kv_norm.reshape(1, -1), "wkk": wkk.astype(BF16), "wkv": wkv_ext.astype(BF16), "vbias": vbias,
        "epl": epl.astype(BF16), "wgf": gate_w(wgf, _S_GAF).astype(BF16), "bgf": bgf.reshape(1, -1),
        "wgb": gate_w(wgb, _S_GAB).astype(BF16), "bgb": bgb.reshape(1, -1), "lmat": lmat, "umat": umat,
    }


def _gqa_head_perm():
    heads = []
    for p in range(GQA_KV_HEADS // 2):
        for i in range(GQA_GROUP):
            heads += [(2 * p) * GQA_GROUP + i, (2 * p + 1) * GQA_GROUP + i]
    return np.asarray(heads)


def _prep_odd(w_in, w_out):
    perm = _gqa_head_perm()
    wq = w_in[:, :GQA_Q_W].reshape(D_MODEL, GQA_HEADS, GQA_HEAD_DIM)[:, perm].reshape(D_MODEL, GQA_Q_W)
    wk = w_in[:, GQA_Q_W:GQA_Q_W + GQA_KV_W]
    wv = w_in[:, GQA_Q_W + GQA_KV_W:].reshape(D_MODEL, GQA_KV_HEADS // 2, 2 * GQA_HEAD_DIM)
    wv_ext = jnp.concatenate([wv, jnp.zeros((D_MODEL, GQA_KV_HEADS // 2, LANES), F32)], axis=-1).reshape(D_MODEL, GQA_VEXT_W)
    win = jnp.concatenate([wq, _rot_cols(wq, GQA_HEAD_DIM // 4), wk, _rot_cols(wk, GQA_HEAD_DIM // 4), wv_ext], axis=1)
    vbias = jnp.tile(jnp.concatenate([jnp.zeros((LANES,), F32), jnp.ones((LANES,), F32)]),
                     GQA_KV_HEADS // 2).reshape(1, GQA_VEXT_W)
    wo = w_out.reshape(GQA_HEADS, GQA_HEAD_DIM, D_MODEL)[perm].reshape(GQA_Q_W, D_MODEL)
    return win.astype(BF16), vbias, wo.astype(BF16)


def _ext_v(v):
    rows = v.shape[0]
    vp = v.reshape(rows, GQA_KV_HEADS // 2, 2 * GQA_HEAD_DIM)
    return jnp.concatenate([vp, jnp.ones((rows, GQA_KV_HEADS // 2, LANES), v.dtype)], axis=-1).reshape(rows, GQA_VEXT_W)


def kernel(x_prompt, x_sample, cache_mla_ckv, cache_mla_kpe, state_gla_fwd, state_gla_bwd, cache_gqa_k, cache_gqa_v, c, c_ctx, w_mod, b_mod, norm_g, w_in_ab, mla_q_norm, mla_w_q_up, mla_kv_norm, mla_w_kv_up, gla_w_gate_f, gla_b_gate_f, gla_w_gate_b, gla_b_gate_b, gla_norm, w_out_ab, ffn_w_gate, ffn_w_up, ffn_w_down, w_in_c, gqa_sink, w_out_c, moe_w_router, moe_b_router, moe_w_gate, moe_w_up, moe_w_down):
    x_in = (x_prompt.reshape(NP_TOK, D_MODEL), x_sample.reshape(NS_TOK, D_MODEL))
    cvec =jnp.concatenate([c_ctx[None, :], c, jnp.zeros((MOD_ROWS - N_GROUPS, D_MODEL), F32)], axis=0)
    mod5 = _modulation(cvec, w_mod, b_mod).reshape(DEPTH, MOD_ROWS, N_MOD, 1, D_MODEL)
    tabs = _prep_tables()
    gvec = lambda l, j: norm_g[l, j].reshape(1, D_MODEL)

    wts = _prep_even(w_in_ab[0], mla_q_norm[0], mla_w_q_up[0], mla_kv_norm[0], mla_w_kv_up[0],
                     gla_w_gate_f[0], gla_b_gate_f[0], gla_w_gate_b[0], gla_b_gate_b[0])
    (q, k, v, ckv, kpe, gq, gk, gv, gr, bf, bb) = _even_in_proj(*x_in, mod5, 0, gvec(0, 0), wts, tabs)
    kc, vc = _cache_kv(cache_mla_ckv[:, 0].reshape(DEC_BATCH * PAST_LEN, MLA_KV_RANK),
                       cache_mla_kpe[:, 0].reshape(DEC_BATCH * PAST_LEN, MLA_ROPE), wts)
    oa_p = _mla_attention(q, [k], [v], n_batch=BATCH, seq_q=SEQ, q_tile=SEQ, tok_off=0, k_batch_rows=[SEQ],
                          pairs=MLA_HEADS // 2)
    oa_s = _mla_attention(q, [k, kc], [v, vc], n_batch=DEC_BATCH, seq_q=DEC_SEQ, q_tile=MLA_Q_TILE,
                          tok_off=NP_TOK, k_batch_rows=[DEC_SEQ, PAST_LEN], pairs=1)
    gn = gla_norm[0].reshape(1, GLA_DV)
    zero_state = jnp.zeros((BATCH, GLA_HEADS, GLA_DK, GLA_DV), F32)
    ob_p, sf, sb = _gla(gq, gk, gv, gr, bf, bb, zero_state, zero_state, gn, n_batch=BATCH, seq=SEQ, tok_off=0)
    ob_s, _, _ = _gla(gq, gk, gv, gr, bf, bb, state_gla_fwd[:, 0], state_gla_bwd[:, 0], gn,
                      n_batch=DEC_BATCH, seq=DEC_SEQ, tok_off=NP_TOK)
    x, h = _out_proj([(oa_p, oa_s), (ob_p, ob_s)], w_out_ab[0].astype(BF16), x_in, mod5, 0, gvec(0, 1),
                     gvec(0, 2), BF16)
    x = _ffn(h, x, ffn_w_gate[0], ffn_w_up[0], ffn_w_down[0], mod5, 0, gvec(0, 3))

    win_c, vbias_c, wo_c = _prep_odd(w_in_c[0], w_out_c[0])
    qg, kgb, vgb, kg_t, vg_t = _odd_in_proj(x, mod5, 1, gvec(1, 0), win_c, vbias_c, tabs["gqa_c"], tabs["gqa_s"])
    sink = gqa_sink[0]
    og_p = _gqa_attention(sink, qg, None, None, kgb, vgb, n_batch=BATCH, seq_q=SEQ, q_tile=SEQ, tok_off=0, n_ctx=SEQ, local=False)
    kc_g = cache_gqa_k[:, 0].reshape(DEC_BATCH * PAST_LEN, GQA_KV_W).astype(BF16)
    vc_g = _ext_v(cache_gqa_v[:, 0].reshape(DEC_BATCH * PAST_LEN, GQA_KV_W)).astype(BF16)
    og_s = _gqa_attention(sink, qg, kgb, vgb, kc_g, vc_g, n_batch=DEC_BATCH, seq_q=DEC_SEQ, q_tile=GQA_Q_TILE,
                          tok_off=NP_TOK, n_ctx=PAST_LEN, local=True)
    w_r = jnp.zeros((D_MODEL, LANES), F32).at[:, :N_EXPERTS].set(moe_w_router[0]).astype(BF16)
    b_r = jnp.zeros((1, LANES), F32).at[0, :N_EXPERTS].set(moe_b_router[0])
    x, h, *routing = _out_proj([(og_p, og_s)], wo_c, x, mod5, 1, gvec(1, 1), gvec(1, 2), F32, router=(w_r, b_r))
    y_p, y_s = _moe(h, x, routing, moe_w_gate[0], moe_w_up[0], moe_w_down[0], mod5, 1, gvec(1, 3))

    y_prompt = y_p.reshape(BATCH, SEQ, D_MODEL)
    y_sample = y_s.reshape(DEC_BATCH, DEC_SEQ, D_MODEL)
    new_ckv = ckv[:NP_TOK].reshape(BATCH, 1, SEQ, MLA_KV_RANK)
    new_kpe = kpe[:NP_TOK].reshape(BATCH, 1, SEQ, MLA_ROPE)
    as_cache = lambda a: jnp.transpose(a.reshape(BATCH, 1, GQA_KV_HEADS, GQA_HEAD_DIM, SEQ), (0, 1, 4, 2, 3))
    new_k = as_cache(kg_t)
    new_v = as_cache(vg_t)
    return (y_prompt, y_sample, new_ckv, new_kpe, sf[:, None], sb[:, None], new_k, new_v)
```

```python
import functools

import jax
import jax.numpy as jnp
import numpy as np
from jax import lax
from jax.experimental import pallas as pl
from jax.experimental.pallas import tpu as pltpu
from jax.experimental.pallas import tpu_sc as plsc

F32 = jnp.float32
BF16 = jnp.bfloat16

D_MODEL = 1024
BATCH = 16
SEQ = 256
DEPTH = 2
DEC_BATCH = 4
DEC_SEQ = 4096
PAST_LEN = 256
GRID_W = 64
N_MOD = 6
EPS = 1e-6
ROPE_BASE = 10000.0
NEG_INF = -1e30

MLA_HEADS = 8
MLA_NOPE = 64
MLA_ROPE = 32
MLA_V = 64
MLA_Q_RANK = 384
MLA_KV_RANK = 256
GLA_HEADS = 4
GLA_DK = 64
GLA_DV = 128
GLA_GATE_RANK = 16
GLA_GATE_NORM = 16.0
GLA_CHUNK = 64
GQA_HEADS = 16
GQA_KV_HEADS = 4
GQA_GROUP = GQA_HEADS // GQA_KV_HEADS
GQA_HEAD_DIM = 64
WINDOW = 128
D_FF = 2816
N_EXPERTS = 8
TOP_K = 2

NP_TOK = BATCH * SEQ
NS_TOK = DEC_BATCH * DEC_SEQ
N_TOK = NP_TOK + NS_TOK
N_GROUPS = 1 + DEC_BATCH
MOD_ROWS = 8

LANES = 128
MXU_COLS = 256
VMEM_LIMIT_BYTES = 56 * 1024 * 1024

TOK_TILE = 512
CUMSUM_BLOCK = 256
GLA_BLOCK_CHUNKS = MXU_COLS // GLA_CHUNK
GLA_CHUNK_UNROLL = 4
MLA_Q_TILE = 2048
MLA_Q_SUB = 256
GQA_Q_TILE = 256
MOE_ROW_TILE = 512
MOE_ROWS = TOP_K * N_TOK + N_EXPERTS * MOE_ROW_TILE
MOE_SEG = 4096
SC_CORES = 2
SC_SUBCORES = 16
SC_WORKERS = SC_CORES * SC_SUBCORES
SC_INDEX_BLOCK = 128
SC_GATHER_WINDOW = 32
FF_CHUNK = 1408

_C_CQ = 0
_C_CKV = _C_CQ + MLA_Q_RANK
_C_GQ = _C_CKV + MLA_KV_RANK
_C_GK = _C_GQ + GLA_HEADS * GLA_DK
_C_GV = _C_GK + GLA_HEADS * GLA_DK
_C_GR = _C_GV + GLA_HEADS * GLA_DV
_C_SMALL = _C_GR + GLA_HEADS * GLA_DV
IN_AB_EXT = _C_SMALL + LANES
_S_KPE, _S_KPER, _S_GAF, _S_GAB = 0, MLA_ROPE, 2 * MLA_ROPE, 2 * MLA_ROPE + GLA_GATE_RANK
MLA_HEAD_PAD = LANES
MLA_QK_W = MLA_HEADS * MLA_HEAD_PAD
MLA_VEXT_W = (MLA_HEADS // 2) * 2 * LANES


def _cparams(semantics):
    return pltpu.CompilerParams(dimension_semantics=semantics, vmem_limit_bytes=VMEM_LIMIT_BYTES)


def _const_spec(shape):
    nd = len(shape)
    return pl.BlockSpec(shape, lambda *_: (0,) * nd, pipeline_mode=pl.Buffered(1))


def _log_sigmoid(x):
    return jnp.minimum(x, 0.0) - jnp.log1p(jnp.exp(-jnp.abs(x)))


def _rms(x, g):
    return (x * lax.rsqrt(jnp.mean(x * x, axis=-1, keepdims=True) + EPS)) * g


def _modulate(x, g, shift, scale):
    return _rms(x, g) * (1.0 + scale) + shift


def _dot(a, b):
    return jnp.dot(a, b, preferred_element_type=F32)


def _dot_nt(a, b):
    return lax.dot_general(a, b, (((1,), (1,)), ((), ())), preferred_element_type=F32)


def _dot_tn(a, b):
    return lax.dot_general(a, b, (((0,), (0,)), ((), ())), preferred_element_type=F32)


def _split3(x):
    hi = x.astype(BF16)
    r1 = x - hi.astype(F32)
    mid = r1.astype(BF16)
    lo = (r1 - mid.astype(F32)).astype(BF16)
    return hi, mid, lo


def _lane_tile(x, reps):
    return jnp.concatenate([x] * reps, axis=1)


_HI16 = np.uint32(0xFFFF0000)


def _pack_pairs(xb):
    w = xb.shape[1] // 2
    xf = xb.astype(F32)
    lo = lax.bitcast_convert_type(xf[:, :w], jnp.uint32) >> 16
    hi = lax.bitcast_convert_type(xf[:, w:], jnp.uint32) & _HI16
    return lax.bitcast_convert_type(lo | hi, F32)


def _unpack_pairs(words):
    u = lax.bitcast_convert_type(words, jnp.uint32)
    lo = lax.bitcast_convert_type(u << 16, F32)
    hi = lax.bitcast_convert_type(u & _HI16, F32)
    return jnp.concatenate([lo, hi], axis=1).astype(BF16)


def _mod_kernel(c_ref, w_ref, b_ref, o_ref):
    c = c_ref[...]
    s = c * jax.nn.sigmoid(c)
    o_ref[...] = _dot(s.astype(BF16), w_ref[...].astype(BF16)) + b_ref[...]


def _modulation(cvec, w_mod, b_mod):
    ncol = N_MOD * D_MODEL
    blk = 1536
    return pl.pallas_call(
        _mod_kernel,
        out_shape=jax.ShapeDtypeStruct((DEPTH, MOD_ROWS, ncol), F32),
        grid=(DEPTH, ncol // blk),
        in_specs=[
            pl.BlockSpec((MOD_ROWS, D_MODEL), lambda l, j: (0, 0)),
            pl.BlockSpec((None, D_MODEL, blk), lambda l, j: (l, 0, j)),
            pl.BlockSpec((None, 1, blk), lambda l, j: (l, 0, j)),
        ],
        out_specs=pl.BlockSpec((None, MOD_ROWS, blk), lambda l, j: (l, 0, j)),
        compiler_params=_cparams(("arbitrary", "arbitrary")),
        name="modulation",
    )(cvec, w_mod, b_mod.reshape(DEPTH, 1, ncol))


def _same_tile(i):
    return i


def _mod_spec(layer, j, tile, tile_of=_same_tile):
    tpg = NP_TOK // tile
    return pl.BlockSpec((None, None, None, 1, D_MODEL), lambda i: (layer, tile_of(i) // tpg, j, 0, 0))


def _tok_spec(tile, width, tile_of=_same_tile):
    return pl.BlockSpec((tile, width), lambda i: (tile_of(i), 0))


def _split_specs(tile, width, tile_of=_same_tile):
    npt = NP_TOK // tile
    return [pl.BlockSpec((tile, width), lambda i: (jnp.minimum(tile_of(i), npt - 1), 0)),
            pl.BlockSpec((tile, width), lambda i: (jnp.maximum(tile_of(i) - npt, 0), 0))]


def _pick(tile, p_ref, s_ref):
    return jnp.where(pl.program_id(0) < NP_TOK // tile, p_ref[...], s_ref[...])


def _rope_row_spec(tile, width):
    npt = NP_TOK // tile
    spt = DEC_SEQ // tile
    return pl.BlockSpec((tile, width), lambda i: (jnp.where(i < npt, 0, 1 + (i - npt) % spt), 0))


def _even_in_kernel(xp_ref, xs_ref, g_ref, shift_ref, scale_ref, win_ref, qn_ref, wq_ref, kvn_ref, wkk_ref,
                    wkv_ref, vbias_ref, epl_ref, wgf_ref, bgf_ref, wgb_ref, bgb_ref, lmat_ref,
                    umat_ref, cq_ref, sq_ref, ck_ref, sk_ref,
                    q_out, k_out, v_out, ckv_out, kpe_out, gq_out, gk_out, gv_out, gr_out,
                    bf_out, bb_out):
    h = _modulate(_pick(TOK_TILE, xp_ref, xs_ref), g_ref[...], shift_ref[...], scale_ref[...])
    z = _dot(h.astype(BF16), win_ref[...])

    cqn = _rms(z[:, _C_CQ:_C_CQ + MLA_Q_RANK], qn_ref[...]).astype(BF16)
    qf = _dot(cqn, wq_ref[...])
    cq_t = _lane_tile(cq_ref[...], MLA_HEADS)
    sq_t = _lane_tile(sq_ref[...], MLA_HEADS)
    q_out[...] = (qf[:, :MLA_QK_W] * cq_t + qf[:, MLA_QK_W:] * sq_t).astype(BF16)

    ckvn = _rms(z[:, _C_CKV:_C_CKV + MLA_KV_RANK], kvn_ref[...])
    ckv_out[...] = ckvn
    small = z[:, _C_SMALL:_C_SMALL + LANES]
    kpe = (small[:, _S_KPE:_S_KPE + MLA_ROPE] * ck_ref[...]
           + small[:, _S_KPER:_S_KPER + MLA_ROPE] * sk_ref[...])
    kpe_out[...] = kpe
    ckvn_b = ckvn.astype(BF16)
    k_out[...] = (_dot(ckvn_b, wkk_ref[...]) + _dot(kpe.astype(BF16), epl_ref[...])).astype(BF16)
    v_out[...] = (_dot(ckvn_b, wkv_ref[...]) + vbias_ref[...]).astype(BF16)

    gq_out[...] = z[:, _C_GQ:_C_GQ + GLA_HEADS * GLA_DK] * (GLA_DK ** -0.5)
    gk_out[...] = z[:, _C_GK:_C_GK + GLA_HEADS * GLA_DK]
    gv_out[...] = z[:, _C_GV:_C_GV + GLA_HEADS * GLA_DV].astype(BF16)
    gr_out[...] = z[:, _C_GR:_C_GR + GLA_HEADS * GLA_DV]

    small_b = small.astype(BF16)
    la_f = _log_sigmoid(_dot(small_b, wgf_ref[...]) + bgf_ref[...]) * (1.0 / GLA_GATE_NORM)
    la_b = _log_sigmoid(_dot(small_b, wgb_ref[...]) + bgb_ref[...]) * (1.0 / GLA_GATE_NORM)
    lmat = lmat_ref[...]
    umat = umat_ref[...]
    for r in range(TOK_TILE // CUMSUM_BLOCK):
        rows = slice(r * CUMSUM_BLOCK, (r + 1) * CUMSUM_BLOCK)
        f_hi, f_mid, f_lo = _split3(la_f[rows])
        bf_out[rows, :] = _dot(lmat, f_hi) + _dot(lmat, f_mid) + _dot(lmat, f_lo)
        b_hi, b_mid, b_lo = _split3(la_b[rows])
        bb_out[rows, :] = _dot(umat, b_hi) + _dot(umat, b_mid) + _dot(umat, b_lo)


def _even_in_proj(xp, xs, mod5, layer, g, wts, tabs):
    t = TOK_TILE
    out_widths = [(MLA_QK_W, BF16), (MLA_QK_W, BF16), (MLA_VEXT_W, BF16), (MLA_KV_RANK, F32),
                  (MLA_ROPE, F32), (GLA_HEADS * GLA_DK, F32), (GLA_HEADS * GLA_DK, F32),
                  (GLA_HEADS * GLA_DV, BF16), (GLA_HEADS * GLA_DV, F32),
                  (GLA_HEADS * GLA_DK, F32), (GLA_HEADS * GLA_DK, F32)]
    const_names = ["win", "qn", "wq", "kvn", "wkk", "wkv", "vbias", "epl", "wgf", "bgf", "wgb",
                   "bgb", "lmat", "umat"]
    consts = [wts[n] for n in const_names]
    in_specs = (_split_specs(t, D_MODEL)
                + [_const_spec((1, D_MODEL)), _mod_spec(layer, 0, t), _mod_spec(layer, 1, t)]
                + [_const_spec(c.shape) for c in consts]
                + [_rope_row_spec(t, LANES), _rope_row_spec(t, LANES),
                   _rope_row_spec(t, MLA_ROPE), _rope_row_spec(t, MLA_ROPE)])
    return pl.pallas_call(
        _even_in_kernel,
        out_shape=[jax.ShapeDtypeStruct((N_TOK, w), dt) for w, dt in out_widths],
        grid=(N_TOK // t,),
        in_specs=in_specs,
        out_specs=[_tok_spec(t, w) for w, _ in out_widths],
        compiler_params=_cparams(("parallel",)),
        name="even_in_proj",
    )(xp, xs, g, mod5, mod5, *consts, tabs["mla_cq"], tabs["mla_sq"], tabs["mla_ck"], tabs["mla_sk"])


def _cache_kv_kernel(ckv_ref, kpe_ref, wkk_ref, wkv_ref, vbias_ref, epl_ref, k_out, v_out):
    ckv_b = ckv_ref[...].astype(BF16)
    k_out[...] = (_dot(ckv_b, wkk_ref[...]) + _dot(kpe_ref[...].astype(BF16), epl_ref[...])).astype(BF16)
    v_out[...] = (_dot(ckv_b, wkv_ref[...]) + vbias_ref[...]).astype(BF16)


def _cache_kv(ckv, kpe, wts):
    n = ckv.shape[0]
    consts = [wts[k] for k in ("wkk", "wkv", "vbias", "epl")]
    return pl.pallas_call(
        _cache_kv_kernel,
        out_shape=[jax.ShapeDtypeStruct((n, MLA_QK_W), BF16), jax.ShapeDtypeStruct((n, MLA_VEXT_W), BF16)],
        grid=(1,),
        in_specs=[_const_spec(ckv.shape), _const_spec(kpe.shape)] + [_const_spec(c.shape) for c in consts],
        out_specs=[_const_spec((n, MLA_QK_W)), _const_spec((n, MLA_VEXT_W))],
        compiler_params=_cparams(("arbitrary",)),
        name="mla_cache_kv",
    )(ckv, kpe, *consts)


def _mla_attn_kernel(*refs, n_seg):
    q_ref = refs[0]
    k_refs = refs[1:1 + n_seg]
    v_refs = refs[1 + n_seg:1 + 2 * n_seg]
    o_ref = refs[1 + 2 * n_seg]
    scale = (MLA_NOPE + MLA_ROPE) ** -0.5
    c = scale * float(np.log2(np.e))
    tq = q_ref.shape[0]
    q_sub = min(tq, MLA_Q_SUB)
    kt = MXU_COLS
    tiles = [(si, r0) for si, k in enumerate(k_refs) for r0 in range(0, k.shape[0], kt)]
    lane = lax.broadcasted_iota(jnp.int32, (q_sub, LANES), 1)
    n_pairs = o_ref.shape[1] // LANES
    for pi, qs in [(pi, qs) for pi in range(n_pairs) for qs in range(tq // q_sub)]:
        rows = slice(qs * q_sub, (qs + 1) * q_sub)
        vl = slice(pi * 2 * LANES, (pi + 1) * 2 * LANES)
        res = []
        for j in range(2):
            hl = slice((2 * pi + j) * LANES, (2 * pi + j + 1) * LANES)
            qj = q_ref[rows, hl]
            macc = None
            s_tiles = []
            for si, r0 in tiles:
                s = _dot_nt(qj, k_refs[si][r0:r0 + kt, hl])
                s_tiles.append(s)
                mt = jnp.maximum(s[:, :LANES], s[:, LANES:])
                macc = mt if macc is None else jnp.maximum(macc, mt)
            m = macc.max(axis=-1, keepdims=True)
            r = None
            for (si, r0), s in zip(tiles, s_tiles):
                p = jnp.exp2((s - m) * c).astype(BF16)
                rj = _dot(p, v_refs[si][r0:r0 + kt, vl])
                r = rj if r is None else r + rj
            res.append(r[:, :LANES] / r[:, LANES:])
        o_ref[rows, pi * LANES:(pi + 1) * LANES] = jnp.where(lane < MLA_V, res[0], res[1]).astype(BF16)


def _mla_attention(q, ks, vs, *, n_batch, seq_q, q_tile, tok_off, k_batch_rows, pairs):
    n_seg = len(ks)
    nq = seq_q // q_tile
    qoff = tok_off // q_tile
    grid = (n_batch, MLA_HEADS // 2 // pairs, nq)
    wq = pairs * 2 * LANES
    in_specs = [pl.BlockSpec((q_tile, wq), lambda b, hp, i: (qoff + b * nq + i, hp))]
    for s in range(n_seg):
        rows = k_batch_rows[s]
        off = (tok_off // rows) if s == 0 else 0
        in_specs.append(pl.BlockSpec((rows, wq), functools.partial(lambda b, hp, i, off: (off + b, hp), off=off)))
    for s in range(n_seg):
        rows = k_batch_rows[s]
        off = (tok_off // rows) if s == 0 else 0
        in_specs.append(pl.BlockSpec((rows, wq), functools.partial(lambda b, hp, i, off: (off + b, hp), off=off)))
    return pl.pallas_call(
        functools.partial(_mla_attn_kernel, n_seg=n_seg),
        out_shape=jax.ShapeDtypeStruct((n_batch * seq_q, MLA_HEADS * MLA_V), BF16),
        grid=grid,
        in_specs=in_specs,
        out_specs=pl.BlockSpec((q_tile, pairs * LANES), lambda b, hp, i: (b * nq + i, hp)),
        compiler_params=_cparams(("parallel", "parallel", "arbitrary")),
        name=f"mla_attention_{n_seg}seg",
    )(q, *ks, *vs)


def _gla_kernel(q_ref, k_ref, v_ref, gr_ref, bf_ref, bb_ref, s0f_ref, s0b_ref, gn_ref,
                o_ref, sf_ref, sb_ref, acc_ref, kdf_ref, kdb_ref, qdf_ref, qdb_ref, hist_ref, *, n_chunks):
    c = GLA_CHUNK
    cpb = min(n_chunks, GLA_BLOCK_CHUNKS)
    blk = cpb * c
    lane = lax.broadcasted_iota(jnp.int32, (blk, LANES), 1)
    lo = lane < GLA_DK
    row = lax.broadcasted_iota(jnp.int32, (blk, blk), 0)
    col = lax.broadcasted_iota(jnp.int32, (blk, blk), 1)
    chunk_bits = c.bit_length() - 1
    same_chunk = jnp.right_shift(row, chunk_bits) == jnp.right_shift(col, chunk_bits)
    tril = same_chunk & (row >= col)
    triu = same_chunk & (row <= col)
    zero_blk = jnp.zeros((GLA_DK, GLA_DV), F32)

    def pair_state_t(s_ref):
        blockdiag = jnp.concatenate(
            [jnp.concatenate([s_ref[0], zero_blk], axis=1),
             jnp.concatenate([zero_blk, s_ref[1]], axis=1)], axis=0)
        return blockdiag.T

    dirs = ((bf_ref, c // 2 - 1, c - 1, tril, kdf_ref, qdf_ref),
            (bb_ref, c // 2, 0, triu, kdb_ref, qdb_ref))
    sels = (lo, jnp.logical_not(lo))
    lo_c = lax.broadcasted_iota(jnp.int32, (c, LANES), 1) < GLA_DK
    sels_c = (lo_c, jnp.logical_not(lo_c))
    hcols = (slice(0, GLA_DV), slice(GLA_DV, 2 * GLA_DV))

    def block(r, carry):
        rows = pl.ds(pl.multiple_of(r * blk, blk), blk)
        q = q_ref[rows, :]
        k = k_ref[rows, :]
        v = v_ref[rows, :]

        def chunk_row(b, r):
            return jnp.concatenate([jnp.broadcast_to(b[ch * c + r:ch * c + r + 1, :], (c, LANES))
                                    for ch in range(cpb)], axis=0)

        for d, (b_ref, mid_row, last_row, causal, kd_ref, qd_ref) in enumerate(dirs):
            b = b_ref[rows, :]
            b_mid = chunk_row(b, mid_row)
            b_last = chunk_row(b, last_row)
            qe = q * jnp.exp(b - b_mid)
            ke = (k * jnp.exp(b_mid - b)).astype(BF16)
            kd_ref[rows, :] = (k * jnp.exp(b_last - b)).astype(BF16)
            qd_ref[rows, :] = (q * jnp.exp(b)).astype(BF16)
            for j in range(2):
                a = _dot_nt(jnp.where(sels[j], qe, 0.0).astype(BF16), ke)
                o = _dot(jnp.where(causal, a, 0.0).astype(BF16), v[:, hcols[j]])
                if d == 0:
                    acc_ref[rows, hcols[j]] = o
                else:
                    acc_ref[rows, hcols[j]] += o
        return carry

    lax.fori_loop(0, n_chunks // cpb, block, 0, unroll=min(2, n_chunks // cpb))

    def scan(i, carry):
        new = []
        for d, (b_ref, _, last_row, _, kd_ref, _) in enumerate(dirs):
            ci = i if d == 0 else n_chunks - 1 - i
            rows = pl.ds(pl.multiple_of(ci * c, c), c)
            hist_ref[d, ci] = carry[d].astype(BF16)
            grp = b_ref[pl.ds(pl.multiple_of(ci * c + (last_row // 8) * 8, 8), 8), :]
            b_last = grp[last_row % 8:last_row % 8 + 1, :]
            new.append(carry[d] * jnp.exp(b_last) + _dot_tn(v_ref[rows, :], kd_ref[rows, :]))
        return tuple(new)

    st_f, st_b = lax.fori_loop(0, n_chunks, scan, (pair_state_t(s0f_ref), pair_state_t(s0b_ref)),
                               unroll=GLA_CHUNK_UNROLL)

    def inter(ci, carry):
        rows = pl.ds(pl.multiple_of(ci * c, c), c)
        for d, (_, _, _, _, _, qd_ref) in enumerate(dirs):
            qd = qd_ref[rows, :]
            st = hist_ref[d, ci]
            for j in range(2):
                acc_ref[rows, hcols[j]] += _dot_nt(jnp.where(sels_c[j], qd, jnp.zeros_like(qd)), st[hcols[j], :])
        return carry

    lax.fori_loop(0, n_chunks, inter, 0, unroll=GLA_CHUNK_UNROLL)
    s_f = st_f.T
    s_b = st_b.T
    sf_ref[0] = s_f[:GLA_DK, :GLA_DV]
    sf_ref[1] = s_f[GLA_DK:, GLA_DV:]
    sb_ref[0] = s_b[:GLA_DK, :GLA_DV]
    sb_ref[1] = s_b[GLA_DK:, GLA_DV:]

    gn = gn_ref[...]
    for j in range(2):
        cols = slice(j * GLA_DV, (j + 1) * GLA_DV)
        gr = gr_ref[:, cols]
        o_ref[:, cols] = (_rms(acc_ref[:, cols], gn) * (gr * jax.nn.sigmoid(gr))).astype(BF16)


def _gla(gq, gk, gv, gr, bf, bb, s0f, s0b, gnorm, *, n_batch, seq, tok_off):
    n_chunks = seq // GLA_CHUNK
    boff = tok_off // seq
    hp = GLA_HEADS // 2
    tok = lambda w: pl.BlockSpec((seq, w), lambda b, p: (boff + b, p))
    st = pl.BlockSpec((None, 2, GLA_DK, GLA_DV), lambda b, p: (b, p, 0, 0))
    return pl.pallas_call(
        functools.partial(_gla_kernel, n_chunks=n_chunks),
        out_shape=[jax.ShapeDtypeStruct((n_batch * seq, GLA_HEADS * GLA_DV), BF16),
                   jax.ShapeDtypeStruct((n_batch, GLA_HEADS, GLA_DK, GLA_DV), F32),
                   jax.ShapeDtypeStruct((n_batch, GLA_HEADS, GLA_DK, GLA_DV), F32)],
        grid=(n_batch, hp),
        in_specs=[tok(2 * GLA_DK), tok(2 * GLA_DK), tok(2 * GLA_DV), tok(2 * GLA_DV),
                  tok(2 * GLA_DK), tok(2 * GLA_DK), st, st, _const_spec((1, GLA_DV))],
        out_specs=[pl.BlockSpec((seq, 2 * GLA_DV), lambda b, p: (b, p)), st, st],
        scratch_shapes=[pltpu.VMEM((seq, 2 * GLA_DV), F32)]
                       + [pltpu.VMEM((seq, 2 * GLA_DK), BF16)] * 4
                       + [pltpu.VMEM((2, n_chunks, 2 * GLA_DV, 2 * GLA_DK), BF16)],
        compiler_params=_cparams(("parallel", "parallel")),
        name=f"gla_seq{seq}",
    )(gq, gk, gv, gr, bf, bb, s0f, s0b, gnorm)


def _out_proj_kernel(*refs, n_o, x_split, route):
    t = TOK_TILE
    o = [_pick(t, refs[2 * j], refs[2 * j + 1]) for j in range(n_o)]
    rest = refs[2 * n_o:]
    w_ref = rest[0]
    if x_split:
        x_in = _pick(t, rest[1], rest[2])
        rest = rest[3:]
    else:
        x_in = rest[1][...]
        rest = rest[2:]
    g1_ref, gate_ref, g2_ref, shift_ref, scale_ref = rest[:5]
    rest = rest[5:]
    if route:
        router_in, rest = rest[:3], rest[3:]
    x_out, h_out = rest[:2]
    y = _dot(o[0] if n_o == 1 else jnp.concatenate(o, axis=1), w_ref[...])
    x = x_in + gate_ref[...] * _rms(y, g1_ref[...])
    x_out[...] = x
    h = _modulate(x, g2_ref[...], shift_ref[...], scale_ref[...])
    if route:
        hb = h.astype(BF16)
        h_out[...] = _pack_pairs(hb)
        _route(hb, *router_in, *rest[2:])
    else:
        h_out[...] = h.astype(h_out.dtype)


def _out_proj(os_, w, x, mod5, layer, g1, g2, h_dtype, router=None):
    t = TOK_TILE
    n_tiles = N_TOK // t
    x_split = isinstance(x, tuple)
    tile_of = _same_tile
    in_specs, args = [], []
    for o_p, o_s in os_:
        in_specs += _split_specs(t, o_p.shape[1], tile_of)
        args += [o_p, o_s]
    in_specs.append(_const_spec(w.shape))
    args.append(w)
    if x_split:
        in_specs += _split_specs(t, D_MODEL, tile_of)
        args += list(x)
    else:
        in_specs.append(_tok_spec(t, D_MODEL, tile_of))
        args.append(x)
    in_specs += [_const_spec((1, D_MODEL)), _mod_spec(layer, 2, t, tile_of), _const_spec((1, D_MODEL)),
                 _mod_spec(layer, 3, t, tile_of), _mod_spec(layer, 4, t, tile_of)]
    args += [g1, mod5, g2, mod5, mod5]
    h_width = D_MODEL // 2 if router is not None else D_MODEL
    out_shape = [jax.ShapeDtypeStruct((N_TOK, D_MODEL), F32), jax.ShapeDtypeStruct((N_TOK, h_width), h_dtype)]
    out_specs = [_tok_spec(t, D_MODEL, tile_of), _tok_spec(t, h_width, tile_of)]
    scratch = []
    if router is not None:
        r = np.arange(t)
        ltri = jnp.asarray(r[:, None] > r[None, :], BF16)
        args += [router[0], router[1], ltri]
        in_specs += [_const_spec(router[0].shape), _const_spec(router[1].shape), _const_spec((t, t))]
        out_shape += [jax.ShapeDtypeStruct((N_TOK, LANES), F32), jax.ShapeDtypeStruct((N_TOK, LANES), jnp.int32),
                      jax.ShapeDtypeStruct((N_TOK, LANES), jnp.int32), jax.ShapeDtypeStruct((8, LANES), jnp.int32)]
        out_specs += [_tok_spec(t, LANES), _tok_spec(t, LANES), _tok_spec(t, LANES),
                      pl.BlockSpec((8, LANES), lambda i: (0, 0))]
        scratch = [pltpu.VMEM((8, LANES), F32)]
    return pl.pallas_call(
        functools.partial(_out_proj_kernel, n_o=len(os_), x_split=x_split, route=router is not None),
        out_shape=out_shape,
        grid=(n_tiles,),
        in_specs=in_specs,
        out_specs=out_specs,
        scratch_shapes=scratch,
        compiler_params=_cparams(("arbitrary",) if router is not None else ("parallel",)),
        name=f"out_proj_{len(os_)}",
    )(*args)


def _ffn_kernel(h_ref, x_ref, wg_ref, wu_ref, wd_ref, g_ref, gate_ref, x_out):
    h = h_ref[...]
    f = None
    for cidx in range(D_FF // FF_CHUNK):
        cols = slice(cidx * FF_CHUNK, (cidx + 1) * FF_CHUNK)
        a = _dot(h, wg_ref[:, cols].astype(BF16))
        u = _dot(h, wu_ref[:, cols].astype(BF16))
        fc = _dot(((a * jax.nn.sigmoid(a)) * u).astype(BF16), wd_ref[cols, :].astype(BF16))
        f = fc if f is None else f + fc
    x_out[...] = x_ref[...] + gate_ref[...] * _rms(f, g_ref[...])


def _ffn(h, x, wg, wu, wd, mod5, layer, g3):
    t = TOK_TILE
    return pl.pallas_call(
        _ffn_kernel,
        out_shape=jax.ShapeDtypeStruct((N_TOK, D_MODEL), F32),
        grid=(N_TOK // t,),
        in_specs=[_tok_spec(t, D_MODEL), _tok_spec(t, D_MODEL), _const_spec(wg.shape),
                  _const_spec(wu.shape), _const_spec(wd.shape), _const_spec((1, D_MODEL)),
                  _mod_spec(layer, 5, t)],
        out_specs=_tok_spec(t, D_MODEL),
        compiler_params=_cparams(("parallel",)),
        name="ffn_swiglu",
    )(h, x, wg, wu, wd, g3, mod5)


GQA_Q_W = GQA_HEADS * GQA_HEAD_DIM
GQA_KV_W = GQA_KV_HEADS * GQA_HEAD_DIM
GQA_VEXT_W = (GQA_KV_HEADS // 2) * 2 * LANES
_O_Q, _O_QR = 0, GQA_Q_W
_O_K, _O_KR = 2 * GQA_Q_W, 2 * GQA_Q_W + GQA_KV_W
_O_V = 2 * GQA_Q_W + 2 * GQA_KV_W
IN_C_EXT = _O_V + GQA_VEXT_W


def _odd_in_kernel(x_ref, g_ref, shift_ref, scale_ref, win_ref, vbias_ref, c_ref, s_ref,
                   q_out, kb_out, vb_out, kt_out, vt_out):
    h = _modulate(x_ref[...], g_ref[...], shift_ref[...], scale_ref[...])
    z = _dot(h.astype(BF16), win_ref[...])
    c_t = _lane_tile(c_ref[...], GQA_Q_W // LANES)
    s_t = _lane_tile(s_ref[...], GQA_Q_W // LANES)
    q = z[:, _O_Q:_O_Q + GQA_Q_W] * c_t + z[:, _O_QR:_O_QR + GQA_Q_W] * s_t
    q_out[...] = (q * (GQA_HEAD_DIM ** -0.5)).astype(BF16)
    k = (z[:, _O_K:_O_K + GQA_KV_W] * c_t[:, :GQA_KV_W]
         + z[:, _O_KR:_O_KR + GQA_KV_W] * s_t[:, :GQA_KV_W])
    kb_out[...] = k.astype(BF16)
    vext = z[:, _O_V:_O_V + GQA_VEXT_W] + vbias_ref[...]
    vb_out[...] = vext.astype(BF16)

    @pl.when(pl.program_id(0) < NP_TOK // TOK_TILE)
    def _():
        v = jnp.concatenate([vext[:, 2 * p * LANES:(2 * p + 1) * LANES] for p in range(GQA_KV_HEADS // 2)], axis=1)
        for b in range(TOK_TILE // SEQ):
            kt_out[b] = k[b * SEQ:(b + 1) * SEQ, :].T
            vt_out[b] = v[b * SEQ:(b + 1) * SEQ, :].T


def _odd_in_proj(x, mod5, layer, g, win, vbias, tab_c, tab_s):
    t = TOK_TILE
    npt = NP_TOK // t
    out_widths = [(GQA_Q_W, BF16), (GQA_KV_W, BF16), (GQA_VEXT_W, BF16)]
    cache_shape = jax.ShapeDtypeStruct((BATCH, GQA_KV_W, SEQ), F32)
    cache_spec = pl.BlockSpec((t // SEQ, GQA_KV_W, SEQ), lambda i: (jnp.minimum(i, npt - 1), 0, 0))
    return pl.pallas_call(
        _odd_in_kernel,
        out_shape=[jax.ShapeDtypeStruct((N_TOK, w), dt) for w, dt in out_widths] + [cache_shape, cache_shape],
        grid=(N_TOK // t,),
        in_specs=[_tok_spec(t, D_MODEL), _const_spec((1, D_MODEL)), _mod_spec(layer, 0, t),
                  _mod_spec(layer, 1, t), _const_spec(win.shape), _const_spec(vbias.shape),
                  _rope_row_spec(t, LANES), _rope_row_spec(t, LANES)],
        out_specs=[_tok_spec(t, w) for w, _ in out_widths] + [cache_spec, cache_spec],
        compiler_params=_cparams(("arbitrary",)),
        name="odd_in_proj",
    )(x, g, mod5, mod5, win, vbias, tab_c, tab_s)


def _gqa_kernel(sink_ref, q_ref, *refs, local_len):
    if local_len:
        kl_ref, vl_ref, kc_ref, vc_ref, o_ref = refs
    else:
        kc_ref, vc_ref, o_ref = refs
    tq = q_ref.shape[0]
    lane = lax.broadcasted_iota(jnp.int32, (tq, LANES), 1)
    lo = lane < GQA_HEAD_DIM
    if local_len:
        i = pl.program_id(1)
        q0 = i * tq
        seq = kl_ref.shape[0]
        kstart = pl.multiple_of(jnp.clip(q0 - WINDOW, 0, seq - local_len), LANES)
        qpos = q0 + lax.broadcasted_iota(jnp.int32, (tq, local_len), 0)
        kpos = kstart + lax.broadcasted_iota(jnp.int32, (tq, local_len), 1)
        band = jnp.abs(qpos - kpos) <= WINDOW
    for p in range(GQA_KV_HEADS // 2):
        kc = kc_ref[:, p * LANES:(p + 1) * LANES]
        vc = vc_ref[:, 2 * p * LANES:(2 * p + 2) * LANES]
        if local_len:
            kl = kl_ref[pl.ds(kstart, local_len), p * LANES:(p + 1) * LANES]
            vl = vl_ref[pl.ds(kstart, local_len), 2 * p * LANES:(2 * p + 2) * LANES]
        for blk in range(GQA_GROUP):
            cols = slice((p * GQA_GROUP + blk) * LANES, (p * GQA_GROUP + blk + 1) * LANES)
            qb = q_ref[:, cols]
            res = []
            for half in range(2):
                head = (2 * p + half) * GQA_GROUP + blk
                sink = sink_ref[head]
                qh = jnp.where(lo if half == 0 else jnp.logical_not(lo), qb, jnp.zeros_like(qb))
                s_c = _dot_nt(qh, kc)
                m = jnp.maximum(s_c.max(axis=-1, keepdims=True), sink)
                if local_len:
                    s_l = jnp.where(band, _dot_nt(qh, kl), NEG_INF)
                    m = jnp.maximum(m, s_l.max(axis=-1, keepdims=True))
                r = _dot(jnp.exp(s_c - m).astype(BF16), vc)
                if local_len:
                    r = r + _dot(jnp.exp(s_l - m).astype(BF16), vl)
                res.append(r[:, :LANES] / (r[:, LANES:] + jnp.exp(sink - m)))
            o_ref[:, cols] = jnp.where(lo, res[0], res[1]).astype(BF16)


def _gqa_attention(sink, q, k_loc, v_loc, k_ctx, v_ctx, *, n_batch, seq_q, q_tile, tok_off, n_ctx, local):
    nq = seq_q // q_tile
    qoff = tok_off // q_tile
    local_len = q_tile + 2 * WINDOW if local else 0
    in_specs = [pl.BlockSpec(memory_space=pltpu.SMEM),
                pl.BlockSpec((q_tile, GQA_Q_W), lambda b, i: (qoff + b * nq + i, 0))]
    args = [sink, q]
    if local:
        boff = tok_off // seq_q
        in_specs += [pl.BlockSpec((seq_q, GQA_KV_W), lambda b, i: (boff + b, 0)),
                     pl.BlockSpec((seq_q, GQA_VEXT_W), lambda b, i: (boff + b, 0))]
        args += [k_loc, v_loc]
    in_specs += [pl.BlockSpec((n_ctx, GQA_KV_W), lambda b, i: (b, 0)),
                 pl.BlockSpec((n_ctx, GQA_VEXT_W), lambda b, i: (b, 0))]
    args += [k_ctx, v_ctx]
    return pl.pallas_call(
        functools.partial(_gqa_kernel, local_len=local_len),
        out_shape=jax.ShapeDtypeStruct((n_batch * seq_q, GQA_Q_W), BF16),
        grid=(n_batch, nq),
        in_specs=in_specs,
        out_specs=pl.BlockSpec((q_tile, GQA_Q_W), lambda b, i: (b * nq + i, 0)),
        compiler_params=_cparams(("parallel", "arbitrary")),
        name="gqa_local" if local else "gqa_ctx",
    )(*args)


def _route(h, w_ref, b_ref, ltri_ref, wsel_out, isel_out, rank_out, cnt_out, carry_ref):
    @pl.when(pl.program_id(0) == 0)
    def _():
        carry_ref[...] = jnp.zeros_like(carry_ref)

    logits = _dot(h, w_ref[...]) + b_ref[...]
    lane = lax.broadcasted_iota(jnp.int32, logits.shape, 1)
    neg = float(np.finfo(np.float32).min)
    lg = jnp.where(lane < N_EXPERTS, logits, neg)
    v1 = lg.max(axis=-1, keepdims=True)
    i1 = jnp.min(jnp.where(lg == v1, lane, LANES), axis=-1, keepdims=True)
    lg2 = jnp.where(lane == i1, neg, lg)
    v2 = lg2.max(axis=-1, keepdims=True)
    i2 = jnp.min(jnp.where(lg2 == v2, lane, LANES), axis=-1, keepdims=True)
    e2 = jnp.exp(v2 - v1)
    den = 1.0 + e2
    wsel_out[...] = jnp.where(lane == 0, 1.0 / den, jnp.where(lane == 1, e2 / den, 0.0))
    isel_out[...] = jnp.where(lane == 0, i1, jnp.where(lane == 1, i2, 0))
    hit = jnp.where(lane == i1, 1.0, jnp.where(lane == i2, 1.0, 0.0))
    carry = carry_ref[...]
    rank_out[...] = (_dot(ltri_ref[...], hit.astype(BF16)) + carry[0:1, :]).astype(jnp.int32)
    carry = carry + jnp.sum(hit, axis=0, keepdims=True)
    carry_ref[...] = carry
    cnt_out[...] = carry.astype(jnp.int32)


def _route_tables(isel, rank, cnt):
    tm = MOE_ROW_TILE
    counts = cnt[0, :N_EXPERTS]
    padded = ((counts + tm - 1) // tm) * tm
    ends = jnp.cumsum(padded)
    base = ends - padded
    e_ids = jnp.arange(N_EXPERTS, dtype=jnp.int32)
    row = rank[:, :N_EXPERTS] + base[None, :]
    pos1 = jnp.sum(jnp.where(e_ids[None, :] == isel[:, 0:1], row, 0), axis=1)
    pos2 = jnp.sum(jnp.where(e_ids[None, :] == isel[:, 1:2], row, 0), axis=1)
    tile_start = jnp.arange(MOE_ROWS // tm, dtype=jnp.int32) * tm
    tile_expert = jnp.minimum(jnp.sum(tile_start[:, None] >= ends[None, :], axis=1), N_EXPERTS - 1).astype(jnp.int32)
    tile_valid = jnp.clip((base + counts)[tile_expert] - tile_start, 0, tm).astype(jnp.int32)
    tile_valid = jnp.where(tile_start < ends[-1], tile_valid, 0)
    return pos1, pos2, tile_expert, tile_valid


def _scatter_rows(rows, pos1, pos2, n_out):
    n_tok, d = rows.shape
    per_w = n_tok // SC_WORKERS
    w = SC_GATHER_WINDOW * D_MODEL // d
    assert per_w * SC_WORKERS == n_tok and per_w % w == 0
    mesh = plsc.VectorSubcoreMesh(core_axis_name="core", subcore_axis_name="subcore")

    assert (per_w // w) % 2 == 0
    slot_types = [pltpu.VMEM((w,), jnp.int32), pltpu.VMEM((w,), jnp.int32), pltpu.VMEM((w, d), rows.dtype),
                  pltpu.SemaphoreType.DMA, pltpu.SemaphoreType.DMA]

    @functools.partial(
        pl.kernel, out_type=jax.ShapeDtypeStruct((n_out, d), rows.dtype), mesh=mesh,
        scratch_types=slot_types * 2, name="sc_scatter_rows")
    def scatter(x_hbm, p1_hbm, p2_hbm, o_hbm, *scratch):
        wid = lax.axis_index("subcore") * SC_CORES + lax.axis_index("core")
        base = wid * per_w
        slots = (scratch[:5], scratch[5:])

        @pl.loop(0, per_w // (2 * w))
        def _(g):
            loads = []
            for s, (i1_v, i2_v, rows_v, sem_in, _) in enumerate(slots):
                off = base + (2 * g + s) * w
                loads.append([pltpu.async_copy(p1_hbm.at[pl.ds(off, w)], i1_v, sem_in),
                              pltpu.async_copy(p2_hbm.at[pl.ds(off, w)], i2_v, sem_in),
                              pltpu.async_copy(x_hbm.at[pl.ds(off, w)], rows_v, sem_in)])
            stores = []
            for s, (i1_v, i2_v, rows_v, _, sem_out) in enumerate(slots):
                for cp in loads[s]:
                    cp.wait()
                stores += [pltpu.async_copy(rows_v, o_hbm.at[i1_v], sem_out),
                           pltpu.async_copy(rows_v, o_hbm.at[i2_v], sem_out)]
            for cp in stores:
                cp.wait()

    return scatter(rows, pos1, pos2)


def _gather_rows(table, idx):
    n_idx = idx.shape[0]
    d = table.shape[1]
    per_w = n_idx // SC_WORKERS
    assert per_w * SC_WORKERS == n_idx and per_w % SC_INDEX_BLOCK == 0
    mesh = plsc.VectorSubcoreMesh(core_axis_name="core", subcore_axis_name="subcore")

    w = SC_GATHER_WINDOW
    n_sub = SC_INDEX_BLOCK // w
    slot_types = [pltpu.VMEM((w, d), table.dtype), pltpu.SemaphoreType.DMA, pltpu.SemaphoreType.DMA]

    @functools.partial(
        pl.kernel, out_type=jax.ShapeDtypeStruct((n_idx, d), table.dtype), mesh=mesh,
        scratch_types=[pltpu.VMEM((SC_INDEX_BLOCK,), jnp.int32)] + slot_types * 2,
        name="sc_gather_rows")
    def gather(x_hbm, i_hbm, o_hbm, idx_v, *scratch):
        wid = lax.axis_index("subcore") * SC_CORES + lax.axis_index("core")
        base = wid * per_w
        slots = (scratch[:3], scratch[3:])

        @pl.loop(0, per_w // SC_INDEX_BLOCK)
        def _(g):
            off = base + g * SC_INDEX_BLOCK
            pltpu.sync_copy(i_hbm.at[pl.ds(off, SC_INDEX_BLOCK)], idx_v)

            def start_gather(s):
                rows_v, sem_in, _ = slots[s % 2]
                return pltpu.async_copy(x_hbm.at[idx_v.at[pl.ds(s * w, w)]], rows_v, sem_in)

            gathers = {0: start_gather(0)}
            writes = {}
            for s in range(n_sub):
                if s + 1 < n_sub:
                    if s >= 1:
                        writes[s - 1].wait()
                    gathers[s + 1] = start_gather(s + 1)
                gathers[s].wait()
                rows_v, _, sem_out = slots[s % 2]
                writes[s] = pltpu.async_copy(rows_v, o_hbm.at[pl.ds(off + s * w, w)], sem_out)
            writes[n_sub - 2].wait()
            writes[n_sub - 1].wait()

    return gather(table, idx)


def _expert_ffn_kernel(te_ref, nv_ref, x_ref, wg_ref, wu_ref, wd_ref, y_out):
    n_valid = nv_ref[pl.program_id(0)]

    @pl.when(n_valid > 0)
    def _():
        row = lax.broadcasted_iota(jnp.int32, x_ref.shape, 0)
        h = _unpack_pairs(jnp.where(row < n_valid, x_ref[...], 0.0))
        f = None
        for cidx in range(D_FF // FF_CHUNK):
            cols = slice(cidx * FF_CHUNK, (cidx + 1) * FF_CHUNK)
            a = _dot(h, wg_ref[:, cols].astype(BF16))
            u = _dot(h, wu_ref[:, cols].astype(BF16))
            fc = _dot(((a * jax.nn.sigmoid(a)) * u).astype(BF16), wd_ref[cols, :].astype(BF16))
            f = fc if f is None else f + fc
        y_out[...] = f

    @pl.when(n_valid == 0)
    def _():
        y_out[...] = jnp.zeros_like(y_out)


def _expert_ffn(xs, tile_expert, tile_valid, wg, wu, wd):
    tm = MOE_ROW_TILE
    wspec = lambda shape: pl.BlockSpec((None,) + shape, lambda j, te, nu: (te[j], 0, 0),
                                       pipeline_mode=pl.Buffered(1))
    return pl.pallas_call(
        _expert_ffn_kernel,
        out_shape=jax.ShapeDtypeStruct((MOE_ROWS, D_MODEL), F32),
        grid_spec=pltpu.PrefetchScalarGridSpec(
            num_scalar_prefetch=2,
            grid=(MOE_ROWS // tm,),
            in_specs=[pl.BlockSpec((tm, xs.shape[1]), lambda j, te, nu: (j, 0)),
                      wspec((D_MODEL, D_FF)), wspec((D_MODEL, D_FF)), wspec((D_FF, D_MODEL))],
            out_specs=pl.BlockSpec((tm, D_MODEL), lambda j, te, nu: (j, 0)),
        ),
        compiler_params=_cparams(("arbitrary",)),
        name="moe_expert_ffn",
    )(tile_expert, tile_valid, xs, wg, wu, wd)


def _moe_combine_kernel(y1_ref, y2_ref, wsel_ref, x_ref, g_ref, gate_ref, *rest):
    x_out = rest[-1]
    w = wsel_ref[...]
    f = w[:, 0:1] * y1_ref[...] + w[:, 1:2] * y2_ref[...]
    x_out[...] = x_ref[...] + gate_ref[...] * _rms(f, g_ref[...])


def _moe_combine(yg, wsel, x, mod5, layer, g3, *, tok_off, out_rows, out_off, y_prev=None):
    t = TOK_TILE
    nt = MOE_SEG // t
    off = tok_off // t
    ooff = out_off // t
    tpg = NP_TOK // t
    tok = lambda w: pl.BlockSpec((t, w), lambda i: (off + i, 0))
    in_specs = [_tok_spec(t, D_MODEL), pl.BlockSpec((t, D_MODEL), lambda i: (nt + i, 0)), tok(LANES), tok(D_MODEL),
                _const_spec((1, D_MODEL)),
                pl.BlockSpec((None, None, None, 1, D_MODEL), lambda i: (layer, (off + i) // tpg, 5, 0, 0))]
    args = [yg, yg, wsel, x, g3, mod5]
    aliases = {}
    if y_prev is not None:
        in_specs.append(pl.BlockSpec(memory_space=pl.ANY))
        args.append(y_prev)
        aliases = {len(args) - 1: 0}
    return pl.pallas_call(
        _moe_combine_kernel,
        out_shape=jax.ShapeDtypeStruct((out_rows, D_MODEL), F32),
        grid=(nt,),
        in_specs=in_specs,
        out_specs=pl.BlockSpec((t, D_MODEL), lambda i: (ooff + i, 0)),
        input_output_aliases=aliases,
        compiler_params=_cparams(("parallel",)),
        name="moe_combine",
    )(*args)


def _moe(h, x, routing, wg, wu, wd, mod5, layer, g3):
    wsel, isel, rank, cnt = routing
    pos1, pos2, tile_expert, tile_valid = _route_tables(isel, rank, cnt)
    xs = _scatter_rows(h, pos1, pos2, MOE_ROWS)
    ys = _expert_ffn(xs, tile_expert, tile_valid, wg, wu, wd)
    y_p = y_s = None
    for seg in range(N_TOK // MOE_SEG):
        rows = slice(seg * MOE_SEG, (seg + 1) * MOE_SEG)
        yg = _gather_rows(ys, jnp.concatenate([pos1[rows], pos2[rows]]))
        common = dict(tok_off=seg * MOE_SEG)
        if seg * MOE_SEG < NP_TOK:
            y_p = _moe_combine(yg, wsel, x, mod5, layer, g3, out_rows=NP_TOK, out_off=seg * MOE_SEG, y_prev=y_p, **common)
        else:
            y_s = _moe_combine(yg, wsel, x, mod5, layer, g3, out_rows=NS_TOK, out_off=seg * MOE_SEG - NP_TOK,
                               y_prev=y_s, **common)
    return y_p, y_s


def _rot_cols(w, half):
    k, n = w.shape
    wb = w.reshape(k, n // (2 * half), 2, half)
    return jnp.stack([-wb[:, :, 1], wb[:, :, 0]], axis=2).reshape(k, n)


def _axis_tables(r, pos):
    inv = np.float32(ROPE_BASE) ** (-np.arange(0, r, 2, dtype=np.float32) / np.float32(r))
    ang = pos.astype(np.float32)[:, None] * inv[None, :]
    cos, sin = np.cos(ang), np.sin(ang)
    return np.concatenate([cos, cos], axis=1), np.concatenate([sin, sin], axis=1)


def _rope_tables(r):
    s = np.arange(DEC_SEQ)
    cr, sr = _axis_tables(r // 2, s // GRID_W)
    cc, sc = _axis_tables(r // 2, s % GRID_W)
    return np.concatenate([cr, cc], axis=1), np.concatenate([sr, sc], axis=1)


def _with_identity(tab, ident):
    return np.concatenate([np.full((TOK_TILE, tab.shape[1]), ident, np.float32), tab], axis=0)


@functools.lru_cache(maxsize=None)
def _rope_constants():
    c32, s32 = _rope_tables(MLA_ROPE)
    ones = np.ones((DEC_SEQ, MLA_NOPE), np.float32)
    pad1 = np.ones((DEC_SEQ, MLA_HEAD_PAD - MLA_NOPE - MLA_ROPE), np.float32)
    cq = np.concatenate([ones, c32, pad1], axis=1)
    sq = np.concatenate([0 * ones, s32, 0 * pad1], axis=1)
    c64, s64 = _rope_tables(GQA_HEAD_DIM)
    return {
        "mla_cq": _with_identity(cq, 1.0), "mla_sq": _with_identity(sq, 0.0),
        "mla_ck": _with_identity(c32, 1.0), "mla_sk": _with_identity(s32, 0.0),
        "gqa_c": _with_identity(np.concatenate([c64, c64], axis=1), 1.0),
        "gqa_s": _with_identity(np.concatenate([s64, s64], axis=1), 0.0),
    }


def _prep_tables():
    return {k: jnp.asarray(v, F32) for k, v in _rope_constants().items()}


def _prep_even(w_in, q_norm, w_q_up, kv_norm, w_kv_up, wgf, bgf, wgb, bgb):
    sizes = [MLA_Q_RANK, MLA_KV_RANK, MLA_ROPE, GLA_HEADS * GLA_DK, GLA_HEADS * GLA_DK,
             GLA_HEADS * GLA_DV, GLA_HEADS * GLA_DV, GLA_GATE_RANK, GLA_GATE_RANK]
    cq, ckv, kpe, gq, gk, gv, gr, gaf, gab = jnp.split(w_in, [int(s) for s in np.cumsum(sizes)[:-1]], axis=1)
    pad = jnp.zeros((D_MODEL, LANES - 2 * MLA_ROPE - 2 * GLA_GATE_RANK), F32)
    win = jnp.concatenate([cq, ckv, gq, gk, gv, gr, kpe, _rot_cols(kpe, MLA_ROPE // 4), gaf, gab, pad], axis=1)

    wq = w_q_up.reshape(MLA_Q_RANK, MLA_HEADS, MLA_NOPE + MLA_ROPE)
    nope, pe = wq[..., :MLA_NOPE], wq[..., MLA_NOPE:]
    pe_rot = _rot_cols(pe.reshape(MLA_Q_RANK, MLA_HEADS * MLA_ROPE), MLA_ROPE // 4).reshape(pe.shape)
    zpad = jnp.zeros((MLA_Q_RANK, MLA_HEADS, MLA_HEAD_PAD - MLA_NOPE - MLA_ROPE), F32)
    wq_main = jnp.concatenate([nope, pe, zpad], axis=-1).reshape(MLA_Q_RANK, MLA_QK_W)
    wq_rot = jnp.concatenate([0 * nope, pe_rot, zpad], axis=-1).reshape(MLA_Q_RANK, MLA_QK_W)

    wkv = w_kv_up.reshape(MLA_KV_RANK, MLA_HEADS, MLA_NOPE + MLA_V)
    knope, vv = wkv[..., :MLA_NOPE], wkv[..., MLA_NOPE:]
    wkk = jnp.concatenate([knope, jnp.zeros((MLA_KV_RANK, MLA_HEADS, MLA_HEAD_PAD - MLA_NOPE), F32)],
                          axis=-1).reshape(MLA_KV_RANK, MLA_QK_W)
    vpair = vv.reshape(MLA_KV_RANK, MLA_HEADS // 2, 2 * MLA_V)
    wkv_ext = jnp.concatenate([vpair, jnp.zeros((MLA_KV_RANK, MLA_HEADS // 2, LANES), F32)],
                              axis=-1).reshape(MLA_KV_RANK, MLA_VEXT_W)
    vbias = jnp.tile(jnp.concatenate([jnp.zeros((LANES,), F32), jnp.ones((LANES,), F32)]),
                     MLA_HEADS // 2).reshape(1, MLA_VEXT_W)
    epl = jnp.tile(jnp.concatenate([jnp.zeros((MLA_ROPE, MLA_NOPE), F32), jnp.eye(MLA_ROPE, dtype=F32),
                                    jnp.zeros((MLA_ROPE, MLA_HEAD_PAD - MLA_NOPE - MLA_ROPE), F32)], axis=1),
                   (1, MLA_HEADS))

    def gate_w(w, off):
        return jnp.zeros((LANES, GLA_HEADS * GLA_DK), F32).at[off:off + GLA_GATE_RANK].set(w)

    r = np.arange(CUMSUM_BLOCK)
    same = (r[:, None] // GLA_CHUNK) == (r[None, :] // GLA_CHUNK)
    lmat = jnp.asarray(same & (r[:, None] >= r[None, :]), BF16)
    umat = jnp.asarray(same & (r[:, None] <= r[None, :]), BF16)
    return {
        "win": win.astype(BF16), "qn": q_norm.reshape(1, -1), "wq": jnp.concatenate([wq_main, wq_rot], axis=1).astype(BF16),
        "kvn": kv_norm.reshape(1, -1), "wkk": wkk.astype(BF16), "wkv": wkv_ext.astype(BF16), "vbias": vbias,
        "epl": epl.astype(BF16), "wgf": gate_w(wgf, _S_GAF).astype(BF16), "bgf": bgf.reshape(1, -1),
        "wgb": gate_w(wgb, _S_GAB).astype(BF16), "bgb": bgb.reshape(1, -1), "lmat": lmat, "umat": umat,
    }


def _gqa_head_perm():
    heads = []
    for p in range(GQA_KV_HEADS // 2):
        for i in range(GQA_GROUP):
            heads += [(2 * p) * GQA_GROUP + i, (2 * p + 1) * GQA_GROUP + i]
    return np.asarray(heads)


def _prep_odd(w_in, w_out):
    perm = _gqa_head_perm()
    wq = w_in[:, :GQA_Q_W].reshape(D_MODEL, GQA_HEADS, GQA_HEAD_DIM)[:, perm].reshape(D_MODEL, GQA_Q_W)
    wk = w_in[:, GQA_Q_W:GQA_Q_W + GQA_KV_W]
    wv = w_in[:, GQA_Q_W + GQA_KV_W:].reshape(D_MODEL, GQA_KV_HEADS // 2, 2 * GQA_HEAD_DIM)
    wv_ext = jnp.concatenate([wv, jnp.zeros((D_MODEL, GQA_KV_HEADS // 2, LANES), F32)], axis=-1).reshape(D_MODEL, GQA_VEXT_W)
    win = jnp.concatenate([wq, _rot_cols(wq, GQA_HEAD_DIM // 4), wk, _rot_cols(wk, GQA_HEAD_DIM // 4), wv_ext], axis=1)
    vbias = jnp.tile(jnp.concatenate([jnp.zeros((LANES,), F32), jnp.ones((LANES,), F32)]),
                     GQA_KV_HEADS // 2).reshape(1, GQA_VEXT_W)
    wo = w_out.reshape(GQA_HEADS, GQA_HEAD_DIM, D_MODEL)[perm].reshape(GQA_Q_W, D_MODEL)
    return win.astype(BF16), vbias, wo.astype(BF16)


def _ext_v(v):
    rows = v.shape[0]
    vp = v.reshape(rows, GQA_KV_HEADS // 2, 2 * GQA_HEAD_DIM)
    return jnp.concatenate([vp, jnp.ones((rows, GQA_KV_HEADS // 2, LANES), v.dtype)], axis=-1).reshape(rows, GQA_VEXT_W)


def kernel(x_prompt, x_sample, cache_mla_ckv, cache_mla_kpe, state_gla_fwd, state_gla_bwd, cache_gqa_k, cache_gqa_v, c, c_ctx, w_mod, b_mod, norm_g, w_in_ab, mla_q_norm, mla_w_q_up, mla_kv_norm, mla_w_kv_up, gla_w_gate_f, gla_b_gate_f, gla_w_gate_b, gla_b_gate_b, gla_norm, w_out_ab, ffn_w_gate, ffn_w_up, ffn_w_down, w_in_c, gqa_sink, w_out_c, moe_w_router, moe_b_router, moe_w_gate, moe_w_up, moe_w_down):
    x_in = (x_prompt.reshape(NP_TOK, D_MODEL), x_sample.reshape(NS_TOK, D_MODEL))
    cvec =jnp.concatenate([c_ctx[None, :], c, jnp.zeros((MOD_ROWS - N_GROUPS, D_MODEL), F32)], axis=0)
    mod5 = _modulation(cvec, w_mod, b_mod).reshape(DEPTH, MOD_ROWS, N_MOD, 1, D_MODEL)
    tabs = _prep_tables()
    gvec = lambda l, j: norm_g[l, j].reshape(1, D_MODEL)

    wts = _prep_even(w_in_ab[0], mla_q_norm[0], mla_w_q_up[0], mla_kv_norm[0], mla_w_kv_up[0],
                     gla_w_gate_f[0], gla_b_gate_f[0], gla_w_gate_b[0], gla_b_gate_b[0])
    (q, k, v, ckv, kpe, gq, gk, gv, gr, bf, bb) = _even_in_proj(*x_in, mod5, 0, gvec(0, 0), wts, tabs)
    kc, vc = _cache_kv(cache_mla_ckv[:, 0].reshape(DEC_BATCH * PAST_LEN, MLA_KV_RANK),
                       cache_mla_kpe[:, 0].reshape(DEC_BATCH * PAST_LEN, MLA_ROPE), wts)
    oa_p = _mla_attention(q, [k], [v], n_batch=BATCH, seq_q=SEQ, q_tile=SEQ, tok_off=0, k_batch_rows=[SEQ],
                          pairs=MLA_HEADS // 2)
    oa_s = _mla_attention(q, [k, kc], [v, vc], n_batch=DEC_BATCH, seq_q=DEC_SEQ, q_tile=MLA_Q_TILE,
                          tok_off=NP_TOK, k_batch_rows=[DEC_SEQ, PAST_LEN], pairs=1)
    gn = gla_norm[0].reshape(1, GLA_DV)
    zero_state = jnp.zeros((BATCH, GLA_HEADS, GLA_DK, GLA_DV), F32)
    ob_p, sf, sb = _gla(gq, gk, gv, gr, bf, bb, zero_state, zero_state, gn, n_batch=BATCH, seq=SEQ, tok_off=0)
    ob_s, _, _ = _gla(gq, gk, gv, gr, bf, bb, state_gla_fwd[:, 0], state_gla_bwd[:, 0], gn,
                      n_batch=DEC_BATCH, seq=DEC_SEQ, tok_off=NP_TOK)
    x, h = _out_proj([(oa_p, oa_s), (ob_p, ob_s)], w_out_ab[0].astype(BF16), x_in, mod5, 0, gvec(0, 1),
                     gvec(0, 2), BF16)
    x = _ffn(h, x, ffn_w_gate[0], ffn_w_up[0], ffn_w_down[0], mod5, 0, gvec(0, 3))

    win_c, vbias_c, wo_c = _prep_odd(w_in_c[0], w_out_c[0])
    qg, kgb, vgb, kg_t, vg_t = _odd_in_proj(x, mod5, 1, gvec(1, 0), win_c, vbias_c, tabs["gqa_c"], tabs["gqa_s"])
    sink = gqa_sink[0]
    og_p = _gqa_attention(sink, qg, None, None, kgb, vgb, n_batch=BATCH, seq_q=SEQ, q_tile=SEQ, tok_off=0, n_ctx=SEQ, local=False)
    kc_g = cache_gqa_k[:, 0].reshape(DEC_BATCH * PAST_LEN, GQA_KV_W).astype(BF16)
    vc_g = _ext_v(cache_gqa_v[:, 0].reshape(DEC_BATCH * PAST_LEN, GQA_KV_W)).astype(BF16)
    og_s = _gqa_attention(sink, qg, kgb, vgb, kc_g, vc_g, n_batch=DEC_BATCH, seq_q=DEC_SEQ, q_tile=GQA_Q_TILE,
                          tok_off=NP_TOK, n_ctx=PAST_LEN, local=True)
    w_r = jnp.zeros((D_MODEL, LANES), F32).at[:, :N_EXPERTS].set(moe_w_router[0]).astype(BF16)
    b_r = jnp.zeros((1, LANES), F32).at[0, :N_EXPERTS].set(moe_b_router[0])
    x, h, *routing = _out_proj([(og_p, og_s)], wo_c, x, mod5, 1, gvec(1, 1), gvec(1, 2), F32, router=(w_r, b_r))
    y_p, y_s = _moe(h, x, routing, moe_w_gate[0], moe_w_up[0], moe_w_down[0], mod5, 1, gvec(1, 3))

    y_prompt = y_p.reshape(BATCH, SEQ, D_MODEL)
    y_sample = y_s.reshape(DEC_BATCH, DEC_SEQ, D_MODEL)
    new_ckv = ckv[:NP_TOK].reshape(BATCH, 1, SEQ, MLA_KV_RANK)
    new_kpe = kpe[:NP_TOK].reshape(BATCH, 1, SEQ, MLA_ROPE)
    as_cache = lambda a: jnp.transpose(a.reshape(BATCH, 1, GQA_KV_HEADS, GQA_HEAD_DIM, SEQ), (0, 1, 4, 2, 3))
    new_k = as_cache(kg_t)
    new_v = as_cache(vg_t)
    return (y_prompt, y_sample, new_ckv, new_kpe, sf[:, None], sb[:, None], new_k, new_v)
```

```python
import functools

import jax
import jax.numpy as jnp
import numpy as np
from jax import lax
from jax.experimental import pallas as pl
from jax.experimental.pallas import tpu as pltpu
from jax.experimental.pallas import tpu_sc as plsc

F32 = jnp.float32
BF16 = jnp.bfloat16

D_MODEL = 1024
BATCH = 16
SEQ = 256
DEPTH = 2
DEC_BATCH = 4
DEC_SEQ = 4096
PAST_LEN = 256
GRID_W = 64
N_MOD = 6
EPS = 1e-6
ROPE_BASE = 10000.0
NEG_INF = -1e30

MLA_HEADS = 8
MLA_NOPE = 64
MLA_ROPE = 32
MLA_V = 64
MLA_Q_RANK = 384
MLA_KV_RANK = 256
GLA_HEADS = 4
GLA_DK = 64
GLA_DV = 128
GLA_GATE_RANK = 16
GLA_GATE_NORM = 16.0
GLA_CHUNK = 64
GQA_HEADS = 16
GQA_KV_HEADS = 4
GQA_GROUP = GQA_HEADS // GQA_KV_HEADS
GQA_HEAD_DIM = 64
WINDOW = 128
D_FF = 2816
N_EXPERTS = 8
TOP_K = 2

NP_TOK = BATCH * SEQ
NS_TOK = DEC_BATCH * DEC_SEQ
N_TOK = NP_TOK + NS_TOK
N_GROUPS = 1 + DEC_BATCH
MOD_ROWS = 8

LANES = 128
MXU_COLS = 256
VMEM_LIMIT_BYTES = 56 * 1024 * 1024

TOK_TILE = 512
MOD_COL_BLOCK = 1536
CUMSUM_BLOCK = 256
GLA_BLOCK_CHUNKS = MXU_COLS // GLA_CHUNK
GLA_CHUNK_UNROLL = 8
MLA_Q_TILE = 2048
MLA_Q_SUB = 256
GQA_Q_TILE = 256
MOE_ROW_TILE = 512
MOE_ROWS = TOP_K * N_TOK + N_EXPERTS * MOE_ROW_TILE
MOE_SEG = 4096
SC_CORES = 2
SC_SUBCORES = 16
SC_WORKERS = SC_CORES * SC_SUBCORES
SC_INDEX_BLOCK = 128
SC_GATHER_WINDOW = 32
FF_CHUNK = 1408

_C_CQ = 0
_C_CKV = _C_CQ + MLA_Q_RANK
_C_GQ = _C_CKV + MLA_KV_RANK
_C_GK = _C_GQ + GLA_HEADS * GLA_DK
_C_GV = _C_GK + GLA_HEADS * GLA_DK
_C_GR = _C_GV + GLA_HEADS * GLA_DV
_C_SMALL = _C_GR + GLA_HEADS * GLA_DV
IN_AB_EXT = _C_SMALL + LANES
_S_KPE, _S_KPER, _S_GAF, _S_GAB = 0, MLA_ROPE, 2 * MLA_ROPE, 2 * MLA_ROPE + GLA_GATE_RANK
MLA_HEAD_PAD = LANES
MLA_QK_W = MLA_HEADS * MLA_HEAD_PAD
MLA_VEXT_W = (MLA_HEADS // 2) * 2 * LANES


def _cparams(semantics):
    return pltpu.CompilerParams(dimension_semantics=semantics, vmem_limit_bytes=VMEM_LIMIT_BYTES)


def _const_spec(shape):
    nd = len(shape)
    return pl.BlockSpec(shape, lambda *_: (0,) * nd, pipeline_mode=pl.Buffered(1))


def _log_sigmoid(x):
    return jnp.minimum(x, 0.0) - jnp.log1p(jnp.exp(-jnp.abs(x)))


def _rms(x, g):
    return (x * lax.rsqrt(jnp.mean(x * x, axis=-1, keepdims=True) + EPS)) * g


def _modulate(x, g, shift, scale):
    return _rms(x, g) * (1.0 + scale) + shift


def _dot(a, b):
    return jnp.dot(a, b, preferred_element_type=F32)


def _dot_nt(a, b):
    return lax.dot_general(a, b, (((1,), (1,)), ((), ())), preferred_element_type=F32)


def _dot_tn(a, b):
    return lax.dot_general(a, b, (((0,), (0,)), ((), ())), preferred_element_type=F32)


def _split3(x):
    hi = x.astype(BF16)
    r1 = x - hi.astype(F32)
    mid = r1.astype(BF16)
    lo = (r1 - mid.astype(F32)).astype(BF16)
    return hi, mid, lo


def _lane_tile(x, reps):
    return jnp.concatenate([x] * reps, axis=1)


_HI16 = np.uint32(0xFFFF0000)


def _pack_pairs(xb):
    w = xb.shape[1] // 2
    xf = xb.astype(F32)
    lo = lax.bitcast_convert_type(xf[:, :w], jnp.uint32) >> 16
    hi = lax.bitcast_convert_type(xf[:, w:], jnp.uint32) & _HI16
    return lax.bitcast_convert_type(lo | hi, F32)


def _unpack_pairs(words):
    u = lax.bitcast_convert_type(words, jnp.uint32)
    lo = lax.bitcast_convert_type(u << 16, F32)
    hi = lax.bitcast_convert_type(u & _HI16, F32)
    return jnp.concatenate([lo, hi], axis=1).astype(BF16)


def _mod_kernel(c_ref, w_ref, b_ref, o_ref):
    c = c_ref[...]
    s = c * jax.nn.sigmoid(c)
    o_ref[...] = _dot(s.astype(BF16), w_ref[...].astype(BF16)) + b_ref[...]


def _modulation(cvec, w_mod, b_mod):
    ncol = N_MOD * D_MODEL
    blk = MOD_COL_BLOCK
    return pl.pallas_call(
        _mod_kernel,
        out_shape=jax.ShapeDtypeStruct((DEPTH, MOD_ROWS, ncol), F32),
        grid=(DEPTH, ncol // blk),
        in_specs=[
            pl.BlockSpec((MOD_ROWS, D_MODEL), lambda l, j: (0, 0)),
            pl.BlockSpec((None, D_MODEL, blk), lambda l, j: (l, 0, j)),
            pl.BlockSpec((None, 1, blk), lambda l, j: (l, 0, j)),
        ],
        out_specs=pl.BlockSpec((None, MOD_ROWS, blk), lambda l, j: (l, 0, j)),
        compiler_params=_cparams(("arbitrary", "arbitrary")),
        name="modulation",
    )(cvec, w_mod, b_mod.reshape(DEPTH, 1, ncol))


def _mod_spec(layer, j, tile):
    tpg = NP_TOK // tile
    return pl.BlockSpec((None, None, None, 1, D_MODEL), lambda i: (layer, i // tpg, j, 0, 0))


def _tok_spec(tile, width):
    return pl.BlockSpec((tile, width), lambda i: (i, 0))


def _split_specs(tile, width):
    npt = NP_TOK // tile
    return [pl.BlockSpec((tile, width), lambda i: (jnp.minimum(i, npt - 1), 0)),
            pl.BlockSpec((tile, width), lambda i: (jnp.maximum(i - npt, 0), 0))]


def _pick(tile, p_ref, s_ref):
    return jnp.where(pl.program_id(0) < NP_TOK // tile, p_ref[...], s_ref[...])


def _rope_row_spec(tile, width):
    npt = NP_TOK // tile
    spt = DEC_SEQ // tile
    return pl.BlockSpec((tile, width), lambda i: (jnp.where(i < npt, 0, 1 + (i - npt) % spt), 0))


def _even_in_kernel(xp_ref, xs_ref, g_ref, shift_ref, scale_ref, win_ref, qn_ref, wq_ref, kvn_ref, wkk_ref,
                    wkv_ref, vbias_ref, epl_ref, wgf_ref, bgf_ref, wgb_ref, bgb_ref, lmat_ref,
                    umat_ref, cq_ref, sq_ref, ck_ref, sk_ref,
                    q_out, k_out, v_out, ckv_out, kpe_out, gq_out, gk_out, gv_out, gr_out,
                    bf_out, bb_out):
    h = _modulate(_pick(TOK_TILE, xp_ref, xs_ref), g_ref[...], shift_ref[...], scale_ref[...])
    z = _dot(h.astype(BF16), win_ref[...])

    cqn = _rms(z[:, _C_CQ:_C_CQ + MLA_Q_RANK], qn_ref[...]).astype(BF16)
    qf = _dot(cqn, wq_ref[...])
    cq_t = _lane_tile(cq_ref[...], MLA_HEADS)
    sq_t = _lane_tile(sq_ref[...], MLA_HEADS)
    q_out[...] = (qf[:, :MLA_QK_W] * cq_t + qf[:, MLA_QK_W:] * sq_t).astype(BF16)

    ckvn = _rms(z[:, _C_CKV:_C_CKV + MLA_KV_RANK], kvn_ref[...])
    ckv_out[...] = ckvn
    small = z[:, _C_SMALL:_C_SMALL + LANES]
    kpe = (small[:, _S_KPE:_S_KPE + MLA_ROPE] * ck_ref[...]
           + small[:, _S_KPER:_S_KPER + MLA_ROPE] * sk_ref[...])
    kpe_out[...] = kpe
    ckvn_b = ckvn.astype(BF16)
    k_out[...] = (_dot(ckvn_b, wkk_ref[...]) + _dot(kpe.astype(BF16), epl_ref[...])).astype(BF16)
    v_out[...] = (_dot(ckvn_b, wkv_ref[...]) + vbias_ref[...]).astype(BF16)

    gq_out[...] = z[:, _C_GQ:_C_GQ + GLA_HEADS * GLA_DK] * (GLA_DK ** -0.5)
    gk_out[...] = z[:, _C_GK:_C_GK + GLA_HEADS * GLA_DK]
    gv_out[...] = z[:, _C_GV:_C_GV + GLA_HEADS * GLA_DV].astype(BF16)
    gr_out[...] = z[:, _C_GR:_C_GR + GLA_HEADS * GLA_DV]

    small_b = small.astype(BF16)
    la_f = _log_sigmoid(_dot(small_b, wgf_ref[...]) + bgf_ref[...]) * (1.0 / GLA_GATE_NORM)
    la_b = _log_sigmoid(_dot(small_b, wgb_ref[...]) + bgb_ref[...]) * (1.0 / GLA_GATE_NORM)
    lmat = lmat_ref[...]
    umat = umat_ref[...]
    for r in range(TOK_TILE // CUMSUM_BLOCK):
        rows = slice(r * CUMSUM_BLOCK, (r + 1) * CUMSUM_BLOCK)
        f_hi, f_mid, f_lo = _split3(la_f[rows])
        bf_out[rows, :] = _dot(lmat, f_hi) + _dot(lmat, f_mid) + _dot(lmat, f_lo)
        b_hi, b_mid, b_lo = _split3(la_b[rows])
        bb_out[rows, :] = _dot(umat, b_hi) + _dot(umat, b_mid) + _dot(umat, b_lo)


def _even_in_proj(xp, xs, mod5, layer, g, wts, tabs):
    t = TOK_TILE
    out_widths = [(MLA_QK_W, BF16), (MLA_QK_W, BF16), (MLA_VEXT_W, BF16), (MLA_KV_RANK, F32),
                  (MLA_ROPE, F32), (GLA_HEADS * GLA_DK, F32), (GLA_HEADS * GLA_DK, F32),
                  (GLA_HEADS * GLA_DV, BF16), (GLA_HEADS * GLA_DV, F32),
                  (GLA_HEADS * GLA_DK, F32), (GLA_HEADS * GLA_DK, F32)]
    const_names = ["win", "qn", "wq", "kvn", "wkk", "wkv", "vbias", "epl", "wgf", "bgf", "wgb",
                   "bgb", "lmat", "umat"]
    consts = [wts[n] for n in const_names]
    in_specs = (_split_specs(t, D_MODEL)
                + [_const_spec((1, D_MODEL)), _mod_spec(layer, 0, t), _mod_spec(layer, 1, t)]
                + [_const_spec(c.shape) for c in consts]
                + [_rope_row_spec(t, LANES), _rope_row_spec(t, LANES),
                   _rope_row_spec(t, MLA_ROPE), _rope_row_spec(t, MLA_ROPE)])
    return pl.pallas_call(
        _even_in_kernel,
        out_shape=[jax.ShapeDtypeStruct((N_TOK, w), dt) for w, dt in out_widths],
        grid=(N_TOK // t,),
        in_specs=in_specs,
        out_specs=[_tok_spec(t, w) for w, _ in out_widths],
        compiler_params=_cparams(("parallel",)),
        name="even_in_proj",
    )(xp, xs, g, mod5, mod5, *consts, tabs["mla_cq"], tabs["mla_sq"], tabs["mla_ck"], tabs["mla_sk"])


def _cache_kv_kernel(ckv_ref, kpe_ref, wkk_ref, wkv_ref, vbias_ref, epl_ref, k_out, v_out):
    ckv_b = ckv_ref[...].astype(BF16)
    k_out[...] = (_dot(ckv_b, wkk_ref[...]) + _dot(kpe_ref[...].astype(BF16), epl_ref[...])).astype(BF16)
    v_out[...] = (_dot(ckv_b, wkv_ref[...]) + vbias_ref[...]).astype(BF16)


def _cache_kv(ckv, kpe, wts):
    n = ckv.shape[0]
    consts = [wts[k] for k in ("wkk", "wkv", "vbias", "epl")]
    return pl.pallas_call(
        _cache_kv_kernel,
        out_shape=[jax.ShapeDtypeStruct((n, MLA_QK_W), BF16), jax.ShapeDtypeStruct((n, MLA_VEXT_W), BF16)],
        grid=(1,),
        in_specs=[_const_spec(ckv.shape), _const_spec(kpe.shape)] + [_const_spec(c.shape) for c in consts],
        out_specs=[_const_spec((n, MLA_QK_W)), _const_spec((n, MLA_VEXT_W))],
        compiler_params=_cparams(("arbitrary",)),
        name="mla_cache_kv",
    )(ckv, kpe, *consts)


def _mla_attn_kernel(*refs, n_seg):
    q_ref = refs[0]
    k_refs = refs[1:1 + n_seg]
    v_refs = refs[1 + n_seg:1 + 2 * n_seg]
    o_ref = refs[1 + 2 * n_seg]
    scale = (MLA_NOPE + MLA_ROPE) ** -0.5
    c = scale * float(np.log2(np.e))
    tq = q_ref.shape[0]
    q_sub = min(tq, MLA_Q_SUB)
    kt = MXU_COLS
    tiles = [(si, r0) for si, k in enumerate(k_refs) for r0 in range(0, k.shape[0], kt)]
    lane = lax.broadcasted_iota(jnp.int32, (q_sub, LANES), 1)
    n_pairs = o_ref.shape[1] // LANES
    for pi, qs in [(pi, qs) for pi in range(n_pairs) for qs in range(tq // q_sub)]:
        rows = slice(qs * q_sub, (qs + 1) * q_sub)
        vl = slice(pi * 2 * LANES, (pi + 1) * 2 * LANES)
        res = []
        for j in range(2):
            hl = slice((2 * pi + j) * LANES, (2 * pi + j + 1) * LANES)
            qj = q_ref[rows, hl]
            macc = None
            s_tiles = []
            for si, r0 in tiles:
                s = _dot_nt(qj, k_refs[si][r0:r0 + kt, hl])
                s_tiles.append(s)
                mt = jnp.maximum(s[:, :LANES], s[:, LANES:])
                macc = mt if macc is None else jnp.maximum(macc, mt)
            m = macc.max(axis=-1, keepdims=True)
            r = None
            for (si, r0), s in zip(tiles, s_tiles):
                p = jnp.exp2((s - m) * c).astype(BF16)
                rj = _dot(p, v_refs[si][r0:r0 + kt, vl])
                r = rj if r is None else r + rj
            res.append(r[:, :LANES] / r[:, LANES:])
        o_ref[rows, pi * LANES:(pi + 1) * LANES] = jnp.where(lane < MLA_V, res[0], res[1]).astype(BF16)


def _mla_attention(q, ks, vs, *, n_batch, seq_q, q_tile, tok_off, k_batch_rows, pairs):
    n_seg = len(ks)
    nq = seq_q // q_tile
    qoff = tok_off // q_tile
    grid = (n_batch, MLA_HEADS // 2 // pairs, nq)
    wq = pairs * 2 * LANES
    in_specs = [pl.BlockSpec((q_tile, wq), lambda b, hp, i: (qoff + b * nq + i, hp))]
    for s in range(n_seg):
        rows = k_batch_rows[s]
        off = (tok_off // rows) if s == 0 else 0
        in_specs.append(pl.BlockSpec((rows, wq), functools.partial(lambda b, hp, i, off: (off + b, hp), off=off)))
    for s in range(n_seg):
        rows = k_batch_rows[s]
        off = (tok_off // rows) if s == 0 else 0
        in_specs.append(pl.BlockSpec((rows, wq), functools.partial(lambda b, hp, i, off: (off + b, hp), off=off)))
    return pl.pallas_call(
        functools.partial(_mla_attn_kernel, n_seg=n_seg),
        out_shape=jax.ShapeDtypeStruct((n_batch * seq_q, MLA_HEADS * MLA_V), BF16),
        grid=grid,
        in_specs=in_specs,
        out_specs=pl.BlockSpec((q_tile, pairs * LANES), lambda b, hp, i: (b * nq + i, hp)),
        compiler_params=_cparams(("parallel", "parallel", "arbitrary")),
        name=f"mla_attention_{n_seg}seg",
    )(q, *ks, *vs)


def _gla_kernel(q_ref, k_ref, v_ref, gr_ref, bf_ref, bb_ref, s0f_ref, s0b_ref, gn_ref,
                o_ref, sf_ref, sb_ref, acc_ref, kdf_ref, kdb_ref, qdf_ref, qdb_ref, hist_ref, *, n_chunks):
    c = GLA_CHUNK
    cpb = min(n_chunks, GLA_BLOCK_CHUNKS)
    blk = cpb * c
    lane = lax.broadcasted_iota(jnp.int32, (blk, LANES), 1)
    lo = lane < GLA_DK
    row = lax.broadcasted_iota(jnp.int32, (blk, blk), 0)
    col = lax.broadcasted_iota(jnp.int32, (blk, blk), 1)
    chunk_bits = c.bit_length() - 1
    same_chunk = jnp.right_shift(row, chunk_bits) == jnp.right_shift(col, chunk_bits)
    tril = same_chunk & (row >= col)
    triu = same_chunk & (row <= col)
    zero_blk = jnp.zeros((GLA_DK, GLA_DV), F32)

    def pair_state_t(s_ref):
        blockdiag = jnp.concatenate(
            [jnp.concatenate([s_ref[0], zero_blk], axis=1),
             jnp.concatenate([zero_blk, s_ref[1]], axis=1)], axis=0)
        return blockdiag.T

    dirs = ((bf_ref, c // 2 - 1, c - 1, tril, kdf_ref, qdf_ref),
            (bb_ref, c // 2, 0, triu, kdb_ref, qdb_ref))
    sels = (lo, jnp.logical_not(lo))
    lo_c = lax.broadcasted_iota(jnp.int32, (c, LANES), 1) < GLA_DK
    sels_c = (lo_c, jnp.logical_not(lo_c))
    hcols = (slice(0, GLA_DV), slice(GLA_DV, 2 * GLA_DV))

    def block(r, carry):
        rows = pl.ds(pl.multiple_of(r * blk, blk), blk)
        q = q_ref[rows, :]
        k = k_ref[rows, :]
        v = v_ref[rows, :]

        def chunk_row(b, r):
            return jnp.concatenate([jnp.broadcast_to(b[ch * c + r:ch * c + r + 1, :], (c, LANES))
                                    for ch in range(cpb)], axis=0)

        for d, (b_ref, mid_row, last_row, causal, kd_ref, qd_ref) in enumerate(dirs):
            b = b_ref[rows, :]
            b_mid = chunk_row(b, mid_row)
            b_last = chunk_row(b, last_row)
            qe = q * jnp.exp(b - b_mid)
            ke = (k * jnp.exp(b_mid - b)).astype(BF16)
            kd_ref[rows, :] = (k * jnp.exp(b_last - b)).astype(BF16)
            qd_ref[rows, :] = (q * jnp.exp(b)).astype(BF16)
            for j in range(2):
                a = _dot_nt(jnp.where(sels[j], qe, 0.0).astype(BF16), ke)
                o = _dot(jnp.where(causal, a, 0.0).astype(BF16), v[:, hcols[j]])
                if d == 0:
                    acc_ref[rows, hcols[j]] = o
                else:
                    acc_ref[rows, hcols[j]] += o
        return carry

    lax.fori_loop(0, n_chunks // cpb, block, 0, unroll=min(2, n_chunks // cpb))

    def scan(i, carry):
        new = []
        for d, (b_ref, _, last_row, _, kd_ref, _) in enumerate(dirs):
            ci = i if d == 0 else n_chunks - 1 - i
            rows = pl.ds(pl.multiple_of(ci * c, c), c)
            hist_ref[d, ci] = carry[d].astype(BF16)
            grp = b_ref[pl.ds(pl.multiple_of(ci * c + (last_row // 8) * 8, 8), 8), :]
            b_last = grp[last_row % 8:last_row % 8 + 1, :]
            new.append(carry[d] * jnp.exp(b_last) + _dot_tn(v_ref[rows, :], kd_ref[rows, :]))
        return tuple(new)

    st_f, st_b = lax.fori_loop(0, n_chunks, scan, (pair_state_t(s0f_ref), pair_state_t(s0b_ref)),
                               unroll=GLA_CHUNK_UNROLL)

    def inter(ci, carry):
        rows = pl.ds(pl.multiple_of(ci * c, c), c)
        for d, (_, _, _, _, _, qd_ref) in enumerate(dirs):
            qd = qd_ref[rows, :]
            st = hist_ref[d, ci]
            for j in range(2):
                acc_ref[rows, hcols[j]] += _dot_nt(jnp.where(sels_c[j], qd, jnp.zeros_like(qd)), st[hcols[j], :])
        return carry

    lax.fori_loop(0, n_chunks, inter, 0, unroll=GLA_CHUNK_UNROLL)
    s_f = st_f.T
    s_b = st_b.T
    sf_ref[0] = s_f[:GLA_DK, :GLA_DV]
    sf_ref[1] = s_f[GLA_DK:, GLA_DV:]
    sb_ref[0] = s_b[:GLA_DK, :GLA_DV]
    sb_ref[1] = s_b[GLA_DK:, GLA_DV:]

    gn = gn_ref[...]
    for j in range(2):
        cols = slice(j * GLA_DV, (j + 1) * GLA_DV)
        gr = gr_ref[:, cols]
        o_ref[:, cols] = (_rms(acc_ref[:, cols], gn) * (gr * jax.nn.sigmoid(gr))).astype(BF16)


def _gla(gq, gk, gv, gr, bf, bb, s0f, s0b, gnorm, *, n_batch, seq, tok_off):
    n_chunks = seq // GLA_CHUNK
    boff = tok_off // seq
    hp = GLA_HEADS // 2
    tok = lambda w: pl.BlockSpec((seq, w), lambda b, p: (boff + b, p))
    st = pl.BlockSpec((None, 2, GLA_DK, GLA_DV), lambda b, p: (b, p, 0, 0))
    return pl.pallas_call(
        functools.partial(_gla_kernel, n_chunks=n_chunks),
        out_shape=[jax.ShapeDtypeStruct((n_batch * seq, GLA_HEADS * GLA_DV), BF16),
                   jax.ShapeDtypeStruct((n_batch, GLA_HEADS, GLA_DK, GLA_DV), F32),
                   jax.ShapeDtypeStruct((n_batch, GLA_HEADS, GLA_DK, GLA_DV), F32)],
        grid=(n_batch, hp),
        in_specs=[tok(2 * GLA_DK), tok(2 * GLA_DK), tok(2 * GLA_DV), tok(2 * GLA_DV),
                  tok(2 * GLA_DK), tok(2 * GLA_DK), st, st, _const_spec((1, GLA_DV))],
        out_specs=[pl.BlockSpec((seq, 2 * GLA_DV), lambda b, p: (b, p)), st, st],
        scratch_shapes=[pltpu.VMEM((seq, 2 * GLA_DV), F32)]
                       + [pltpu.VMEM((seq, 2 * GLA_DK), BF16)] * 4
                       + [pltpu.VMEM((2, n_chunks, 2 * GLA_DV, 2 * GLA_DK), BF16)],
        compiler_params=_cparams(("parallel", "parallel")),
        name=f"gla_seq{seq}",
    )(gq, gk, gv, gr, bf, bb, s0f, s0b, gnorm)


def _out_proj_kernel(*refs, n_o, x_split, route):
    t = TOK_TILE
    o = [_pick(t, refs[2 * j], refs[2 * j + 1]) for j in range(n_o)]
    rest = refs[2 * n_o:]
    w_ref = rest[0]
    if x_split:
        x_in = _pick(t, rest[1], rest[2])
        rest = rest[3:]
    else:
        x_in = rest[1][...]
        rest = rest[2:]
    g1_ref, gate_ref, g2_ref, shift_ref, scale_ref = rest[:5]
    rest = rest[5:]
    if route:
        router_in, rest = rest[:3], rest[3:]
    x_out, h_out = rest[:2]
    y = _dot(o[0] if n_o == 1 else jnp.concatenate(o, axis=1), w_ref[...])
    x = x_in + gate_ref[...] * _rms(y, g1_ref[...])
    x_out[...] = x
    h = _modulate(x, g2_ref[...], shift_ref[...], scale_ref[...])
    if route:
        hb = h.astype(BF16)
        h_out[...] = _pack_pairs(hb)
        _route(hb, *router_in, *rest[2:])
    else:
        h_out[...] = h.astype(h_out.dtype)


def _out_proj(os_, w, x, mod5, layer, g1, g2, h_dtype, router=None):
    t = TOK_TILE
    x_split = isinstance(x, tuple)
    in_specs, args = [], []
    for o_p, o_s in os_:
        in_specs += _split_specs(t, o_p.shape[1])
        args += [o_p, o_s]
    in_specs.append(_const_spec(w.shape))
    args.append(w)
    if x_split:
        in_specs += _split_specs(t, D_MODEL)
        args += list(x)
    else:
        in_specs.append(_tok_spec(t, D_MODEL))
        args.append(x)
    in_specs += [_const_spec((1, D_MODEL)), _mod_spec(layer, 2, t), _const_spec((1, D_MODEL)),
                 _mod_spec(layer, 3, t), _mod_spec(layer, 4, t)]
    args += [g1, mod5, g2, mod5, mod5]
    h_width = D_MODEL // 2 if router is not None else D_MODEL
    out_shape = [jax.ShapeDtypeStruct((N_TOK, D_MODEL), F32), jax.ShapeDtypeStruct((N_TOK, h_width), h_dtype)]
    out_specs = [_tok_spec(t, D_MODEL), _tok_spec(t, h_width)]
    scratch = []
    if router is not None:
        r = np.arange(t)
        ltri = jnp.asarray(r[:, None] > r[None, :], BF16)
        args += [router[0], router[1], ltri]
        in_specs += [_const_spec(router[0].shape), _const_spec(router[1].shape), _const_spec((t, t))]
        out_shape += [jax.ShapeDtypeStruct((N_TOK, LANES), F32), jax.ShapeDtypeStruct((N_TOK, LANES), jnp.int32),
                      jax.ShapeDtypeStruct((N_TOK, LANES), jnp.int32), jax.ShapeDtypeStruct((8, LANES), jnp.int32)]
        out_specs += [_tok_spec(t, LANES), _tok_spec(t, LANES), _tok_spec(t, LANES),
                      pl.BlockSpec((8, LANES), lambda i: (0, 0))]
        scratch = [pltpu.VMEM((8, LANES), F32)]
    return pl.pallas_call(
        functools.partial(_out_proj_kernel, n_o=len(os_), x_split=x_split, route=router is not None),
        out_shape=out_shape,
        grid=(N_TOK // t,),
        in_specs=in_specs,
        out_specs=out_specs,
        scratch_shapes=scratch,
        compiler_params=_cparams(("arbitrary",) if router is not None else ("parallel",)),
        name=f"out_proj_{len(os_)}",
    )(*args)


def _ffn_kernel(h_ref, x_ref, wg_ref, wu_ref, wd_ref, g_ref, gate_ref, x_out):
    h = h_ref[...]
    f = None
    for cidx in range(D_FF // FF_CHUNK):
        cols = slice(cidx * FF_CHUNK, (cidx + 1) * FF_CHUNK)
        a = _dot(h, wg_ref[:, cols].astype(BF16))
        u = _dot(h, wu_ref[:, cols].astype(BF16))
        fc = _dot(((a * jax.nn.sigmoid(a)) * u).astype(BF16), wd_ref[cols, :].astype(BF16))
        f = fc if f is None else f + fc
    x_out[...] = x_ref[...] + gate_ref[...] * _rms(f, g_ref[...])


def _ffn(h, x, wg, wu, wd, mod5, layer, g3):
    t = TOK_TILE
    return pl.pallas_call(
        _ffn_kernel,
        out_shape=jax.ShapeDtypeStruct((N_TOK, D_MODEL), F32),
        grid=(N_TOK // t,),
        in_specs=[_tok_spec(t, D_MODEL), _tok_spec(t, D_MODEL), _const_spec(wg.shape),
                  _const_spec(wu.shape), _const_spec(wd.shape), _const_spec((1, D_MODEL)),
                  _mod_spec(layer, 5, t)],
        out_specs=_tok_spec(t, D_MODEL),
        compiler_params=_cparams(("parallel",)),
        name="ffn_swiglu",
    )(h, x, wg, wu, wd, g3, mod5)


GQA_Q_W = GQA_HEADS * GQA_HEAD_DIM
GQA_KV_W = GQA_KV_HEADS * GQA_HEAD_DIM
GQA_VEXT_W = (GQA_KV_HEADS // 2) * 2 * LANES
_O_Q, _O_QR = 0, GQA_Q_W
_O_K, _O_KR = 2 * GQA_Q_W, 2 * GQA_Q_W + GQA_KV_W
_O_V = 2 * GQA_Q_W + 2 * GQA_KV_W
IN_C_EXT = _O_V + GQA_VEXT_W


def _odd_in_kernel(x_ref, g_ref, shift_ref, scale_ref, win_ref, vbias_ref, c_ref, s_ref,
                   q_out, kb_out, vb_out, kt_out, vt_out):
    h = _modulate(x_ref[...], g_ref[...], shift_ref[...], scale_ref[...])
    z = _dot(h.astype(BF16), win_ref[...])
    c_t = _lane_tile(c_ref[...], GQA_Q_W // LANES)
    s_t = _lane_tile(s_ref[...], GQA_Q_W // LANES)
    q = z[:, _O_Q:_O_Q + GQA_Q_W] * c_t + z[:, _O_QR:_O_QR + GQA_Q_W] * s_t
    q_out[...] = (q * (GQA_HEAD_DIM ** -0.5)).astype(BF16)
    k = (z[:, _O_K:_O_K + GQA_KV_W] * c_t[:, :GQA_KV_W]
         + z[:, _O_KR:_O_KR + GQA_KV_W] * s_t[:, :GQA_KV_W])
    kb_out[...] = k.astype(BF16)
    vext = z[:, _O_V:_O_V + GQA_VEXT_W] + vbias_ref[...]
    vb_out[...] = vext.astype(BF16)

    @pl.when(pl.program_id(0) < NP_TOK // TOK_TILE)
    def _():
        v = jnp.concatenate([vext[:, 2 * p * LANES:(2 * p + 1) * LANES] for p in range(GQA_KV_HEADS // 2)], axis=1)
        for b in range(TOK_TILE // SEQ):
            kt_out[b] = k[b * SEQ:(b + 1) * SEQ, :].T
            vt_out[b] = v[b * SEQ:(b + 1) * SEQ, :].T


def _odd_in_proj(x, mod5, layer, g, win, vbias, tab_c, tab_s):
    t = TOK_TILE
    npt = NP_TOK // t
    out_widths = [(GQA_Q_W, BF16), (GQA_KV_W, BF16), (GQA_VEXT_W, BF16)]
    cache_shape = jax.ShapeDtypeStruct((BATCH, GQA_KV_W, SEQ), F32)
    cache_spec = pl.BlockSpec((t // SEQ, GQA_KV_W, SEQ), lambda i: (jnp.minimum(i, npt - 1), 0, 0))
    return pl.pallas_call(
        _odd_in_kernel,
        out_shape=[jax.ShapeDtypeStruct((N_TOK, w), dt) for w, dt in out_widths] + [cache_shape, cache_shape],
        grid=(N_TOK // t,),
        in_specs=[_tok_spec(t, D_MODEL), _const_spec((1, D_MODEL)), _mod_spec(layer, 0, t),
                  _mod_spec(layer, 1, t), _const_spec(win.shape), _const_spec(vbias.shape),
                  _rope_row_spec(t, LANES), _rope_row_spec(t, LANES)],
        out_specs=[_tok_spec(t, w) for w, _ in out_widths] + [cache_spec, cache_spec],
        compiler_params=_cparams(("arbitrary",)),
        name="odd_in_proj",
    )(x, g, mod5, mod5, win, vbias, tab_c, tab_s)


def _gqa_kernel(sink_ref, q_ref, *refs, local_len):
    if local_len:
        kl_ref, vl_ref, kc_ref, vc_ref, o_ref = refs
    else:
        kc_ref, vc_ref, o_ref = refs
    tq = q_ref.shape[0]
    lane = lax.broadcasted_iota(jnp.int32, (tq, LANES), 1)
    lo = lane < GQA_HEAD_DIM
    if local_len:
        i = pl.program_id(1)
        q0 = i * tq
        seq = kl_ref.shape[0]
        kstart = pl.multiple_of(jnp.clip(q0 - WINDOW, 0, seq - local_len), LANES)
        qpos = q0 + lax.broadcasted_iota(jnp.int32, (tq, local_len), 0)
        kpos = kstart + lax.broadcasted_iota(jnp.int32, (tq, local_len), 1)
        band = jnp.abs(qpos - kpos) <= WINDOW
    for p in range(GQA_KV_HEADS // 2):
        kc = kc_ref[:, p * LANES:(p + 1) * LANES]
        vc = vc_ref[:, 2 * p * LANES:(2 * p + 2) * LANES]
        if local_len:
            kl = kl_ref[pl.ds(kstart, local_len), p * LANES:(p + 1) * LANES]
            vl = vl_ref[pl.ds(kstart, local_len), 2 * p * LANES:(2 * p + 2) * LANES]
        for blk in range(GQA_GROUP):
            cols = slice((p * GQA_GROUP + blk) * LANES, (p * GQA_GROUP + blk + 1) * LANES)
            qb = q_ref[:, cols]
            res = []
            for half in range(2):
                head = (2 * p + half) * GQA_GROUP + blk
                sink = sink_ref[head]
                qh = jnp.where(lo if half == 0 else jnp.logical_not(lo), qb, jnp.zeros_like(qb))
                s_c = _dot_nt(qh, kc)
                m = jnp.maximum(s_c.max(axis=-1, keepdims=True), sink)
                if local_len:
                    s_l = jnp.where(band, _dot_nt(qh, kl), NEG_INF)
                    m = jnp.maximum(m, s_l.max(axis=-1, keepdims=True))
                r = _dot(jnp.exp(s_c - m).astype(BF16), vc)
                if local_len:
                    r = r + _dot(jnp.exp(s_l - m).astype(BF16), vl)
                res.append(r[:, :LANES] / (r[:, LANES:] + jnp.exp(sink - m)))
            o_ref[:, cols] = jnp.where(lo, res[0], res[1]).astype(BF16)


def _gqa_attention(sink, q, k_loc, v_loc, k_ctx, v_ctx, *, n_batch, seq_q, q_tile, tok_off, n_ctx, local):
    nq = seq_q // q_tile
    qoff = tok_off // q_tile
    local_len = q_tile + 2 * WINDOW if local else 0
    in_specs = [pl.BlockSpec(memory_space=pltpu.SMEM),
                pl.BlockSpec((q_tile, GQA_Q_W), lambda b, i: (qoff + b * nq + i, 0))]
    args = [sink, q]
    if local:
        boff = tok_off // seq_q
        in_specs += [pl.BlockSpec((seq_q, GQA_KV_W), lambda b, i: (boff + b, 0)),
                     pl.BlockSpec((seq_q, GQA_VEXT_W), lambda b, i: (boff + b, 0))]
        args += [k_loc, v_loc]
    in_specs += [pl.BlockSpec((n_ctx, GQA_KV_W), lambda b, i: (b, 0)),
                 pl.BlockSpec((n_ctx, GQA_VEXT_W), lambda b, i: (b, 0))]
    args += [k_ctx, v_ctx]
    return pl.pallas_call(
        functools.partial(_gqa_kernel, local_len=local_len),
        out_shape=jax.ShapeDtypeStruct((n_batch * seq_q, GQA_Q_W), BF16),
        grid=(n_batch, nq),
        in_specs=in_specs,
        out_specs=pl.BlockSpec((q_tile, GQA_Q_W), lambda b, i: (b * nq + i, 0)),
        compiler_params=_cparams(("parallel", "arbitrary")),
        name="gqa_local" if local else "gqa_ctx",
    )(*args)


def _route(h, w_ref, b_ref, ltri_ref, wsel_out, isel_out, rank_out, cnt_out, carry_ref):
    @pl.when(pl.program_id(0) == 0)
    def _():
        carry_ref[...] = jnp.zeros_like(carry_ref)

    logits = _dot(h, w_ref[...]) + b_ref[...]
    lane = lax.broadcasted_iota(jnp.int32, logits.shape, 1)
    neg = float(np.finfo(np.float32).min)
    lg = jnp.where(lane < N_EXPERTS, logits, neg)
    v1 = lg.max(axis=-1, keepdims=True)
    i1 = jnp.min(jnp.where(lg == v1, lane, LANES), axis=-1, keepdims=True)
    lg2 = jnp.where(lane == i1, neg, lg)
    v2 = lg2.max(axis=-1, keepdims=True)
    i2 = jnp.min(jnp.where(lg2 == v2, lane, LANES), axis=-1, keepdims=True)
    e2 = jnp.exp(v2 - v1)
    den = 1.0 + e2
    wsel_out[...] = jnp.where(lane == 0, 1.0 / den, jnp.where(lane == 1, e2 / den, 0.0))
    isel_out[...] = jnp.where(lane == 0, i1, jnp.where(lane == 1, i2, 0))
    hit = jnp.where(lane == i1, 1.0, jnp.where(lane == i2, 1.0, 0.0))
    carry = carry_ref[...]
    rank_out[...] = (_dot(ltri_ref[...], hit.astype(BF16)) + carry[0:1, :]).astype(jnp.int32)
    carry = carry + jnp.sum(hit, axis=0, keepdims=True)
    carry_ref[...] = carry
    cnt_out[...] = carry.astype(jnp.int32)


def _route_tables(isel, rank, cnt):
    tm = MOE_ROW_TILE
    counts = cnt[0, :N_EXPERTS]
    padded = ((counts + tm - 1) // tm) * tm
    ends = jnp.cumsum(padded)
    base = ends - padded
    e_ids = jnp.arange(N_EXPERTS, dtype=jnp.int32)
    row = rank[:, :N_EXPERTS] + base[None, :]
    pos1 = jnp.sum(jnp.where(e_ids[None, :] == isel[:, 0:1], row, 0), axis=1)
    pos2 = jnp.sum(jnp.where(e_ids[None, :] == isel[:, 1:2], row, 0), axis=1)
    tile_start = jnp.arange(MOE_ROWS // tm, dtype=jnp.int32) * tm
    tile_expert = jnp.minimum(jnp.sum(tile_start[:, None] >= ends[None, :], axis=1), N_EXPERTS - 1).astype(jnp.int32)
    tile_valid = jnp.clip((base + counts)[tile_expert] - tile_start, 0, tm).astype(jnp.int32)
    tile_valid = jnp.where(tile_start < ends[-1], tile_valid, 0)
    return pos1, pos2, tile_expert, tile_valid


def _scatter_rows(rows, pos1, pos2, n_out):
    n_tok, d = rows.shape
    per_w = n_tok // SC_WORKERS
    w = SC_GATHER_WINDOW * D_MODEL // d
    assert per_w * SC_WORKERS == n_tok and per_w % w == 0
    mesh = plsc.VectorSubcoreMesh(core_axis_name="core", subcore_axis_name="subcore")

    assert (per_w // w) % 2 == 0
    slot_types = [pltpu.VMEM((w,), jnp.int32), pltpu.VMEM((w,), jnp.int32), pltpu.VMEM((w, d), rows.dtype),
                  pltpu.SemaphoreType.DMA, pltpu.SemaphoreType.DMA]

    @functools.partial(
        pl.kernel, out_type=jax.ShapeDtypeStruct((n_out, d), rows.dtype), mesh=mesh,
        scratch_types=slot_types * 2, name="sc_scatter_rows")
    def scatter(x_hbm, p1_hbm, p2_hbm, o_hbm, *scratch):
        wid = lax.axis_index("subcore") * SC_CORES + lax.axis_index("core")
        base = wid * per_w
        slots = (scratch[:5], scratch[5:])

        @pl.loop(0, per_w // (2 * w))
        def _(g):
            loads = []
            for s, (i1_v, i2_v, rows_v, sem_in, _) in enumerate(slots):
                off = base + (2 * g + s) * w
                loads.append([pltpu.async_copy(p1_hbm.at[pl.ds(off, w)], i1_v, sem_in),
                              pltpu.async_copy(p2_hbm.at[pl.ds(off, w)], i2_v, sem_in),
                              pltpu.async_copy(x_hbm.at[pl.ds(off, w)], rows_v, sem_in)])
            stores = []
            for s, (i1_v, i2_v, rows_v, _, sem_out) in enumerate(slots):
                for cp in loads[s]:
                    cp.wait()
                stores += [pltpu.async_copy(rows_v, o_hbm.at[i1_v], sem_out),
                           pltpu.async_copy(rows_v, o_hbm.at[i2_v], sem_out)]
            for cp in stores:
                cp.wait()

    return scatter(rows, pos1, pos2)


def _gather_rows(table, idx):
    n_idx = idx.shape[0]
    d = table.shape[1]
    per_w = n_idx // SC_WORKERS
    assert per_w * SC_WORKERS == n_idx and per_w % SC_INDEX_BLOCK == 0
    mesh = plsc.VectorSubcoreMesh(core_axis_name="core", subcore_axis_name="subcore")

    w = SC_GATHER_WINDOW
    n_sub = SC_INDEX_BLOCK // w
    slot_types = [pltpu.VMEM((w, d), table.dtype), pltpu.SemaphoreType.DMA, pltpu.SemaphoreType.DMA]

    @functools.partial(
        pl.kernel, out_type=jax.ShapeDtypeStruct((n_idx, d), table.dtype), mesh=mesh,
        scratch_types=[pltpu.VMEM((SC_INDEX_BLOCK,), jnp.int32)] + slot_types * 2,
        name="sc_gather_rows")
    def gather(x_hbm, i_hbm, o_hbm, idx_v, *scratch):
        wid = lax.axis_index("subcore") * SC_CORES + lax.axis_index("core")
        base = wid * per_w
        slots = (scratch[:3], scratch[3:])

        @pl.loop(0, per_w // SC_INDEX_BLOCK)
        def _(g):
            off = base + g * SC_INDEX_BLOCK
            pltpu.sync_copy(i_hbm.at[pl.ds(off, SC_INDEX_BLOCK)], idx_v)

            def start_gather(s):
                rows_v, sem_in, _ = slots[s % 2]
                return pltpu.async_copy(x_hbm.at[idx_v.at[pl.ds(s * w, w)]], rows_v, sem_in)

            gathers = {0: start_gather(0)}
            writes = {}
            for s in range(n_sub):
                if s + 1 < n_sub:
                    if s >= 1:
                        writes[s - 1].wait()
                    gathers[s + 1] = start_gather(s + 1)
                gathers[s].wait()
                rows_v, _, sem_out = slots[s % 2]
                writes[s] = pltpu.async_copy(rows_v, o_hbm.at[pl.ds(off + s * w, w)], sem_out)
            writes[n_sub - 2].wait()
            writes[n_sub - 1].wait()

    return gather(table, idx)


def _expert_ffn_kernel(te_ref, nv_ref, x_ref, wg_ref, wu_ref, wd_ref, y_out):
    n_valid = nv_ref[pl.program_id(0)]

    @pl.when(n_valid > 0)
    def _():
        row = lax.broadcasted_iota(jnp.int32, x_ref.shape, 0)
        h = _unpack_pairs(jnp.where(row < n_valid, x_ref[...], 0.0))
        f = None
        for cidx in range(D_FF // FF_CHUNK):
            cols = slice(cidx * FF_CHUNK, (cidx + 1) * FF_CHUNK)
            a = _dot(h, wg_ref[:, cols].astype(BF16))
            u = _dot(h, wu_ref[:, cols].astype(BF16))
            fc = _dot(((a * jax.nn.sigmoid(a)) * u).astype(BF16), wd_ref[cols, :].astype(BF16))
            f = fc if f is None else f + fc
        y_out[...] = f

    @pl.when(n_valid == 0)
    def _():
        y_out[...] = jnp.zeros_like(y_out)


def _expert_ffn(xs, tile_expert, tile_valid, wg, wu, wd):
    tm = MOE_ROW_TILE
    wspec = lambda shape: pl.BlockSpec((None,) + shape, lambda j, te, nu: (te[j], 0, 0),
                                       pipeline_mode=pl.Buffered(1))
    return pl.pallas_call(
        _expert_ffn_kernel,
        out_shape=jax.ShapeDtypeStruct((MOE_ROWS, D_MODEL), F32),
        grid_spec=pltpu.PrefetchScalarGridSpec(
            num_scalar_prefetch=2,
            grid=(MOE_ROWS // tm,),
            in_specs=[pl.BlockSpec((tm, xs.shape[1]), lambda j, te, nu: (j, 0)),
                      wspec((D_MODEL, D_FF)), wspec((D_MODEL, D_FF)), wspec((D_FF, D_MODEL))],
            out_specs=pl.BlockSpec((tm, D_MODEL), lambda j, te, nu: (j, 0)),
        ),
        compiler_params=_cparams(("arbitrary",)),
        name="moe_expert_ffn",
    )(tile_expert, tile_valid, xs, wg, wu, wd)


def _moe_combine_kernel(y1_ref, y2_ref, wsel_ref, x_ref, g_ref, gate_ref, *rest):
    x_out = rest[-1]
    w = wsel_ref[...]
    f = w[:, 0:1] * y1_ref[...] + w[:, 1:2] * y2_ref[...]
    x_out[...] = x_ref[...] + gate_ref[...] * _rms(f, g_ref[...])


def _moe_combine(yg, wsel, x, mod5, layer, g3, *, tok_off, out_rows, out_off, y_prev=None):
    t = TOK_TILE
    nt = MOE_SEG // t
    off = tok_off // t
    ooff = out_off // t
    tpg = NP_TOK // t
    tok = lambda w: pl.BlockSpec((t, w), lambda i: (off + i, 0))
    in_specs = [_tok_spec(t, D_MODEL), pl.BlockSpec((t, D_MODEL), lambda i: (nt + i, 0)), tok(LANES), tok(D_MODEL),
                _const_spec((1, D_MODEL)),
                pl.BlockSpec((None, None, None, 1, D_MODEL), lambda i: (layer, (off + i) // tpg, 5, 0, 0))]
    args = [yg, yg, wsel, x, g3, mod5]
    aliases = {}
    if y_prev is not None:
        in_specs.append(pl.BlockSpec(memory_space=pl.ANY))
        args.append(y_prev)
        aliases = {len(args) - 1: 0}
    return pl.pallas_call(
        _moe_combine_kernel,
        out_shape=jax.ShapeDtypeStruct((out_rows, D_MODEL), F32),
        grid=(nt,),
        in_specs=in_specs,
        out_specs=pl.BlockSpec((t, D_MODEL), lambda i: (ooff + i, 0)),
        input_output_aliases=aliases,
        compiler_params=_cparams(("parallel",)),
        name="moe_combine",
    )(*args)


def _moe(h, x, routing, wg, wu, wd, mod5, layer, g3):
    wsel, isel, rank, cnt = routing
    pos1, pos2, tile_expert, tile_valid = _route_tables(isel, rank, cnt)
    xs = _scatter_rows(h, pos1, pos2, MOE_ROWS)
    ys = _expert_ffn(xs, tile_expert, tile_valid, wg, wu, wd)
    y_p = y_s = None
    for seg in range(N_TOK // MOE_SEG):
        rows = slice(seg * MOE_SEG, (seg + 1) * MOE_SEG)
        yg = _gather_rows(ys, jnp.concatenate([pos1[rows], pos2[rows]]))
        common = dict(tok_off=seg * MOE_SEG)
        if seg * MOE_SEG < NP_TOK:
            y_p = _moe_combine(yg, wsel, x, mod5, layer, g3, out_rows=NP_TOK, out_off=seg * MOE_SEG, y_prev=y_p, **common)
        else:
            y_s = _moe_combine(yg, wsel, x, mod5, layer, g3, out_rows=NS_TOK, out_off=seg * MOE_SEG - NP_TOK,
                               y_prev=y_s, **common)
    return y_p, y_s


def _rot_cols(w, half):
    k, n = w.shape
    wb = w.reshape(k, n // (2 * half), 2, half)
    return jnp.stack([-wb[:, :, 1], wb[:, :, 0]], axis=2).reshape(k, n)


def _axis_tables(r, pos):
    inv = np.float32(ROPE_BASE) ** (-np.arange(0, r, 2, dtype=np.float32) / np.float32(r))
    ang = pos.astype(np.float32)[:, None] * inv[None, :]
    cos, sin = np.cos(ang), np.sin(ang)
    return np.concatenate([cos, cos], axis=1), np.concatenate([sin, sin], axis=1)


def _rope_tables(r):
    s = np.arange(DEC_SEQ)
    cr, sr = _axis_tables(r // 2, s // GRID_W)
    cc, sc = _axis_tables(r // 2, s % GRID_W)
    return np.concatenate([cr, cc], axis=1), np.concatenate([sr, sc], axis=1)


def _with_identity(tab, ident):
    return np.concatenate([np.full((TOK_TILE, tab.shape[1]), ident, np.float32), tab], axis=0)


@functools.lru_cache(maxsize=None)
def _rope_constants():
    c32, s32 = _rope_tables(MLA_ROPE)
    ones = np.ones((DEC_SEQ, MLA_NOPE), np.float32)
    pad1 = np.ones((DEC_SEQ, MLA_HEAD_PAD - MLA_NOPE - MLA_ROPE), np.float32)
    cq = np.concatenate([ones, c32, pad1], axis=1)
    sq = np.concatenate([0 * ones, s32, 0 * pad1], axis=1)
    c64, s64 = _rope_tables(GQA_HEAD_DIM)
    return {
        "mla_cq": _with_identity(cq, 1.0), "mla_sq": _with_identity(sq, 0.0),
        "mla_ck": _with_identity(c32, 1.0), "mla_sk": _with_identity(s32, 0.0),
        "gqa_c": _with_identity(np.concatenate([c64, c64], axis=1), 1.0),
        "gqa_s": _with_identity(np.concatenate([s64, s64], axis=1), 0.0),
    }


def _prep_tables():
    return {k: jnp.asarray(v, F32) for k, v in _rope_constants().items()}


def _prep_even(w_in, q_norm, w_q_up, kv_norm, w_kv_up, wgf, bgf, wgb, bgb):
    sizes = [MLA_Q_RANK, MLA_KV_RANK, MLA_ROPE, GLA_HEADS * GLA_DK, GLA_HEADS * GLA_DK,
             GLA_HEADS * GLA_DV, GLA_HEADS * GLA_DV, GLA_GATE_RANK, GLA_GATE_RANK]
    cq, ckv, kpe, gq, gk, gv, gr, gaf, gab = jnp.split(w_in, [int(s) for s in np.cumsum(sizes)[:-1]], axis=1)
    pad = jnp.zeros((D_MODEL, LANES - 2 * MLA_ROPE - 2 * GLA_GATE_RANK), F32)
    win = jnp.concatenate([cq, ckv, gq, gk, gv, gr, kpe, _rot_cols(kpe, MLA_ROPE // 4), gaf, gab, pad], axis=1)

    wq = w_q_up.reshape(MLA_Q_RANK, MLA_HEADS, MLA_NOPE + MLA_ROPE)
    nope, pe = wq[..., :MLA_NOPE], wq[..., MLA_NOPE:]
    pe_rot = _rot_cols(pe.reshape(MLA_Q_RANK, MLA_HEADS * MLA_ROPE), MLA_ROPE // 4).reshape(pe.shape)
    zpad = jnp.zeros((MLA_Q_RANK, MLA_HEADS, MLA_HEAD_PAD - MLA_NOPE - MLA_ROPE), F32)
    wq_main = jnp.concatenate([nope, pe, zpad], axis=-1).reshape(MLA_Q_RANK, MLA_QK_W)
    wq_rot = jnp.concatenate([0 * nope, pe_rot, zpad], axis=-1).reshape(MLA_Q_RANK, MLA_QK_W)

    wkv = w_kv_up.reshape(MLA_KV_RANK, MLA_HEADS, MLA_NOPE + MLA_V)
    knope, vv = wkv[..., :MLA_NOPE], wkv[..., MLA_NOPE:]
    wkk = jnp.concatenate([knope, jnp.zeros((MLA_KV_RANK, MLA_HEADS, MLA_HEAD_PAD - MLA_NOPE), F32)],
                          axis=-1).reshape(MLA_KV_RANK, MLA_QK_W)
    vpair = vv.reshape(MLA_KV_RANK, MLA_HEADS // 2, 2 * MLA_V)
    wkv_ext = jnp.concatenate([vpair, jnp.zeros((MLA_KV_RANK, MLA_HEADS // 2, LANES), F32)],
                              axis=-1).reshape(MLA_KV_RANK, MLA_VEXT_W)
    vbias = jnp.tile(jnp.concatenate([jnp.zeros((LANES,), F32), jnp.ones((LANES,), F32)]),
                     MLA_HEADS // 2).reshape(1, MLA_VEXT_W)
    epl = jnp.tile(jnp.concatenate([jnp.zeros((MLA_ROPE, MLA_NOPE), F32), jnp.eye(MLA_ROPE, dtype=F32),
                                    jnp.zeros((MLA_ROPE, MLA_HEAD_PAD - MLA_NOPE - MLA_ROPE), F32)], axis=1),
                   (1, MLA_HEADS))

    def gate_w(w, off):
        return jnp.zeros((LANES, GLA_HEADS * GLA_DK), F32).at[off:off + GLA_GATE_RANK].set(w)

    r = np.arange(CUMSUM_BLOCK)
    same = (r[:, None] // GLA_CHUNK) == (r[None, :] // GLA_CHUNK)
    lmat = jnp.asarray(same & (r[:, None] >= r[None, :]), BF16)
    umat = jnp.asarray(same & (r[:, None] <= r[None, :]), BF16)
    return {
        "win": win.astype(BF16), "qn": q_norm.reshape(1, -1), "wq": jnp.concatenate([wq_main, wq_rot], axis=1).astype(BF16),
        "kvn": kv_norm.reshape(1, -1), "wkk": wkk.astype(BF16), "wkv": wkv_ext.astype(BF16), "vbias": vbias,
        "epl": epl.astype(BF16), "wgf": gate_w(wgf, _S_GAF).astype(BF16), "bgf": bgf.reshape(1, -1),
        "wgb": gate_w(wgb, _S_GAB).astype(BF16), "bgb": bgb.reshape(1, -1), "lmat": lmat, "umat": umat,
    }


def _gqa_head_perm():
    heads = []
    for p in range(GQA_KV_HEADS // 2):
        for i in range(GQA_GROUP):
            heads += [(2 * p) * GQA_GROUP + i, (2 * p + 1) * GQA_GROUP + i]
    return np.asarray(heads)


def _prep_odd(w_in, w_out):
    perm = _gqa_head_perm()
    wq = w_in[:, :GQA_Q_W].reshape(D_MODEL, GQA_HEADS, GQA_HEAD_DIM)[:, perm].reshape(D_MODEL, GQA_Q_W)
    wk = w_in[:, GQA_Q_W:GQA_Q_W + GQA_KV_W]
    wv = w_in[:, GQA_Q_W + GQA_KV_W:].reshape(D_MODEL, GQA_KV_HEADS // 2, 2 * GQA_HEAD_DIM)
    wv_ext = jnp.concatenate([wv, jnp.zeros((D_MODEL, GQA_KV_HEADS // 2, LANES), F32)], axis=-1).reshape(D_MODEL, GQA_VEXT_W)
    win = jnp.concatenate([wq, _rot_cols(wq, GQA_HEAD_DIM // 4), wk, _rot_cols(wk, GQA_HEAD_DIM // 4), wv_ext], axis=1)
    vbias = jnp.tile(jnp.concatenate([jnp.zeros((LANES,), F32), jnp.ones((LANES,), F32)]),
                     GQA_KV_HEADS // 2).reshape(1, GQA_VEXT_W)
    wo = w_out.reshape(GQA_HEADS, GQA_HEAD_DIM, D_MODEL)[perm].reshape(GQA_Q_W, D_MODEL)
    return win.astype(BF16), vbias, wo.astype(BF16)


def _ext_v(v):
    rows = v.shape[0]
    vp = v.reshape(rows, GQA_KV_HEADS // 2, 2 * GQA_HEAD_DIM)
    return jnp.concatenate([vp, jnp.ones((rows, GQA_KV_HEADS // 2, LANES), v.dtype)], axis=-1).reshape(rows, GQA_VEXT_W)


def kernel(x_prompt, x_sample, cache_mla_ckv, cache_mla_kpe, state_gla_fwd, state_gla_bwd, cache_gqa_k, cache_gqa_v, c, c_ctx, w_mod, b_mod, norm_g, w_in_ab, mla_q_norm, mla_w_q_up, mla_kv_norm, mla_w_kv_up, gla_w_gate_f, gla_b_gate_f, gla_w_gate_b, gla_b_gate_b, gla_norm, w_out_ab, ffn_w_gate, ffn_w_up, ffn_w_down, w_in_c, gqa_sink, w_out_c, moe_w_router, moe_b_router, moe_w_gate, moe_w_up, moe_w_down):
    x_in = (x_prompt.reshape(NP_TOK, D_MODEL), x_sample.reshape(NS_TOK, D_MODEL))
    cvec =jnp.concatenate([c_ctx[None, :], c, jnp.zeros((MOD_ROWS - N_GROUPS, D_MODEL), F32)], axis=0)
    mod5 = _modulation(cvec, w_mod, b_mod).reshape(DEPTH, MOD_ROWS, N_MOD, 1, D_MODEL)
    tabs = _prep_tables()
    gvec = lambda l, j: norm_g[l, j].reshape(1, D_MODEL)

    wts = _prep_even(w_in_ab[0], mla_q_norm[0], mla_w_q_up[0], mla_kv_norm[0], mla_w_kv_up[0],
                     gla_w_gate_f[0], gla_b_gate_f[0], gla_w_gate_b[0], gla_b_gate_b[0])
    (q, k, v, ckv, kpe, gq, gk, gv, gr, bf, bb) = _even_in_proj(*x_in, mod5, 0, gvec(0, 0), wts, tabs)
    kc, vc = _cache_kv(cache_mla_ckv[:, 0].reshape(DEC_BATCH * PAST_LEN, MLA_KV_RANK),
                       cache_mla_kpe[:, 0].reshape(DEC_BATCH * PAST_LEN, MLA_ROPE), wts)
    oa_p = _mla_attention(q, [k], [v], n_batch=BATCH, seq_q=SEQ, q_tile=SEQ, tok_off=0, k_batch_rows=[SEQ],
                          pairs=MLA_HEADS // 2)
    oa_s = _mla_attention(q, [k, kc], [v, vc], n_batch=DEC_BATCH, seq_q=DEC_SEQ, q_tile=MLA_Q_TILE,
                          tok_off=NP_TOK, k_batch_rows=[DEC_SEQ, PAST_LEN], pairs=1)
    gn = gla_norm[0].reshape(1, GLA_DV)
    zero_state = jnp.zeros((BATCH, GLA_HEADS, GLA_DK, GLA_DV), F32)
    ob_p, sf, sb = _gla(gq, gk, gv, gr, bf, bb, zero_state, zero_state, gn, n_batch=BATCH, seq=SEQ, tok_off=0)
    ob_s, _, _ = _gla(gq, gk, gv, gr, bf, bb, state_gla_fwd[:, 0], state_gla_bwd[:, 0], gn,
                      n_batch=DEC_BATCH, seq=DEC_SEQ, tok_off=NP_TOK)
    x, h = _out_proj([(oa_p, oa_s), (ob_p, ob_s)], w_out_ab[0].astype(BF16), x_in, mod5, 0, gvec(0, 1),
                     gvec(0, 2), BF16)
    x = _ffn(h, x, ffn_w_gate[0], ffn_w_up[0], ffn_w_down[0], mod5, 0, gvec(0, 3))

    win_c, vbias_c, wo_c = _prep_odd(w_in_c[0], w_out_c[0])
    qg, kgb, vgb, kg_t, vg_t = _odd_in_proj(x, mod5, 1, gvec(1, 0), win_c, vbias_c, tabs["gqa_c"], tabs["gqa_s"])
    sink = gqa_sink[0]
    og_p = _gqa_attention(sink, qg, None, None, kgb, vgb, n_batch=BATCH, seq_q=SEQ, q_tile=SEQ, tok_off=0, n_ctx=SEQ, local=False)
    kc_g = cache_gqa_k[:, 0].reshape(DEC_BATCH * PAST_LEN, GQA_KV_W).astype(BF16)
    vc_g = _ext_v(cache_gqa_v[:, 0].reshape(DEC_BATCH * PAST_LEN, GQA_KV_W)).astype(BF16)
    og_s = _gqa_attention(sink, qg, kgb, vgb, kc_g, vc_g, n_batch=DEC_BATCH, seq_q=DEC_SEQ, q_tile=GQA_Q_TILE,
                          tok_off=NP_TOK, n_ctx=PAST_LEN, local=True)
    w_r = jnp.zeros((D_MODEL, LANES), F32).at[:, :N_EXPERTS].set(moe_w_router[0]).astype(BF16)
    b_r = jnp.zeros((1, LANES), F32).at[0, :N_EXPERTS].set(moe_b_router[0])
    x, h, *routing = _out_proj([(og_p, og_s)], wo_c, x, mod5, 1, gvec(1, 1), gvec(1, 2), F32, router=(w_r, b_r))
    y_p, y_s = _moe(h, x, routing, moe_w_gate[0], moe_w_up[0], moe_w_down[0], mod5, 1, gvec(1, 3))

    y_prompt = y_p.reshape(BATCH, SEQ, D_MODEL)
    y_sample = y_s.reshape(DEC_BATCH, DEC_SEQ, D_MODEL)
    new_ckv = ckv[:NP_TOK].reshape(BATCH, 1, SEQ, MLA_KV_RANK)
    new_kpe = kpe[:NP_TOK].reshape(BATCH, 1, SEQ, MLA_ROPE)
    as_cache = lambda a: jnp.transpose(a.reshape(BATCH, 1, GQA_KV_HEADS, GQA_HEAD_DIM, SEQ), (0, 1, 4, 2, 3))
    new_k = as_cache(kg_t)
    new_v = as_cache(vg_t)
    return (y_prompt, y_sample, new_ckv, new_kpe, sf[:, None], sb[:, None], new_k, new_v)
```

```python
import functools

import jax
import jax.numpy as jnp
import numpy as np
from jax import lax
from jax.experimental import pallas as pl
from jax.experimental.pallas import tpu as pltpu
from jax.experimental.pallas import tpu_sc as plsc

F32 = jnp.float32
BF16 = jnp.bfloat16

D_MODEL = 1024
BATCH = 16
SEQ = 256
DEPTH = 2
DEC_BATCH = 4
DEC_SEQ = 4096
PAST_LEN = 256
GRID_W = 64
N_MOD = 6
EPS = 1e-6
ROPE_BASE = 10000.0
NEG_INF = -1e30

MLA_HEADS = 8
MLA_NOPE = 64
MLA_ROPE = 32
MLA_V = 64
MLA_Q_RANK = 384
MLA_KV_RANK = 256
GLA_HEADS = 4
GLA_DK = 64
GLA_DV = 128
GLA_GATE_RANK = 16
GLA_GATE_NORM = 16.0
GLA_CHUNK = 64
GQA_HEADS = 16
GQA_KV_HEADS = 4
GQA_GROUP = GQA_HEADS // GQA_KV_HEADS
GQA_HEAD_DIM = 64
WINDOW = 128
D_FF = 2816
N_EXPERTS = 8
TOP_K = 2

NP_TOK = BATCH * SEQ
NS_TOK = DEC_BATCH * DEC_SEQ
N_TOK = NP_TOK + NS_TOK
N_GROUPS = 1 + DEC_BATCH
MOD_ROWS = 8

LANES = 128
MXU_COLS = 256
VMEM_LIMIT_BYTES = 56 * 1024 * 1024

TOK_TILE = 512
MOD_COL_BLOCK = 1536
CUMSUM_BLOCK = 256
GLA_BLOCK_CHUNKS = MXU_COLS // GLA_CHUNK
GLA_CHUNK_UNROLL = 8
MLA_Q_TILE = 2048
MLA_Q_SUB = 256
GQA_Q_TILE = 256
MOE_ROW_TILE = 512
MOE_ROWS = TOP_K * N_TOK + N_EXPERTS * MOE_ROW_TILE
MOE_SEG = 4096
SC_CORES = 2
SC_SUBCORES = 16
SC_WORKERS = SC_CORES * SC_SUBCORES
SC_INDEX_BLOCK = 128
SC_GATHER_WINDOW = 32
FF_CHUNK = 1408

_C_CQ = 0
_C_CKV = _C_CQ + MLA_Q_RANK
_C_GQ = _C_CKV + MLA_KV_RANK
_C_GK = _C_GQ + GLA_HEADS * GLA_DK
_C_GV = _C_GK + GLA_HEADS * GLA_DK
_C_GR = _C_GV + GLA_HEADS * GLA_DV
_C_SMALL = _C_GR + GLA_HEADS * GLA_DV
IN_AB_EXT = _C_SMALL + LANES
_S_KPE, _S_KPER, _S_GAF, _S_GAB = 0, MLA_ROPE, 2 * MLA_ROPE, 2 * MLA_ROPE + GLA_GATE_RANK
MLA_HEAD_PAD = LANES
MLA_QK_W = MLA_HEADS * MLA_HEAD_PAD
MLA_VEXT_W = (MLA_HEADS // 2) * 2 * LANES


def _cparams(semantics):
    return pltpu.CompilerParams(dimension_semantics=semantics, vmem_limit_bytes=VMEM_LIMIT_BYTES)


def _const_spec(shape):
    nd = len(shape)
    return pl.BlockSpec(shape, lambda *_: (0,) * nd, pipeline_mode=pl.Buffered(1))


def _log_sigmoid(x):
    return jnp.minimum(x, 0.0) - jnp.log1p(jnp.exp(-jnp.abs(x)))


def _rms(x, g):
    return (x * lax.rsqrt(jnp.mean(x * x, axis=-1, keepdims=True) + EPS)) * g


def _modulate(x, g, shift, scale):
    return _rms(x, g) * (1.0 + scale) + shift


def _dot(a, b):
    return jnp.dot(a, b, preferred_element_type=F32)


def _dot_nt(a, b):
    return lax.dot_general(a, b, (((1,), (1,)), ((), ())), preferred_element_type=F32)


def _dot_tn(a, b):
    return lax.dot_general(a, b, (((0,), (0,)), ((), ())), preferred_element_type=F32)


def _split3(x):
    hi = x.astype(BF16)
    r1 = x - hi.astype(F32)
    mid = r1.astype(BF16)
    lo = (r1 - mid.astype(F32)).astype(BF16)
    return hi, mid, lo


def _lane_tile(x, reps):
    return jnp.concatenate([x] * reps, axis=1)


_HI16 = np.uint32(0xFFFF0000)


def _pack_pairs(xb):
    w = xb.shape[1] // 2
    xf = xb.astype(F32)
    lo = lax.bitcast_convert_type(xf[:, :w], jnp.uint32) >> 16
    hi = lax.bitcast_convert_type(xf[:, w:], jnp.uint32) & _HI16
    return lax.bitcast_convert_type(lo | hi, F32)


def _unpack_pairs(words):
    u = lax.bitcast_convert_type(words, jnp.uint32)
    lo = lax.bitcast_convert_type(u << 16, F32)
    hi = lax.bitcast_convert_type(u & _HI16, F32)
    return jnp.concatenate([lo, hi], axis=1).astype(BF16)


def _mod_kernel(c_ref, w_ref, b_ref, o_ref):
    c = c_ref[...]
    s = c * jax.nn.sigmoid(c)
    o_ref[...] = _dot(s.astype(BF16), w_ref[...].astype(BF16)) + b_ref[...]


def _modulation(cvec, w_mod, b_mod):
    ncol = N_MOD * D_MODEL
    blk = MOD_COL_BLOCK
    return pl.pallas_call(
        _mod_kernel,
        out_shape=jax.ShapeDtypeStruct((DEPTH, MOD_ROWS, ncol), F32),
        grid=(DEPTH, ncol // blk),
        in_specs=[
            pl.BlockSpec((MOD_ROWS, D_MODEL), lambda l, j: (0, 0)),
            pl.BlockSpec((None, D_MODEL, blk), lambda l, j: (l, 0, j)),
            pl.BlockSpec((None, 1, blk), lambda l, j: (l, 0, j)),
        ],
        out_specs=pl.BlockSpec((None, MOD_ROWS, blk), lambda l, j: (l, 0, j)),
        compiler_params=_cparams(("arbitrary", "arbitrary")),
        name="modulation",
    )(cvec, w_mod, b_mod.reshape(DEPTH, 1, ncol))


def _mod_spec(layer, j, tile):
    tpg = NP_TOK // tile
    return pl.BlockSpec((None, None, None, 1, D_MODEL), lambda i: (layer, i // tpg, j, 0, 0))


def _tok_spec(tile, width):
    return pl.BlockSpec((tile, width), lambda i: (i, 0))


def _split_specs(tile, width):
    npt = NP_TOK // tile
    return [pl.BlockSpec((tile, width), lambda i: (jnp.minimum(i, npt - 1), 0)),
            pl.BlockSpec((tile, width), lambda i: (jnp.maximum(i - npt, 0), 0))]


def _pick(tile, p_ref, s_ref):
    return jnp.where(pl.program_id(0) < NP_TOK // tile, p_ref[...], s_ref[...])


def _rope_row_spec(tile, width):
    npt = NP_TOK // tile
    spt = DEC_SEQ // tile
    return pl.BlockSpec((tile, width), lambda i: (jnp.where(i < npt, 0, 1 + (i - npt) % spt), 0))


def _even_in_kernel(xp_ref, xs_ref, g_ref, shift_ref, scale_ref, win_ref, qn_ref, wq_ref, kvn_ref, wkk_ref,
                    wkv_ref, vbias_ref, epl_ref, wgf_ref, bgf_ref, wgb_ref, bgb_ref, lmat_ref,
                    umat_ref, cq_ref, sq_ref, ck_ref, sk_ref,
                    q_out, k_out, v_out, ckv_out, kpe_out, gq_out, gk_out, gv_out, gr_out,
                    bf_out, bb_out):
    h = _modulate(_pick(TOK_TILE, xp_ref, xs_ref), g_ref[...], shift_ref[...], scale_ref[...])
    z = _dot(h.astype(BF16), win_ref[...])

    cqn = _rms(z[:, _C_CQ:_C_CQ + MLA_Q_RANK], qn_ref[...]).astype(BF16)
    qf = _dot(cqn, wq_ref[...])
    cq_t = _lane_tile(cq_ref[...], MLA_HEADS)
    sq_t = _lane_tile(sq_ref[...], MLA_HEADS)
    q_out[...] = (qf * cq_t + _rotate_half_lanes(qf, MLA_ROPE // 4) * sq_t).astype(BF16)

    ckvn = _rms(z[:, _C_CKV:_C_CKV + MLA_KV_RANK], kvn_ref[...])
    ckv_out[...] = ckvn
    small = z[:, _C_SMALL:_C_SMALL + LANES]
    kpe = (small[:, _S_KPE:_S_KPE + MLA_ROPE] * ck_ref[...]
           + small[:, _S_KPER:_S_KPER + MLA_ROPE] * sk_ref[...])
    kpe_out[...] = kpe
    ckvn_b = ckvn.astype(BF16)
    k_out[...] = (_dot(ckvn_b, wkk_ref[...]) + _dot(kpe.astype(BF16), epl_ref[...])).astype(BF16)
    v_out[...] = (_dot(ckvn_b, wkv_ref[...]) + vbias_ref[...]).astype(BF16)

    gq_out[...] = z[:, _C_GQ:_C_GQ + GLA_HEADS * GLA_DK] * (GLA_DK ** -0.5)
    gk_out[...] = z[:, _C_GK:_C_GK + GLA_HEADS * GLA_DK]
    gv_out[...] = z[:, _C_GV:_C_GV + GLA_HEADS * GLA_DV].astype(BF16)
    gr_out[...] = z[:, _C_GR:_C_GR + GLA_HEADS * GLA_DV]

    small_b = small.astype(BF16)
    la_f = _log_sigmoid(_dot(small_b, wgf_ref[...]) + bgf_ref[...]) * (1.0 / GLA_GATE_NORM)
    la_b = _log_sigmoid(_dot(small_b, wgb_ref[...]) + bgb_ref[...]) * (1.0 / GLA_GATE_NORM)
    lmat = lmat_ref[...]
    umat = umat_ref[...]
    for r in range(TOK_TILE // CUMSUM_BLOCK):
        rows = slice(r * CUMSUM_BLOCK, (r + 1) * CUMSUM_BLOCK)
        f_hi, f_mid, f_lo = _split3(la_f[rows])
        bf_out[rows, :] = _dot(lmat, f_hi) + _dot(lmat, f_mid) + _dot(lmat, f_lo)
        b_hi, b_mid, b_lo = _split3(la_b[rows])
        bb_out[rows, :] = _dot(umat, b_hi) + _dot(umat, b_mid) + _dot(umat, b_lo)


def _even_in_proj(xp, xs, mod5, layer, g, wts, tabs):
    t = TOK_TILE
    out_widths = [(MLA_QK_W, BF16), (MLA_QK_W, BF16), (MLA_VEXT_W, BF16), (MLA_KV_RANK, F32),
                  (MLA_ROPE, F32), (GLA_HEADS * GLA_DK, F32), (GLA_HEADS * GLA_DK, F32),
                  (GLA_HEADS * GLA_DV, BF16), (GLA_HEADS * GLA_DV, F32),
                  (GLA_HEADS * GLA_DK, F32), (GLA_HEADS * GLA_DK, F32)]
    const_names = ["win", "qn", "wq", "kvn", "wkk", "wkv", "vbias", "epl", "wgf", "bgf", "wgb",
                   "bgb", "lmat", "umat"]
    consts = [wts[n] for n in const_names]
    in_specs = (_split_specs(t, D_MODEL)
                + [_const_spec((1, D_MODEL)), _mod_spec(layer, 0, t), _mod_spec(layer, 1, t)]
                + [_const_spec(c.shape) for c in consts]
                + [_rope_row_spec(t, LANES), _rope_row_spec(t, LANES),
                   _rope_row_spec(t, MLA_ROPE), _rope_row_spec(t, MLA_ROPE)])
    return pl.pallas_call(
        _even_in_kernel,
        out_shape=[jax.ShapeDtypeStruct((N_TOK, w), dt) for w, dt in out_widths],
        grid=(N_TOK // t,),
        in_specs=in_specs,
        out_specs=[_tok_spec(t, w) for w, _ in out_widths],
        compiler_params=_cparams(("parallel",)),
        name="even_in_proj",
    )(xp, xs, g, mod5, mod5, *consts, tabs["mla_cq"], tabs["mla_sq"], tabs["mla_ck"], tabs["mla_sk"])


def _cache_kv_kernel(ckv_ref, kpe_ref, wkk_ref, wkv_ref, vbias_ref, epl_ref, k_out, v_out):
    ckv_b = ckv_ref[...].astype(BF16)
    k_out[...] = (_dot(ckv_b, wkk_ref[...]) + _dot(kpe_ref[...].astype(BF16), epl_ref[...])).astype(BF16)
    v_out[...] = (_dot(ckv_b, wkv_ref[...]) + vbias_ref[...]).astype(BF16)


def _cache_kv(ckv, kpe, wts):
    n = ckv.shape[0]
    consts = [wts[k] for k in ("wkk", "wkv", "vbias", "epl")]
    return pl.pallas_call(
        _cache_kv_kernel,
        out_shape=[jax.ShapeDtypeStruct((n, MLA_QK_W), BF16), jax.ShapeDtypeStruct((n, MLA_VEXT_W), BF16)],
        grid=(1,),
        in_specs=[_const_spec(ckv.shape), _const_spec(kpe.shape)] + [_const_spec(c.shape) for c in consts],
        out_specs=[_const_spec((n, MLA_QK_W)), _const_spec((n, MLA_VEXT_W))],
        compiler_params=_cparams(("arbitrary",)),
        name="mla_cache_kv",
    )(ckv, kpe, *consts)


def _mla_attn_kernel(*refs, n_seg):
    q_ref = refs[0]
    k_refs = refs[1:1 + n_seg]
    v_refs = refs[1 + n_seg:1 + 2 * n_seg]
    o_ref = refs[1 + 2 * n_seg]
    scale = (MLA_NOPE + MLA_ROPE) ** -0.5
    c = scale * float(np.log2(np.e))
    tq = q_ref.shape[0]
    q_sub = min(tq, MLA_Q_SUB)
    kt = MXU_COLS
    tiles = [(si, r0) for si, k in enumerate(k_refs) for r0 in range(0, k.shape[0], kt)]
    lane = lax.broadcasted_iota(jnp.int32, (q_sub, LANES), 1)
    n_pairs = o_ref.shape[1] // LANES
    for pi, qs in [(pi, qs) for pi in range(n_pairs) for qs in range(tq // q_sub)]:
        rows = slice(qs * q_sub, (qs + 1) * q_sub)
        vl = slice(pi * 2 * LANES, (pi + 1) * 2 * LANES)
        res = []
        for j in range(2):
            hl = slice((2 * pi + j) * LANES, (2 * pi + j + 1) * LANES)
            qj = q_ref[rows, hl]
            macc = None
            s_tiles = []
            for si, r0 in tiles:
                s = _dot_nt(qj, k_refs[si][r0:r0 + kt, hl])
                s_tiles.append(s)
                mt = jnp.maximum(s[:, :LANES], s[:, LANES:])
                macc = mt if macc is None else jnp.maximum(macc, mt)
            m = macc.max(axis=-1, keepdims=True)
            r = None
            for (si, r0), s in zip(tiles, s_tiles):
                p = jnp.exp2((s - m) * c).astype(BF16)
                rj = _dot(p, v_refs[si][r0:r0 + kt, vl])
                r = rj if r is None else r + rj
            res.append(r[:, :LANES] / r[:, LANES:])
        o_ref[rows, pi * LANES:(pi + 1) * LANES] = jnp.where(lane < MLA_V, res[0], res[1]).astype(BF16)


def _mla_attention(q, ks, vs, *, n_batch, seq_q, q_tile, tok_off, k_batch_rows, pairs):
    n_seg = len(ks)
    nq = seq_q // q_tile
    qoff = tok_off // q_tile
    grid = (n_batch, MLA_HEADS // 2 // pairs, nq)
    wq = pairs * 2 * LANES
    in_specs = [pl.BlockSpec((q_tile, wq), lambda b, hp, i: (qoff + b * nq + i, hp))]
    for s in range(n_seg):
        rows = k_batch_rows[s]
        off = (tok_off // rows) if s == 0 else 0
        in_specs.append(pl.BlockSpec((rows, wq), functools.partial(lambda b, hp, i, off: (off + b, hp), off=off)))
    for s in range(n_seg):
        rows = k_batch_rows[s]
        off = (tok_off // rows) if s == 0 else 0
        in_specs.append(pl.BlockSpec((rows, wq), functools.partial(lambda b, hp, i, off: (off + b, hp), off=off)))
    return pl.pallas_call(
        functools.partial(_mla_attn_kernel, n_seg=n_seg),
        out_shape=jax.ShapeDtypeStruct((n_batch * seq_q, MLA_HEADS * MLA_V), BF16),
        grid=grid,
        in_specs=in_specs,
        out_specs=pl.BlockSpec((q_tile, pairs * LANES), lambda b, hp, i: (b * nq + i, hp)),
        compiler_params=_cparams(("parallel", "parallel", "arbitrary")),
        name=f"mla_attention_{n_seg}seg",
    )(q, *ks, *vs)


def _gla_kernel(q_ref, k_ref, v_ref, gr_ref, bf_ref, bb_ref, s0f_ref, s0b_ref, gn_ref,
                o_ref, sf_ref, sb_ref, acc_ref, kdf_ref, kdb_ref, qdf_ref, qdb_ref, hist_ref, *, n_chunks):
    c = GLA_CHUNK
    cpb = min(n_chunks, GLA_BLOCK_CHUNKS)
    blk = cpb * c
    lane = lax.broadcasted_iota(jnp.int32, (blk, LANES), 1)
    lo = lane < GLA_DK
    row = lax.broadcasted_iota(jnp.int32, (blk, blk), 0)
    col = lax.broadcasted_iota(jnp.int32, (blk, blk), 1)
    chunk_bits = c.bit_length() - 1
    same_chunk = jnp.right_shift(row, chunk_bits) == jnp.right_shift(col, chunk_bits)
    tril = same_chunk & (row >= col)
    triu = same_chunk & (row <= col)
    zero_blk = jnp.zeros((GLA_DK, GLA_DV), F32)

    def pair_state_t(s_ref):
        blockdiag = jnp.concatenate(
            [jnp.concatenate([s_ref[0], zero_blk], axis=1),
             jnp.concatenate([zero_blk, s_ref[1]], axis=1)], axis=0)
        return blockdiag.T

    dirs = ((bf_ref, c // 2 - 1, c - 1, tril, kdf_ref, qdf_ref),
            (bb_ref, c // 2, 0, triu, kdb_ref, qdb_ref))
    sels = (lo, jnp.logical_not(lo))
    lo_c = lax.broadcasted_iota(jnp.int32, (c, LANES), 1) < GLA_DK
    sels_c = (lo_c, jnp.logical_not(lo_c))
    hcols = (slice(0, GLA_DV), slice(GLA_DV, 2 * GLA_DV))

    def block(r, carry):
        rows = pl.ds(pl.multiple_of(r * blk, blk), blk)
        q = q_ref[rows, :]
        k = k_ref[rows, :]
        v = v_ref[rows, :]

        def chunk_row(b, r):
            return jnp.concatenate([jnp.broadcast_to(b[ch * c + r:ch * c + r + 1, :], (c, LANES))
                                    for ch in range(cpb)], axis=0)

        for d, (b_ref, mid_row, last_row, causal, kd_ref, qd_ref) in enumerate(dirs):
            b = b_ref[rows, :]
            b_mid = chunk_row(b, mid_row)
            b_last = chunk_row(b, last_row)
            qe = q * jnp.exp(b - b_mid)
            ke = (k * jnp.exp(b_mid - b)).astype(BF16)
            kd_ref[rows, :] = (k * jnp.exp(b_last - b)).astype(BF16)
            qd_ref[rows, :] = (q * jnp.exp(b)).astype(BF16)
            for j in range(2):
                a = _dot_nt(jnp.where(sels[j], qe, 0.0).astype(BF16), ke)
                o = _dot(jnp.where(causal, a, 0.0).astype(BF16), v[:, hcols[j]])
                if d == 0:
                    acc_ref[rows, hcols[j]] = o
                else:
                    acc_ref[rows, hcols[j]] += o
        return carry

    lax.fori_loop(0, n_chunks // cpb, block, 0, unroll=min(2, n_chunks // cpb))

    def scan(i, carry):
        new = []
        for d, (b_ref, _, last_row, _, kd_ref, _) in enumerate(dirs):
            ci = i if d == 0 else n_chunks - 1 - i
            rows = pl.ds(pl.multiple_of(ci * c, c), c)
            hist_ref[d, ci] = carry[d].astype(BF16)
            grp = b_ref[pl.ds(pl.multiple_of(ci * c + (last_row // 8) * 8, 8), 8), :]
            b_last = grp[last_row % 8:last_row % 8 + 1, :]
            new.append(carry[d] * jnp.exp(b_last) + _dot_tn(v_ref[rows, :], kd_ref[rows, :]))
        return tuple(new)

    st_f, st_b = lax.fori_loop(0, n_chunks, scan, (pair_state_t(s0f_ref), pair_state_t(s0b_ref)),
                               unroll=GLA_CHUNK_UNROLL)

    def inter(ci, carry):
        rows = pl.ds(pl.multiple_of(ci * c, c), c)
        for d, (_, _, _, _, _, qd_ref) in enumerate(dirs):
            qd = qd_ref[rows, :]
            st = hist_ref[d, ci]
            for j in range(2):
                acc_ref[rows, hcols[j]] += _dot_nt(jnp.where(sels_c[j], qd, jnp.zeros_like(qd)), st[hcols[j], :])
        return carry

    lax.fori_loop(0, n_chunks, inter, 0, unroll=GLA_CHUNK_UNROLL)
    s_f = st_f.T
    s_b = st_b.T
    sf_ref[0] = s_f[:GLA_DK, :GLA_DV]
    sf_ref[1] = s_f[GLA_DK:, GLA_DV:]
    sb_ref[0] = s_b[:GLA_DK, :GLA_DV]
    sb_ref[1] = s_b[GLA_DK:, GLA_DV:]

    gn = gn_ref[...]
    for j in range(2):
        cols = slice(j * GLA_DV, (j + 1) * GLA_DV)
        gr = gr_ref[:, cols]
        o_ref[:, cols] = (_rms(acc_ref[:, cols], gn) * (gr * jax.nn.sigmoid(gr))).astype(BF16)


def _gla(gq, gk, gv, gr, bf, bb, s0f, s0b, gnorm, *, n_batch, seq, tok_off):
    n_chunks = seq // GLA_CHUNK
    boff = tok_off // seq
    hp = GLA_HEADS // 2
    tok = lambda w: pl.BlockSpec((seq, w), lambda b, p: (boff + b, p))
    st = pl.BlockSpec((None, 2, GLA_DK, GLA_DV), lambda b, p: (b, p, 0, 0))
    return pl.pallas_call(
        functools.partial(_gla_kernel, n_chunks=n_chunks),
        out_shape=[jax.ShapeDtypeStruct((n_batch * seq, GLA_HEADS * GLA_DV), BF16),
                   jax.ShapeDtypeStruct((n_batch, GLA_HEADS, GLA_DK, GLA_DV), F32),
                   jax.ShapeDtypeStruct((n_batch, GLA_HEADS, GLA_DK, GLA_DV), F32)],
        grid=(n_batch, hp),
        in_specs=[tok(2 * GLA_DK), tok(2 * GLA_DK), tok(2 * GLA_DV), tok(2 * GLA_DV),
                  tok(2 * GLA_DK), tok(2 * GLA_DK), st, st, _const_spec((1, GLA_DV))],
        out_specs=[pl.BlockSpec((seq, 2 * GLA_DV), lambda b, p: (b, p)), st, st],
        scratch_shapes=[pltpu.VMEM((seq, 2 * GLA_DV), F32)]
                       + [pltpu.VMEM((seq, 2 * GLA_DK), BF16)] * 4
                       + [pltpu.VMEM((2, n_chunks, 2 * GLA_DV, 2 * GLA_DK), BF16)],
        compiler_params=_cparams(("parallel", "parallel")),
        name=f"gla_seq{seq}",
    )(gq, gk, gv, gr, bf, bb, s0f, s0b, gnorm)


def _out_proj_kernel(*refs, n_o, x_split, route):
    t = TOK_TILE
    o = [_pick(t, refs[2 * j], refs[2 * j + 1]) for j in range(n_o)]
    rest = refs[2 * n_o:]
    w_ref = rest[0]
    if x_split:
        x_in = _pick(t, rest[1], rest[2])
        rest = rest[3:]
    else:
        x_in = rest[1][...]
        rest = rest[2:]
    g1_ref, gate_ref, g2_ref, shift_ref, scale_ref = rest[:5]
    rest = rest[5:]
    if route:
        router_in, rest = rest[:3], rest[3:]
    x_out, h_out = rest[:2]
    y = _dot(o[0] if n_o == 1 else jnp.concatenate(o, axis=1), w_ref[...])
    x = x_in + gate_ref[...] * _rms(y, g1_ref[...])
    x_out[...] = x
    h = _modulate(x, g2_ref[...], shift_ref[...], scale_ref[...])
    if route:
        hb = h.astype(BF16)
        h_out[...] = _pack_pairs(hb)
        _route(hb, *router_in, *rest[2:])
    else:
        h_out[...] = h.astype(h_out.dtype)


def _out_proj(os_, w, x, mod5, layer, g1, g2, h_dtype, router=None):
    t = TOK_TILE
    x_split = isinstance(x, tuple)
    in_specs, args = [], []
    for o_p, o_s in os_:
        in_specs += _split_specs(t, o_p.shape[1])
        args += [o_p, o_s]
    in_specs.append(_const_spec(w.shape))
    args.append(w)
    if x_split:
        in_specs += _split_specs(t, D_MODEL)
        args += list(x)
    else:
        in_specs.append(_tok_spec(t, D_MODEL))
        args.append(x)
    in_specs += [_const_spec((1, D_MODEL)), _mod_spec(layer, 2, t), _const_spec((1, D_MODEL)),
                 _mod_spec(layer, 3, t), _mod_spec(layer, 4, t)]
    args += [g1, mod5, g2, mod5, mod5]
    h_width = D_MODEL // 2 if router is not None else D_MODEL
    out_shape = [jax.ShapeDtypeStruct((N_TOK, D_MODEL), F32), jax.ShapeDtypeStruct((N_TOK, h_width), h_dtype)]
    out_specs = [_tok_spec(t, D_MODEL), _tok_spec(t, h_width)]
    scratch = []
    if router is not None:
        r = np.arange(t)
        ltri = jnp.asarray(r[:, None] > r[None, :], BF16)
        args += [router[0], router[1], ltri]
        in_specs += [_const_spec(router[0].shape), _const_spec(router[1].shape), _const_spec((t, t))]
        out_shape += [jax.ShapeDtypeStruct((N_TOK, LANES), F32), jax.ShapeDtypeStruct((N_TOK, LANES), jnp.int32),
                      jax.ShapeDtypeStruct((N_TOK, LANES), jnp.int32), jax.ShapeDtypeStruct((8, LANES), jnp.int32)]
        out_specs += [_tok_spec(t, LANES), _tok_spec(t, LANES), _tok_spec(t, LANES),
                      pl.BlockSpec((8, LANES), lambda i: (0, 0))]
        scratch = [pltpu.VMEM((8, LANES), F32)]
    return pl.pallas_call(
        functools.partial(_out_proj_kernel, n_o=len(os_), x_split=x_split, route=router is not None),
        out_shape=out_shape,
        grid=(N_TOK // t,),
        in_specs=in_specs,
        out_specs=out_specs,
        scratch_shapes=scratch,
        compiler_params=_cparams(("arbitrary",) if router is not None else ("parallel",)),
        name=f"out_proj_{len(os_)}",
    )(*args)


def _ffn_kernel(h_ref, x_ref, wg_ref, wu_ref, wd_ref, g_ref, gate_ref, x_out):
    h = h_ref[...]
    f = None
    for cidx in range(D_FF // FF_CHUNK):
        cols = slice(cidx * FF_CHUNK, (cidx + 1) * FF_CHUNK)
        a = _dot(h, wg_ref[:, cols].astype(BF16))
        u = _dot(h, wu_ref[:, cols].astype(BF16))
        fc = _dot(((a * jax.nn.sigmoid(a)) * u).astype(BF16), wd_ref[cols, :].astype(BF16))
        f = fc if f is None else f + fc
    x_out[...] = x_ref[...] + gate_ref[...] * _rms(f, g_ref[...])


def _ffn(h, x, wg, wu, wd, mod5, layer, g3):
    t = TOK_TILE
    return pl.pallas_call(
        _ffn_kernel,
        out_shape=jax.ShapeDtypeStruct((N_TOK, D_MODEL), F32),
        grid=(N_TOK // t,),
        in_specs=[_tok_spec(t, D_MODEL), _tok_spec(t, D_MODEL), _const_spec(wg.shape),
                  _const_spec(wu.shape), _const_spec(wd.shape), _const_spec((1, D_MODEL)),
                  _mod_spec(layer, 5, t)],
        out_specs=_tok_spec(t, D_MODEL),
        compiler_params=_cparams(("parallel",)),
        name="ffn_swiglu",
    )(h, x, wg, wu, wd, g3, mod5)


GQA_Q_W = GQA_HEADS * GQA_HEAD_DIM
GQA_KV_W = GQA_KV_HEADS * GQA_HEAD_DIM
GQA_VEXT_W = (GQA_KV_HEADS // 2) * 2 * LANES
_O_Q = 0
_O_K = GQA_Q_W
_O_V = GQA_Q_W + GQA_KV_W
IN_C_EXT = _O_V + GQA_VEXT_W
GQA_ROT_HALF = GQA_HEAD_DIM // 4


def _rotate_half_lanes(x, half):
    lane = lax.broadcasted_iota(jnp.int32, (x.shape[0], LANES), 1)
    first = (lane & (2 * half - 1)) < half
    out = []
    for i in range(0, x.shape[1], LANES):
        blk = x[:, i:i + LANES]
        out.append(jnp.where(first, pltpu.roll(blk, LANES - half, 1), pltpu.roll(blk, half, 1)))
    return jnp.concatenate(out, axis=1)


def _odd_in_kernel(x_ref, g_ref, shift_ref, scale_ref, win_ref, vbias_ref, c_ref, s_ref,
                   q_out, kb_out, vb_out, kt_out, vt_out):
    h = _modulate(x_ref[...], g_ref[...], shift_ref[...], scale_ref[...])
    z = _dot(h.astype(BF16), win_ref[...])
    c_t = _lane_tile(c_ref[...], GQA_Q_W // LANES)
    s_t = _lane_tile(s_ref[...], GQA_Q_W // LANES)
    zq = z[:, _O_Q:_O_Q + GQA_Q_W]
    q = zq * c_t + _rotate_half_lanes(zq, GQA_ROT_HALF) * s_t
    q_out[...] = (q * (GQA_HEAD_DIM ** -0.5)).astype(BF16)
    zk = z[:, _O_K:_O_K + GQA_KV_W]
    k = zk * c_t[:, :GQA_KV_W] + _rotate_half_lanes(zk, GQA_ROT_HALF) * s_t[:, :GQA_KV_W]
    kb_out[...] = k.astype(BF16)
    vext = z[:, _O_V:_O_V + GQA_VEXT_W] + vbias_ref[...]
    vb_out[...] = vext.astype(BF16)

    @pl.when(pl.program_id(0) < NP_TOK // TOK_TILE)
    def _():
        v = jnp.concatenate([vext[:, 2 * p * LANES:(2 * p + 1) * LANES] for p in range(GQA_KV_HEADS // 2)], axis=1)
        for b in range(TOK_TILE // SEQ):
            kt_out[b] = k[b * SEQ:(b + 1) * SEQ, :].T
            vt_out[b] = v[b * SEQ:(b + 1) * SEQ, :].T


def _odd_in_proj(x, mod5, layer, g, win, vbias, tab_c, tab_s):
    t = TOK_TILE
    npt = NP_TOK // t
    out_widths = [(GQA_Q_W, BF16), (GQA_KV_W, BF16), (GQA_VEXT_W, BF16)]
    cache_shape = jax.ShapeDtypeStruct((BATCH, GQA_KV_W, SEQ), F32)
    cache_spec = pl.BlockSpec((t // SEQ, GQA_KV_W, SEQ), lambda i: (jnp.minimum(i, npt - 1), 0, 0))
    return pl.pallas_call(
        _odd_in_kernel,
        out_shape=[jax.ShapeDtypeStruct((N_TOK, w), dt) for w, dt in out_widths] + [cache_shape, cache_shape],
        grid=(N_TOK // t,),
        in_specs=[_tok_spec(t, D_MODEL), _const_spec((1, D_MODEL)), _mod_spec(layer, 0, t),
                  _mod_spec(layer, 1, t), _const_spec(win.shape), _const_spec(vbias.shape),
                  _rope_row_spec(t, LANES), _rope_row_spec(t, LANES)],
        out_specs=[_tok_spec(t, w) for w, _ in out_widths] + [cache_spec, cache_spec],
        compiler_params=_cparams(("arbitrary",)),
        name="odd_in_proj",
    )(x, g, mod5, mod5, win, vbias, tab_c, tab_s)


def _gqa_kernel(sink_ref, q_ref, *refs, local_len):
    if local_len:
        kl_ref, vl_ref, kc_ref, vc_ref, o_ref = refs
    else:
        kc_ref, vc_ref, o_ref = refs
    tq = q_ref.shape[0]
    lane = lax.broadcasted_iota(jnp.int32, (tq, LANES), 1)
    lo = lane < GQA_HEAD_DIM
    if local_len:
        i = pl.program_id(1)
        q0 = i * tq
        seq = kl_ref.shape[0]
        kstart = pl.multiple_of(jnp.clip(q0 - WINDOW, 0, seq - local_len), LANES)
        qpos = q0 + lax.broadcasted_iota(jnp.int32, (tq, local_len), 0)
        kpos = kstart + lax.broadcasted_iota(jnp.int32, (tq, local_len), 1)
        band = jnp.abs(qpos - kpos) <= WINDOW
    for p in range(GQA_KV_HEADS // 2):
        kc = kc_ref[:, p * LANES:(p + 1) * LANES]
        vc = vc_ref[:, 2 * p * LANES:(2 * p + 2) * LANES]
        if local_len:
            kl = kl_ref[pl.ds(kstart, local_len), p * LANES:(p + 1) * LANES]
            vl = vl_ref[pl.ds(kstart, local_len), 2 * p * LANES:(2 * p + 2) * LANES]
        for blk in range(GQA_GROUP):
            cols = slice((p * GQA_GROUP + blk) * LANES, (p * GQA_GROUP + blk + 1) * LANES)
            qb = q_ref[:, cols]
            res = []
            for half in range(2):
                head = (2 * p + half) * GQA_GROUP + blk
                sink = sink_ref[head]
                qh = jnp.where(lo if half == 0 else jnp.logical_not(lo), qb, jnp.zeros_like(qb))
                s_c = _dot_nt(qh, kc)
                m = jnp.maximum(s_c.max(axis=-1, keepdims=True), sink)
                if local_len:
                    s_l = jnp.where(band, _dot_nt(qh, kl), NEG_INF)
                    m = jnp.maximum(m, s_l.max(axis=-1, keepdims=True))
                r = _dot(jnp.exp(s_c - m).astype(BF16), vc)
                if local_len:
                    r = r + _dot(jnp.exp(s_l - m).astype(BF16), vl)
                res.append(r[:, :LANES] / (r[:, LANES:] + jnp.exp(sink - m)))
            o_ref[:, cols] = jnp.where(lo, res[0], res[1]).astype(BF16)


def _gqa_attention(sink, q, k_loc, v_loc, k_ctx, v_ctx, *, n_batch, seq_q, q_tile, tok_off, n_ctx, local):
    nq = seq_q // q_tile
    qoff = tok_off // q_tile
    local_len = q_tile + 2 * WINDOW if local else 0
    in_specs = [pl.BlockSpec(memory_space=pltpu.SMEM),
                pl.BlockSpec((q_tile, GQA_Q_W), lambda b, i: (qoff + b * nq + i, 0))]
    args = [sink, q]
    if local:
        boff = tok_off // seq_q
        in_specs += [pl.BlockSpec((seq_q, GQA_KV_W), lambda b, i: (boff + b, 0)),
                     pl.BlockSpec((seq_q, GQA_VEXT_W), lambda b, i: (boff + b, 0))]
        args += [k_loc, v_loc]
    in_specs += [pl.BlockSpec((n_ctx, GQA_KV_W), lambda b, i: (b, 0)),
                 pl.BlockSpec((n_ctx, GQA_VEXT_W), lambda b, i: (b, 0))]
    args += [k_ctx, v_ctx]
    return pl.pallas_call(
        functools.partial(_gqa_kernel, local_len=local_len),
        out_shape=jax.ShapeDtypeStruct((n_batch * seq_q, GQA_Q_W), BF16),
        grid=(n_batch, nq),
        in_specs=in_specs,
        out_specs=pl.BlockSpec((q_tile, GQA_Q_W), lambda b, i: (b * nq + i, 0)),
        compiler_params=_cparams(("parallel", "arbitrary")),
        name="gqa_local" if local else "gqa_ctx",
    )(*args)


def _route(h, w_ref, b_ref, ltri_ref, wsel_out, isel_out, rank_out, cnt_out, carry_ref):
    @pl.when(pl.program_id(0) == 0)
    def _():
        carry_ref[...] = jnp.zeros_like(carry_ref)

    logits = _dot(h, w_ref[...]) + b_ref[...]
    lane = lax.broadcasted_iota(jnp.int32, logits.shape, 1)
    neg = float(np.finfo(np.float32).min)
    lg = jnp.where(lane < N_EXPERTS, logits, neg)
    v1 = lg.max(axis=-1, keepdims=True)
    i1 = jnp.min(jnp.where(lg == v1, lane, LANES), axis=-1, keepdims=True)
    lg2 = jnp.where(lane == i1, neg, lg)
    v2 = lg2.max(axis=-1, keepdims=True)
    i2 = jnp.min(jnp.where(lg2 == v2, lane, LANES), axis=-1, keepdims=True)
    e2 = jnp.exp(v2 - v1)
    den = 1.0 + e2
    wsel_out[...] = jnp.where(lane == 0, 1.0 / den, jnp.where(lane == 1, e2 / den, 0.0))
    isel_out[...] = jnp.where(lane == 0, i1, jnp.where(lane == 1, i2, 0))
    hit = jnp.where(lane == i1, 1.0, jnp.where(lane == i2, 1.0, 0.0))
    carry = carry_ref[...]
    rank_out[...] = (_dot(ltri_ref[...], hit.astype(BF16)) + carry[0:1, :]).astype(jnp.int32)
    carry = carry + jnp.sum(hit, axis=0, keepdims=True)
    carry_ref[...] = carry
    cnt_out[...] = carry.astype(jnp.int32)


def _route_tables(isel, rank, cnt):
    tm = MOE_ROW_TILE
    counts = cnt[0, :N_EXPERTS]
    padded = ((counts + tm - 1) // tm) * tm
    ends = jnp.cumsum(padded)
    base = ends - padded
    e_ids = jnp.arange(N_EXPERTS, dtype=jnp.int32)
    row = rank[:, :N_EXPERTS] + base[None, :]
    pos1 = jnp.sum(jnp.where(e_ids[None, :] == isel[:, 0:1], row, 0), axis=1)
    pos2 = jnp.sum(jnp.where(e_ids[None, :] == isel[:, 1:2], row, 0), axis=1)
    tile_start = jnp.arange(MOE_ROWS // tm, dtype=jnp.int32) * tm
    tile_expert = jnp.minimum(jnp.sum(tile_start[:, None] >= ends[None, :], axis=1), N_EXPERTS - 1).astype(jnp.int32)
    tile_valid = jnp.clip((base + counts)[tile_expert] - tile_start, 0, tm).astype(jnp.int32)
    tile_valid = jnp.where(tile_start < ends[-1], tile_valid, 0)
    return pos1, pos2, tile_expert, tile_valid


def _scatter_rows(rows, pos1, pos2, n_out):
    n_tok, d = rows.shape
    per_w = n_tok // SC_WORKERS
    w = SC_GATHER_WINDOW * D_MODEL // d
    assert per_w * SC_WORKERS == n_tok and per_w % w == 0
    mesh = plsc.VectorSubcoreMesh(core_axis_name="core", subcore_axis_name="subcore")

    assert (per_w // w) % 2 == 0
    slot_types = [pltpu.VMEM((w,), jnp.int32), pltpu.VMEM((w,), jnp.int32), pltpu.VMEM((w, d), rows.dtype),
                  pltpu.SemaphoreType.DMA, pltpu.SemaphoreType.DMA]

    @functools.partial(
        pl.kernel, out_type=jax.ShapeDtypeStruct((n_out, d), rows.dtype), mesh=mesh,
        scratch_types=slot_types * 2, name="sc_scatter_rows")
    def scatter(x_hbm, p1_hbm, p2_hbm, o_hbm, *scratch):
        wid = lax.axis_index("subcore") * SC_CORES + lax.axis_index("core")
        base = wid * per_w
        slots = (scratch[:5], scratch[5:])

        @pl.loop(0, per_w // (2 * w))
        def _(g):
            loads = []
            for s, (i1_v, i2_v, rows_v, sem_in, _) in enumerate(slots):
                off = base + (2 * g + s) * w
                loads.append([pltpu.async_copy(p1_hbm.at[pl.ds(off, w)], i1_v, sem_in),
                              pltpu.async_copy(p2_hbm.at[pl.ds(off, w)], i2_v, sem_in),
                              pltpu.async_copy(x_hbm.at[pl.ds(off, w)], rows_v, sem_in)])
            stores = []
            for s, (i1_v, i2_v, rows_v, _, sem_out) in enumerate(slots):
                for cp in loads[s]:
                    cp.wait()
                stores += [pltpu.async_copy(rows_v, o_hbm.at[i1_v], sem_out),
                           pltpu.async_copy(rows_v, o_hbm.at[i2_v], sem_out)]
            for cp in stores:
                cp.wait()

    return scatter(rows, pos1, pos2)


def _gather_rows(table, idx):
    n_idx = idx.shape[0]
    d = table.shape[1]
    per_w = n_idx // SC_WORKERS
    assert per_w * SC_WORKERS == n_idx and per_w % SC_INDEX_BLOCK == 0
    mesh = plsc.VectorSubcoreMesh(core_axis_name="core", subcore_axis_name="subcore")

    w = SC_GATHER_WINDOW
    n_sub = SC_INDEX_BLOCK // w
    slot_types = [pltpu.VMEM((w, d), table.dtype), pltpu.SemaphoreType.DMA, pltpu.SemaphoreType.DMA]

    @functools.partial(
        pl.kernel, out_type=jax.ShapeDtypeStruct((n_idx, d), table.dtype), mesh=mesh,
        scratch_types=[pltpu.VMEM((SC_INDEX_BLOCK,), jnp.int32)] + slot_types * 2,
        name="sc_gather_rows")
    def gather(x_hbm, i_hbm, o_hbm, idx_v, *scratch):
        wid = lax.axis_index("subcore") * SC_CORES + lax.axis_index("core")
        base = wid * per_w
        slots = (scratch[:3], scratch[3:])

        @pl.loop(0, per_w // SC_INDEX_BLOCK)
        def _(g):
            off = base + g * SC_INDEX_BLOCK
            pltpu.sync_copy(i_hbm.at[pl.ds(off, SC_INDEX_BLOCK)], idx_v)

            def start_gather(s):
                rows_v, sem_in, _ = slots[s % 2]
                return pltpu.async_copy(x_hbm.at[idx_v.at[pl.ds(s * w, w)]], rows_v, sem_in)

            gathers = {0: start_gather(0)}
            writes = {}
            for s in range(n_sub):
                if s + 1 < n_sub:
                    if s >= 1:
                        writes[s - 1].wait()
                    gathers[s + 1] = start_gather(s + 1)
                gathers[s].wait()
                rows_v, _, sem_out = slots[s % 2]
                writes[s] = pltpu.async_copy(rows_v, o_hbm.at[pl.ds(off + s * w, w)], sem_out)
            writes[n_sub - 2].wait()
            writes[n_sub - 1].wait()

    return gather(table, idx)


def _expert_ffn_kernel(te_ref, nv_ref, x_ref, wg_ref, wu_ref, wd_ref, y_out):
    n_valid = nv_ref[pl.program_id(0)]

    @pl.when(n_valid > 0)
    def _():
        row = lax.broadcasted_iota(jnp.int32, x_ref.shape, 0)
        h = _unpack_pairs(jnp.where(row < n_valid, x_ref[...], 0.0))
        f = None
        for cidx in range(D_FF // FF_CHUNK):
            cols = slice(cidx * FF_CHUNK, (cidx + 1) * FF_CHUNK)
            a = _dot(h, wg_ref[:, cols].astype(BF16))
            u = _dot(h, wu_ref[:, cols].astype(BF16))
            fc = _dot(((a * jax.nn.sigmoid(a)) * u).astype(BF16), wd_ref[cols, :].astype(BF16))
            f = fc if f is None else f + fc
        y_out[...] = f

    @pl.when(n_valid == 0)
    def _():
        y_out[...] = jnp.zeros_like(y_out)


def _expert_ffn(xs, tile_expert, tile_valid, wg, wu, wd):
    tm = MOE_ROW_TILE
    wspec = lambda shape: pl.BlockSpec((None,) + shape, lambda j, te, nu: (te[j], 0, 0),
                                       pipeline_mode=pl.Buffered(1))
    return pl.pallas_call(
        _expert_ffn_kernel,
        out_shape=jax.ShapeDtypeStruct((MOE_ROWS, D_MODEL), F32),
        grid_spec=pltpu.PrefetchScalarGridSpec(
            num_scalar_prefetch=2,
            grid=(MOE_ROWS // tm,),
            in_specs=[pl.BlockSpec((tm, xs.shape[1]), lambda j, te, nu: (j, 0)),
                      wspec((D_MODEL, D_FF)), wspec((D_MODEL, D_FF)), wspec((D_FF, D_MODEL))],
            out_specs=pl.BlockSpec((tm, D_MODEL), lambda j, te, nu: (j, 0)),
        ),
        compiler_params=_cparams(("arbitrary",)),
        name="moe_expert_ffn",
    )(tile_expert, tile_valid, xs, wg, wu, wd)


def _moe_combine_kernel(y1_ref, y2_ref, wsel_ref, x_ref, g_ref, gate_ref, *rest):
    x_out = rest[-1]
    w = wsel_ref[...]
    f = w[:, 0:1] * y1_ref[...] + w[:, 1:2] * y2_ref[...]
    x_out[...] = x_ref[...] + gate_ref[...] * _rms(f, g_ref[...])


def _moe_combine(yg, wsel, x, mod5, layer, g3, *, tok_off, out_rows, out_off, y_prev=None):
    t = TOK_TILE
    nt = MOE_SEG // t
    off = tok_off // t
    ooff = out_off // t
    tpg = NP_TOK // t
    tok = lambda w: pl.BlockSpec((t, w), lambda i: (off + i, 0))
    in_specs = [_tok_spec(t, D_MODEL), pl.BlockSpec((t, D_MODEL), lambda i: (nt + i, 0)), tok(LANES), tok(D_MODEL),
                _const_spec((1, D_MODEL)),
                pl.BlockSpec((None, None, None, 1, D_MODEL), lambda i: (layer, (off + i) // tpg, 5, 0, 0))]
    args = [yg, yg, wsel, x, g3, mod5]
    aliases = {}
    if y_prev is not None:
        in_specs.append(pl.BlockSpec(memory_space=pl.ANY))
        args.append(y_prev)
        aliases = {len(args) - 1: 0}
    return pl.pallas_call(
        _moe_combine_kernel,
        out_shape=jax.ShapeDtypeStruct((out_rows, D_MODEL), F32),
        grid=(nt,),
        in_specs=in_specs,
        out_specs=pl.BlockSpec((t, D_MODEL), lambda i: (ooff + i, 0)),
        input_output_aliases=aliases,
        compiler_params=_cparams(("parallel",)),
        name="moe_combine",
    )(*args)


def _moe(h, x, routing, wg, wu, wd, mod5, layer, g3):
    wsel, isel, rank, cnt = routing
    pos1, pos2, tile_expert, tile_valid = _route_tables(isel, rank, cnt)
    xs = _scatter_rows(h, pos1, pos2, MOE_ROWS)
    ys = _expert_ffn(xs, tile_expert, tile_valid, wg, wu, wd)
    y_p = y_s = None
    for seg in range(N_TOK // MOE_SEG):
        rows = slice(seg * MOE_SEG, (seg + 1) * MOE_SEG)
        yg = _gather_rows(ys, jnp.concatenate([pos1[rows], pos2[rows]]))
        common = dict(tok_off=seg * MOE_SEG)
        if seg * MOE_SEG < NP_TOK:
            y_p = _moe_combine(yg, wsel, x, mod5, layer, g3, out_rows=NP_TOK, out_off=seg * MOE_SEG, y_prev=y_p, **common)
        else:
            y_s = _moe_combine(yg, wsel, x, mod5, layer, g3, out_rows=NS_TOK, out_off=seg * MOE_SEG - NP_TOK,
                               y_prev=y_s, **common)
    return y_p, y_s


def _rot_cols(w, half):
    k, n = w.shape
    wb = w.reshape(k, n // (2 * half), 2, half)
    return jnp.stack([-wb[:, :, 1], wb[:, :, 0]], axis=2).reshape(k, n)


def _axis_tables(r, pos):
    inv = np.float32(ROPE_BASE) ** (-np.arange(0, r, 2, dtype=np.float32) / np.float32(r))
    ang = pos.astype(np.float32)[:, None] * inv[None, :]
    cos, sin = np.cos(ang), np.sin(ang)
    return np.concatenate([cos, cos], axis=1), np.concatenate([sin, sin], axis=1)


def _rope_tables(r):
    s = np.arange(DEC_SEQ)
    cr, sr = _axis_tables(r // 2, s // GRID_W)
    cc, sc = _axis_tables(r // 2, s % GRID_W)
    return np.concatenate([cr, cc], axis=1), np.concatenate([sr, sc], axis=1)


def _with_identity(tab, ident):
    return np.concatenate([np.full((TOK_TILE, tab.shape[1]), ident, np.float32), tab], axis=0)


@functools.lru_cache(maxsize=None)
def _rope_constants():
    c32, s32 = _rope_tables(MLA_ROPE)
    ones = np.ones((DEC_SEQ, MLA_NOPE), np.float32)
    pad1 = np.ones((DEC_SEQ, MLA_HEAD_PAD - MLA_NOPE - MLA_ROPE), np.float32)
    def signed(s, half):
        return s * np.where(np.arange(s.shape[1]) % (2 * half) < half, -1.0, 1.0).astype(np.float32)

    cq = np.concatenate([ones, c32, pad1], axis=1)
    sq = np.concatenate([0 * ones, signed(s32, MLA_ROPE // 4), 0 * pad1], axis=1)
    c64, s64 = _rope_tables(GQA_HEAD_DIM)
    s64 = signed(s64, GQA_ROT_HALF)
    return {
        "mla_cq": _with_identity(cq, 1.0), "mla_sq": _with_identity(sq, 0.0),
        "mla_ck": _with_identity(c32, 1.0), "mla_sk": _with_identity(s32, 0.0),
        "gqa_c": _with_identity(np.concatenate([c64, c64], axis=1), 1.0),
        "gqa_s": _with_identity(np.concatenate([s64, s64], axis=1), 0.0),
    }


def _prep_tables():
    return {k: jnp.asarray(v, F32) for k, v in _rope_constants().items()}


def _prep_even(w_in, q_norm, w_q_up, kv_norm, w_kv_up, wgf, bgf, wgb, bgb):
    sizes = [MLA_Q_RANK, MLA_KV_RANK, MLA_ROPE, GLA_HEADS * GLA_DK, GLA_HEADS * GLA_DK,
             GLA_HEADS * GLA_DV, GLA_HEADS * GLA_DV, GLA_GATE_RANK, GLA_GATE_RANK]
    cq, ckv, kpe, gq, gk, gv, gr, gaf, gab = jnp.split(w_in, [int(s) for s in np.cumsum(sizes)[:-1]], axis=1)
    pad = jnp.zeros((D_MODEL, LANES - 2 * MLA_ROPE - 2 * GLA_GATE_RANK), F32)
    win = jnp.concatenate([cq, ckv, gq, gk, gv, gr, kpe, _rot_cols(kpe, MLA_ROPE // 4), gaf, gab, pad], axis=1)

    wq = w_q_up.reshape(MLA_Q_RANK, MLA_HEADS, MLA_NOPE + MLA_ROPE)
    nope, pe = wq[..., :MLA_NOPE], wq[..., MLA_NOPE:]
    zpad = jnp.zeros((MLA_Q_RANK, MLA_HEADS, MLA_HEAD_PAD - MLA_NOPE - MLA_ROPE), F32)
    wq_main = jnp.concatenate([nope, pe, zpad], axis=-1).reshape(MLA_Q_RANK, MLA_QK_W)

    wkv = w_kv_up.reshape(MLA_KV_RANK, MLA_HEADS, MLA_NOPE + MLA_V)
    knope, vv = wkv[..., :MLA_NOPE], wkv[..., MLA_NOPE:]
    wkk = jnp.concatenate([knope, jnp.zeros((MLA_KV_RANK, MLA_HEADS, MLA_HEAD_PAD - MLA_NOPE), F32)],
                          axis=-1).reshape(MLA_KV_RANK, MLA_QK_W)
    vpair = vv.reshape(MLA_KV_RANK, MLA_HEADS // 2, 2 * MLA_V)
    wkv_ext = jnp.concatenate([vpair, jnp.zeros((MLA_KV_RANK, MLA_HEADS // 2, LANES), F32)],
                              axis=-1).reshape(MLA_KV_RANK, MLA_VEXT_W)
    vbias = jnp.tile(jnp.concatenate([jnp.zeros((LANES,), F32), jnp.ones((LANES,), F32)]),
                     MLA_HEADS // 2).reshape(1, MLA_VEXT_W)
    epl = jnp.tile(jnp.concatenate([jnp.zeros((MLA_ROPE, MLA_NOPE), F32), jnp.eye(MLA_ROPE, dtype=F32),
                                    jnp.zeros((MLA_ROPE, MLA_HEAD_PAD - MLA_NOPE - MLA_ROPE), F32)], axis=1),
                   (1, MLA_HEADS))

    def gate_w(w, off):
        return jnp.zeros((LANES, GLA_HEADS * GLA_DK), F32).at[off:off + GLA_GATE_RANK].set(w)

    r = np.arange(CUMSUM_BLOCK)
    same = (r[:, None] // GLA_CHUNK) == (r[None, :] // GLA_CHUNK)
    lmat = jnp.asarray(same & (r[:, None] >= r[None, :]), BF16)
    umat = jnp.asarray(same & (r[:, None] <= r[None, :]), BF16)
    return {
        "win": win.astype(BF16), "qn": q_norm.reshape(1, -1), "wq": wq_main.astype(BF16),
        "kvn": kv_norm.reshape(1, -1), "wkk": wkk.astype(BF16), "wkv": wkv_ext.astype(BF16), "vbias": vbias,
        "epl": epl.astype(BF16), "wgf": gate_w(wgf, _S_GAF).astype(BF16), "bgf": bgf.reshape(1, -1),
        "wgb": gate_w(wgb, _S_GAB).astype(BF16), "bgb": bgb.reshape(1, -1), "lmat": lmat, "umat": umat,
    }


def _gqa_head_perm():
    heads = []
    for p in range(GQA_KV_HEADS // 2):
        for i in range(GQA_GROUP):
            heads += [(2 * p) * GQA_GROUP + i, (2 * p + 1) * GQA_GROUP + i]
    return np.asarray(heads)


def _prep_odd(w_in, w_out):
    perm = _gqa_head_perm()
    wq = w_in[:, :GQA_Q_W].reshape(D_MODEL, GQA_HEADS, GQA_HEAD_DIM)[:, perm].reshape(D_MODEL, GQA_Q_W)
    wk = w_in[:, GQA_Q_W:GQA_Q_W + GQA_KV_W]
    wv = w_in[:, GQA_Q_W + GQA_KV_W:].reshape(D_MODEL, GQA_KV_HEADS // 2, 2 * GQA_HEAD_DIM)
    wv_ext = jnp.concatenate([wv, jnp.zeros((D_MODEL, GQA_KV_HEADS // 2, LANES), F32)], axis=-1).reshape(D_MODEL, GQA_VEXT_W)
    win = jnp.concatenate([wq, wk, wv_ext], axis=1)
    vbias = jnp.tile(jnp.concatenate([jnp.zeros((LANES,), F32), jnp.ones((LANES,), F32)]),
                     GQA_KV_HEADS // 2).reshape(1, GQA_VEXT_W)
    wo = w_out.reshape(GQA_HEADS, GQA_HEAD_DIM, D_MODEL)[perm].reshape(GQA_Q_W, D_MODEL)
    return win.astype(BF16), vbias, wo.astype(BF16)


def _ext_v(v):
    rows = v.shape[0]
    vp = v.reshape(rows, GQA_KV_HEADS // 2, 2 * GQA_HEAD_DIM)
    return jnp.concatenate([vp, jnp.ones((rows, GQA_KV_HEADS // 2, LANES), v.dtype)], axis=-1).reshape(rows, GQA_VEXT_W)


def kernel(x_prompt, x_sample, cache_mla_ckv, cache_mla_kpe, state_gla_fwd, state_gla_bwd, cache_gqa_k, cache_gqa_v, c, c_ctx, w_mod, b_mod, norm_g, w_in_ab, mla_q_norm, mla_w_q_up, mla_kv_norm, mla_w_kv_up, gla_w_gate_f, gla_b_gate_f, gla_w_gate_b, gla_b_gate_b, gla_norm, w_out_ab, ffn_w_gate, ffn_w_up, ffn_w_down, w_in_c, gqa_sink, w_out_c, moe_w_router, moe_b_router, moe_w_gate, moe_w_up, moe_w_down):
    x_in = (x_prompt.reshape(NP_TOK, D_MODEL), x_sample.reshape(NS_TOK, D_MODEL))
    cvec =jnp.concatenate([c_ctx[None, :], c, jnp.zeros((MOD_ROWS - N_GROUPS, D_MODEL), F32)], axis=0)
    mod5 = _modulation(cvec, w_mod, b_mod).reshape(DEPTH, MOD_ROWS, N_MOD, 1, D_MODEL)
    tabs = _prep_tables()
    gvec = lambda l, j: norm_g[l, j].reshape(1, D_MODEL)

    wts = _prep_even(w_in_ab[0], mla_q_norm[0], mla_w_q_up[0], mla_kv_norm[0], mla_w_kv_up[0],
                     gla_w_gate_f[0], gla_b_gate_f[0], gla_w_gate_b[0], gla_b_gate_b[0])
    (q, k, v, ckv, kpe, gq, gk, gv, gr, bf, bb) = _even_in_proj(*x_in, mod5, 0, gvec(0, 0), wts, tabs)
    kc, vc = _cache_kv(cache_mla_ckv[:, 0].reshape(DEC_BATCH * PAST_LEN, MLA_KV_RANK),
                       cache_mla_kpe[:, 0].reshape(DEC_BATCH * PAST_LEN, MLA_ROPE), wts)
    oa_p = _mla_attention(q, [k], [v], n_batch=BATCH, seq_q=SEQ, q_tile=SEQ, tok_off=0, k_batch_rows=[SEQ],
                          pairs=MLA_HEADS // 2)
    oa_s = _mla_attention(q, [k, kc], [v, vc], n_batch=DEC_BATCH, seq_q=DEC_SEQ, q_tile=MLA_Q_TILE,
                          tok_off=NP_TOK, k_batch_rows=[DEC_SEQ, PAST_LEN], pairs=1)
    gn = gla_norm[0].reshape(1, GLA_DV)
    zero_state = jnp.zeros((BATCH, GLA_HEADS, GLA_DK, GLA_DV), F32)
    ob_p, sf, sb = _gla(gq, gk, gv, gr, bf, bb, zero_state, zero_state, gn, n_batch=BATCH, seq=SEQ, tok_off=0)
    ob_s, _, _ = _gla(gq, gk, gv, gr, bf, bb, state_gla_fwd[:, 0], state_gla_bwd[:, 0], gn,
                      n_batch=DEC_BATCH, seq=DEC_SEQ, tok_off=NP_TOK)
    x, h = _out_proj([(oa_p, oa_s), (ob_p, ob_s)], w_out_ab[0].astype(BF16), x_in, mod5, 0, gvec(0, 1),
                     gvec(0, 2), BF16)
    x = _ffn(h, x, ffn_w_gate[0], ffn_w_up[0], ffn_w_down[0], mod5, 0, gvec(0, 3))

    win_c, vbias_c, wo_c = _prep_odd(w_in_c[0], w_out_c[0])
    qg, kgb, vgb, kg_t, vg_t = _odd_in_proj(x, mod5, 1, gvec(1, 0), win_c, vbias_c, tabs["gqa_c"], tabs["gqa_s"])
    sink = gqa_sink[0]
    og_p = _gqa_attention(sink, qg, None, None, kgb, vgb, n_batch=BATCH, seq_q=SEQ, q_tile=SEQ, tok_off=0, n_ctx=SEQ, local=False)
    kc_g = cache_gqa_k[:, 0].reshape(DEC_BATCH * PAST_LEN, GQA_KV_W).astype(BF16)
    vc_g = _ext_v(cache_gqa_v[:, 0].reshape(DEC_BATCH * PAST_LEN, GQA_KV_W)).astype(BF16)
    og_s = _gqa_attention(sink, qg, kgb, vgb, kc_g, vc_g, n_batch=DEC_BATCH, seq_q=DEC_SEQ, q_tile=GQA_Q_TILE,
                          tok_off=NP_TOK, n_ctx=PAST_LEN, local=True)
    w_r = jnp.zeros((D_MODEL, LANES), F32).at[:, :N_EXPERTS].set(moe_w_router[0]).astype(BF16)
    b_r = jnp.zeros((1, LANES), F32).at[0, :N_EXPERTS].set(moe_b_router[0])
    x, h, *routing = _out_proj([(og_p, og_s)], wo_c, x, mod5, 1, gvec(1, 1), gvec(1, 2), F32, router=(w_r, b_r))
    y_p, y_s = _moe(h, x, routing, moe_w_gate[0], moe_w_up[0], moe_w_down[0], mod5, 1, gvec(1, 3))

    y_prompt = y_p.reshape(BATCH, SEQ, D_MODEL)
    y_sample = y_s.reshape(DEC_BATCH, DEC_SEQ, D_MODEL)
    new_ckv = ckv[:NP_TOK].reshape(BATCH, 1, SEQ, MLA_KV_RANK)
    new_kpe = kpe[:NP_TOK].reshape(BATCH, 1, SEQ, MLA_ROPE)
    as_cache = lambda a: jnp.transpose(a.reshape(BATCH, 1, GQA_KV_HEADS, GQA_HEAD_DIM, SEQ), (0, 1, 4, 2, 3))
    new_k = as_cache(kg_t)
    new_v = as_cache(vg_t)
    return (y_prompt, y_sample, new_ckv, new_kpe, sf[:, None], sb[:, None], new_k, new_v)
```

```python
import functools

import jax
import jax.numpy as jnp
import numpy as np
from jax import lax
from jax.experimental import pallas as pl
from jax.experimental.pallas import tpu as pltpu
from jax.experimental.pallas import tpu_sc as plsc

F32 = jnp.float32
BF16 = jnp.bfloat16

D_MODEL = 1024
BATCH = 16
SEQ = 256
DEPTH = 2
DEC_BATCH = 4
DEC_SEQ = 4096
PAST_LEN = 256
GRID_W = 64
N_MOD = 6
EPS = 1e-6
ROPE_BASE = 10000.0
NEG_INF = -1e30

MLA_HEADS = 8
MLA_NOPE = 64
MLA_ROPE = 32
MLA_V = 64
MLA_Q_RANK = 384
MLA_KV_RANK = 256
GLA_HEADS = 4
GLA_DK = 64
GLA_DV = 128
GLA_GATE_RANK = 16
GLA_GATE_NORM = 16.0
GLA_CHUNK = 64
GQA_HEADS = 16
GQA_KV_HEADS = 4
GQA_GROUP = GQA_HEADS // GQA_KV_HEADS
GQA_HEAD_DIM = 64
WINDOW = 128
D_FF = 2816
N_EXPERTS = 8
TOP_K = 2

NP_TOK = BATCH * SEQ
NS_TOK = DEC_BATCH * DEC_SEQ
N_TOK = NP_TOK + NS_TOK
N_GROUPS = 1 + DEC_BATCH
MOD_ROWS = 8

LANES = 128
MXU_COLS = 256
VMEM_LIMIT_BYTES = 56 * 1024 * 1024

TOK_TILE = 512
MOD_COL_BLOCK = 1536
CUMSUM_BLOCK = 256
GLA_BLOCK_CHUNKS = MXU_COLS // GLA_CHUNK
GLA_CHUNK_UNROLL = 8
MLA_Q_TILE = 2048
MLA_Q_SUB = 256
GQA_Q_TILE = 256
GQA_SUB_TILE = 128
MOE_ROW_TILE = 512
MOE_ROWS = TOP_K * N_TOK + N_EXPERTS * MOE_ROW_TILE
MOE_SEG = 4096
SC_CORES = 2
SC_SUBCORES = 16
SC_WORKERS = SC_CORES * SC_SUBCORES
SC_INDEX_BLOCK = 128
SC_GATHER_WINDOW = 32
FF_CHUNK = 1408

_C_CQ = 0
_C_CKV = _C_CQ + MLA_Q_RANK
_C_GQ = _C_CKV + MLA_KV_RANK
_C_GK = _C_GQ + GLA_HEADS * GLA_DK
_C_GV = _C_GK + GLA_HEADS * GLA_DK
_C_GR = _C_GV + GLA_HEADS * GLA_DV
_C_SMALL = _C_GR + GLA_HEADS * GLA_DV
IN_AB_EXT = _C_SMALL + LANES
_S_KPE, _S_KPER, _S_GAF, _S_GAB = 0, MLA_ROPE, 2 * MLA_ROPE, 2 * MLA_ROPE + GLA_GATE_RANK
MLA_HEAD_PAD = LANES
MLA_QK_W = MLA_HEADS * MLA_HEAD_PAD
MLA_VEXT_W = (MLA_HEADS // 2) * 2 * LANES


def _cparams(semantics):
    return pltpu.CompilerParams(dimension_semantics=semantics, vmem_limit_bytes=VMEM_LIMIT_BYTES)


def _const_spec(shape):
    nd = len(shape)
    return pl.BlockSpec(shape, lambda *_: (0,) * nd, pipeline_mode=pl.Buffered(1))


def _log_sigmoid(x):
    return jnp.minimum(x, 0.0) - jnp.log1p(jnp.exp(-jnp.abs(x)))


def _rms(x, g):
    return (x * lax.rsqrt(jnp.mean(x * x, axis=-1, keepdims=True) + EPS)) * g


def _modulate(x, g, shift, scale):
    return _rms(x, g) * (1.0 + scale) + shift


def _dot(a, b):
    return jnp.dot(a, b, preferred_element_type=F32)


def _dot_nt(a, b):
    return lax.dot_general(a, b, (((1,), (1,)), ((), ())), preferred_element_type=F32)


def _dot_tn(a, b):
    return lax.dot_general(a, b, (((0,), (0,)), ((), ())), preferred_element_type=F32)


def _split3(x):
    hi = x.astype(BF16)
    r1 = x - hi.astype(F32)
    mid = r1.astype(BF16)
    lo = (r1 - mid.astype(F32)).astype(BF16)
    return hi, mid, lo


def _lane_tile(x, reps):
    return jnp.concatenate([x] * reps, axis=1)


_HI16 = np.uint32(0xFFFF0000)


def _pack_pairs(xb):
    w = xb.shape[1] // 2
    xf = xb.astype(F32)
    lo = lax.bitcast_convert_type(xf[:, :w], jnp.uint32) >> 16
    hi = lax.bitcast_convert_type(xf[:, w:], jnp.uint32) & _HI16
    return lax.bitcast_convert_type(lo | hi, F32)


def _unpack_pairs(words):
    u = lax.bitcast_convert_type(words, jnp.uint32)
    lo = lax.bitcast_convert_type(u << 16, F32)
    hi = lax.bitcast_convert_type(u & _HI16, F32)
    return jnp.concatenate([lo, hi], axis=1).astype(BF16)


def _mod_kernel(c_ref, w_ref, b_ref, o_ref):
    c = c_ref[...]
    s = c * jax.nn.sigmoid(c)
    o_ref[...] = _dot(s.astype(BF16), w_ref[...].astype(BF16)) + b_ref[...]


def _modulation(cvec, w_mod, b_mod):
    ncol = N_MOD * D_MODEL
    blk = MOD_COL_BLOCK
    return pl.pallas_call(
        _mod_kernel,
        out_shape=jax.ShapeDtypeStruct((DEPTH, MOD_ROWS, ncol), F32),
        grid=(DEPTH, ncol // blk),
        in_specs=[
            pl.BlockSpec((MOD_ROWS, D_MODEL), lambda l, j: (0, 0)),
            pl.BlockSpec((None, D_MODEL, blk), lambda l, j: (l, 0, j)),
            pl.BlockSpec((None, 1, blk), lambda l, j: (l, 0, j)),
        ],
        out_specs=pl.BlockSpec((None, MOD_ROWS, blk), lambda l, j: (l, 0, j)),
        compiler_params=_cparams(("arbitrary", "arbitrary")),
        name="modulation",
    )(cvec, w_mod, b_mod.reshape(DEPTH, 1, ncol))


def _mod_spec(layer, j, tile):
    tpg = NP_TOK // tile
    return pl.BlockSpec((None, None, None, 1, D_MODEL), lambda i: (layer, i // tpg, j, 0, 0))


def _tok_spec(tile, width):
    return pl.BlockSpec((tile, width), lambda i: (i, 0))


def _split_specs(tile, width):
    npt = NP_TOK // tile
    return [pl.BlockSpec((tile, width), lambda i: (jnp.minimum(i, npt - 1), 0)),
            pl.BlockSpec((tile, width), lambda i: (jnp.maximum(i - npt, 0), 0))]


def _pick(tile, p_ref, s_ref):
    return jnp.where(pl.program_id(0) < NP_TOK // tile, p_ref[...], s_ref[...])


def _rope_row_spec(tile, width):
    npt = NP_TOK // tile
    spt = DEC_SEQ // tile
    return pl.BlockSpec((tile, width), lambda i: (jnp.where(i < npt, 0, 1 + (i - npt) % spt), 0))


def _even_in_kernel(xp_ref, xs_ref, g_ref, shift_ref, scale_ref, win_ref, qn_ref, wq_ref, kvn_ref, wkk_ref,
                    wkv_ref, vbias_ref, epl_ref, wgf_ref, bgf_ref, wgb_ref, bgb_ref, lmat_ref,
                    umat_ref, cq_ref, sq_ref, ck_ref, sk_ref,
                    q_out, k_out, v_out, ckv_out, kpe_out, gq_out, gk_out, gv_out, gr_out,
                    bf_out, bb_out):
    h = _modulate(_pick(TOK_TILE, xp_ref, xs_ref), g_ref[...], shift_ref[...], scale_ref[...])
    z = _dot(h.astype(BF16), win_ref[...])

    cqn = _rms(z[:, _C_CQ:_C_CQ + MLA_Q_RANK], qn_ref[...]).astype(BF16)
    qf = _dot(cqn, wq_ref[...])
    cq_t = _lane_tile(cq_ref[...], MLA_HEADS)
    sq_t = _lane_tile(sq_ref[...], MLA_HEADS)
    q_out[...] = (qf * cq_t + _rotate_half_lanes(qf, MLA_ROPE // 4) * sq_t).astype(BF16)

    ckvn = _rms(z[:, _C_CKV:_C_CKV + MLA_KV_RANK], kvn_ref[...])
    ckv_out[...] = ckvn
    small = z[:, _C_SMALL:_C_SMALL + LANES]
    kpe = (small[:, _S_KPE:_S_KPE + MLA_ROPE] * ck_ref[...]
           + small[:, _S_KPER:_S_KPER + MLA_ROPE] * sk_ref[...])
    kpe_out[...] = kpe
    ckvn_b = ckvn.astype(BF16)
    k_out[...] = (_dot(ckvn_b, wkk_ref[...]) + _dot(kpe.astype(BF16), epl_ref[...])).astype(BF16)
    v_out[...] = (_dot(ckvn_b, wkv_ref[...]) + vbias_ref[...]).astype(BF16)

    gq_out[...] = z[:, _C_GQ:_C_GQ + GLA_HEADS * GLA_DK] * (GLA_DK ** -0.5)
    gk_out[...] = z[:, _C_GK:_C_GK + GLA_HEADS * GLA_DK]
    gv_out[...] = z[:, _C_GV:_C_GV + GLA_HEADS * GLA_DV].astype(BF16)
    gr_out[...] = z[:, _C_GR:_C_GR + GLA_HEADS * GLA_DV]

    small_b = small.astype(BF16)
    la_f = _log_sigmoid(_dot(small_b, wgf_ref[...]) + bgf_ref[...]) * (1.0 / GLA_GATE_NORM)
    la_b = _log_sigmoid(_dot(small_b, wgb_ref[...]) + bgb_ref[...]) * (1.0 / GLA_GATE_NORM)
    lmat = lmat_ref[...]
    umat = umat_ref[...]
    for r in range(TOK_TILE // CUMSUM_BLOCK):
        rows = slice(r * CUMSUM_BLOCK, (r + 1) * CUMSUM_BLOCK)
        f_hi, f_mid, f_lo = _split3(la_f[rows])
        bf_out[rows, :] = _dot(lmat, f_hi) + _dot(lmat, f_mid) + _dot(lmat, f_lo)
        b_hi, b_mid, b_lo = _split3(la_b[rows])
        bb_out[rows, :] = _dot(umat, b_hi) + _dot(umat, b_mid) + _dot(umat, b_lo)


def _even_in_proj(xp, xs, mod5, layer, g, wts, tabs):
    t = TOK_TILE
    out_widths = [(MLA_QK_W, BF16), (MLA_QK_W, BF16), (MLA_VEXT_W, BF16), (MLA_KV_RANK, F32),
                  (MLA_ROPE, F32), (GLA_HEADS * GLA_DK, F32), (GLA_HEADS * GLA_DK, F32),
                  (GLA_HEADS * GLA_DV, BF16), (GLA_HEADS * GLA_DV, F32),
                  (GLA_HEADS * GLA_DK, F32), (GLA_HEADS * GLA_DK, F32)]
    const_names = ["win", "qn", "wq", "kvn", "wkk", "wkv", "vbias", "epl", "wgf", "bgf", "wgb",
                   "bgb", "lmat", "umat"]
    consts = [wts[n] for n in const_names]
    in_specs = (_split_specs(t, D_MODEL)
                + [_const_spec((1, D_MODEL)), _mod_spec(layer, 0, t), _mod_spec(layer, 1, t)]
                + [_const_spec(c.shape) for c in consts]
                + [_rope_row_spec(t, LANES), _rope_row_spec(t, LANES),
                   _rope_row_spec(t, MLA_ROPE), _rope_row_spec(t, MLA_ROPE)])
    return pl.pallas_call(
        _even_in_kernel,
        out_shape=[jax.ShapeDtypeStruct((N_TOK, w), dt) for w, dt in out_widths],
        grid=(N_TOK // t,),
        in_specs=in_specs,
        out_specs=[_tok_spec(t, w) for w, _ in out_widths],
        compiler_params=_cparams(("parallel",)),
        name="even_in_proj",
    )(xp, xs, g, mod5, mod5, *consts, tabs["mla_cq"], tabs["mla_sq"], tabs["mla_ck"], tabs["mla_sk"])


def _cache_kv_kernel(ckv_ref, kpe_ref, wkk_ref, wkv_ref, vbias_ref, epl_ref, k_out, v_out):
    ckv_b = ckv_ref[...].astype(BF16)
    k_out[...] = (_dot(ckv_b, wkk_ref[...]) + _dot(kpe_ref[...].astype(BF16), epl_ref[...])).astype(BF16)
    v_out[...] = (_dot(ckv_b, wkv_ref[...]) + vbias_ref[...]).astype(BF16)


def _cache_kv(ckv, kpe, wts):
    n = ckv.shape[0]
    consts = [wts[k] for k in ("wkk", "wkv", "vbias", "epl")]
    return pl.pallas_call(
        _cache_kv_kernel,
        out_shape=[jax.ShapeDtypeStruct((n, MLA_QK_W), BF16), jax.ShapeDtypeStruct((n, MLA_VEXT_W), BF16)],
        grid=(1,),
        in_specs=[_const_spec(ckv.shape), _const_spec(kpe.shape)] + [_const_spec(c.shape) for c in consts],
        out_specs=[_const_spec((n, MLA_QK_W)), _const_spec((n, MLA_VEXT_W))],
        compiler_params=_cparams(("arbitrary",)),
        name="mla_cache_kv",
    )(ckv, kpe, *consts)


def _mla_attn_kernel(*refs, n_seg):
    q_ref = refs[0]
    k_refs = refs[1:1 + n_seg]
    v_refs = refs[1 + n_seg:1 + 2 * n_seg]
    o_ref = refs[1 + 2 * n_seg]
    scale = (MLA_NOPE + MLA_ROPE) ** -0.5
    c = scale * float(np.log2(np.e))
    tq = q_ref.shape[0]
    q_sub = min(tq, MLA_Q_SUB)
    kt = MXU_COLS
    tiles = [(si, r0) for si, k in enumerate(k_refs) for r0 in range(0, k.shape[0], kt)]
    lane = lax.broadcasted_iota(jnp.int32, (q_sub, LANES), 1)
    n_pairs = o_ref.shape[1] // LANES
    for pi, qs in [(pi, qs) for pi in range(n_pairs) for qs in range(tq // q_sub)]:
        rows = slice(qs * q_sub, (qs + 1) * q_sub)
        vl = slice(pi * 2 * LANES, (pi + 1) * 2 * LANES)
        res = []
        for j in range(2):
            hl = slice((2 * pi + j) * LANES, (2 * pi + j + 1) * LANES)
            qj = q_ref[rows, hl]
            macc = None
            s_tiles = []
            for si, r0 in tiles:
                s = _dot_nt(qj, k_refs[si][r0:r0 + kt, hl])
                s_tiles.append(s)
                mt = jnp.maximum(s[:, :LANES], s[:, LANES:])
                macc = mt if macc is None else jnp.maximum(macc, mt)
            m = macc.max(axis=-1, keepdims=True)
            r = None
            for (si, r0), s in zip(tiles, s_tiles):
                p = jnp.exp2((s - m) * c).astype(BF16)
                rj = _dot(p, v_refs[si][r0:r0 + kt, vl])
                r = rj if r is None else r + rj
            res.append(r[:, :LANES] / r[:, LANES:])
        o_ref[rows, pi * LANES:(pi + 1) * LANES] = jnp.where(lane < MLA_V, res[0], res[1]).astype(BF16)


def _mla_attention(q, ks, vs, *, n_batch, seq_q, q_tile, tok_off, k_batch_rows, pairs):
    n_seg = len(ks)
    nq = seq_q // q_tile
    qoff = tok_off // q_tile
    grid = (n_batch, MLA_HEADS // 2 // pairs, nq)
    wq = pairs * 2 * LANES
    in_specs = [pl.BlockSpec((q_tile, wq), lambda b, hp, i: (qoff + b * nq + i, hp))]
    for s in range(n_seg):
        rows = k_batch_rows[s]
        off = (tok_off // rows) if s == 0 else 0
        in_specs.append(pl.BlockSpec((rows, wq), functools.partial(lambda b, hp, i, off: (off + b, hp), off=off)))
    for s in range(n_seg):
        rows = k_batch_rows[s]
        off = (tok_off // rows) if s == 0 else 0
        in_specs.append(pl.BlockSpec((rows, wq), functools.partial(lambda b, hp, i, off: (off + b, hp), off=off)))
    return pl.pallas_call(
        functools.partial(_mla_attn_kernel, n_seg=n_seg),
        out_shape=jax.ShapeDtypeStruct((n_batch * seq_q, MLA_HEADS * MLA_V), BF16),
        grid=grid,
        in_specs=in_specs,
        out_specs=pl.BlockSpec((q_tile, pairs * LANES), lambda b, hp, i: (b * nq + i, hp)),
        compiler_params=_cparams(("parallel", "parallel", "arbitrary")),
        name=f"mla_attention_{n_seg}seg",
    )(q, *ks, *vs)


def _gla_kernel(q_ref, k_ref, v_ref, gr_ref, bf_ref, bb_ref, s0f_ref, s0b_ref, gn_ref,
                o_ref, sf_ref, sb_ref, acc_ref, kdf_ref, kdb_ref, qdf_ref, qdb_ref, hist_ref, *, n_chunks):
    c = GLA_CHUNK
    cpb = min(n_chunks, GLA_BLOCK_CHUNKS)
    blk = cpb * c
    lane = lax.broadcasted_iota(jnp.int32, (blk, LANES), 1)
    lo = lane < GLA_DK
    row = lax.broadcasted_iota(jnp.int32, (blk, blk), 0)
    col = lax.broadcasted_iota(jnp.int32, (blk, blk), 1)
    chunk_bits = c.bit_length() - 1
    same_chunk = jnp.right_shift(row, chunk_bits) == jnp.right_shift(col, chunk_bits)
    tril = same_chunk & (row >= col)
    triu = same_chunk & (row <= col)
    zero_blk = jnp.zeros((GLA_DK, GLA_DV), F32)

    def pair_state_t(s_ref):
        blockdiag = jnp.concatenate(
            [jnp.concatenate([s_ref[0], zero_blk], axis=1),
             jnp.concatenate([zero_blk, s_ref[1]], axis=1)], axis=0)
        return blockdiag.T

    dirs = ((bf_ref, c // 2 - 1, c - 1, tril, kdf_ref, qdf_ref),
            (bb_ref, c // 2, 0, triu, kdb_ref, qdb_ref))
    sels = (lo, jnp.logical_not(lo))
    lo_c = lax.broadcasted_iota(jnp.int32, (c, LANES), 1) < GLA_DK
    sels_c = (lo_c, jnp.logical_not(lo_c))
    hcols = (slice(0, GLA_DV), slice(GLA_DV, 2 * GLA_DV))

    def block(r, carry):
        rows = pl.ds(pl.multiple_of(r * blk, blk), blk)
        q = q_ref[rows, :]
        k = k_ref[rows, :]
        v = v_ref[rows, :]

        def chunk_row(b, r):
            return jnp.concatenate([jnp.broadcast_to(b[ch * c + r:ch * c + r + 1, :], (c, LANES))
                                    for ch in range(cpb)], axis=0)

        for d, (b_ref, mid_row, last_row, causal, kd_ref, qd_ref) in enumerate(dirs):
            b = b_ref[rows, :]
            b_mid = chunk_row(b, mid_row)
            b_last = chunk_row(b, last_row)
            qe = q * jnp.exp(b - b_mid)
            ke = (k * jnp.exp(b_mid - b)).astype(BF16)
            kd_ref[rows, :] = (k * jnp.exp(b_last - b)).astype(BF16)
            qd_ref[rows, :] = (q * jnp.exp(b)).astype(BF16)
            for j in range(2):
                a = _dot_nt(jnp.where(sels[j], qe, 0.0).astype(BF16), ke)
                o = _dot(jnp.where(causal, a, 0.0).astype(BF16), v[:, hcols[j]])
                if d == 0:
                    acc_ref[rows, hcols[j]] = o
                else:
                    acc_ref[rows, hcols[j]] += o
        return carry

    lax.fori_loop(0, n_chunks // cpb, block, 0, unroll=min(2, n_chunks // cpb))

    def scan(i, carry):
        new = []
        for d, (b_ref, _, last_row, _, kd_ref, _) in enumerate(dirs):
            ci = i if d == 0 else n_chunks - 1 - i
            rows = pl.ds(pl.multiple_of(ci * c, c), c)
            hist_ref[d, ci] = carry[d].astype(BF16)
            grp = b_ref[pl.ds(pl.multiple_of(ci * c + (last_row // 8) * 8, 8), 8), :]
            b_last = grp[last_row % 8:last_row % 8 + 1, :]
            new.append(carry[d] * jnp.exp(b_last) + _dot_tn(v_ref[rows, :], kd_ref[rows, :]))
        return tuple(new)

    st_f, st_b = lax.fori_loop(0, n_chunks, scan, (pair_state_t(s0f_ref), pair_state_t(s0b_ref)),
                               unroll=GLA_CHUNK_UNROLL)

    def inter(ci, carry):
        rows = pl.ds(pl.multiple_of(ci * c, c), c)
        for d, (_, _, _, _, _, qd_ref) in enumerate(dirs):
            qd = qd_ref[rows, :]
            st = hist_ref[d, ci]
            for j in range(2):
                acc_ref[rows, hcols[j]] += _dot_nt(jnp.where(sels_c[j], qd, jnp.zeros_like(qd)), st[hcols[j], :])
        return carry

    lax.fori_loop(0, n_chunks, inter, 0, unroll=GLA_CHUNK_UNROLL)
    s_f = st_f.T
    s_b = st_b.T
    sf_ref[0] = s_f[:GLA_DK, :GLA_DV]
    sf_ref[1] = s_f[GLA_DK:, GLA_DV:]
    sb_ref[0] = s_b[:GLA_DK, :GLA_DV]
    sb_ref[1] = s_b[GLA_DK:, GLA_DV:]

    gn = gn_ref[...]
    for j in range(2):
        cols = slice(j * GLA_DV, (j + 1) * GLA_DV)
        gr = gr_ref[:, cols]
        o_ref[:, cols] = (_rms(acc_ref[:, cols], gn) * (gr * jax.nn.sigmoid(gr))).astype(BF16)


def _gla(gq, gk, gv, gr, bf, bb, s0f, s0b, gnorm, *, n_batch, seq, tok_off):
    n_chunks = seq // GLA_CHUNK
    boff = tok_off // seq
    hp = GLA_HEADS // 2
    tok = lambda w: pl.BlockSpec((seq, w), lambda b, p: (boff + b, p))
    st = pl.BlockSpec((None, 2, GLA_DK, GLA_DV), lambda b, p: (b, p, 0, 0))
    return pl.pallas_call(
        functools.partial(_gla_kernel, n_chunks=n_chunks),
        out_shape=[jax.ShapeDtypeStruct((n_batch * seq, GLA_HEADS * GLA_DV), BF16),
                   jax.ShapeDtypeStruct((n_batch, GLA_HEADS, GLA_DK, GLA_DV), F32),
                   jax.ShapeDtypeStruct((n_batch, GLA_HEADS, GLA_DK, GLA_DV), F32)],
        grid=(n_batch, hp),
        in_specs=[tok(2 * GLA_DK), tok(2 * GLA_DK), tok(2 * GLA_DV), tok(2 * GLA_DV),
                  tok(2 * GLA_DK), tok(2 * GLA_DK), st, st, _const_spec((1, GLA_DV))],
        out_specs=[pl.BlockSpec((seq, 2 * GLA_DV), lambda b, p: (b, p)), st, st],
        scratch_shapes=[pltpu.VMEM((seq, 2 * GLA_DV), F32)]
                       + [pltpu.VMEM((seq, 2 * GLA_DK), BF16)] * 4
                       + [pltpu.VMEM((2, n_chunks, 2 * GLA_DV, 2 * GLA_DK), BF16)],
        compiler_params=_cparams(("parallel", "parallel")),
        name=f"gla_seq{seq}",
    )(gq, gk, gv, gr, bf, bb, s0f, s0b, gnorm)


def _out_proj_kernel(*refs, n_o, x_split, route):
    t = TOK_TILE
    o = [_pick(t, refs[2 * j], refs[2 * j + 1]) for j in range(n_o)]
    rest = refs[2 * n_o:]
    w_ref = rest[0]
    if x_split:
        x_in = _pick(t, rest[1], rest[2])
        rest = rest[3:]
    else:
        x_in = rest[1][...]
        rest = rest[2:]
    g1_ref, gate_ref, g2_ref, shift_ref, scale_ref = rest[:5]
    rest = rest[5:]
    if route:
        router_in, rest = rest[:3], rest[3:]
    x_out, h_out = rest[:2]
    y = _dot(o[0] if n_o == 1 else jnp.concatenate(o, axis=1), w_ref[...])
    x = x_in + gate_ref[...] * _rms(y, g1_ref[...])
    x_out[...] = x
    h = _modulate(x, g2_ref[...], shift_ref[...], scale_ref[...])
    if route:
        hb = h.astype(BF16)
        h_out[...] = _pack_pairs(hb)
        _route(hb, *router_in, *rest[2:])
    else:
        h_out[...] = h.astype(h_out.dtype)


def _out_proj(os_, w, x, mod5, layer, g1, g2, h_dtype, router=None):
    t = TOK_TILE
    x_split = isinstance(x, tuple)
    in_specs, args = [], []
    for o_p, o_s in os_:
        in_specs += _split_specs(t, o_p.shape[1])
        args += [o_p, o_s]
    in_specs.append(_const_spec(w.shape))
    args.append(w)
    if x_split:
        in_specs += _split_specs(t, D_MODEL)
        args += list(x)
    else:
        in_specs.append(_tok_spec(t, D_MODEL))
        args.append(x)
    in_specs += [_const_spec((1, D_MODEL)), _mod_spec(layer, 2, t), _const_spec((1, D_MODEL)),
                 _mod_spec(layer, 3, t), _mod_spec(layer, 4, t)]
    args += [g1, mod5, g2, mod5, mod5]
    h_width = D_MODEL // 2 if router is not None else D_MODEL
    out_shape = [jax.ShapeDtypeStruct((N_TOK, D_MODEL), F32), jax.ShapeDtypeStruct((N_TOK, h_width), h_dtype)]
    out_specs = [_tok_spec(t, D_MODEL), _tok_spec(t, h_width)]
    scratch = []
    if router is not None:
        r = np.arange(t)
        ltri = jnp.asarray(r[:, None] > r[None, :], BF16)
        args += [router[0], router[1], ltri]
        in_specs += [_const_spec(router[0].shape), _const_spec(router[1].shape), _const_spec((t, t))]
        out_shape += [jax.ShapeDtypeStruct((N_TOK, LANES), F32), jax.ShapeDtypeStruct((N_TOK, LANES), jnp.int32),
                      jax.ShapeDtypeStruct((N_TOK, LANES), jnp.int32), jax.ShapeDtypeStruct((8, LANES), jnp.int32)]
        out_specs += [_tok_spec(t, LANES), _tok_spec(t, LANES), _tok_spec(t, LANES),
                      pl.BlockSpec((8, LANES), lambda i: (0, 0))]
        scratch = [pltpu.VMEM((8, LANES), F32)]
    return pl.pallas_call(
        functools.partial(_out_proj_kernel, n_o=len(os_), x_split=x_split, route=router is not None),
        out_shape=out_shape,
        grid=(N_TOK // t,),
        in_specs=in_specs,
        out_specs=out_specs,
        scratch_shapes=scratch,
        compiler_params=_cparams(("arbitrary",) if router is not None else ("parallel",)),
        name=f"out_proj_{len(os_)}",
    )(*args)


def _ffn_kernel(h_ref, x_ref, wg_ref, wu_ref, wd_ref, g_ref, gate_ref, x_out):
    h = h_ref[...]
    f = None
    for cidx in range(D_FF // FF_CHUNK):
        cols = slice(cidx * FF_CHUNK, (cidx + 1) * FF_CHUNK)
        a = _dot(h, wg_ref[:, cols].astype(BF16))
        u = _dot(h, wu_ref[:, cols].astype(BF16))
        fc = _dot(((a * jax.nn.sigmoid(a)) * u).astype(BF16), wd_ref[cols, :].astype(BF16))
        f = fc if f is None else f + fc
    x_out[...] = x_ref[...] + gate_ref[...] * _rms(f, g_ref[...])


def _ffn(h, x, wg, wu, wd, mod5, layer, g3):
    t = TOK_TILE
    return pl.pallas_call(
        _ffn_kernel,
        out_shape=jax.ShapeDtypeStruct((N_TOK, D_MODEL), F32),
        grid=(N_TOK // t,),
        in_specs=[_tok_spec(t, D_MODEL), _tok_spec(t, D_MODEL), _const_spec(wg.shape),
                  _const_spec(wu.shape), _const_spec(wd.shape), _const_spec((1, D_MODEL)),
                  _mod_spec(layer, 5, t)],
        out_specs=_tok_spec(t, D_MODEL),
        compiler_params=_cparams(("parallel",)),
        name="ffn_swiglu",
    )(h, x, wg, wu, wd, g3, mod5)


GQA_Q_W = GQA_HEADS * GQA_HEAD_DIM
GQA_KV_W = GQA_KV_HEADS * GQA_HEAD_DIM
GQA_VEXT_W = (GQA_KV_HEADS // 2) * 2 * LANES
_O_Q = 0
_O_K = GQA_Q_W
_O_V = GQA_Q_W + GQA_KV_W
IN_C_EXT = _O_V + GQA_VEXT_W
GQA_ROT_HALF = GQA_HEAD_DIM // 4


def _rotate_half_lanes(x, half):
    lane = lax.broadcasted_iota(jnp.int32, (x.shape[0], LANES), 1)
    first = (lane & (2 * half - 1)) < half
    out = []
    for i in range(0, x.shape[1], LANES):
        blk = x[:, i:i + LANES]
        out.append(jnp.where(first, pltpu.roll(blk, LANES - half, 1), pltpu.roll(blk, half, 1)))
    return jnp.concatenate(out, axis=1)


def _odd_in_kernel(x_ref, g_ref, shift_ref, scale_ref, win_ref, vbias_ref, c_ref, s_ref,
                   q_out, kb_out, vb_out, kt_out, vt_out):
    h = _modulate(x_ref[...], g_ref[...], shift_ref[...], scale_ref[...])
    z = _dot(h.astype(BF16), win_ref[...])
    c_t = _lane_tile(c_ref[...], GQA_Q_W // LANES)
    s_t = _lane_tile(s_ref[...], GQA_Q_W // LANES)
    zq = z[:, _O_Q:_O_Q + GQA_Q_W]
    q = zq * c_t + _rotate_half_lanes(zq, GQA_ROT_HALF) * s_t
    q_out[...] = (q * (GQA_HEAD_DIM ** -0.5)).astype(BF16)
    zk = z[:, _O_K:_O_K + GQA_KV_W]
    k = zk * c_t[:, :GQA_KV_W] + _rotate_half_lanes(zk, GQA_ROT_HALF) * s_t[:, :GQA_KV_W]
    kb_out[...] = k.astype(BF16)
    vext = z[:, _O_V:_O_V + GQA_VEXT_W] + vbias_ref[...]
    vb_out[...] = vext.astype(BF16)

    @pl.when(pl.program_id(0) < NP_TOK // TOK_TILE)
    def _():
        v = jnp.concatenate([vext[:, 2 * p * LANES:(2 * p + 1) * LANES] for p in range(GQA_KV_HEADS // 2)], axis=1)
        for b in range(TOK_TILE // SEQ):
            kt_out[b] = k[b * SEQ:(b + 1) * SEQ, :].T
            vt_out[b] = v[b * SEQ:(b + 1) * SEQ, :].T


def _odd_in_proj(x, mod5, layer, g, win, vbias, tab_c, tab_s):
    t = TOK_TILE
    npt = NP_TOK // t
    out_widths = [(GQA_Q_W, BF16), (GQA_KV_W, BF16), (GQA_VEXT_W, BF16)]
    cache_shape = jax.ShapeDtypeStruct((BATCH, GQA_KV_W, SEQ), F32)
    cache_spec = pl.BlockSpec((t // SEQ, GQA_KV_W, SEQ), lambda i: (jnp.minimum(i, npt - 1), 0, 0))
    return pl.pallas_call(
        _odd_in_kernel,
        out_shape=[jax.ShapeDtypeStruct((N_TOK, w), dt) for w, dt in out_widths] + [cache_shape, cache_shape],
        grid=(N_TOK // t,),
        in_specs=[_tok_spec(t, D_MODEL), _const_spec((1, D_MODEL)), _mod_spec(layer, 0, t),
                  _mod_spec(layer, 1, t), _const_spec(win.shape), _const_spec(vbias.shape),
                  _rope_row_spec(t, LANES), _rope_row_spec(t, LANES)],
        out_specs=[_tok_spec(t, w) for w, _ in out_widths] + [cache_spec, cache_spec],
        compiler_params=_cparams(("arbitrary",)),
        name="odd_in_proj",
    )(x, g, mod5, mod5, win, vbias, tab_c, tab_s)


def _gqa_kernel(sink_ref, q_ref, *refs, local_len):
    if local_len:
        kl_ref, vl_ref, kc_ref, vc_ref, o_ref = refs
    else:
        kc_ref, vc_ref, o_ref = refs
    tq = local_len - 2 * WINDOW if local_len else q_ref.shape[0]
    lane = lax.broadcasted_iota(jnp.int32, (tq, LANES), 1)
    lo = lane < GQA_HEAD_DIM
    for sub, p in [(sub, p) for sub in range(q_ref.shape[0] // tq) for p in range(GQA_KV_HEADS // 2)]:
        rows = slice(sub * tq, (sub + 1) * tq)
        if local_len:
            q0 = pl.program_id(1) * q_ref.shape[0] + sub * tq
            seq = kl_ref.shape[0]
            kstart = pl.multiple_of(jnp.clip(q0 - WINDOW, 0, seq - local_len), LANES)
            qpos = q0 + lax.broadcasted_iota(jnp.int32, (tq, local_len), 0)
            kpos = kstart + lax.broadcasted_iota(jnp.int32, (tq, local_len), 1)
            band = jnp.abs(qpos - kpos) <= WINDOW
        kc = kc_ref[:, p * LANES:(p + 1) * LANES]
        vc = vc_ref[:, 2 * p * LANES:(2 * p + 2) * LANES]
        if local_len:
            kl = kl_ref[pl.ds(kstart, local_len), p * LANES:(p + 1) * LANES]
            vl = vl_ref[pl.ds(kstart, local_len), 2 * p * LANES:(2 * p + 2) * LANES]
        for blk in range(GQA_GROUP):
            cols = slice((p * GQA_GROUP + blk) * LANES, (p * GQA_GROUP + blk + 1) * LANES)
            qb = q_ref[rows, cols]
            res = []
            for half in range(2):
                head = (2 * p + half) * GQA_GROUP + blk
                sink = sink_ref[head]
                qh = jnp.where(lo if half == 0 else jnp.logical_not(lo), qb, jnp.zeros_like(qb))
                s_c = _dot_nt(qh, kc)
                m = jnp.maximum(s_c.max(axis=-1, keepdims=True), sink)
                if local_len:
                    s_l = jnp.where(band, _dot_nt(qh, kl), NEG_INF)
                    m = jnp.maximum(m, s_l.max(axis=-1, keepdims=True))
                r = _dot(jnp.exp(s_c - m).astype(BF16), vc)
                if local_len:
                    r = r + _dot(jnp.exp(s_l - m).astype(BF16), vl)
                res.append(r[:, :LANES] / (r[:, LANES:] + jnp.exp(sink - m)))
            o_ref[rows, cols] = jnp.where(lo, res[0], res[1]).astype(BF16)


def _gqa_attention(sink, q, k_loc, v_loc, k_ctx, v_ctx, *, n_batch, seq_q, q_tile, tok_off, n_ctx, local):
    nq = seq_q // q_tile
    qoff = tok_off // q_tile
    local_len = GQA_SUB_TILE + 2 * WINDOW if local else 0
    in_specs = [pl.BlockSpec(memory_space=pltpu.SMEM),
                pl.BlockSpec((q_tile, GQA_Q_W), lambda b, i: (qoff + b * nq + i, 0))]
    args = [sink, q]
    if local:
        boff = tok_off // seq_q
        in_specs += [pl.BlockSpec((seq_q, GQA_KV_W), lambda b, i: (boff + b, 0)),
                     pl.BlockSpec((seq_q, GQA_VEXT_W), lambda b, i: (boff + b, 0))]
        args += [k_loc, v_loc]
    in_specs += [pl.BlockSpec((n_ctx, GQA_KV_W), lambda b, i: (b, 0)),
                 pl.BlockSpec((n_ctx, GQA_VEXT_W), lambda b, i: (b, 0))]
    args += [k_ctx, v_ctx]
    return pl.pallas_call(
        functools.partial(_gqa_kernel, local_len=local_len),
        out_shape=jax.ShapeDtypeStruct((n_batch * seq_q, GQA_Q_W), BF16),
        grid=(n_batch, nq),
        in_specs=in_specs,
        out_specs=pl.BlockSpec((q_tile, GQA_Q_W), lambda b, i: (b * nq + i, 0)),
        compiler_params=_cparams(("parallel", "arbitrary")),
        name="gqa_local" if local else "gqa_ctx",
    )(*args)


def _route(h, w_ref, b_ref, ltri_ref, wsel_out, isel_out, rank_out, cnt_out, carry_ref):
    @pl.when(pl.program_id(0) == 0)
    def _():
        carry_ref[...] = jnp.zeros_like(carry_ref)

    logits = _dot(h, w_ref[...]) + b_ref[...]
    lane = lax.broadcasted_iota(jnp.int32, logits.shape, 1)
    neg = float(np.finfo(np.float32).min)
    lg = jnp.where(lane < N_EXPERTS, logits, neg)
    v1 = lg.max(axis=-1, keepdims=True)
    i1 = jnp.min(jnp.where(lg == v1, lane, LANES), axis=-1, keepdims=True)
    lg2 = jnp.where(lane == i1, neg, lg)
    v2 = lg2.max(axis=-1, keepdims=True)
    i2 = jnp.min(jnp.where(lg2 == v2, lane, LANES), axis=-1, keepdims=True)
    e2 = jnp.exp(v2 - v1)
    den = 1.0 + e2
    wsel_out[...] = jnp.where(lane == 0, 1.0 / den, jnp.where(lane == 1, e2 / den, 0.0))
    isel_out[...] = jnp.where(lane == 0, i1, jnp.where(lane == 1, i2, 0))
    hit = jnp.where(lane == i1, 1.0, jnp.where(lane == i2, 1.0, 0.0))
    carry = carry_ref[...]
    rank_out[...] = (_dot(ltri_ref[...], hit.astype(BF16)) + carry[0:1, :]).astype(jnp.int32)
    carry = carry + jnp.sum(hit, axis=0, keepdims=True)
    carry_ref[...] = carry
    cnt_out[...] = carry.astype(jnp.int32)


def _route_tables(isel, rank, cnt):
    tm = MOE_ROW_TILE
    counts = cnt[0, :N_EXPERTS]
    padded = ((counts + tm - 1) // tm) * tm
    ends = jnp.cumsum(padded)
    base = ends - padded
    e_ids = jnp.arange(N_EXPERTS, dtype=jnp.int32)
    row = rank[:, :N_EXPERTS] + base[None, :]
    pos1 = jnp.sum(jnp.where(e_ids[None, :] == isel[:, 0:1], row, 0), axis=1)
    pos2 = jnp.sum(jnp.where(e_ids[None, :] == isel[:, 1:2], row, 0), axis=1)
    tile_start = jnp.arange(MOE_ROWS // tm, dtype=jnp.int32) * tm
    tile_expert = jnp.minimum(jnp.sum(tile_start[:, None] >= ends[None, :], axis=1), N_EXPERTS - 1).astype(jnp.int32)
    tile_valid = jnp.clip((base + counts)[tile_expert] - tile_start, 0, tm).astype(jnp.int32)
    tile_valid = jnp.where(tile_start < ends[-1], tile_valid, 0)
    return pos1, pos2, tile_expert, tile_valid


def _scatter_rows(rows, pos1, pos2, n_out):
    n_tok, d = rows.shape
    per_w = n_tok // SC_WORKERS
    w = SC_GATHER_WINDOW * D_MODEL // d
    assert per_w * SC_WORKERS == n_tok and per_w % w == 0
    mesh = plsc.VectorSubcoreMesh(core_axis_name="core", subcore_axis_name="subcore")

    assert (per_w // w) % 2 == 0
    slot_types = [pltpu.VMEM((w,), jnp.int32), pltpu.VMEM((w,), jnp.int32), pltpu.VMEM((w, d), rows.dtype),
                  pltpu.SemaphoreType.DMA, pltpu.SemaphoreType.DMA]

    @functools.partial(
        pl.kernel, out_type=jax.ShapeDtypeStruct((n_out, d), rows.dtype), mesh=mesh,
        scratch_types=slot_types * 2, name="sc_scatter_rows")
    def scatter(x_hbm, p1_hbm, p2_hbm, o_hbm, *scratch):
        wid = lax.axis_index("subcore") * SC_CORES + lax.axis_index("core")
        base = wid * per_w
        slots = (scratch[:5], scratch[5:])

        @pl.loop(0, per_w // (2 * w))
        def _(g):
            loads = []
            for s, (i1_v, i2_v, rows_v, sem_in, _) in enumerate(slots):
                off = base + (2 * g + s) * w
                loads.append([pltpu.async_copy(p1_hbm.at[pl.ds(off, w)], i1_v, sem_in),
                              pltpu.async_copy(p2_hbm.at[pl.ds(off, w)], i2_v, sem_in),
                              pltpu.async_copy(x_hbm.at[pl.ds(off, w)], rows_v, sem_in)])
            stores = []
            for s, (i1_v, i2_v, rows_v, _, sem_out) in enumerate(slots):
                for cp in loads[s]:
                    cp.wait()
                stores += [pltpu.async_copy(rows_v, o_hbm.at[i1_v], sem_out),
                           pltpu.async_copy(rows_v, o_hbm.at[i2_v], sem_out)]
            for cp in stores:
                cp.wait()

    return scatter(rows, pos1, pos2)


def _gather_rows(table, idx):
    n_idx = idx.shape[0]
    d = table.shape[1]
    per_w = n_idx // SC_WORKERS
    assert per_w * SC_WORKERS == n_idx and per_w % SC_INDEX_BLOCK == 0
    mesh = plsc.VectorSubcoreMesh(core_axis_name="core", subcore_axis_name="subcore")

    w = SC_GATHER_WINDOW
    n_sub = SC_INDEX_BLOCK // w
    slot_types = [pltpu.VMEM((w, d), table.dtype), pltpu.SemaphoreType.DMA, pltpu.SemaphoreType.DMA]

    @functools.partial(
        pl.kernel, out_type=jax.ShapeDtypeStruct((n_idx, d), table.dtype), mesh=mesh,
        scratch_types=[pltpu.VMEM((SC_INDEX_BLOCK,), jnp.int32)] + slot_types * 2,
        name="sc_gather_rows")
    def gather(x_hbm, i_hbm, o_hbm, idx_v, *scratch):
        wid = lax.axis_index("subcore") * SC_CORES + lax.axis_index("core")
        base = wid * per_w
        slots = (scratch[:3], scratch[3:])

        @pl.loop(0, per_w // SC_INDEX_BLOCK)
        def _(g):
            off = base + g * SC_INDEX_BLOCK
            pltpu.sync_copy(i_hbm.at[pl.ds(off, SC_INDEX_BLOCK)], idx_v)

            def start_gather(s):
                rows_v, sem_in, _ = slots[s % 2]
                return pltpu.async_copy(x_hbm.at[idx_v.at[pl.ds(s * w, w)]], rows_v, sem_in)

            gathers = {0: start_gather(0)}
            writes = {}
            for s in range(n_sub):
                if s + 1 < n_sub:
                    if s >= 1:
                        writes[s - 1].wait()
                    gathers[s + 1] = start_gather(s + 1)
                gathers[s].wait()
                rows_v, _, sem_out = slots[s % 2]
                writes[s] = pltpu.async_copy(rows_v, o_hbm.at[pl.ds(off + s * w, w)], sem_out)
            writes[n_sub - 2].wait()
            writes[n_sub - 1].wait()

    return gather(table, idx)


def _expert_ffn_kernel(te_ref, nv_ref, x_ref, wg_ref, wu_ref, wd_ref, y_out):
    n_valid = nv_ref[pl.program_id(0)]

    @pl.when(n_valid > 0)
    def _():
        row = lax.broadcasted_iota(jnp.int32, x_ref.shape, 0)
        h = _unpack_pairs(jnp.where(row < n_valid, x_ref[...], 0.0))
        f = None
        for cidx in range(D_FF // FF_CHUNK):
            cols = slice(cidx * FF_CHUNK, (cidx + 1) * FF_CHUNK)
            a = _dot(h, wg_ref[:, cols].astype(BF16))
            u = _dot(h, wu_ref[:, cols].astype(BF16))
            fc = _dot(((a * jax.nn.sigmoid(a)) * u).astype(BF16), wd_ref[cols, :].astype(BF16))
            f = fc if f is None else f + fc
        y_out[...] = f

    @pl.when(n_valid == 0)
    def _():
        y_out[...] = jnp.zeros_like(y_out)


def _expert_ffn(xs, tile_expert, tile_valid, wg, wu, wd):
    tm = MOE_ROW_TILE
    wspec = lambda shape: pl.BlockSpec((None,) + shape, lambda j, te, nu: (te[j], 0, 0),
                                       pipeline_mode=pl.Buffered(1))
    return pl.pallas_call(
        _expert_ffn_kernel,
        out_shape=jax.ShapeDtypeStruct((MOE_ROWS, D_MODEL), F32),
        grid_spec=pltpu.PrefetchScalarGridSpec(
            num_scalar_prefetch=2,
            grid=(MOE_ROWS // tm,),
            in_specs=[pl.BlockSpec((tm, xs.shape[1]), lambda j, te, nu: (j, 0)),
                      wspec((D_MODEL, D_FF)), wspec((D_MODEL, D_FF)), wspec((D_FF, D_MODEL))],
            out_specs=pl.BlockSpec((tm, D_MODEL), lambda j, te, nu: (j, 0)),
        ),
        compiler_params=_cparams(("arbitrary",)),
        name="moe_expert_ffn",
    )(tile_expert, tile_valid, xs, wg, wu, wd)


def _moe_combine_kernel(y1_ref, y2_ref, wsel_ref, x_ref, g_ref, gate_ref, *rest):
    x_out = rest[-1]
    w = wsel_ref[...]
    f = w[:, 0:1] * y1_ref[...] + w[:, 1:2] * y2_ref[...]
    x_out[...] = x_ref[...] + gate_ref[...] * _rms(f, g_ref[...])


def _moe_combine(yg, wsel, x, mod5, layer, g3, *, tok_off, out_rows, out_off, y_prev=None):
    t = TOK_TILE
    nt = MOE_SEG // t
    off = tok_off // t
    ooff = out_off // t
    tpg = NP_TOK // t
    tok = lambda w: pl.BlockSpec((t, w), lambda i: (off + i, 0))
    in_specs = [_tok_spec(t, D_MODEL), pl.BlockSpec((t, D_MODEL), lambda i: (nt + i, 0)), tok(LANES), tok(D_MODEL),
                _const_spec((1, D_MODEL)),
                pl.BlockSpec((None, None, None, 1, D_MODEL), lambda i: (layer, (off + i) // tpg, 5, 0, 0))]
    args = [yg, yg, wsel, x, g3, mod5]
    aliases = {}
    if y_prev is not None:
        in_specs.append(pl.BlockSpec(memory_space=pl.ANY))
        args.append(y_prev)
        aliases = {len(args) - 1: 0}
    return pl.pallas_call(
        _moe_combine_kernel,
        out_shape=jax.ShapeDtypeStruct((out_rows, D_MODEL), F32),
        grid=(nt,),
        in_specs=in_specs,
        out_specs=pl.BlockSpec((t, D_MODEL), lambda i: (ooff + i, 0)),
        input_output_aliases=aliases,
        compiler_params=_cparams(("parallel",)),
        name="moe_combine",
    )(*args)


def _moe(h, x, routing, wg, wu, wd, mod5, layer, g3):
    wsel, isel, rank, cnt = routing
    pos1, pos2, tile_expert, tile_valid = _route_tables(isel, rank, cnt)
    xs = _scatter_rows(h, pos1, pos2, MOE_ROWS)
    ys = _expert_ffn(xs, tile_expert, tile_valid, wg, wu, wd)
    y_p = y_s = None
    for seg in range(N_TOK // MOE_SEG):
        rows = slice(seg * MOE_SEG, (seg + 1) * MOE_SEG)
        yg = _gather_rows(ys, jnp.concatenate([pos1[rows], pos2[rows]]))
        common = dict(tok_off=seg * MOE_SEG)
        if seg * MOE_SEG < NP_TOK:
            y_p = _moe_combine(yg, wsel, x, mod5, layer, g3, out_rows=NP_TOK, out_off=seg * MOE_SEG, y_prev=y_p, **common)
        else:
            y_s = _moe_combine(yg, wsel, x, mod5, layer, g3, out_rows=NS_TOK, out_off=seg * MOE_SEG - NP_TOK,
                               y_prev=y_s, **common)
    return y_p, y_s


def _rot_cols(w, half):
    k, n = w.shape
    wb = w.reshape(k, n // (2 * half), 2, half)
    return jnp.stack([-wb[:, :, 1], wb[:, :, 0]], axis=2).reshape(k, n)


def _axis_tables(r, pos):
    inv = np.float32(ROPE_BASE) ** (-np.arange(0, r, 2, dtype=np.float32) / np.float32(r))
    ang = pos.astype(np.float32)[:, None] * inv[None, :]
    cos, sin = np.cos(ang), np.sin(ang)
    return np.concatenate([cos, cos], axis=1), np.concatenate([sin, sin], axis=1)


def _rope_tables(r):
    s = np.arange(DEC_SEQ)
    cr, sr = _axis_tables(r // 2, s // GRID_W)
    cc, sc = _axis_tables(r // 2, s % GRID_W)
    return np.concatenate([cr, cc], axis=1), np.concatenate([sr, sc], axis=1)


def _with_identity(tab, ident):
    return np.concatenate([np.full((TOK_TILE, tab.shape[1]), ident, np.float32), tab], axis=0)


@functools.lru_cache(maxsize=None)
def _rope_constants():
    c32, s32 = _rope_tables(MLA_ROPE)
    ones = np.ones((DEC_SEQ, MLA_NOPE), np.float32)
    pad1 = np.ones((DEC_SEQ, MLA_HEAD_PAD - MLA_NOPE - MLA_ROPE), np.float32)
    def signed(s, half):
        return s * np.where(np.arange(s.shape[1]) % (2 * half) < half, -1.0, 1.0).astype(np.float32)

    cq = np.concatenate([ones, c32, pad1], axis=1)
    sq = np.concatenate([0 * ones, signed(s32, MLA_ROPE // 4), 0 * pad1], axis=1)
    c64, s64 = _rope_tables(GQA_HEAD_DIM)
    s64 = signed(s64, GQA_ROT_HALF)
    return {
        "mla_cq": _with_identity(cq, 1.0), "mla_sq": _with_identity(sq, 0.0),
        "mla_ck": _with_identity(c32, 1.0), "mla_sk": _with_identity(s32, 0.0),
        "gqa_c": _with_identity(np.concatenate([c64, c64], axis=1), 1.0),
        "gqa_s": _with_identity(np.concatenate([s64, s64], axis=1), 0.0),
    }


def _prep_tables():
    return {k: jnp.asarray(v, F32) for k, v in _rope_constants().items()}


def _prep_even(w_in, q_norm, w_q_up, kv_norm, w_kv_up, wgf, bgf, wgb, bgb):
    sizes = [MLA_Q_RANK, MLA_KV_RANK, MLA_ROPE, GLA_HEADS * GLA_DK, GLA_HEADS * GLA_DK,
             GLA_HEADS * GLA_DV, GLA_HEADS * GLA_DV, GLA_GATE_RANK, GLA_GATE_RANK]
    cq, ckv, kpe, gq, gk, gv, gr, gaf, gab = jnp.split(w_in, [int(s) for s in np.cumsum(sizes)[:-1]], axis=1)
    pad = jnp.zeros((D_MODEL, LANES - 2 * MLA_ROPE - 2 * GLA_GATE_RANK), F32)
    win = jnp.concatenate([cq, ckv, gq, gk, gv, gr, kpe, _rot_cols(kpe, MLA_ROPE // 4), gaf, gab, pad], axis=1)

    wq = w_q_up.reshape(MLA_Q_RANK, MLA_HEADS, MLA_NOPE + MLA_ROPE)
    nope, pe = wq[..., :MLA_NOPE], wq[..., MLA_NOPE:]
    zpad = jnp.zeros((MLA_Q_RANK, MLA_HEADS, MLA_HEAD_PAD - MLA_NOPE - MLA_ROPE), F32)
    wq_main = jnp.concatenate([nope, pe, zpad], axis=-1).reshape(MLA_Q_RANK, MLA_QK_W)

    wkv = w_kv_up.reshape(MLA_KV_RANK, MLA_HEADS, MLA_NOPE + MLA_V)
    knope, vv = wkv[..., :MLA_NOPE], wkv[..., MLA_NOPE:]
    wkk = jnp.concatenate([knope, jnp.zeros((MLA_KV_RANK, MLA_HEADS, MLA_HEAD_PAD - MLA_NOPE), F32)],
                          axis=-1).reshape(MLA_KV_RANK, MLA_QK_W)
    vpair = vv.reshape(MLA_KV_RANK, MLA_HEADS // 2, 2 * MLA_V)
    wkv_ext = jnp.concatenate([vpair, jnp.zeros((MLA_KV_RANK, MLA_HEADS // 2, LANES), F32)],
                              axis=-1).reshape(MLA_KV_RANK, MLA_VEXT_W)
    vbias = jnp.tile(jnp.concatenate([jnp.zeros((LANES,), F32), jnp.ones((LANES,), F32)]),
                     MLA_HEADS // 2).reshape(1, MLA_VEXT_W)
    epl = jnp.tile(jnp.concatenate([jnp.zeros((MLA_ROPE, MLA_NOPE), F32), jnp.eye(MLA_ROPE, dtype=F32),
                                    jnp.zeros((MLA_ROPE, MLA_HEAD_PAD - MLA_NOPE - MLA_ROPE), F32)], axis=1),
                   (1, MLA_HEADS))

    def gate_w(w, off):
        return jnp.zeros((LANES, GLA_HEADS * GLA_DK), F32).at[off:off + GLA_GATE_RANK].set(w)

    r = np.arange(CUMSUM_BLOCK)
    same = (r[:, None] // GLA_CHUNK) == (r[None, :] // GLA_CHUNK)
    lmat = jnp.asarray(same & (r[:, None] >= r[None, :]), BF16)
    umat = jnp.asarray(same & (r[:, None] <= r[None, :]), BF16)
    return {
        "win": win.astype(BF16), "qn": q_norm.reshape(1, -1), "wq": wq_main.astype(BF16),
        "kvn": kv_norm.reshape(1, -1), "wkk": wkk.astype(BF16), "wkv": wkv_ext.astype(BF16), "vbias": vbias,
        "epl": epl.astype(BF16), "wgf": gate_w(wgf, _S_GAF).astype(BF16), "bgf": bgf.reshape(1, -1),
        "wgb": gate_w(wgb, _S_GAB).astype(BF16), "bgb": bgb.reshape(1, -1), "lmat": lmat, "umat": umat,
    }


def _gqa_head_perm():
    heads = []
    for p in range(GQA_KV_HEADS // 2):
        for i in range(GQA_GROUP):
            heads += [(2 * p) * GQA_GROUP + i, (2 * p + 1) * GQA_GROUP + i]
    return np.asarray(heads)


def _prep_odd(w_in, w_out):
    perm = _gqa_head_perm()
    wq = w_in[:, :GQA_Q_W].reshape(D_MODEL, GQA_HEADS, GQA_HEAD_DIM)[:, perm].reshape(D_MODEL, GQA_Q_W)
    wk = w_in[:, GQA_Q_W:GQA_Q_W + GQA_KV_W]
    wv = w_in[:, GQA_Q_W + GQA_KV_W:].reshape(D_MODEL, GQA_KV_HEADS // 2, 2 * GQA_HEAD_DIM)
    wv_ext = jnp.concatenate([wv, jnp.zeros((D_MODEL, GQA_KV_HEADS // 2, LANES), F32)], axis=-1).reshape(D_MODEL, GQA_VEXT_W)
    win = jnp.concatenate([wq, wk, wv_ext], axis=1)
    vbias = jnp.tile(jnp.concatenate([jnp.zeros((LANES,), F32), jnp.ones((LANES,), F32)]),
                     GQA_KV_HEADS // 2).reshape(1, GQA_VEXT_W)
    wo = w_out.reshape(GQA_HEADS, GQA_HEAD_DIM, D_MODEL)[perm].reshape(GQA_Q_W, D_MODEL)
    return win.astype(BF16), vbias, wo.astype(BF16)


def _ext_v(v):
    rows = v.shape[0]
    vp = v.reshape(rows, GQA_KV_HEADS // 2, 2 * GQA_HEAD_DIM)
    return jnp.concatenate([vp, jnp.ones((rows, GQA_KV_HEADS // 2, LANES), v.dtype)], axis=-1).reshape(rows, GQA_VEXT_W)


def kernel(x_prompt, x_sample, cache_mla_ckv, cache_mla_kpe, state_gla_fwd, state_gla_bwd, cache_gqa_k, cache_gqa_v, c, c_ctx, w_mod, b_mod, norm_g, w_in_ab, mla_q_norm, mla_w_q_up, mla_kv_norm, mla_w_kv_up, gla_w_gate_f, gla_b_gate_f, gla_w_gate_b, gla_b_gate_b, gla_norm, w_out_ab, ffn_w_gate, ffn_w_up, ffn_w_down, w_in_c, gqa_sink, w_out_c, moe_w_router, moe_b_router, moe_w_gate, moe_w_up, moe_w_down):
    x_in = (x_prompt.reshape(NP_TOK, D_MODEL), x_sample.reshape(NS_TOK, D_MODEL))
    cvec =jnp.concatenate([c_ctx[None, :], c, jnp.zeros((MOD_ROWS - N_GROUPS, D_MODEL), F32)], axis=0)
    mod5 = _modulation(cvec, w_mod, b_mod).reshape(DEPTH, MOD_ROWS, N_MOD, 1, D_MODEL)
    tabs = _prep_tables()
    gvec = lambda l, j: norm_g[l, j].reshape(1, D_MODEL)

    wts = _prep_even(w_in_ab[0], mla_q_norm[0], mla_w_q_up[0], mla_kv_norm[0], mla_w_kv_up[0],
                     gla_w_gate_f[0], gla_b_gate_f[0], gla_w_gate_b[0], gla_b_gate_b[0])
    (q, k, v, ckv, kpe, gq, gk, gv, gr, bf, bb) = _even_in_proj(*x_in, mod5, 0, gvec(0, 0), wts, tabs)
    kc, vc = _cache_kv(cache_mla_ckv[:, 0].reshape(DEC_BATCH * PAST_LEN, MLA_KV_RANK),
                       cache_mla_kpe[:, 0].reshape(DEC_BATCH * PAST_LEN, MLA_ROPE), wts)
    oa_p = _mla_attention(q, [k], [v], n_batch=BATCH, seq_q=SEQ, q_tile=SEQ, tok_off=0, k_batch_rows=[SEQ],
                          pairs=MLA_HEADS // 2)
    oa_s = _mla_attention(q, [k, kc], [v, vc], n_batch=DEC_BATCH, seq_q=DEC_SEQ, q_tile=MLA_Q_TILE,
                          tok_off=NP_TOK, k_batch_rows=[DEC_SEQ, PAST_LEN], pairs=1)
    gn = gla_norm[0].reshape(1, GLA_DV)
    zero_state = jnp.zeros((BATCH, GLA_HEADS, GLA_DK, GLA_DV), F32)
    ob_p, sf, sb = _gla(gq, gk, gv, gr, bf, bb, zero_state, zero_state, gn, n_batch=BATCH, seq=SEQ, tok_off=0)
    ob_s, _, _ = _gla(gq, gk, gv, gr, bf, bb, state_gla_fwd[:, 0], state_gla_bwd[:, 0], gn,
                      n_batch=DEC_BATCH, seq=DEC_SEQ, tok_off=NP_TOK)
    x, h = _out_proj([(oa_p, oa_s), (ob_p, ob_s)], w_out_ab[0].astype(BF16), x_in, mod5, 0, gvec(0, 1),
                     gvec(0, 2), BF16)
    x = _ffn(h, x, ffn_w_gate[0], ffn_w_up[0], ffn_w_down[0], mod5, 0, gvec(0, 3))

    win_c, vbias_c, wo_c = _prep_odd(w_in_c[0], w_out_c[0])
    qg, kgb, vgb, kg_t, vg_t = _odd_in_proj(x, mod5, 1, gvec(1, 0), win_c, vbias_c, tabs["gqa_c"], tabs["gqa_s"])
    sink = gqa_sink[0]
    og_p = _gqa_attention(sink, qg, None, None, kgb, vgb, n_batch=BATCH, seq_q=SEQ, q_tile=SEQ, tok_off=0, n_ctx=SEQ, local=False)
    kc_g = cache_gqa_k[:, 0].reshape(DEC_BATCH * PAST_LEN, GQA_KV_W).astype(BF16)
    vc_g = _ext_v(cache_gqa_v[:, 0].reshape(DEC_BATCH * PAST_LEN, GQA_KV_W)).astype(BF16)
    og_s = _gqa_attention(sink, qg, kgb, vgb, kc_g, vc_g, n_batch=DEC_BATCH, seq_q=DEC_SEQ, q_tile=GQA_Q_TILE,
                          tok_off=NP_TOK, n_ctx=PAST_LEN, local=True)
    w_r = jnp.zeros((D_MODEL, LANES), F32).at[:, :N_EXPERTS].set(moe_w_router[0]).astype(BF16)
    b_r = jnp.zeros((1, LANES), F32).at[0, :N_EXPERTS].set(moe_b_router[0])
    x, h, *routing = _out_proj([(og_p, og_s)], wo_c, x, mod5, 1, gvec(1, 1), gvec(1, 2), F32, router=(w_r, b_r))
    y_p, y_s = _moe(h, x, routing, moe_w_gate[0], moe_w_up[0], moe_w_down[0], mod5, 1, gvec(1, 3))

    y_prompt = y_p.reshape(BATCH, SEQ, D_MODEL)
    y_sample = y_s.reshape(DEC_BATCH, DEC_SEQ, D_MODEL)
    new_ckv = ckv[:NP_TOK].reshape(BATCH, 1, SEQ, MLA_KV_RANK)
    new_kpe = kpe[:NP_TOK].reshape(BATCH, 1, SEQ, MLA_ROPE)
    as_cache = lambda a: jnp.transpose(a.reshape(BATCH, 1, GQA_KV_HEADS, GQA_HEAD_DIM, SEQ), (0, 1, 4, 2, 3))
    new_k = as_cache(kg_t)
    new_v = as_cache(vg_t)
    return (y_prompt, y_sample, new_ckv, new_kpe, sf[:, None], sb[:, None], new_k, new_v)
```
